```python
import math
import jax
import jax.numpy as jnp
from jax import lax
import numpy as np


D_MODEL = 1024
BATCH = 8
SEQ = 2048
DEPTH = 4

GRID_W = 64
CTX_LEN = 256
N_MIXERS = 4
EPS = 1e-6
NEG_INF = -1e30
ROPE_BASE = 10000.0
HEAD_DIM = 64

LRU_WIDTH = 1408
LRU_BLOCKS = 16
LRU_BW = LRU_WIDTH // LRU_BLOCKS
CONV_W = 4
LRU_C = 8.0

SWA_HEADS = 16
SWA_KV_HEADS = 4
WINDOW = 128
BLOCK_Q = 128

NA_HEADS = 16
NA_ROWS = 8
NA_COLS = 16

S5_WIDTH = 1024
S5_GROUP = 16
S5_GROUPS = S5_WIDTH // S5_GROUP
S5_STATE = 64

kernel_name = 'hybrid_interleaved_dit_trunk'


def rmsnorm(x, g):
    xf = x.astype(jnp.float32)
    y = xf * lax.rsqrt(jnp.mean(xf * xf, axis=-1, keepdims=True) + EPS)
    return (y * g.astype(jnp.float32)).astype(x.dtype)


def modulation(vec, w, b):
    m = jax.nn.silu(vec) @ w + b
    shift, scale, gate = jnp.split(m, 3, axis=-1)
    return shift[:, None, :], scale[:, None, :], gate[:, None, :]


def axial_rope(t_len):
    pos = jnp.arange(t_len)
    row = (pos // GRID_W).astype(jnp.float32)
    col = (pos % GRID_W).astype(jnp.float32)
    n_ax = HEAD_DIM // 4
    freqs = ROPE_BASE ** (-jnp.arange(n_ax, dtype=jnp.float32) / n_ax)
    ang = jnp.concatenate([row[:, None] * freqs, col[:, None] * freqs], axis=-1)
    return jnp.cos(ang), jnp.sin(ang)


def apply_rope(x, cos, sin):
    shape = (1, x.shape[1]) + (1,) * (x.ndim - 3) + (cos.shape[-1],)
    cos = cos.reshape(shape)
    sin = sin.reshape(shape)
    x1, x2 = jnp.split(x.astype(jnp.float32), 2, axis=-1)
    return jnp.concatenate([x1 * cos - x2 * sin, x2 * cos + x1 * sin], axis=-1).astype(x.dtype)


def sink_softmax(s, sink):
    s = s.astype(jnp.float32)
    sk = jnp.broadcast_to(sink.astype(jnp.float32), s.shape[:-1] + (1,))
    return jax.nn.softmax(jnp.concatenate([sk, s], axis=-1), axis=-1)[..., 1:]


def linear_scan(a, b, h0=None):
    def combine(l, r):
        return l[0] * r[0], r[0] * l[1] + r[1]
    a_cum, h = lax.associative_scan(combine, (a, b), axis=1)
    if h0 is not None:
        h = h + a_cum * h0[:, None]
    return h


def scan_with_prefix(a_c, b_c, a_l, b_l, reverse):
    if reverse:
        a_c, b_c, a_l, b_l = [jnp.flip(t, axis=1) for t in (a_c, b_c, a_l, b_l)]
    h_c = linear_scan(a_c, b_c)
    h_l = linear_scan(a_l, b_l, h_c[:, -1])
    if reverse:
        h_c, h_l = jnp.flip(h_c, axis=1), jnp.flip(h_l, axis=1)
    return h_c, h_l


def centred_dwconv(u, w, b):
    pad_l = CONV_W // 2
    y = lax.conv_general_dilated(u, w[:, None, :], window_strides=(1,), padding=[(pad_l, CONV_W - 1 - pad_l)],
                                 dimension_numbers=('NWC', 'WIO', 'NWC'), feature_group_count=u.shape[-1])
    return y + b


def lru_coeffs(u, wa, ba, wx, bx, lam):
    bsz, t_len, _ = u.shape
    uf = u.astype(jnp.float32)
    ub = uf.reshape(bsz, t_len, LRU_BLOCKS, LRU_BW)
    r = jax.nn.sigmoid(jnp.einsum('btki,kij->btkj', ub, wa.astype(jnp.float32)).reshape(bsz, t_len, LRU_WIDTH)
                       + ba.astype(jnp.float32))
    i = jax.nn.sigmoid(jnp.einsum('btki,kij->btkj', ub, wx.astype(jnp.float32)).reshape(bsz, t_len, LRU_WIDTH)
                       + bx.astype(jnp.float32))
    log_a = -LRU_C * r * jax.nn.softplus(-lam.astype(jnp.float32))
    a = jnp.exp(log_a)
    b = jnp.sqrt(-jnp.expm1(2.0 * log_a)) * (i * uf)
    return a, b


def rglru_mixer(n_c, n_l, w_in, conv_w, conv_b, wa, ba, wx, bx, lam, w_out, ctx_out):
    if ctx_out:
        u_c, g_c = jnp.split(n_c @ w_in, 2, axis=-1)
    else:
        u_c = n_c @ w_in[:, :LRU_WIDTH]
    u_l, g_l = jnp.split(n_l @ w_in, 2, axis=-1)
    u_c = centred_dwconv(u_c, conv_w, conv_b)
    u_l = centred_dwconv(u_l, conv_w, conv_b)
    hc_dirs, hl_dirs = [], []
    for d in range(2):
        a_c, b_c = lru_coeffs(u_c, wa[d], ba[d], wx[d], bx[d], lam[d])
        a_l, b_l = lru_coeffs(u_l, wa[d], ba[d], wx[d], bx[d], lam[d])
        h_c, h_l = scan_with_prefix(a_c, b_c, a_l, b_l, reverse=(d == 1))
        hc_dirs.append(h_c)
        hl_dirs.append(h_l)
    h_l = hl_dirs[0] + hl_dirs[1]
    y_l = (h_l * jax.nn.silu(g_l.astype(jnp.float32))).astype(n_l.dtype) @ w_out
    y_c = None
    if ctx_out:
        h_c = hc_dirs[0] + hc_dirs[1]
        y_c = (h_c * jax.nn.silu(g_c.astype(jnp.float32))).astype(n_c.dtype) @ w_out
    return y_c, y_l


def swa_mixer(n_c, n_l, w_in, sink, w_out, ctx_out):
    bsz, t_len, _ = n_l.shape
    c_len = n_c.shape[1]
    qd = SWA_HEADS * HEAD_DIM
    kvd = SWA_KV_HEADS * HEAD_DIM
    grp = SWA_HEADS // SWA_KV_HEADS
    scale = HEAD_DIM ** -0.5
    sink_kg = sink.reshape(SWA_KV_HEADS, grp)[:, :, None, None]
    q, k, v, g = jnp.split(n_l @ w_in, [qd, qd + kvd, qd + 2 * kvd], axis=-1)
    q = (q * scale).reshape(bsz, t_len, SWA_KV_HEADS, grp, HEAD_DIM)
    k = k.reshape(bsz, t_len, SWA_KV_HEADS, HEAD_DIM)
    v = v.reshape(bsz, t_len, SWA_KV_HEADS, HEAD_DIM)
    if ctx_out:
        q_c, k_c, v_c, g_c = jnp.split(n_c @ w_in, [qd, qd + kvd, qd + 2 * kvd], axis=-1)
    else:
        k_c, v_c = jnp.split(n_c @ w_in[:, qd:qd + 2 * kvd], 2, axis=-1)
    k_c = k_c.reshape(bsz, c_len, SWA_KV_HEADS, HEAD_DIM)
    v_c = v_c.reshape(bsz, c_len, SWA_KV_HEADS, HEAD_DIM)
    cos, sin = axial_rope(t_len)
    q_rot = apply_rope(q, cos, sin)
    k_rot = apply_rope(k, cos, sin)
    nb = t_len // BLOCK_Q
    qb_rot = q_rot.reshape(bsz, nb, BLOCK_Q, SWA_KV_HEADS, grp, HEAD_DIM)
    qb = q.reshape(bsz, nb, BLOCK_Q, SWA_KV_HEADS, grp, HEAD_DIM)
    pad = ((0, 0), (BLOCK_Q, BLOCK_Q), (0, 0), (0, 0))
    kp = jnp.pad(k_rot, pad).reshape(bsz, nb + 2, BLOCK_Q, SWA_KV_HEADS, HEAD_DIM)
    vp = jnp.pad(v, pad).reshape(bsz, nb + 2, BLOCK_Q, SWA_KV_HEADS, HEAD_DIM)
    kb = jnp.concatenate([kp[:, :-2], kp[:, 1:-1], kp[:, 2:]], axis=2)
    vb = jnp.concatenate([vp[:, :-2], vp[:, 1:-1], vp[:, 2:]], axis=2)
    s_band = jnp.einsum('bnqkgd,bnskd->bnkgqs', qb_rot, kb).astype(jnp.float32)
    blk = jnp.arange(nb)[:, None]
    qpos = blk * BLOCK_Q + jnp.arange(BLOCK_Q)[None, :]
    kpos = (blk - 1) * BLOCK_Q + jnp.arange(3 * BLOCK_Q)[None, :]
    valid = ((jnp.abs(qpos[:, :, None] - kpos[:, None, :]) <= WINDOW)
             & (kpos[:, None, :] >= 0) & (kpos[:, None, :] < t_len))
    s_band = jnp.where(valid[None, :, None, None], s_band, NEG_INF)
    s_ctx = jnp.einsum('bnqkgd,bckd->bnkgqc', qb, k_c).astype(jnp.float32)
    p = sink_softmax(jnp.concatenate([s_ctx, s_band], axis=-1), sink_kg).astype(v.dtype)
    o = (jnp.einsum('bnkgqc,bckd->bnqkgd', p[..., :c_len], v_c)
         + jnp.einsum('bnkgqs,bnskd->bnqkgd', p[..., c_len:], vb)).reshape(bsz, t_len, qd)
    y_l = (o * jax.nn.silu(g)) @ w_out
    y_c = None
    if ctx_out:
        q_c = (q_c * scale).reshape(bsz, c_len, SWA_KV_HEADS, grp, HEAD_DIM)
        pc = sink_softmax(jnp.einsum('bckgd,bskd->bkgcs', q_c, k_c), sink_kg).astype(v_c.dtype)
        o_c = jnp.einsum('bkgcs,bskd->bckgd', pc, v_c).reshape(bsz, c_len, qd)
        y_c = (o_c * jax.nn.silu(g_c)) @ w_out
    return y_c, y_l


def na_mixer(n_c, n_l, w_in, rpb, w_out, ctx_out):
    bsz, t_len, _ = n_l.shape
    c_len = n_c.shape[1]
    rows = t_len // GRID_W
    kr = min(NA_ROWS, rows)
    wd = NA_HEADS * HEAD_DIM
    scale = HEAD_DIM ** -0.5
    q, k, v, g = jnp.split(n_l @ w_in, 4, axis=-1)
    if ctx_out:
        q_c, k_c, v_c, g_c = jnp.split(n_c @ w_in, 4, axis=-1)
    else:
        k_c, v_c = jnp.split(n_c @ w_in[:, wd:3 * wd], 2, axis=-1)
    k_c = k_c.reshape(bsz, c_len, NA_HEADS, HEAD_DIM)
    v_c = v_c.reshape(bsz, c_len, NA_HEADS, HEAD_DIM)
    qg = (q * scale).reshape(bsz, rows, GRID_W, NA_HEADS, HEAD_DIM)
    kg = k.reshape(bsz, rows, GRID_W, NA_HEADS, HEAD_DIM)
    vg = v.reshape(bsz, rows, GRID_W, NA_HEADS, HEAD_DIM)
    r = jnp.arange(rows)
    ridx = jnp.clip(r - kr // 2, 0, rows - kr)[:, None] + jnp.arange(kr)[None, :]
    kb = kg[:, ridx].reshape(bsz, rows, kr * GRID_W, NA_HEADS, HEAD_DIM)
    vb = vg[:, ridx].reshape(bsz, rows, kr * GRID_W, NA_HEADS, HEAD_DIM)
    s_nb = jnp.einsum('brqhd,brkhd->brhqk', qg, kb).astype(jnp.float32)
    cq = jnp.arange(GRID_W)
    cstart = jnp.clip(cq - NA_COLS // 2, 0, GRID_W - NA_COLS)
    col_ok = (cq[None, :] >= cstart[:, None]) & (cq[None, :] < cstart[:, None] + NA_COLS)
    dy = ridx - r[:, None]
    dxi = jnp.clip(cq[None, :] - cq[:, None] + NA_COLS - 1, 0, 2 * NA_COLS - 2)
    bias = rpb[:, dy[:, None, :, None] + NA_ROWS - 1, dxi[None, :, None, :]]
    bias = jnp.moveaxis(bias, 0, 1).reshape(rows, NA_HEADS, GRID_W, kr * GRID_W).astype(jnp.float32)
    mask = jnp.broadcast_to(col_ok[:, None, :], (GRID_W, kr, GRID_W)).reshape(GRID_W, kr * GRID_W)
    s_nb = jnp.where(mask, s_nb + bias, NEG_INF)
    s_ctx = jnp.einsum('brqhd,bchd->brhqc', qg, k_c).astype(jnp.float32)
    p = jax.nn.softmax(jnp.concatenate([s_ctx, s_nb], axis=-1), axis=-1).astype(v.dtype)
    o = (jnp.einsum('brhqc,bchd->brqhd', p[..., :c_len], v_c)
         + jnp.einsum('brhqk,brkhd->brqhd', p[..., c_len:], vb)).reshape(bsz, t_len, wd)
    y_l = (o * jax.nn.silu(g)) @ w_out
    y_c = None
    if ctx_out:
        q_c = (q_c * scale).reshape(bsz, c_len, NA_HEADS, HEAD_DIM)
        pc = jax.nn.softmax(jnp.einsum('bchd,bshd->bhcs', q_c, k_c).astype(jnp.float32), axis=-1).astype(v_c.dtype)
        o_c = jnp.einsum('bhcs,bshd->bchd', pc, v_c).reshape(bsz, c_len, wd)
        y_c = (o_c * jax.nn.silu(g_c)) @ w_out
    return y_c, y_l


def s5_readout(y, g, glu_w, glu_b, w_out, dtype):
    y = jax.nn.gelu(y)
    y = y * jax.nn.sigmoid(y @ glu_w.astype(jnp.float32) + glu_b.astype(jnp.float32))
    return (y * jax.nn.silu(g.astype(jnp.float32))).astype(dtype) @ w_out


def s5_mixer(n_c, n_l, w_in, a_re, a_im, log_dt, b_re, b_im, c_re, c_im, d_skip, glu_w, glu_b, w_out, ctx_out):
    bsz, t_len, _ = n_l.shape
    c_len = n_c.shape[1]
    f32 = jnp.float32
    u_l, g_l = jnp.split(n_l @ w_in, 2, axis=-1)
    if ctx_out:
        u_c, g_c = jnp.split(n_c @ w_in, 2, axis=-1)
    else:
        u_c = n_c @ w_in[:, :S5_WIDTH]

    def groups(u):
        return u.astype(f32).reshape(u.shape[0], u.shape[1], S5_GROUPS, S5_GROUP).astype(jnp.complex64)

    uc_g, ul_g = groups(u_c), groups(u_l)
    y_l = d_skip.astype(f32) * u_l.astype(f32)
    y_c = d_skip.astype(f32) * u_c.astype(f32)
    for d in range(2):
        lam = lax.complex(a_re[d].astype(f32), a_im[d].astype(f32))
        dt = jnp.exp(log_dt[d].astype(f32))[:, None]
        lam_bar = jnp.exp(lam * dt)
        b_bar = ((lam_bar - 1.0) / lam)[..., None] * lax.complex(b_re[d].astype(f32), b_im[d].astype(f32))
        c_mat = lax.complex(c_re[d].astype(f32), c_im[d].astype(f32))
        bu_c = jnp.einsum('btgi,gpi->btgp', uc_g, b_bar)
        bu_l = jnp.einsum('btgi,gpi->btgp', ul_g, b_bar)
        a_c = jnp.broadcast_to(lam_bar, (1, c_len, S5_GROUPS, S5_STATE))
        a_l = jnp.broadcast_to(lam_bar, (1, t_len, S5_GROUPS, S5_STATE))
        h_c, h_l = scan_with_prefix(a_c, bu_c, a_l, bu_l, reverse=(d == 1))
        y_l = y_l + jnp.einsum('btgp,gip->btgi', h_l, c_mat).real.reshape(bsz, t_len, S5_WIDTH)
        if ctx_out:
            y_c = y_c + jnp.einsum('btgp,gip->btgi', h_c, c_mat).real.reshape(bsz, c_len, S5_WIDTH)
    out_l = s5_readout(y_l, g_l, glu_w, glu_b, w_out, n_l.dtype)
    out_c = s5_readout(y_c, g_c, glu_w, glu_b, w_out, n_c.dtype) if ctx_out else None
    return out_c, out_l


def setup_inputs(seed: int = 0) -> dict:
    key = jax.random.key(seed)
    keys = iter(jax.random.split(key, 64))
    f32 = jnp.float32
    D = D_MODEL

    def nrm(shape, scale):
        return scale * jax.random.normal(next(keys), shape, f32)

    def gain(n):
        return 1.0 + nrm((n,), 0.05)

    qd = SWA_HEADS * HEAD_DIM
    kvd = SWA_KV_HEADS * HEAD_DIM
    wd = NA_HEADS * HEAD_DIM
    inp = {}
    inp['x'] = nrm((BATCH, SEQ, D), 1.0)
    inp['c'] = nrm((BATCH, D), 1.0)
    inp['ctx'] = nrm((BATCH, CTX_LEN, D), 1.0)
    inp['c_ctx'] = nrm((D,), 1.0)
    inp['ada_w0'] = nrm((D, 3 * D), 0.5 * D ** -0.5)
    inp['ada_b0'] = nrm((3 * D,), 0.02)
    inp['norm0'] = gain(D)
    inp['w_in0'] = nrm((D, 2 * LRU_WIDTH), D ** -0.5)
    inp['conv_w0'] = nrm((CONV_W, LRU_WIDTH), CONV_W ** -0.5)
    inp['conv_b0'] = nrm((LRU_WIDTH,), 0.02)
    inp['lru_wa0'] = nrm((2, LRU_BLOCKS, LRU_BW, LRU_BW), LRU_BW ** -0.5)
    inp['lru_ba0'] = nrm((2, LRU_WIDTH), 0.02)
    inp['lru_wx0'] = nrm((2, LRU_BLOCKS, LRU_BW, LRU_BW), LRU_BW ** -0.5)
    inp['lru_bx0'] = nrm((2, LRU_WIDTH), 0.02)
    a0 = jax.random.uniform(next(keys), (2, LRU_WIDTH), f32, 0.9 ** (1.0 / LRU_C), 0.999 ** (1.0 / LRU_C))
    inp['lru_lam0'] = jnp.log(a0) - jnp.log1p(-a0)
    inp['w_out0'] = nrm((LRU_WIDTH, D), LRU_WIDTH ** -0.5)
    inp['ada_w1'] = nrm((D, 3 * D), 0.5 * D ** -0.5)
    inp['ada_b1'] = nrm((3 * D,), 0.02)
    inp['norm1'] = gain(D)
    inp['w_in1'] = nrm((D, 2 * qd + 2 * kvd), D ** -0.5)
    inp['sink1'] = nrm((SWA_HEADS,), 1.0)
    inp['w_out1'] = nrm((qd, D), qd ** -0.5)
    inp['ada_w2'] = nrm((D, 3 * D), 0.5 * D ** -0.5)
    inp['ada_b2'] = nrm((3 * D,), 0.02)
    inp['norm2'] = gain(D)
    inp['w_in2'] = nrm((D, 4 * wd), D ** -0.5)
    inp['rpb2'] = nrm((NA_HEADS, 2 * NA_ROWS - 1, 2 * NA_COLS - 1), 0.1)
    inp['w_out2'] = nrm((wd, D), wd ** -0.5)
    inp['ada_w3'] = nrm((D, 3 * D), 0.5 * D ** -0.5)
    inp['ada_b3'] = nrm((3 * D,), 0.02)
    inp['norm3'] = gain(D)
    inp['w_in3'] = nrm((D, 2 * S5_WIDTH), D ** -0.5)
    inp['s5_a_re3'] = -0.5 + nrm((2, S5_GROUPS, S5_STATE), 0.01)
    inp['s5_a_im3'] = math.pi * jnp.arange(S5_STATE, dtype=f32) + nrm((2, S5_GROUPS, S5_STATE), 0.01)
    inp['s5_log_dt3'] = jax.random.uniform(next(keys), (2, S5_GROUPS), f32, math.log(1e-3), math.log(1e-1))
    inp['s5_b_re3'] = nrm((2, S5_GROUPS, S5_STATE, S5_GROUP), (2 * S5_GROUP) ** -0.5)
    inp['s5_b_im3'] = nrm((2, S5_GROUPS, S5_STATE, S5_GROUP), (2 * S5_GROUP) ** -0.5)
    inp['s5_c_re3'] = nrm((2, S5_GROUPS, S5_GROUP, S5_STATE), (2 * S5_STATE) ** -0.5)
    inp['s5_c_im3'] = nrm((2, S5_GROUPS, S5_GROUP, S5_STATE), (2 * S5_STATE) ** -0.5)
    inp['s5_d3'] = nrm((S5_WIDTH,), 1.0)
    inp['glu_w3'] = nrm((S5_WIDTH, S5_WIDTH), S5_WIDTH ** -0.5)
    inp['glu_b3'] = nrm((S5_WIDTH,), 0.02)
    inp['w_out3'] = nrm((S5_WIDTH, D), S5_WIDTH ** -0.5)
    inp['norm_f'] = gain(D)
    return inp


def reference(x, c, ctx, c_ctx,
              ada_w0, ada_b0, norm0, w_in0, conv_w0, conv_b0, lru_wa0, lru_ba0, lru_wx0, lru_bx0, lru_lam0, w_out0,
              ada_w1, ada_b1, norm1, w_in1, sink1, w_out1,
              ada_w2, ada_b2, norm2, w_in2, rpb2, w_out2,
              ada_w3, ada_b3, norm3, w_in3, s5_a_re3, s5_a_im3, s5_log_dt3, s5_b_re3, s5_b_im3, s5_c_re3, s5_c_im3,
              s5_d3, glu_w3, glu_b3, w_out3,
              norm_f):
    mixers = (rglru_mixer, swa_mixer, na_mixer, s5_mixer)
    layers = (
        ((ada_w0, ada_b0, norm0), (w_in0, conv_w0, conv_b0, lru_wa0, lru_ba0, lru_wx0, lru_bx0, lru_lam0, w_out0)),
        ((ada_w1, ada_b1, norm1), (w_in1, sink1, w_out1)),
        ((ada_w2, ada_b2, norm2), (w_in2, rpb2, w_out2)),
        ((ada_w3, ada_b3, norm3), (w_in3, s5_a_re3, s5_a_im3, s5_log_dt3, s5_b_re3, s5_b_im3, s5_c_re3, s5_c_im3,
                                   s5_d3, glu_w3, glu_b3, w_out3)),
    )
    h_lat, h_ctx = x, ctx
    for i in range(DEPTH):
        (ada_w, ada_b, g_norm), margs = layers[i]
        ctx_out = i < DEPTH - 1
        sh_l, sc_l, gt_l = modulation(c, ada_w, ada_b)
        sh_c, sc_c, gt_c = modulation(c_ctx[None, :], ada_w, ada_b)
        n_l = rmsnorm(h_lat, g_norm) * (1.0 + sc_l) + sh_l
        n_c = rmsnorm(h_ctx, g_norm) * (1.0 + sc_c) + sh_c
        y_c, y_l = mixers[i % N_MIXERS](n_c, n_l, *margs, ctx_out=ctx_out)
        h_lat = h_lat + gt_l * y_l
        if ctx_out:
            h_ctx = h_ctx + gt_c * y_c
    return rmsnorm(h_lat, norm_f)
```

```python
import functools
import math

import jax
import jax.numpy as jnp
from jax import lax
from jax.experimental import pallas as pl
from jax.experimental.pallas import tpu as pltpu

F32 = jnp.float32
BF16 = jnp.bfloat16

D = 1024
HEAD_DIM = 64
GRID_W = 64
EPS = 1e-6
NEG_INF = -1e30
ROPE_BASE = 10000.0
LANES = 128

LRU_WIDTH = 1408
LRU_BLOCKS = 16
LRU_BW = LRU_WIDTH // LRU_BLOCKS
LRU_PW = LRU_BLOCKS * LANES
CONV_W = 4
LRU_C = 8.0
LRU_TT = 32

SWA_HEADS = 16
SWA_KV_HEADS = 4
WINDOW = 128
BLOCK_Q = 128

NA_HEADS = 16
NA_ROWS = 8
NA_COLS = 16
NA_QROWS = 4
NA_KROWS = 12

S5_WIDTH = 1024
S5_GROUP = 16
S5_GROUPS = S5_WIDTH // S5_GROUP
S5_STATE = 64
S5_CHUNK = 16

ROW_TILE = 256
VMEM_LIMIT = 56 * 1024 * 1024


def _cparams(sem):
    return pltpu.CompilerParams(dimension_semantics=sem, vmem_limit_bytes=VMEM_LIMIT)


def _silu(v):
    return v * jax.nn.sigmoid(v)


def _bmajor(nb, rows, n_ctx, tm):
    grid = (nb, rows // tm)
    nct = n_ctx // tm

    def row(width, off=0):
        return pl.BlockSpec((1, tm, width), lambda b, i: (b, i + off, 0))

    mod = pl.BlockSpec((1, 1, 3 * D), lambda b, i: (jnp.where(i < nct, nb, b), 0, 0))
    return grid, row, mod


def _tmajor(nb, rows, n_ctx, tt):
    grid = (1, rows // tt)
    nct = n_ctx // tt

    def row(width, off=0):
        return pl.BlockSpec((tt, nb, width), lambda b, i: (i + off, 0, 0))

    mod = pl.BlockSpec((1, nb, 3 * D), lambda b, i: (jnp.where(i < nct, 0, 1), 0, 0))
    return grid, row, mod


def _mod_body(c_ref, w_ref, b_ref, o_ref):
    s = _silu(c_ref[...]).astype(BF16)
    o_ref[...] = jnp.dot(s, w_ref[...].astype(BF16), preferred_element_type=F32) + b_ref[...]


def _modulation(cvec, ada_w, ada_b):
    n = ada_w.shape[1]
    tn = 1024
    return pl.pallas_call(
        _mod_body,
        grid=(n // tn,),
        in_specs=[pl.BlockSpec((16, D), lambda j: (0, 0)),
                  pl.BlockSpec((D, tn), lambda j: (0, j)),
                  pl.BlockSpec((1, tn), lambda j: (0, j))],
        out_specs=pl.BlockSpec((16, tn), lambda j: (0, j)),
        out_shape=jax.ShapeDtypeStruct((16, n), F32),
        compiler_params=_cparams(("parallel",)),
        name="modulation",
    )(cvec, ada_w, ada_b.reshape(1, n))


def _norm_mod(x3, mod3, gn):
    ms = jnp.mean(x3 * x3, axis=-1, keepdims=True)
    y = x3 * lax.rsqrt(ms + EPS) * gn
    return y * (1.0 + mod3[:, :, D:2 * D]) + mod3[:, :, 0:D]


def _in_proj_body(x_ref, mod_ref, gn_ref, w_ref, *rest, epilogue, n_extra, chunk):
    extra, outs = rest[:n_extra], rest[n_extra:]
    x3 = x_ref[...]
    n3 = _norm_mod(x3, mod_ref[...], gn_ref[...])
    lead = x3.shape[:2]
    n = n3.reshape(lead[0] * lead[1], D).astype(BF16)
    for j in range(w_ref.shape[1] // chunk):
        acc = jnp.dot(n, w_ref[:, j * chunk:(j + 1) * chunk], preferred_element_type=F32)
        epilogue(j, acc, lead, extra, outs)


def _in_proj(h, mod, gn, w, layout, outs, epilogue, extras=(), extra_specs=(), chunk=512, name="in_proj"):
    grid, row, mod_spec = layout
    n = w.shape[1]
    body = functools.partial(_in_proj_body, epilogue=epilogue, n_extra=len(extras), chunk=chunk)
    return pl.pallas_call(
        body,
        grid=grid,
        in_specs=[row(D), mod_spec,
                  pl.BlockSpec((1, D), lambda b, i: (0, 0)),
                  pl.BlockSpec((D, n), lambda b, i: (0, 0)),
                  *extra_specs],
        out_specs=[row(wd) for wd, _ in outs],
        out_shape=[jax.ShapeDtypeStruct(h.shape[:2] + (wd,), dt) for wd, dt in outs],
        compiler_params=_cparams(("parallel", "parallel")),
        name=name,
    )(h, mod, gn.reshape(1, D), w, *extras)


def _put(ref, lo, val, lead):
    ref[:, :, lo:lo + val.shape[1]] = val.reshape(lead + (val.shape[1],)).astype(ref.dtype)


def _out_proj_body(z_ref, w_ref, h_ref, mod_ref, o_ref):
    z3 = z_ref[...]
    a, b, wd = z3.shape
    y = jnp.dot(z3.reshape(a * b, wd), w_ref[...], preferred_element_type=F32)
    o_ref[...] = h_ref[...] + mod_ref[:, :, 2 * D:3 * D] * y.reshape(a, b, D)


def _out_proj(z, w, h, mod, layout, name):
    grid, row, mod_spec = layout
    wd = w.shape[0]
    return pl.pallas_call(
        _out_proj_body,
        grid=grid,
        in_specs=[row(wd), pl.BlockSpec((wd, D), lambda b, i: (0, 0)), row(D), mod_spec],
        out_specs=row(D),
        out_shape=jax.ShapeDtypeStruct(h.shape, F32),
        input_output_aliases={2: 0},
        compiler_params=_cparams(("parallel", "parallel")),
        name=name,
    )(z, w, h, mod)


def _lru_epilogue(j, acc, lead, extra, outs):
    u_ref, g_ref = outs
    lo = j * acc.shape[1]
    if lo < LRU_PW:
        _put(u_ref, lo, acc, lead)
    else:
        _put(g_ref, lo - LRU_PW, acc, lead)


def _lru_tile_index(s, nct, ntiles, reverse):
    if not reverse:
        return s
    return jnp.where(s < nct, nct - 1 - s, ntiles + nct - 1 - s)


def _lru_sweep_body(up_ref, uc_ref, un_ref, cw_ref, cb_ref, wg_ref, bg_ref, lam_ref, *rest,
                    nct, ntiles, reverse):
    if reverse:
        hf_ref, g_ref, o_ref, ext_ref, a_ref, b_ref, carry_ref = rest
    else:
        o_ref, ext_ref, a_ref, b_ref, carry_ref = rest
    s = pl.program_id(0)
    ti = _lru_tile_index(s, nct, ntiles, reverse)
    tt, nb, _ = uc_ref.shape

    @pl.when(s == 0)
    def _():
        carry_ref[...] = jnp.zeros_like(carry_ref)

    seg_start = (ti == 0) | (ti == nct)
    seg_end = (ti == nct - 1) | (ti == ntiles - 1)
    ext_ref[0:2] = jnp.where(seg_start, 0.0, up_ref[...])
    ext_ref[2:2 + tt] = uc_ref[...]
    ext_ref[2 + tt:3 + tt] = jnp.where(seg_end, 0.0, un_ref[...])

    softplus_neg_lam = jax.nn.softplus(-lam_ref[...])
    for k in range(LRU_BLOCKS):
        sl = slice(k * LANES, (k + 1) * LANES)
        u = cb_ref[:, sl]
        for tap in range(CONV_W):
            u = u + cw_ref[tap:tap + 1, sl] * ext_ref[tap:tap + tt, :, sl]
        u = u.reshape(tt * nb, LANES)
        pre = jnp.dot(u.astype(BF16), wg_ref[k], preferred_element_type=F32) + bg_ref[k]
        r = jax.nn.sigmoid(pre[:, :LANES])
        i = jax.nn.sigmoid(pre[:, LANES:])
        a = jnp.exp(-LRU_C * r * softplus_neg_lam[:, sl])
        a_ref[:, sl] = a
        b_ref[:, sl] = jnp.sqrt(1.0 - a * a) * (i * u)

    def step(q, h):
        t = tt - 1 - q if reverse else q
        rows = pl.ds(pl.multiple_of(t * nb, nb), nb)
        h = a_ref[rows, :] * h + b_ref[rows, :]
        if reverse:
            o_ref[t] = ((hf_ref[t] + h) * _silu(g_ref[t].astype(F32))).astype(o_ref.dtype)
        else:
            o_ref[t] = h
        return h

    carry_ref[...] = lax.fori_loop(0, tt, step, carry_ref[...], unroll=4)


def _lru_sweep(u, cw, cb, wg, bg, lam, nct_rows, reverse, hf=None, g=None):
    rows, nb, pw = u.shape
    tt = LRU_TT
    ntiles = rows // tt
    nct = nct_rows // tt
    tile = functools.partial(_lru_tile_index, nct=nct, ntiles=ntiles, reverse=reverse)
    cur = lambda wd: pl.BlockSpec((tt, nb, wd), lambda s: (tile(s), 0, 0))
    const = lambda shape: pl.BlockSpec(shape, lambda s: (0,) * len(shape))
    in_specs = [
        pl.BlockSpec((2, nb, pw), lambda s: (jnp.maximum(tile(s) * (tt // 2) - 1, 0), 0, 0)),
        cur(pw),
        pl.BlockSpec((1, nb, pw), lambda s: (jnp.minimum((tile(s) + 1) * tt, rows - 1), 0, 0)),
        const((CONV_W, pw)), const((1, pw)),
        const((LRU_BLOCKS, LANES, 2 * LANES)), const((LRU_BLOCKS, 1, 2 * LANES)), const((1, pw)),
    ]
    args = [u, u, u, cw, cb, wg, bg, lam]
    if reverse:
        in_specs += [cur(pw), cur(pw)]
        args += [hf, g]
    body = functools.partial(_lru_sweep_body, nct=nct, ntiles=ntiles, reverse=reverse)
    return pl.pallas_call(
        body,
        grid=(ntiles,),
        in_specs=in_specs,
        out_specs=cur(pw),
        out_shape=jax.ShapeDtypeStruct(u.shape, BF16 if reverse else F32),
        scratch_shapes=[pltpu.VMEM((tt + 3, nb, pw), F32),
                        pltpu.VMEM((tt * nb, pw), F32),
                        pltpu.VMEM((tt * nb, pw), F32),
                        pltpu.VMEM((nb, pw), F32)],
        compiler_params=_cparams(("arbitrary",)),
        name="lru_bwd" if reverse else "lru_fwd",
    )(*args)


def _pad_blocks(a, axis):
    axis = axis % a.ndim
    shp = a.shape
    a = a.reshape(shp[:axis] + (LRU_BLOCKS, LRU_BW) + shp[axis + 1:])
    pad = [(0, 0)] * a.ndim
    pad[axis + 1] = (0, LANES - LRU_BW)
    a = jnp.pad(a, pad)
    return a.reshape(shp[:axis] + (LRU_PW,) + shp[axis + 1:])


def _rglru_layer(h_t, mod_t, gn, w_in, conv_w, conv_b, wa, ba, wx, bx, lam, w_out, n_ctx):
    rows, nb, _ = h_t.shape
    layout = _tmajor(nb, rows, n_ctx, LRU_TT)
    w_pad = jnp.concatenate([_pad_blocks(w_in[:, :LRU_WIDTH], 1), _pad_blocks(w_in[:, LRU_WIDTH:], 1)],
                            axis=1).astype(BF16)
    u, g = _in_proj(h_t, mod_t, gn, w_pad, layout, [(LRU_PW, F32), (LRU_PW, BF16)], _lru_epilogue,
                    name="lru_in_proj")
    cw = _pad_blocks(conv_w, 1)
    cb = _pad_blocks(conv_b.reshape(1, LRU_WIDTH), 1)
    pad_w = ((0, 0), (0, 0), (0, LANES - LRU_BW), (0, LANES - LRU_BW))
    wg = jnp.concatenate([jnp.pad(wa, pad_w), jnp.pad(wx, pad_w)], axis=-1).astype(BF16)
    pad_b = ((0, 0), (0, 0), (0, LANES - LRU_BW))
    bg = jnp.concatenate([jnp.pad(ba.reshape(2, LRU_BLOCKS, LRU_BW), pad_b),
                          jnp.pad(bx.reshape(2, LRU_BLOCKS, LRU_BW), pad_b)], axis=-1)
    bg = bg.reshape(2, LRU_BLOCKS, 1, 2 * LANES)
    lam_p = _pad_blocks(lam, 1)
    hf = _lru_sweep(u, cw, cb, wg[0], bg[0], lam_p[0:1], n_ctx, reverse=False)
    z = _lru_sweep(u, cw, cb, wg[1], bg[1], lam_p[1:2], n_ctx, reverse=True, hf=hf, g=g)
    return _out_proj(z, _pad_blocks(w_out, 0).astype(BF16), h_t, mod_t, layout, name="lru_out_proj")


def _rope_tables(n_ctx, t_len):
    pos = jnp.arange(t_len)
    row = (pos // GRID_W).astype(F32)
    col = (pos % GRID_W).astype(F32)
    n_ax = HEAD_DIM // 4
    freqs = ROPE_BASE ** (-jnp.arange(n_ax, dtype=F32) / n_ax)
    ang = jnp.concatenate([row[:, None] * freqs, col[:, None] * freqs], axis=-1)
    cos, sin = jnp.cos(ang), jnp.sin(ang)
    cos_h = jnp.concatenate([cos, cos], axis=-1)
    sin_h = jnp.concatenate([-sin, sin], axis=-1)
    cos_f = jnp.concatenate([jnp.ones((n_ctx, HEAD_DIM), F32), cos_h], axis=0)
    sin_f = jnp.concatenate([jnp.zeros((n_ctx, HEAD_DIM), F32), sin_h], axis=0)
    return jnp.tile(cos_f, (1, 2)), jnp.tile(sin_f, (1, 2))


def _rope(x, cos, sin):
    lane = lax.broadcasted_iota(jnp.int32, x.shape, 1)
    swapped = jnp.where(lane % HEAD_DIM < HEAD_DIM // 2,
                        pltpu.roll(x, LANES - HEAD_DIM // 2, 1), pltpu.roll(x, HEAD_DIM // 2, 1))
    return x * cos + swapped * sin


def _swa_epilogue(j, acc, lead, extra, outs):
    cos_ref, sin_ref = extra
    q_ref, qr_ref, k_ref, v_ref, g_ref = outs
    qd = SWA_HEADS * HEAD_DIM
    kvd = SWA_KV_HEADS * HEAD_DIM
    lo = j * acc.shape[1]
    cos, sin = cos_ref[...], sin_ref[...]

    def roped(x):
        return jnp.concatenate([_rope(x[:, s:s + LANES], cos, sin) for s in range(0, x.shape[1], LANES)], axis=1)

    if lo < qd:
        q = acc * (HEAD_DIM ** -0.5)
        _put(q_ref, lo, q, lead)
        _put(qr_ref, lo, roped(q), lead)
    elif lo == qd:
        _put(k_ref, 0, roped(acc[:, :kvd]), lead)
        _put(v_ref, 0, acc[:, kvd:], lead)
    else:
        _put(g_ref, lo - qd - 2 * kvd, acc, lead)


def _swa_body(sink_ref, q_ref, qr_ref, kc_ref, k0_ref, k1_ref, k2_ref, vc_ref, v0_ref, v1_ref, v2_ref,
              g_ref, o_ref, *, nct, nt):
    pair = pl.program_id(1)
    n = pl.program_id(2)
    grp = SWA_HEADS // SWA_KV_HEADS
    n_c = kc_ref.shape[1]
    q, qr = q_ref[0], qr_ref[0]
    kall = jnp.concatenate([kc_ref[0], k0_ref[0], k1_ref[0], k2_ref[0]], axis=0)
    vall = jnp.concatenate([vc_ref[0], v0_ref[0], v1_ref[0], v2_ref[0]], axis=0)

    qi = lax.broadcasted_iota(jnp.int32, (grp * BLOCK_Q, 3 * BLOCK_Q), 0) % BLOCK_Q
    kj = lax.broadcasted_iota(jnp.int32, (grp * BLOCK_Q, 3 * BLOCK_Q), 1)
    key_tile = n - 1 + kj // BLOCK_Q
    valid = ((jnp.abs(kj - BLOCK_Q - qi) <= WINDOW) & (n >= nct) & (key_tile >= nct) & (key_tile < nt))
    dims = (((1,), (1,)), ((), ()))

    for kvl in range(2):
        ks = slice(kvl * HEAD_DIM, (kvl + 1) * HEAD_DIM)
        kh, vh = kall[:, ks], vall[:, ks]
        heads = [kvl * grp + gq for gq in range(grp)]
        qs = jnp.concatenate([q[:, hl * HEAD_DIM:(hl + 1) * HEAD_DIM] for hl in heads], axis=0)
        qrs = jnp.concatenate([qr[:, hl * HEAD_DIM:(hl + 1) * HEAD_DIM] for hl in heads], axis=0)
        s_ctx = lax.dot_general(qs, kh[:n_c], dims, preferred_element_type=F32)
        s_band = lax.dot_general(qrs, kh[n_c:], dims, preferred_element_type=F32)
        s_band = jnp.where(valid, s_band, NEG_INF)
        sink = jnp.concatenate(
            [jnp.full((BLOCK_Q, 1), sink_ref[pair * 2 * grp + hl], F32) for hl in heads], axis=0)
        m = jnp.maximum(jnp.maximum(jnp.max(s_ctx, axis=1, keepdims=True),
                                    jnp.max(s_band, axis=1, keepdims=True)), sink)
        p_c = jnp.exp(s_ctx - m)
        p_b = jnp.exp(s_band - m)
        denom = (jnp.sum(p_c, axis=1, keepdims=True) + jnp.sum(p_b, axis=1, keepdims=True)
                 + jnp.exp(sink - m))
        o = (jnp.dot(p_c.astype(BF16), vh[:n_c], preferred_element_type=F32)
             + jnp.dot(p_b.astype(BF16), vh[n_c:], preferred_element_type=F32)) / denom
        for gq, hl in enumerate(heads):
            hs = slice(hl * HEAD_DIM, (hl + 1) * HEAD_DIM)
            gate = g_ref[0, :, hs].astype(F32)
            o_ref[0, :, hs] = (o[gq * BLOCK_Q:(gq + 1) * BLOCK_Q] * _silu(gate)).astype(o_ref.dtype)


def _swa_attention(q, qr, k, v, g, sink, n_ctx):
    nb, rows, qd = q.shape
    nt = rows // BLOCK_Q
    nct = n_ctx // BLOCK_Q
    half = qd // 2
    qspec = pl.BlockSpec((1, BLOCK_Q, half), lambda b, p, n: (b, n, p))
    ctx = pl.BlockSpec((1, n_ctx, LANES), lambda b, p, n: (b, 0, p))

    def band(off):
        return pl.BlockSpec((1, BLOCK_Q, LANES), lambda b, p, n: (b, jnp.clip(n + off, nct, nt - 1), p))

    body = functools.partial(_swa_body, nct=nct, nt=nt)
    return pl.pallas_call(
        body,
        grid=(nb, 2, nt),
        in_specs=[pl.BlockSpec(memory_space=pltpu.SMEM), qspec, qspec,
                  ctx, band(-1), band(0), band(1), ctx, band(-1), band(0), band(1), qspec],
        out_specs=qspec,
        out_shape=jax.ShapeDtypeStruct(q.shape, BF16),
        compiler_params=_cparams(("parallel", "parallel", "parallel")),
        name="swa_attention",
    )(sink, q, qr, k, k, k, k, v, v, v, v, g)


def _swa_layer(h, mod_b, gn, w_in, sink, w_out, n_ctx):
    nb, rows, _ = h.shape
    layout = _bmajor(nb, rows, n_ctx, ROW_TILE)
    qd = SWA_HEADS * HEAD_DIM
    kvd = SWA_KV_HEADS * HEAD_DIM
    cos, sin = _rope_tables(n_ctx, rows - n_ctx)
    tab = pl.BlockSpec((ROW_TILE, LANES), lambda b, i: (i, 0))
    q, qr, k, v, g = _in_proj(
        h, mod_b, gn, w_in.astype(BF16), layout,
        [(qd, BF16), (qd, BF16), (kvd, BF16), (kvd, BF16), (qd, BF16)], _swa_epilogue,
        extras=(cos, sin), extra_specs=(tab, tab), name="swa_in_proj")
    z = _swa_attention(q, qr, k, v, g, sink, n_ctx)
    return _out_proj(z, w_out.astype(BF16), h, mod_b, layout, name="swa_out_proj")


def _na_epilogue(j, acc, lead, extra, outs):
    q_ref, k_ref, v_ref, g_ref = outs
    wd = NA_HEADS * HEAD_DIM
    lo = j * acc.shape[1]
    which, off = lo // wd, lo % wd
    if which == 0:
        _put(q_ref, off, acc * (HEAD_DIM ** -0.5), lead)
    else:
        _put((k_ref, v_ref, g_ref)[which - 1], off, acc, lead)


def _na_bias_table(rpb, grid_rows):
    half = NA_ROWS // 2
    last_r0 = grid_rows - NA_QROWS
    pats = [(0, 0), (NA_QROWS, 0), (last_r0, grid_rows - NA_KROWS)]
    qi = jnp.arange(NA_QROWS * GRID_W)
    kj = jnp.arange(NA_KROWS * GRID_W)
    qcol, kcol = qi % GRID_W, kj % GRID_W
    cstart = jnp.clip(qcol - NA_COLS // 2, 0, GRID_W - NA_COLS)
    col_ok = (kcol[None, :] >= cstart[:, None]) & (kcol[None, :] < cstart[:, None] + NA_COLS)
    dx = jnp.clip(kcol[None, :] - qcol[:, None] + NA_COLS - 1, 0, 2 * NA_COLS - 2)
    tabs = []
    for r0, kb in pats:
        qrow = r0 + qi // GRID_W
        krow = kb + kj // GRID_W
        start = jnp.clip(qrow - half, 0, grid_rows - NA_ROWS)
        row_ok = (krow[None, :] >= start[:, None]) & (krow[None, :] < start[:, None] + NA_ROWS)
        dy = jnp.clip(krow[None, :] - qrow[:, None] + NA_ROWS - 1, 0, 2 * NA_ROWS - 2)
        tabs.append(jnp.where(row_ok & col_ok, rpb[:, dy, dx].astype(F32), NEG_INF))
    t = jnp.stack(tabs, axis=1)
    return t.reshape(NA_HEADS // 2, 2, 3, t.shape[2], t.shape[3]).transpose(0, 2, 1, 3, 4)


def _na_body(q_ref, k_ref, v_ref, g_ref, bias_ref, o_ref, *, n_ctx, grid_rows):
    rb = pl.program_id(2)
    is_latent = rb >= 1
    r0 = (rb - 1) * NA_QROWS
    kb = jnp.clip(r0 - NA_ROWS // 2, 0, grid_rows - NA_KROWS)
    start = pl.multiple_of(n_ctx + kb * GRID_W, GRID_W)
    nk = NA_KROWS * GRID_W
    q = q_ref[0]
    kc, vc = k_ref[0, 0:n_ctx, :], v_ref[0, 0:n_ctx, :]
    kn, vn = k_ref[0, pl.ds(start, nk), :], v_ref[0, pl.ds(start, nk), :]
    dims = (((1,), (1,)), ((), ()))
    for hl in range(2):
        hs = slice(hl * HEAD_DIM, (hl + 1) * HEAD_DIM)
        s_c = lax.dot_general(q[:, hs], kc[:, hs], dims, preferred_element_type=F32)
        s_n = lax.dot_general(q[:, hs], kn[:, hs], dims, preferred_element_type=F32) + bias_ref[0, 0, hl]
        s_n = jnp.where(is_latent, s_n, NEG_INF)
        m = jnp.maximum(jnp.max(s_c, axis=1, keepdims=True), jnp.max(s_n, axis=1, keepdims=True))
        p_c = jnp.exp(s_c - m)
        p_n = jnp.exp(s_n - m)
        denom = jnp.sum(p_c, axis=1, keepdims=True) + jnp.sum(p_n, axis=1, keepdims=True)
        o = (jnp.dot(p_c.astype(BF16), vc[:, hs], preferred_element_type=F32)
             + jnp.dot(p_n.astype(BF16), vn[:, hs], preferred_element_type=F32)) / denom
        gate = g_ref[0, :, hs].astype(F32)
        o_ref[0, :, hs] = (o * _silu(gate)).astype(o_ref.dtype)


def _na_attention(q, k, v, g, bias, n_ctx):
    nb, rows, wd = q.shape
    grid_rows = (rows - n_ctx) // GRID_W
    nq = NA_QROWS * GRID_W
    assert n_ctx == nq, "context tokens form exactly one query block"
    nblk = rows // nq
    last = nblk - 1
    qspec = pl.BlockSpec((1, nq, LANES), lambda b, p, r: (b, r, p))
    kvspec = pl.BlockSpec((1, rows, LANES), lambda b, p, r: (b, 0, p))
    bspec = pl.BlockSpec((1, 1, 2, nq, NA_KROWS * GRID_W),
                         lambda b, p, r: (p, jnp.where(r <= 1, 0, jnp.where(r == last, 2, 1)), 0, 0, 0))
    body = functools.partial(_na_body, n_ctx=n_ctx, grid_rows=grid_rows)
    return pl.pallas_call(
        body,
        grid=(nb, wd // LANES, nblk),
        in_specs=[qspec, kvspec, kvspec, qspec, bspec],
        out_specs=qspec,
        out_shape=jax.ShapeDtypeStruct(q.shape, BF16),
        compiler_params=_cparams(("parallel", "parallel", "parallel")),
        name="na_attention",
    )(q, k, v, g, bias)


def _na_layer(h, mod_b, gn, w_in, rpb, w_out, n_ctx):
    nb, rows, _ = h.shape
    layout = _bmajor(nb, rows, n_ctx, ROW_TILE)
    wd = NA_HEADS * HEAD_DIM
    q, k, v, g = _in_proj(h, mod_b, gn, w_in.astype(BF16), layout, [(wd, BF16)] * 4, _na_epilogue,
                          name="na_in_proj")
    bias = _na_bias_table(rpb, (rows - n_ctx) // GRID_W)
    z = _na_attention(q, k, v, g, bias, n_ctx)
    return _out_proj(z, w_out.astype(BF16), h, mod_b, layout, name="na_out_proj")


def _s5_epilogue(j, acc, lead, extra, outs):
    u_ref, g_ref = outs
    lo = j * acc.shape[1]
    if lo < S5_WIDTH:
        _put(u_ref, lo, acc, lead)
    else:
        _put(g_ref, lo - S5_WIDTH, acc, lead)


def _s5_matrices(a_re, a_im, log_dt, b_re, b_im, c_re, c_im, d_skip):
    L = S5_CHUNK
    lam = lax.complex(a_re.astype(F32), a_im.astype(F32))
    lam_dt = lam * jnp.exp(log_dt.astype(F32))[..., None]
    lam_bar = jnp.exp(lam_dt)
    b_bar = ((lam_bar - 1.0) / lam)[..., None] * lax.complex(b_re.astype(F32), b_im.astype(F32))
    c_mat = lax.complex(c_re.astype(F32), c_im.astype(F32))
    kk = jnp.arange(L + 1, dtype=F32)
    pw = jnp.exp(kk[:, None, None, None] * lam_dt[None])
    kern = jnp.real(jnp.einsum('dgip,kdgp,dgpj->dkgij', c_mat, pw[:L], b_bar))
    tau = jnp.arange(L)
    lag = tau[None, :] - tau[:, None]
    fwd = jnp.where((lag >= 0)[:, :, None, None, None], kern[0][jnp.clip(lag, 0, L - 1)], 0.0)
    bwd = jnp.where((lag <= 0)[:, :, None, None, None], kern[1][jnp.clip(-lag, 0, L - 1)], 0.0)
    both = fwd + bwd
    eye_t = jnp.eye(L, dtype=F32)[:, :, None, None, None]
    skip = jnp.eye(S5_GROUP, dtype=F32)[None] * d_skip.astype(F32).reshape(S5_GROUPS, S5_GROUP, 1)
    both = both + eye_t * skip[None, None]
    toep = both.transpose(2, 0, 4, 1, 3).reshape(S5_GROUPS, L * S5_GROUP, L * S5_GROUP)

    def lanes(z):
        pad = [(0, 0)] * (z.ndim - 1) + [(0, LANES - S5_STATE)]
        return jnp.concatenate([jnp.pad(jnp.real(z), pad), jnp.pad(jnp.imag(z), pad)], axis=-1)

    pf = pw[L - 1 - tau, 0][:, :, :, None] * b_bar[0][None]
    pb = pw[tau, 1][:, :, :, None] * b_bar[1][None]
    pmat = jnp.concatenate([lanes(pf.transpose(1, 0, 3, 2)), lanes(pb.transpose(1, 0, 3, 2))], axis=-1)
    pmat = pmat.reshape(S5_GROUPS, L * S5_GROUP, 4 * LANES)
    qf = c_mat[0][None] * pw[tau + 1, 0][:, :, None, :]
    qb = c_mat[1][None] * pw[L - tau, 1][:, :, None, :]

    def state_rows(z):
        pad = ((0, 0), (0, LANES - S5_STATE), (0, 0), (0, 0))
        zr = jnp.pad(jnp.real(z).transpose(1, 3, 0, 2), pad)
        zi = jnp.pad(-jnp.imag(z).transpose(1, 3, 0, 2), pad)
        return jnp.concatenate([zr, zi], axis=1).reshape(S5_GROUPS, 2 * LANES, L * S5_GROUP)

    qmat = jnp.concatenate([state_rows(qf), state_rows(qb)], axis=1)
    lam_l = pw[L]
    pad = ((0, 0), (0, LANES - S5_STATE))
    coef = jnp.stack([jnp.pad(jnp.real(lam_l[0]), pad), jnp.pad(jnp.imag(lam_l[0]), pad),
                      jnp.pad(jnp.real(lam_l[1]), pad), jnp.pad(jnp.imag(lam_l[1]), pad)], axis=1)
    coef = jnp.pad(coef, ((0, 0), (0, 4), (0, 0)))
    return toep.astype(BF16), pmat.astype(BF16), qmat.astype(BF16), coef


def _s5_body(u_ref, toep_ref, p_ref, q_ref, coef_ref, y_ref, s_ref, hin_ref, *, nb, nc_ctx, nc):
    u = u_ref[0]
    s_ref[...] = jnp.dot(u, p_ref[0], preferred_element_type=F32)
    coef = coef_ref[0]
    shape = (nb, LANES)
    lr_f, li_f = jnp.broadcast_to(coef[0:1], shape), jnp.broadcast_to(coef[1:2], shape)
    lr_b, li_b = jnp.broadcast_to(coef[2:3], shape), jnp.broadcast_to(coef[3:4], shape)
    zero = jnp.zeros(shape, F32)

    def fwd(c, carry):
        hr, hi = carry
        rows = pl.ds(pl.multiple_of(c * nb, nb), nb)
        hin_ref[rows, 0:LANES] = hr
        hin_ref[rows, LANES:2 * LANES] = hi
        sr, si = s_ref[rows, 0:LANES], s_ref[rows, LANES:2 * LANES]
        return lr_f * hr - li_f * hi + sr, lr_f * hi + li_f * hr + si

    lax.fori_loop(0, nc, fwd, (zero, zero))

    def bwd(q, carry):
        c = jnp.where(q < nc_ctx, nc_ctx - 1 - q, nc + nc_ctx - 1 - q)
        hr, hi = carry
        rows = pl.ds(pl.multiple_of(c * nb, nb), nb)
        hin_ref[rows, 2 * LANES:3 * LANES] = hr
        hin_ref[rows, 3 * LANES:4 * LANES] = hi
        sr, si = s_ref[rows, 2 * LANES:3 * LANES], s_ref[rows, 3 * LANES:4 * LANES]
        return lr_b * hr - li_b * hi + sr, lr_b * hi + li_b * hr + si

    lax.fori_loop(0, nc, bwd, (zero, zero))

    lat = slice(nc_ctx * nb, nc * nb)
    y_ref[0] = (jnp.dot(u[lat], toep_ref[0], preferred_element_type=F32)
                + jnp.dot(hin_ref[lat, :].astype(BF16), q_ref[0], preferred_element_type=F32))


def _s5_scan(u_patch, toep, pmat, qmat, coef, nb, n_ctx):
    ng, nrow, pk = u_patch.shape
    nc = nrow // nb
    nc_ctx = n_ctx // S5_CHUNK
    nlat = (nc - nc_ctx) * nb
    per_g = lambda shape: pl.BlockSpec((1,) + shape, lambda g: (g, 0, 0))
    body = functools.partial(_s5_body, nb=nb, nc_ctx=nc_ctx, nc=nc)
    return pl.pallas_call(
        body,
        grid=(ng,),
        in_specs=[per_g((nrow, pk)), per_g((pk, pk)), per_g((pk, 4 * LANES)), per_g((4 * LANES, pk)),
                  per_g((8, LANES))],
        out_specs=per_g((nlat, pk)),
        out_shape=jax.ShapeDtypeStruct((ng, nlat, pk), F32),
        scratch_shapes=[pltpu.VMEM((nrow, 4 * LANES), F32), pltpu.VMEM((nrow, 4 * LANES), F32)],
        compiler_params=_cparams(("parallel",)),
        name="s5_scan",
    )(u_patch, toep, pmat, qmat, coef)


def _rms(x, g):
    return x * lax.rsqrt(jnp.mean(x * x, axis=-1, keepdims=True) + EPS) * g


def _s5_readout_body(y_ref, g_ref, gw_ref, gb_ref, w_ref, h_ref, mod_ref, nf_ref, o_ref):
    y = jax.nn.gelu(y_ref[0])
    t = y * jax.nn.sigmoid(jnp.dot(y.astype(BF16), gw_ref[...], preferred_element_type=F32) + gb_ref[...])
    z = (t * _silu(g_ref[0].astype(F32))).astype(BF16)
    out = jnp.dot(z, w_ref[...], preferred_element_type=F32)
    h = h_ref[0] + mod_ref[0, :, 2 * D:3 * D] * out
    o_ref[0] = _rms(h, nf_ref[...])


def _s5_readout(y, g, glu_w, glu_b, w_out, h, mod_b, norm_f, n_ctx):
    nb, t_len, _ = y.shape
    tm = ROW_TILE
    off = n_ctx // tm
    lat = lambda wd: pl.BlockSpec((1, tm, wd), lambda b, i: (b, i, 0))
    full = lambda wd: pl.BlockSpec((1, tm, wd), lambda b, i: (b, i + off, 0))
    const = lambda shape: pl.BlockSpec(shape, lambda b, i: (0, 0))
    return pl.pallas_call(
        _s5_readout_body,
        grid=(nb, t_len // tm),
        in_specs=[lat(S5_WIDTH), full(S5_WIDTH), const((S5_WIDTH, S5_WIDTH)), const((1, S5_WIDTH)),
                  const((S5_WIDTH, D)), full(D),
                  pl.BlockSpec((1, 1, 3 * D), lambda b, i: (b, 0, 0)), const((1, D))],
        out_specs=lat(D),
        out_shape=jax.ShapeDtypeStruct((nb, t_len, D), F32),
        compiler_params=_cparams(("parallel", "parallel")),
        name="s5_readout",
    )(y, g, glu_w.astype(BF16), glu_b.reshape(1, S5_WIDTH), w_out.astype(BF16), h, mod_b,
      norm_f.reshape(1, D))


def _s5_layer(h, mod_b, gn, w_in, a_re, a_im, log_dt, b_re, b_im, c_re, c_im, d_skip, glu_w, glu_b,
              w_out, norm_f, n_ctx):
    nb, rows, _ = h.shape
    L = S5_CHUNK
    layout = _bmajor(nb, rows, n_ctx, ROW_TILE)
    u, g = _in_proj(h, mod_b, gn, w_in.astype(BF16), layout, [(S5_WIDTH, BF16)] * 2, _s5_epilogue,
                    name="s5_in_proj")
    nc = rows // L
    u_patch = (u.reshape(nb, nc, L, S5_GROUPS, S5_GROUP).transpose(3, 1, 0, 2, 4)
               .reshape(S5_GROUPS, nc * nb, L * S5_GROUP))
    toep, pmat, qmat, coef = _s5_matrices(a_re, a_im, log_dt, b_re, b_im, c_re, c_im, d_skip)
    y_patch = _s5_scan(u_patch, toep, pmat, qmat, coef, nb, n_ctx)
    t_len = rows - n_ctx
    y = (y_patch.reshape(S5_GROUPS, t_len // L, nb, L, S5_GROUP).transpose(2, 1, 3, 0, 4)
         .reshape(nb, t_len, S5_WIDTH))
    return _s5_readout(y, g, glu_w, glu_b, w_out, h, mod_b, norm_f, n_ctx)


def kernel(x, c, ctx, c_ctx, ada_w0, ada_b0, norm0, w_in0, conv_w0, conv_b0, lru_wa0, lru_ba0, lru_wx0, lru_bx0, lru_lam0, w_out0, ada_w1, ada_b1, norm1, w_in1, sink1, w_out1, ada_w2, ada_b2, norm2, w_in2, rpb2, w_out2, ada_w3, ada_b3, norm3, w_in3, s5_a_re3, s5_a_im3, s5_log_dt3, s5_b_re3, s5_b_im3, s5_c_re3, s5_c_im3, s5_d3, glu_w3, glu_b3, w_out3, norm_f):
    nb, t_len, _ = x.shape
    n_ctx = ctx.shape[1]
    cvec = jnp.concatenate([c, c_ctx[None, :], jnp.zeros((16 - nb - 1, D), F32)], axis=0)
    mods = [_modulation(cvec, w, b) for w, b in
            ((ada_w0, ada_b0), (ada_w1, ada_b1), (ada_w2, ada_b2), (ada_w3, ada_b3))]
    mod_b = [m[:, None, :] for m in mods]

    h_t = jnp.concatenate([ctx, x], axis=1).transpose(1, 0, 2)
    mod_t0 = jnp.stack([jnp.broadcast_to(mods[0][nb], (nb, 3 * D)), mods[0][:nb]], axis=0)
    h_t = _rglru_layer(h_t, mod_t0, norm0, w_in0, conv_w0, conv_b0, lru_wa0, lru_ba0, lru_wx0, lru_bx0,
                       lru_lam0, w_out0, n_ctx)
    h = h_t.transpose(1, 0, 2)
    h = _swa_layer(h, mod_b[1], norm1, w_in1, sink1, w_out1, n_ctx)
    h = _na_layer(h, mod_b[2], norm2, w_in2, rpb2, w_out2, n_ctx)
    return _s5_layer(h, mod_b[3], norm3, w_in3, s5_a_re3, s5_a_im3, s5_log_dt3, s5_b_re3, s5_b_im3,
                     s5_c_re3, s5_c_im3, s5_d3, glu_w3, glu_b3, w_out3, norm_f, n_ctx)
```

```python
import functools
import math

import jax
import jax.numpy as jnp
import numpy as np
from jax import lax
from jax.experimental import pallas as pl
from jax.experimental.pallas import tpu as pltpu

F32 = jnp.float32
BF16 = jnp.bfloat16

D = 1024
HEAD_DIM = 64
GRID_W = 64
EPS = 1e-6
NEG_INF = -1e30
ROPE_BASE = 10000.0
LANES = 128

LRU_WIDTH = 1408
LRU_BLOCKS = 16
LRU_BW = LRU_WIDTH // LRU_BLOCKS
LRU_PW = LRU_BLOCKS * LANES
CONV_W = 4
LRU_C = 8.0
LRU_TT = 32

SWA_HEADS = 16
SWA_KV_HEADS = 4
WINDOW = 128
BLOCK_Q = 128
SWA_CHUNK = 64

NA_HEADS = 16
NA_ROWS = 8
NA_COLS = 16
NA_QROWS = 4
NA_KROWS = 12
NA_HSTEP = 4

S5_WIDTH = 1024
S5_GROUP = 16
S5_GROUPS = S5_WIDTH // S5_GROUP
S5_STATE = 64
S5_CHUNK = 16

ROW_TILE = 256
VMEM_LIMIT = 56 * 1024 * 1024


def _cparams(sem):
    return pltpu.CompilerParams(dimension_semantics=sem, vmem_limit_bytes=VMEM_LIMIT)


def _silu(v):
    return v * jax.nn.sigmoid(v)


def _bmajor(nb, rows, n_ctx, tm):
    grid = (nb, rows // tm)
    nct = n_ctx // tm

    def row(width, off=0):
        return pl.BlockSpec((1, tm, width), lambda b, i: (b, i + off, 0))

    mod = pl.BlockSpec((1, 1, 3 * D), lambda b, i: (jnp.where(i < nct, nb, b), 0, 0))
    return grid, row, mod


def _tmajor(nb, rows, n_ctx, tt):
    grid = (1, rows // tt)
    nct = n_ctx // tt

    def row(width, off=0):
        return pl.BlockSpec((tt, nb, width), lambda b, i: (i + off, 0, 0))

    mod = pl.BlockSpec((1, nb, 3 * D), lambda b, i: (jnp.where(i < nct, 0, 1), 0, 0))
    return grid, row, mod


def _mod_body(c_ref, w_ref, b_ref, o_ref):
    s = _silu(c_ref[...]).astype(BF16)
    o_ref[...] = jnp.dot(s, w_ref[...].astype(BF16), preferred_element_type=F32) + b_ref[...]


def _modulation(cvec, ada_w, ada_b):
    n = ada_w.shape[1]
    tn = 1024
    return pl.pallas_call(
        _mod_body,
        grid=(n // tn,),
        in_specs=[pl.BlockSpec((16, D), lambda j: (0, 0)),
                  pl.BlockSpec((D, tn), lambda j: (0, j)),
                  pl.BlockSpec((1, tn), lambda j: (0, j))],
        out_specs=pl.BlockSpec((16, tn), lambda j: (0, j)),
        out_shape=jax.ShapeDtypeStruct((16, n), F32),
        compiler_params=_cparams(("parallel",)),
        name="modulation",
    )(cvec, ada_w, ada_b.reshape(1, n))


def _norm_mod(x3, mod3, gn):
    ms = jnp.mean(x3 * x3, axis=-1, keepdims=True)
    y = x3 * lax.rsqrt(ms + EPS) * gn
    return y * (1.0 + mod3[:, :, D:2 * D]) + mod3[:, :, 0:D]


def _in_proj_body(*refs, load, n_x, epilogue, n_extra, chunk):
    xs, (mod_ref, gn_ref, w_ref), rest = refs[:n_x], refs[n_x:n_x + 3], refs[n_x + 3:]
    extra, outs = rest[:n_extra], rest[n_extra:]
    x3 = load(*xs)
    n3 = _norm_mod(x3, mod_ref[...], gn_ref[...])
    lead = x3.shape[:2]
    n = n3.reshape(lead[0] * lead[1], D).astype(BF16)
    for j in range(w_ref.shape[1] // chunk):
        acc = jnp.dot(n, w_ref[:, j * chunk:(j + 1) * chunk], preferred_element_type=F32)
        epilogue(j, acc, lead, extra, outs)


def _load_block(x_ref):
    return x_ref[...]


def _in_proj(xs, x_specs, load, mod, gn, w, layout, out_shapes, out_specs, epilogue, extras=(),
             extra_specs=(), chunk=512, name="in_proj"):
    grid, _, mod_spec = layout
    n = w.shape[1]
    body = functools.partial(_in_proj_body, load=load, n_x=len(xs), epilogue=epilogue,
                             n_extra=len(extras), chunk=chunk)
    return pl.pallas_call(
        body,
        grid=grid,
        in_specs=[*x_specs, mod_spec,
                  pl.BlockSpec((1, D), lambda b, i: (0, 0)),
                  pl.BlockSpec((D, n), lambda b, i: (0, 0)),
                  *extra_specs],
        out_specs=out_specs,
        out_shape=out_shapes,
        compiler_params=_cparams(("parallel", "parallel")),
        name=name,
    )(*xs, mod, gn.reshape(1, D), w, *extras)


def _rows_out(shape2, row, outs):
    return ([jax.ShapeDtypeStruct(shape2 + (wd,), dt) for wd, dt in outs], [row(wd) for wd, _ in outs])


def _put(ref, lo, val, lead):
    ref[:, :, lo:lo + val.shape[1]] = val.reshape(lead + (val.shape[1],)).astype(ref.dtype)


def _out_proj_body(*refs, load, n_h, store):
    (z_ref, w_ref), hs, (mod_ref, o_ref) = refs[:2], refs[2:2 + n_h], refs[2 + n_h:]
    z3 = z_ref[...]
    a, b, wd = z3.shape
    y = jnp.dot(z3.reshape(a * b, wd), w_ref[...], preferred_element_type=F32)
    store(o_ref, load(*hs) + mod_ref[:, :, 2 * D:3 * D] * y.reshape(a, b, D))


def _store_block(o_ref, val):
    o_ref[...] = val


def _out_proj(z, w, h, mod, layout, name):
    grid, row, mod_spec = layout
    wd = w.shape[0]
    body = functools.partial(_out_proj_body, load=_load_block, n_h=1, store=_store_block)
    return pl.pallas_call(
        body,
        grid=grid,
        in_specs=[row(wd), pl.BlockSpec((wd, D), lambda b, i: (0, 0)), row(D), mod_spec],
        out_specs=row(D),
        out_shape=jax.ShapeDtypeStruct(h.shape, F32),
        input_output_aliases={2: 0},
        compiler_params=_cparams(("parallel", "parallel")),
        name=name,
    )(z, w, h, mod)


def _lru_epilogue(j, acc, lead, extra, outs):
    u_ref, g_ref = outs
    lo = j * acc.shape[1]
    if lo < LRU_PW:
        _put(u_ref, lo, acc, lead)
    else:
        _put(g_ref, lo - LRU_PW, acc, lead)


def _lru_tile_index(s, nct, ntiles, reverse):
    if not reverse:
        return s
    return jnp.where(s < nct, nct - 1 - s, ntiles + nct - 1 - s)


def _lru_sweep_body(up_ref, uc_ref, un_ref, cw_ref, cb_ref, wg_ref, bg_ref, lam_ref, *rest,
                    nct, ntiles, reverse):
    if reverse:
        hf_ref, g_ref, o_ref, ext_ref, a_ref, b_ref, carry_ref = rest
    else:
        o_ref, ext_ref, a_ref, b_ref, carry_ref = rest
    s = pl.program_id(0)
    ti = _lru_tile_index(s, nct, ntiles, reverse)
    tt, nb, _ = uc_ref.shape

    @pl.when(s == 0)
    def _():
        carry_ref[...] = jnp.zeros_like(carry_ref)

    seg_start = (ti == 0) | (ti == nct)
    seg_end = (ti == nct - 1) | (ti == ntiles - 1)
    ext_ref[0:2] = jnp.where(seg_start, 0.0, up_ref[...])
    ext_ref[2:2 + tt] = uc_ref[...]
    ext_ref[2 + tt:3 + tt] = jnp.where(seg_end, 0.0, un_ref[...])

    softplus_neg_lam = jax.nn.softplus(-lam_ref[...])
    for k in range(LRU_BLOCKS):
        sl = slice(k * LANES, (k + 1) * LANES)
        u = cb_ref[:, sl]
        for tap in range(CONV_W):
            u = u + cw_ref[tap:tap + 1, sl] * ext_ref[tap:tap + tt, :, sl]
        u = u.reshape(tt * nb, LANES)
        pre = jnp.dot(u.astype(BF16), wg_ref[k], preferred_element_type=F32) + bg_ref[k]
        r = jax.nn.sigmoid(pre[:, :LANES])
        i = jax.nn.sigmoid(pre[:, LANES:])
        a = jnp.exp(-LRU_C * r * softplus_neg_lam[:, sl])
        a_ref[:, sl] = a
        b_ref[:, sl] = jnp.sqrt(1.0 - a * a) * (i * u)

    def step(q, h):
        t = tt - 1 - q if reverse else q
        rows = pl.ds(pl.multiple_of(t * nb, nb), nb)
        h = a_ref[rows, :] * h + b_ref[rows, :]
        if reverse:
            o_ref[t] = ((hf_ref[t] + h) * _silu(g_ref[t].astype(F32))).astype(o_ref.dtype)
        else:
            o_ref[t] = h
        return h

    carry_ref[...] = lax.fori_loop(0, tt, step, carry_ref[...], unroll=4)


def _lru_sweep(u, cw, cb, wg, bg, lam, nct_rows, reverse, hf=None, g=None):
    rows, nb, pw = u.shape
    tt = LRU_TT
    ntiles = rows // tt
    nct = nct_rows // tt
    tile = functools.partial(_lru_tile_index, nct=nct, ntiles=ntiles, reverse=reverse)
    cur = lambda wd: pl.BlockSpec((tt, nb, wd), lambda s: (tile(s), 0, 0))
    const = lambda shape: pl.BlockSpec(shape, lambda s: (0,) * len(shape))
    in_specs = [
        pl.BlockSpec((2, nb, pw), lambda s: (jnp.maximum(tile(s) * (tt // 2) - 1, 0), 0, 0)),
        cur(pw),
        pl.BlockSpec((1, nb, pw), lambda s: (jnp.minimum((tile(s) + 1) * tt, rows - 1), 0, 0)),
        const((CONV_W, pw)), const((1, pw)),
        const((LRU_BLOCKS, LANES, 2 * LANES)), const((LRU_BLOCKS, 1, 2 * LANES)), const((1, pw)),
    ]
    args = [u, u, u, cw, cb, wg, bg, lam]
    if reverse:
        in_specs += [cur(pw), cur(pw)]
        args += [hf, g]
    body = functools.partial(_lru_sweep_body, nct=nct, ntiles=ntiles, reverse=reverse)
    return pl.pallas_call(
        body,
        grid=(ntiles,),
        in_specs=in_specs,
        out_specs=cur(pw),
        out_shape=jax.ShapeDtypeStruct(u.shape, BF16 if reverse else F32),
        scratch_shapes=[pltpu.VMEM((tt + 3, nb, pw), F32),
                        pltpu.VMEM((tt * nb, pw), F32),
                        pltpu.VMEM((tt * nb, pw), F32),
                        pltpu.VMEM((nb, pw), F32)],
        compiler_params=_cparams(("arbitrary",)),
        name="lru_bwd" if reverse else "lru_fwd",
    )(*args)


def _pad_blocks(a, axis):
    axis = axis % a.ndim
    shp = a.shape
    a = a.reshape(shp[:axis] + (LRU_BLOCKS, LRU_BW) + shp[axis + 1:])
    pad = [(0, 0)] * a.ndim
    pad[axis + 1] = (0, LANES - LRU_BW)
    a = jnp.pad(a, pad)
    return a.reshape(shp[:axis] + (LRU_PW,) + shp[axis + 1:])


def _load_time_major(ctx_ref, x_ref, *, nct):
    blk = jnp.where(pl.program_id(1) < nct, ctx_ref[...], x_ref[...])
    return pltpu.einshape("btd->tbd", blk)


def _store_batch_major(o_ref, val):
    o_ref[...] = pltpu.einshape("tbd->btd", val)


def _rglru_layer(ctx, x, mod_t, gn, w_in, conv_w, conv_b, wa, ba, wx, bx, lam, w_out):
    nb, n_ctx, _ = ctx.shape
    rows = n_ctx + x.shape[1]
    tt = LRU_TT
    nct = n_ctx // tt
    layout = _tmajor(nb, rows, n_ctx, tt)
    grid, row, mod_spec = layout
    src_specs = [pl.BlockSpec((nb, tt, D), lambda b, i: (0, jnp.minimum(i, nct - 1), 0)),
                 pl.BlockSpec((nb, tt, D), lambda b, i: (0, jnp.maximum(i - nct, 0), 0))]
    load = functools.partial(_load_time_major, nct=nct)
    w_pad = jnp.concatenate([_pad_blocks(w_in[:, :LRU_WIDTH], 1), _pad_blocks(w_in[:, LRU_WIDTH:], 1)],
                            axis=1).astype(BF16)
    out_shapes, out_specs = _rows_out((rows, nb), row, [(LRU_PW, F32), (LRU_PW, BF16)])
    u, g = _in_proj((ctx, x), src_specs, load, mod_t, gn, w_pad, layout, out_shapes, out_specs,
                    _lru_epilogue, name="lru_in_proj")
    cw = _pad_blocks(conv_w, 1)
    cb = _pad_blocks(conv_b.reshape(1, LRU_WIDTH), 1)
    pad_w = ((0, 0), (0, 0), (0, LANES - LRU_BW), (0, LANES - LRU_BW))
    wg = jnp.concatenate([jnp.pad(wa, pad_w), jnp.pad(wx, pad_w)], axis=-1).astype(BF16)
    pad_b = ((0, 0), (0, 0), (0, LANES - LRU_BW))
    bg = jnp.concatenate([jnp.pad(ba.reshape(2, LRU_BLOCKS, LRU_BW), pad_b),
                          jnp.pad(bx.reshape(2, LRU_BLOCKS, LRU_BW), pad_b)], axis=-1)
    bg = bg.reshape(2, LRU_BLOCKS, 1, 2 * LANES)
    lam_p = _pad_blocks(lam, 1)
    hf = _lru_sweep(u, cw, cb, wg[0], bg[0], lam_p[0:1], n_ctx, reverse=False)
    z = _lru_sweep(u, cw, cb, wg[1], bg[1], lam_p[1:2], n_ctx, reverse=True, hf=hf, g=g)
    body = functools.partial(_out_proj_body, load=load, n_h=2, store=_store_batch_major)
    return pl.pallas_call(
        body,
        grid=grid,
        in_specs=[row(LRU_PW), pl.BlockSpec((LRU_PW, D), lambda b, i: (0, 0)), *src_specs, mod_spec],
        out_specs=pl.BlockSpec((nb, tt, D), lambda b, i: (0, i, 0)),
        out_shape=jax.ShapeDtypeStruct((nb, rows, D), F32),
        compiler_params=_cparams(("parallel", "parallel")),
        name="lru_out_proj",
    )(z, _pad_blocks(w_out, 0).astype(BF16), ctx, x, mod_t)


def _rope_tables(n_ctx, t_len):
    pos = jnp.arange(t_len)
    row = (pos // GRID_W).astype(F32)
    col = (pos % GRID_W).astype(F32)
    n_ax = HEAD_DIM // 4
    freqs = ROPE_BASE ** (-jnp.arange(n_ax, dtype=F32) / n_ax)
    ang = jnp.concatenate([row[:, None] * freqs, col[:, None] * freqs], axis=-1)
    cos, sin = jnp.cos(ang), jnp.sin(ang)
    cos_h = jnp.concatenate([cos, cos], axis=-1)
    sin_h = jnp.concatenate([-sin, sin], axis=-1)
    cos_f = jnp.concatenate([jnp.ones((n_ctx, HEAD_DIM), F32), cos_h], axis=0)
    sin_f = jnp.concatenate([jnp.zeros((n_ctx, HEAD_DIM), F32), sin_h], axis=0)
    return jnp.tile(cos_f, (1, 2)), jnp.tile(sin_f, (1, 2))


def _rope(x, cos, sin):
    lane = lax.broadcasted_iota(jnp.int32, x.shape, 1)
    swapped = jnp.where(lane % HEAD_DIM < HEAD_DIM // 2,
                        pltpu.roll(x, LANES - HEAD_DIM // 2, 1), pltpu.roll(x, HEAD_DIM // 2, 1))
    return x * cos + swapped * sin


def _swa_epilogue(j, acc, lead, extra, outs):
    cos_ref, sin_ref = extra
    q_ref, qr_ref, k_ref, v_ref, g_ref = outs
    qd = SWA_HEADS * HEAD_DIM
    kvd = SWA_KV_HEADS * HEAD_DIM
    grp = SWA_HEADS // SWA_KV_HEADS
    width = acc.shape[1]
    lo = j * width
    cos, sin = cos_ref[...], sin_ref[...]

    def roped(x):
        return jnp.concatenate([_rope(x[:, s:s + LANES], cos, sin) for s in range(0, x.shape[1], LANES)], axis=1)

    def head(x, hh):
        return x[:, hh * HEAD_DIM:(hh + 1) * HEAD_DIM]

    if lo < qd:
        q = acc * (HEAD_DIM ** -0.5)
        qr = roped(q)
        for hh in range(width // HEAD_DIM):
            kv, gq = divmod(lo // HEAD_DIM + hh, grp)
            for rt in range(lead[1] // BLOCK_Q):
                rows = slice(rt * BLOCK_Q, (rt + 1) * BLOCK_Q)
                dst = slice(gq * BLOCK_Q, (gq + 1) * BLOCK_Q)
                q_ref[0, kv, rt, dst, :] = head(q, hh)[rows].astype(q_ref.dtype)
                qr_ref[0, kv, rt, dst, :] = head(qr, hh)[rows].astype(qr_ref.dtype)
    elif lo == qd:
        kr = roped(acc[:, :kvd])
        for kv in range(SWA_KV_HEADS):
            k_ref[0, kv] = head(kr, kv).astype(k_ref.dtype)
            v_ref[0, kv] = head(acc[:, kvd:], kv).astype(v_ref.dtype)
    else:
        _put(g_ref, lo - qd - 2 * kvd, acc, lead)


def _swa_mask_table():
    qi = np.arange(BLOCK_Q)[:, None]
    kj = np.arange(3 * BLOCK_Q)[None, :]
    base = np.where(np.abs(kj - BLOCK_Q - qi) <= WINDOW, 0.0, NEG_INF).astype(np.float32)
    first = np.where(kj < BLOCK_Q, NEG_INF, base).astype(np.float32)
    last = np.where(kj >= 2 * BLOCK_Q, NEG_INF, base).astype(np.float32)
    return jnp.asarray(np.stack([np.full_like(base, NEG_INF), first, base, last]))


def _swa_body(sink_ref, mask_ref, q_ref, qr_ref, kc_ref, k0_ref, k1_ref, k2_ref, vc_ref, v0_ref, v1_ref,
              v2_ref, g_ref, o_ref, s_ref, p_ref, l_ref):
    grp = SWA_HEADS // SWA_KV_HEADS
    dims = (((1,), (1,)), ((), ()))
    nc = kc_ref.shape[2]
    for kv in range(SWA_KV_HEADS):
        slot = kv % 2
        kb = jnp.concatenate([k0_ref[0, kv], k1_ref[0, kv], k2_ref[0, kv]], axis=0)
        vb = jnp.concatenate([v0_ref[0, kv], v1_ref[0, kv], v2_ref[0, kv]], axis=0)
        s_ref[slot, :, :nc] = lax.dot_general(q_ref[0, kv, 0], kc_ref[0, kv], dims, preferred_element_type=F32)
        s_ref[slot, :, nc:] = lax.dot_general(qr_ref[0, kv, 0], kb, dims, preferred_element_type=F32)
        for c in range(grp * BLOCK_Q // SWA_CHUNK):
            rows = slice(c * SWA_CHUNK, (c + 1) * SWA_CHUNK)
            mrow = (c * SWA_CHUNK) % BLOCK_Q
            sink = sink_ref[kv * grp + (c * SWA_CHUNK) // BLOCK_Q]
            s_c = s_ref[slot, rows, :nc]
            s_b = s_ref[slot, rows, nc:] + mask_ref[0, mrow:mrow + SWA_CHUNK, :]
            m = jnp.maximum(jnp.maximum(jnp.max(s_c, axis=1, keepdims=True),
                                        jnp.max(s_b, axis=1, keepdims=True)), sink)
            p_c = jnp.exp(s_c - m)
            p_b = jnp.exp(s_b - m)
            l_ref[slot, rows, :] = (jnp.sum(p_c, axis=1, keepdims=True) + jnp.sum(p_b, axis=1, keepdims=True)
                                    + jnp.exp(sink - m))
            p_ref[slot, rows, :nc] = p_c.astype(BF16)
            p_ref[slot, rows, nc:] = p_b.astype(BF16)
        o = (jnp.dot(p_ref[slot, :, :nc], vc_ref[0, kv], preferred_element_type=F32)
             + jnp.dot(p_ref[slot, :, nc:], vb, preferred_element_type=F32)) / l_ref[slot]
        slab = jnp.concatenate([o[gq * BLOCK_Q:(gq + 1) * BLOCK_Q] for gq in range(grp)], axis=1)
        lanes = slice(kv * grp * HEAD_DIM, (kv + 1) * grp * HEAD_DIM)
        o_ref[0, :, lanes] = (slab * _silu(g_ref[0, :, lanes].astype(F32))).astype(o_ref.dtype)


def _swa_attention(q, qr, k, v, g, sink, n_ctx):
    nb, nkv, nt, qrows, hd = q.shape
    rows = k.shape[2]
    nct = n_ctx // BLOCK_Q
    qspec = pl.BlockSpec((1, nkv, 1, qrows, hd), lambda b, n: (b, 0, n, 0, 0))
    ctx = pl.BlockSpec((1, nkv, n_ctx, hd), lambda b, n: (b, 0, 0, 0))
    rowspec = pl.BlockSpec((1, BLOCK_Q, g.shape[2]), lambda b, n: (b, n, 0))

    def band(off):
        return pl.BlockSpec((1, nkv, BLOCK_Q, hd), lambda b, n: (b, 0, jnp.clip(n + off, nct, nt - 1), 0))

    def variant(b, n):
        return (jnp.where(n < nct, 0, jnp.where(n == nct, 1, jnp.where(n == nt - 1, 3, 2))), 0, 0)

    return pl.pallas_call(
        _swa_body,
        grid=(nb, nt),
        in_specs=[pl.BlockSpec(memory_space=pltpu.SMEM),
                  pl.BlockSpec((1, BLOCK_Q, 3 * BLOCK_Q), variant), qspec, qspec,
                  ctx, band(-1), band(0), band(1), ctx, band(-1), band(0), band(1), rowspec],
        out_specs=rowspec,
        out_shape=jax.ShapeDtypeStruct((nb, rows, g.shape[2]), BF16),
        scratch_shapes=[pltpu.VMEM((2, qrows, n_ctx + 3 * BLOCK_Q), F32),
                        pltpu.VMEM((2, qrows, n_ctx + 3 * BLOCK_Q), BF16),
                        pltpu.VMEM((2, qrows, 1), F32)],
        compiler_params=_cparams(("parallel", "parallel")),
        name="swa_attention",
    )(sink, _swa_mask_table(), q, qr, k, k, k, k, v, v, v, v, g)


def _swa_layer(h, mod_b, gn, w_in, sink, w_out, n_ctx):
    nb, rows, _ = h.shape
    tm = ROW_TILE
    layout = _bmajor(nb, rows, n_ctx, tm)
    _, row, _ = layout
    qd = SWA_HEADS * HEAD_DIM
    grp = SWA_HEADS // SWA_KV_HEADS
    nkv = SWA_KV_HEADS
    cos, sin = _rope_tables(n_ctx, rows - n_ctx)
    tab = pl.BlockSpec((tm, LANES), lambda b, i: (i, 0))
    q_shape = jax.ShapeDtypeStruct((nb, nkv, rows // BLOCK_Q, grp * BLOCK_Q, HEAD_DIM), BF16)
    q_spec = pl.BlockSpec((1, nkv, tm // BLOCK_Q, grp * BLOCK_Q, HEAD_DIM), lambda b, i: (b, 0, i, 0, 0))
    kv_shape = jax.ShapeDtypeStruct((nb, nkv, rows, HEAD_DIM), BF16)
    kv_spec = pl.BlockSpec((1, nkv, tm, HEAD_DIM), lambda b, i: (b, 0, i, 0))
    q, qr, k, v, g = _in_proj(
        (h,), [row(D)], _load_block, mod_b, gn, w_in.astype(BF16), layout,
        [q_shape, q_shape, kv_shape, kv_shape, jax.ShapeDtypeStruct((nb, rows, qd), BF16)],
        [q_spec, q_spec, kv_spec, kv_spec, row(qd)], _swa_epilogue,
        extras=(cos, sin), extra_specs=(tab, tab), name="swa_in_proj")
    z = _swa_attention(q, qr, k, v, g, sink, n_ctx)
    return _out_proj(z, w_out.astype(BF16), h, mod_b, layout, name="swa_out_proj")


def _na_epilogue(j, acc, lead, extra, outs):
    q_ref, k_ref, v_ref, g_ref = outs
    wd = NA_HEADS * HEAD_DIM
    lo = j * acc.shape[1]
    which, off = lo // wd, lo % wd
    if which == 0:
        _put(q_ref, off, acc * (HEAD_DIM ** -0.5), lead)
    else:
        _put((k_ref, v_ref, g_ref)[which - 1], off, acc, lead)


def _na_bias_table(rpb, grid_rows):
    half = NA_ROWS // 2
    reach = GRID_W - NA_COLS
    edge = jnp.pad(rpb.astype(F32), ((0, 0), (0, 0), (reach, reach)), mode="edge")
    e = jnp.stack([edge[:, :, GRID_W - 1 - qc:2 * GRID_W - 1 - qc] for qc in range(GRID_W)], axis=2)
    qcol = np.arange(GRID_W)
    cstart = np.clip(qcol - NA_COLS // 2, 0, GRID_W - NA_COLS)
    col_ok = (qcol[None, :] >= cstart[:, None]) & (qcol[None, :] < cstart[:, None] + NA_COLS)
    e = jnp.where(jnp.asarray(col_ok)[None, None], e, NEG_INF)
    masked = jnp.full((NA_HEADS, GRID_W, GRID_W), NEG_INF, F32)
    pats = []
    for r0, kb in [(0, 0), (NA_QROWS, 0), (grid_rows - NA_QROWS, grid_rows - NA_KROWS)]:
        blocks = []
        for qr in range(NA_QROWS):
            qrow = r0 + qr
            start = min(max(qrow - half, 0), grid_rows - NA_ROWS)
            tiles = []
            for kr in range(NA_KROWS):
                krow = kb + kr
                inside = start <= krow < start + NA_ROWS
                tiles.append(e[:, krow - qrow + NA_ROWS - 1] if inside else masked)
            blocks.append(jnp.concatenate(tiles, axis=2))
        pats.append(jnp.concatenate(blocks, axis=1))
    pats.append(jnp.full_like(pats[0], NEG_INF))
    t = jnp.stack(pats, axis=1)
    return t.reshape(NA_HEADS // NA_HSTEP, NA_HSTEP, 4, t.shape[2], t.shape[3]).transpose(0, 2, 1, 3, 4)


def _na_body(q_ref, k_ref, v_ref, g_ref, bias_ref, o_ref, *, n_ctx, grid_rows):
    rb = pl.program_id(2)
    r0 = (rb - 1) * NA_QROWS
    kb = jnp.clip(r0 - NA_ROWS // 2, 0, grid_rows - NA_KROWS)
    start = pl.multiple_of(n_ctx + kb * GRID_W, GRID_W)
    nk = NA_KROWS * GRID_W
    q = q_ref[0]
    kc, vc = k_ref[0, 0:n_ctx, :], v_ref[0, 0:n_ctx, :]
    kn, vn = k_ref[0, pl.ds(start, nk), :], v_ref[0, pl.ds(start, nk), :]
    dims = (((1,), (1,)), ((), ()))
    for hl in range(NA_HSTEP):
        hs = slice(hl * HEAD_DIM, (hl + 1) * HEAD_DIM)
        s_c = lax.dot_general(q[:, hs], kc[:, hs], dims, preferred_element_type=F32)
        s_n = lax.dot_general(q[:, hs], kn[:, hs], dims, preferred_element_type=F32) + bias_ref[0, 0, hl]
        m = jnp.maximum(jnp.max(s_c, axis=1, keepdims=True), jnp.max(s_n, axis=1, keepdims=True))
        p_c = jnp.exp(s_c - m)
        p_n = jnp.exp(s_n - m)
        denom = jnp.sum(p_c, axis=1, keepdims=True) + jnp.sum(p_n, axis=1, keepdims=True)
        o = (jnp.dot(p_c.astype(BF16), vc[:, hs], preferred_element_type=F32)
             + jnp.dot(p_n.astype(BF16), vn[:, hs], preferred_element_type=F32)) / denom
        gate = g_ref[0, :, hs].astype(F32)
        o_ref[0, :, hs] = (o * _silu(gate)).astype(o_ref.dtype)


def _na_attention(q, k, v, g, bias, n_ctx):
    nb, rows, wd = q.shape
    grid_rows = (rows - n_ctx) // GRID_W
    nq = NA_QROWS * GRID_W
    assert n_ctx == nq, "context tokens form exactly one query block"
    nblk = rows // nq
    last = nblk - 1
    lanes = NA_HSTEP * HEAD_DIM
    qspec = pl.BlockSpec((1, nq, lanes), lambda b, p, r: (b, r, p))
    kvspec = pl.BlockSpec((1, rows, lanes), lambda b, p, r: (b, 0, p))

    def pattern(b, p, r):
        return (p, jnp.where(r == 0, 3, jnp.where(r == 1, 0, jnp.where(r == last, 2, 1))), 0, 0, 0)

    bspec = pl.BlockSpec((1, 1, NA_HSTEP, nq, NA_KROWS * GRID_W), pattern)
    body = functools.partial(_na_body, n_ctx=n_ctx, grid_rows=grid_rows)
    return pl.pallas_call(
        body,
        grid=(nb, wd // lanes, nblk),
        in_specs=[qspec, kvspec, kvspec, qspec, bspec],
        out_specs=qspec,
        out_shape=jax.ShapeDtypeStruct(q.shape, BF16),
        compiler_params=_cparams(("parallel", "parallel", "parallel")),
        name="na_attention",
    )(q, k, v, g, bias)


def _na_layer(h, mod_b, gn, w_in, rpb, w_out, n_ctx):
    nb, rows, _ = h.shape
    layout = _bmajor(nb, rows, n_ctx, ROW_TILE)
    _, row, _ = layout
    wd = NA_HEADS * HEAD_DIM
    out_shapes, out_specs = _rows_out((nb, rows), row, [(wd, BF16)] * 4)
    q, k, v, g = _in_proj((h,), [row(D)], _load_block, mod_b, gn, w_in.astype(BF16), layout,
                          out_shapes, out_specs, _na_epilogue, name="na_in_proj")
    bias = _na_bias_table(rpb, (rows - n_ctx) // GRID_W)
    z = _na_attention(q, k, v, g, bias, n_ctx)
    return _out_proj(z, w_out.astype(BF16), h, mod_b, layout, name="na_out_proj")


def _s5_epilogue(j, acc, lead, extra, outs):
    u_ref, g_ref = outs
    lo = j * acc.shape[1]
    if lo < S5_WIDTH:
        _put(u_ref, lo, acc, lead)
    else:
        _put(g_ref, lo - S5_WIDTH, acc, lead)


def _s5_matrices(a_re, a_im, log_dt, b_re, b_im, c_re, c_im, d_skip):
    L = S5_CHUNK
    lam = lax.complex(a_re.astype(F32), a_im.astype(F32))
    lam_dt = lam * jnp.exp(log_dt.astype(F32))[..., None]
    lam_bar = jnp.exp(lam_dt)
    b_bar = ((lam_bar - 1.0) / lam)[..., None] * lax.complex(b_re.astype(F32), b_im.astype(F32))
    c_mat = lax.complex(c_re.astype(F32), c_im.astype(F32))
    kk = jnp.arange(L + 1, dtype=F32)
    pw = jnp.exp(kk[:, None, None, None] * lam_dt[None])
    kern = jnp.real(jnp.einsum('dgip,kdgp,dgpj->dkgij', c_mat, pw[:L], b_bar))
    tail = ((0, 0),) * 3
    fwd = jnp.stack([jnp.pad(kern[0][:L - tp], ((tp, 0),) + tail) for tp in range(L)])
    bwd = jnp.stack([jnp.pad(jnp.flip(kern[1][:tp + 1], axis=0), ((0, L - 1 - tp),) + tail) for tp in range(L)])
    both = fwd + bwd
    eye_t = jnp.eye(L, dtype=F32)[:, :, None, None, None]
    skip = jnp.eye(S5_GROUP, dtype=F32)[None] * d_skip.astype(F32).reshape(S5_GROUPS, S5_GROUP, 1)
    both = both + eye_t * skip[None, None]
    toep = both.transpose(2, 0, 4, 1, 3).reshape(S5_GROUPS, L * S5_GROUP, L * S5_GROUP)

    def lanes(z):
        pad = [(0, 0)] * (z.ndim - 1) + [(0, LANES - S5_STATE)]
        return jnp.concatenate([jnp.pad(jnp.real(z), pad), jnp.pad(jnp.imag(z), pad)], axis=-1)

    pf = jnp.flip(pw[:L, 0], axis=0)[:, :, :, None] * b_bar[0][None]
    pb = pw[:L, 1][:, :, :, None] * b_bar[1][None]
    pmat = jnp.concatenate([lanes(pf.transpose(1, 0, 3, 2)), lanes(pb.transpose(1, 0, 3, 2))], axis=-1)
    pmat = pmat.reshape(S5_GROUPS, L * S5_GROUP, 4 * LANES)
    qf = c_mat[0][None] * pw[1:, 0][:, :, None, :]
    qb = c_mat[1][None] * jnp.flip(pw[1:, 1], axis=0)[:, :, None, :]

    def state_rows(z):
        pad = ((0, 0), (0, LANES - S5_STATE), (0, 0), (0, 0))
        zr = jnp.pad(jnp.real(z).transpose(1, 3, 0, 2), pad)
        zi = jnp.pad(-jnp.imag(z).transpose(1, 3, 0, 2), pad)
        return jnp.concatenate([zr, zi], axis=1).reshape(S5_GROUPS, 2 * LANES, L * S5_GROUP)

    qmat = jnp.concatenate([state_rows(qf), state_rows(qb)], axis=1)
    lam_l = pw[L]
    pad = ((0, 0), (0, LANES - S5_STATE))
    coef = jnp.stack([jnp.pad(jnp.real(lam_l[0]), pad), jnp.pad(jnp.imag(lam_l[0]), pad),
                      jnp.pad(jnp.real(lam_l[1]), pad), jnp.pad(jnp.imag(lam_l[1]), pad)], axis=1)
    coef = jnp.pad(coef, ((0, 0), (0, 4), (0, 0)))
    return toep.astype(BF16), pmat.astype(BF16), qmat.astype(BF16), coef


def _s5_body(u_ref, toep_ref, p_ref, q_ref, coef_ref, y_ref, s_ref, hin_ref, *, nb, nc_ctx, nc):
    u = u_ref[0]
    s_ref[...] = jnp.dot(u, p_ref[0], preferred_element_type=F32)
    coef = coef_ref[0]
    shape = (nb, LANES)
    lr_f, li_f = jnp.broadcast_to(coef[0:1], shape), jnp.broadcast_to(coef[1:2], shape)
    lr_b, li_b = jnp.broadcast_to(coef[2:3], shape), jnp.broadcast_to(coef[3:4], shape)
    zero = jnp.zeros(shape, F32)

    def fwd(c, carry):
        hr, hi = carry
        rows = pl.ds(pl.multiple_of(c * nb, nb), nb)
        hin_ref[rows, 0:LANES] = hr
        hin_ref[rows, LANES:2 * LANES] = hi
        sr, si = s_ref[rows, 0:LANES], s_ref[rows, LANES:2 * LANES]
        return lr_f * hr - li_f * hi + sr, lr_f * hi + li_f * hr + si

    lax.fori_loop(0, nc, fwd, (zero, zero))

    def bwd(q, carry):
        c = jnp.where(q < nc_ctx, nc_ctx - 1 - q, nc + nc_ctx - 1 - q)
        hr, hi = carry
        rows = pl.ds(pl.multiple_of(c * nb, nb), nb)
        hin_ref[rows, 2 * LANES:3 * LANES] = hr
        hin_ref[rows, 3 * LANES:4 * LANES] = hi
        sr, si = s_ref[rows, 2 * LANES:3 * LANES], s_ref[rows, 3 * LANES:4 * LANES]
        return lr_b * hr - li_b * hi + sr, lr_b * hi + li_b * hr + si

    lax.fori_loop(0, nc, bwd, (zero, zero))

    lat = slice(nc_ctx * nb, nc * nb)
    y_ref[0] = (jnp.dot(u[lat], toep_ref[0], preferred_element_type=F32)
                + jnp.dot(hin_ref[lat, :].astype(BF16), q_ref[0], preferred_element_type=F32))


def _s5_scan(u_patch, toep, pmat, qmat, coef, nb, n_ctx):
    ng, nrow, pk = u_patch.shape
    nc = nrow // nb
    nc_ctx = n_ctx // S5_CHUNK
    nlat = (nc - nc_ctx) * nb
    per_g = lambda shape: pl.BlockSpec((1,) + shape, lambda g: (g, 0, 0))
    body = functools.partial(_s5_body, nb=nb, nc_ctx=nc_ctx, nc=nc)
    return pl.pallas_call(
        body,
        grid=(ng,),
        in_specs=[per_g((nrow, pk)), per_g((pk, pk)), per_g((pk, 4 * LANES)), per_g((4 * LANES, pk)),
                  per_g((8, LANES))],
        out_specs=per_g((nlat, pk)),
        out_shape=jax.ShapeDtypeStruct((ng, nlat, pk), F32),
        scratch_shapes=[pltpu.VMEM((nrow, 4 * LANES), F32), pltpu.VMEM((nrow, 4 * LANES), F32)],
        compiler_params=_cparams(("parallel",)),
        name="s5_scan",
    )(u_patch, toep, pmat, qmat, coef)


def _rms(x, g):
    return x * lax.rsqrt(jnp.mean(x * x, axis=-1, keepdims=True) + EPS) * g


def _s5_readout_body(y_ref, g_ref, gw_ref, gb_ref, w_ref, h_ref, mod_ref, nf_ref, o_ref):
    y = jax.nn.gelu(y_ref[0])
    t = y * jax.nn.sigmoid(jnp.dot(y.astype(BF16), gw_ref[...], preferred_element_type=F32) + gb_ref[...])
    z = (t * _silu(g_ref[0].astype(F32))).astype(BF16)
    out = jnp.dot(z, w_ref[...], preferred_element_type=F32)
    h = h_ref[0] + mod_ref[0, :, 2 * D:3 * D] * out
    o_ref[0] = _rms(h, nf_ref[...])


def _s5_readout(y, g, glu_w, glu_b, w_out, h, mod_b, norm_f, n_ctx):
    nb, t_len, _ = y.shape
    tm = ROW_TILE
    off = n_ctx // tm
    lat = lambda wd: pl.BlockSpec((1, tm, wd), lambda b, i: (b, i, 0))
    full = lambda wd: pl.BlockSpec((1, tm, wd), lambda b, i: (b, i + off, 0))
    const = lambda shape: pl.BlockSpec(shape, lambda b, i: (0, 0))
    return pl.pallas_call(
        _s5_readout_body,
        grid=(nb, t_len // tm),
        in_specs=[lat(S5_WIDTH), full(S5_WIDTH), const((S5_WIDTH, S5_WIDTH)), const((1, S5_WIDTH)),
                  const((S5_WIDTH, D)), full(D),
                  pl.BlockSpec((1, 1, 3 * D), lambda b, i: (b, 0, 0)), const((1, D))],
        out_specs=lat(D),
        out_shape=jax.ShapeDtypeStruct((nb, t_len, D), F32),
        compiler_params=_cparams(("parallel", "parallel")),
        name="s5_readout",
    )(y, g, glu_w.astype(BF16), glu_b.reshape(1, S5_WIDTH), w_out.astype(BF16), h, mod_b,
      norm_f.reshape(1, D))


def _s5_layer(h, mod_b, gn, w_in, a_re, a_im, log_dt, b_re, b_im, c_re, c_im, d_skip, glu_w, glu_b,
              w_out, norm_f, n_ctx):
    nb, rows, _ = h.shape
    L = S5_CHUNK
    layout = _bmajor(nb, rows, n_ctx, ROW_TILE)
    _, row, _ = layout
    out_shapes, out_specs = _rows_out((nb, rows), row, [(S5_WIDTH, BF16)] * 2)
    u, g = _in_proj((h,), [row(D)], _load_block, mod_b, gn, w_in.astype(BF16), layout, out_shapes, out_specs,
                    _s5_epilogue, name="s5_in_proj")
    nc = rows // L
    u_patch = (u.reshape(nb, nc, L, S5_GROUPS, S5_GROUP).transpose(3, 1, 0, 2, 4)
               .reshape(S5_GROUPS, nc * nb, L * S5_GROUP))
    toep, pmat, qmat, coef = _s5_matrices(a_re, a_im, log_dt, b_re, b_im, c_re, c_im, d_skip)
    y_patch = _s5_scan(u_patch, toep, pmat, qmat, coef, nb, n_ctx)
    t_len = rows - n_ctx
    y = (y_patch.reshape(S5_GROUPS, t_len // L, nb, L, S5_GROUP).transpose(2, 1, 3, 0, 4)
         .reshape(nb, t_len, S5_WIDTH))
    return _s5_readout(y, g, glu_w, glu_b, w_out, h, mod_b, norm_f, n_ctx)


def kernel(x, c, ctx, c_ctx, ada_w0, ada_b0, norm0, w_in0, conv_w0, conv_b0, lru_wa0, lru_ba0, lru_wx0, lru_bx0, lru_lam0, w_out0, ada_w1, ada_b1, norm1, w_in1, sink1, w_out1, ada_w2, ada_b2, norm2, w_in2, rpb2, w_out2, ada_w3, ada_b3, norm3, w_in3, s5_a_re3, s5_a_im3, s5_log_dt3, s5_b_re3, s5_b_im3, s5_c_re3, s5_c_im3, s5_d3, glu_w3, glu_b3, w_out3, norm_f):
    nb, t_len, _ = x.shape
    n_ctx = ctx.shape[1]
    cvec = jnp.concatenate([c, c_ctx[None, :], jnp.zeros((16 - nb - 1, D), F32)], axis=0)
    mods = [_modulation(cvec, w, b) for w, b in
            ((ada_w0, ada_b0), (ada_w1, ada_b1), (ada_w2, ada_b2), (ada_w3, ada_b3))]
    mod_b = [m[:, None, :] for m in mods]

    mod_t0 = jnp.stack([jnp.broadcast_to(mods[0][nb], (nb, 3 * D)), mods[0][:nb]], axis=0)
    h = _rglru_layer(ctx, x, mod_t0, norm0, w_in0, conv_w0, conv_b0, lru_wa0, lru_ba0, lru_wx0, lru_bx0,
                     lru_lam0, w_out0)
    h = _swa_layer(h, mod_b[1], norm1, w_in1, sink1, w_out1, n_ctx)
    h = _na_layer(h, mod_b[2], norm2, w_in2, rpb2, w_out2, n_ctx)
    return _s5_layer(h, mod_b[3], norm3, w_in3, s5_a_re3, s5_a_im3, s5_log_dt3, s5_b_re3, s5_b_im3,
                     s5_c_re3, s5_c_im3, s5_d3, glu_w3, glu_b3, w_out3, norm_f, n_ctx)
```

```python
import functools
import math

import jax
import jax.numpy as jnp
import numpy as np
from jax import lax
from jax.experimental import pallas as pl
from jax.experimental.pallas import tpu as pltpu

F32 = jnp.float32
BF16 = jnp.bfloat16

D = 1024
HEAD_DIM = 64
GRID_W = 64
EPS = 1e-6
NEG_INF = -1e30
ROPE_BASE = 10000.0
LANES = 128

LRU_WIDTH = 1408
LRU_BLOCKS = 16
LRU_BW = LRU_WIDTH // LRU_BLOCKS
LRU_PW = LRU_BLOCKS * LANES
CONV_W = 4
LRU_C = 8.0
LRU_TT = 32

SWA_HEADS = 16
SWA_KV_HEADS = 4
WINDOW = 128
BLOCK_Q = 128
SWA_CHUNK = 64

NA_HEADS = 16
NA_ROWS = 8
NA_COLS = 16
NA_QROWS = 4
NA_KROWS = 12
NA_HSTEP = 4
NA_MASKED = 2 * NA_ROWS - 1

S5_WIDTH = 1024
S5_GROUP = 16
S5_GROUPS = S5_WIDTH // S5_GROUP
S5_STATE = 64
S5_CHUNK = 16
S5_SLAB = LANES // S5_GROUP

ROW_TILE = 256
VMEM_LIMIT = 56 * 1024 * 1024


def _cparams(sem):
    return pltpu.CompilerParams(dimension_semantics=sem, vmem_limit_bytes=VMEM_LIMIT)


def _silu(v):
    return v * jax.nn.sigmoid(v)


def _bmajor(nb, rows, n_ctx, tm):
    grid = (nb, rows // tm)
    nct = n_ctx // tm

    def row(width, off=0):
        return pl.BlockSpec((1, tm, width), lambda b, i: (b, i + off, 0))

    mod = pl.BlockSpec((1, 1, 3 * D), lambda b, i: (jnp.where(i < nct, nb, b), 0, 0))
    return grid, row, mod


def _tmajor(nb, rows, n_ctx, tt):
    grid = (1, rows // tt)
    nct = n_ctx // tt

    def row(width, off=0):
        return pl.BlockSpec((tt, nb, width), lambda b, i: (i + off, 0, 0))

    mod = pl.BlockSpec((1, nb, 3 * D), lambda b, i: (jnp.where(i < nct, 0, 1), 0, 0))
    return grid, row, mod


def _mod_body(c_ref, w_ref, b_ref, o_ref):
    s = _silu(c_ref[...]).astype(BF16)
    o_ref[...] = jnp.dot(s, w_ref[...].astype(BF16), preferred_element_type=F32) + b_ref[...]


def _modulation(cvec, ada_w, ada_b):
    n = ada_w.shape[1]
    tn = 1024
    return pl.pallas_call(
        _mod_body,
        grid=(n // tn,),
        in_specs=[pl.BlockSpec((16, D), lambda j: (0, 0)),
                  pl.BlockSpec((D, tn), lambda j: (0, j)),
                  pl.BlockSpec((1, tn), lambda j: (0, j))],
        out_specs=pl.BlockSpec((16, tn), lambda j: (0, j)),
        out_shape=jax.ShapeDtypeStruct((16, n), F32),
        compiler_params=_cparams(("parallel",)),
        name="modulation",
    )(cvec, ada_w, ada_b.reshape(1, n))


def _norm_mod(x3, mod3, gn):
    ms = jnp.mean(x3 * x3, axis=-1, keepdims=True)
    y = x3 * lax.rsqrt(ms + EPS) * gn
    return y * (1.0 + mod3[:, :, D:2 * D]) + mod3[:, :, 0:D]


def _in_proj_body(*refs, load, n_x, epilogue, n_extra, chunk):
    xs, (mod_ref, gn_ref, w_ref), rest = refs[:n_x], refs[n_x:n_x + 3], refs[n_x + 3:]
    extra, outs = rest[:n_extra], rest[n_extra:]
    x3 = load(*xs)
    n3 = _norm_mod(x3, mod_ref[...], gn_ref[...])
    lead = x3.shape[:2]
    n = n3.reshape(lead[0] * lead[1], D).astype(BF16)
    for j in range(w_ref.shape[1] // chunk):
        acc = jnp.dot(n, w_ref[:, j * chunk:(j + 1) * chunk], preferred_element_type=F32)
        epilogue(j, acc, lead, extra, outs)


def _load_block(x_ref):
    return x_ref[...]


def _in_proj(xs, x_specs, load, mod, gn, w, layout, out_shapes, out_specs, epilogue, extras=(),
             extra_specs=(), chunk=512, name="in_proj"):
    grid, _, mod_spec = layout
    n = w.shape[1]
    body = functools.partial(_in_proj_body, load=load, n_x=len(xs), epilogue=epilogue,
                             n_extra=len(extras), chunk=chunk)
    return pl.pallas_call(
        body,
        grid=grid,
        in_specs=[*x_specs, mod_spec,
                  pl.BlockSpec((1, D), lambda b, i: (0, 0)),
                  pl.BlockSpec((D, n), lambda b, i: (0, 0)),
                  *extra_specs],
        out_specs=out_specs,
        out_shape=out_shapes,
        compiler_params=_cparams(("parallel", "parallel")),
        name=name,
    )(*xs, mod, gn.reshape(1, D), w, *extras)


def _rows_out(shape2, row, outs):
    return ([jax.ShapeDtypeStruct(shape2 + (wd,), dt) for wd, dt in outs], [row(wd) for wd, _ in outs])


def _put(ref, lo, val, lead):
    ref[:, :, lo:lo + val.shape[1]] = val.reshape(lead + (val.shape[1],)).astype(ref.dtype)


def _out_proj_body(*refs, load, n_h, store):
    (z_ref, w_ref), hs, (mod_ref, o_ref) = refs[:2], refs[2:2 + n_h], refs[2 + n_h:]
    z3 = z_ref[...]
    a, b, wd = z3.shape
    y = jnp.dot(z3.reshape(a * b, wd), w_ref[...], preferred_element_type=F32)
    store(o_ref, load(*hs) + mod_ref[:, :, 2 * D:3 * D] * y.reshape(a, b, D))


def _store_block(o_ref, val):
    o_ref[...] = val


def _out_proj(z, w, h, mod, layout, name):
    grid, row, mod_spec = layout
    wd = w.shape[0]
    body = functools.partial(_out_proj_body, load=_load_block, n_h=1, store=_store_block)
    return pl.pallas_call(
        body,
        grid=grid,
        in_specs=[row(wd), pl.BlockSpec((wd, D), lambda b, i: (0, 0)), row(D), mod_spec],
        out_specs=row(D),
        out_shape=jax.ShapeDtypeStruct(h.shape, F32),
        input_output_aliases={2: 0},
        compiler_params=_cparams(("parallel", "parallel")),
        name=name,
    )(z, w, h, mod)


def _lru_epilogue(j, acc, lead, extra, outs):
    u_ref, g_ref = outs
    lo = j * acc.shape[1]
    if lo < LRU_PW:
        _put(u_ref, lo, acc, lead)
    else:
        _put(g_ref, lo - LRU_PW, acc, lead)


def _lru_tile_index(s, nct, ntiles, reverse):
    if not reverse:
        return s
    return jnp.where(s < nct, nct - 1 - s, ntiles + nct - 1 - s)


def _lru_sweep_body(up_ref, uc_ref, un_ref, cw_ref, cb_ref, wg_ref, bg_ref, lam_ref, *rest,
                    nct, ntiles, reverse):
    if reverse:
        hf_ref, g_ref, o_ref, ext_ref, a_ref, b_ref, carry_ref = rest
    else:
        o_ref, ext_ref, a_ref, b_ref, carry_ref = rest
    s = pl.program_id(0)
    ti = _lru_tile_index(s, nct, ntiles, reverse)
    tt, nb, _ = uc_ref.shape

    @pl.when(s == 0)
    def _():
        carry_ref[...] = jnp.zeros_like(carry_ref)

    seg_start = (ti == 0) | (ti == nct)
    seg_end = (ti == nct - 1) | (ti == ntiles - 1)
    ext_ref[0:2] = jnp.where(seg_start, 0.0, up_ref[...])
    ext_ref[2:2 + tt] = uc_ref[...]
    ext_ref[2 + tt:3 + tt] = jnp.where(seg_end, 0.0, un_ref[...])

    softplus_neg_lam = jax.nn.softplus(-lam_ref[...])
    for k in range(LRU_BLOCKS):
        sl = slice(k * LANES, (k + 1) * LANES)
        u = cb_ref[:, sl]
        for tap in range(CONV_W):
            u = u + cw_ref[tap:tap + 1, sl] * ext_ref[tap:tap + tt, :, sl]
        u = u.reshape(tt * nb, LANES)
        pre = jnp.dot(u.astype(BF16), wg_ref[k], preferred_element_type=F32) + bg_ref[k]
        r = jax.nn.sigmoid(pre[:, :LANES])
        i = jax.nn.sigmoid(pre[:, LANES:])
        a = jnp.exp(-LRU_C * r * softplus_neg_lam[:, sl])
        a_ref[:, sl] = a
        b_ref[:, sl] = jnp.sqrt(1.0 - a * a) * (i * u)

    def step(q, h):
        t = tt - 1 - q if reverse else q
        rows = pl.ds(pl.multiple_of(t * nb, nb), nb)
        h = a_ref[rows, :] * h + b_ref[rows, :]
        if reverse:
            o_ref[t] = ((hf_ref[t] + h) * _silu(g_ref[t].astype(F32))).astype(o_ref.dtype)
        else:
            o_ref[t] = h
        return h

    carry_ref[...] = lax.fori_loop(0, tt, step, carry_ref[...], unroll=4)


def _lru_sweep(u, cw, cb, wg, bg, lam, nct_rows, reverse, hf=None, g=None):
    rows, nb, pw = u.shape
    tt = LRU_TT
    ntiles = rows // tt
    nct = nct_rows // tt
    tile = functools.partial(_lru_tile_index, nct=nct, ntiles=ntiles, reverse=reverse)
    cur = lambda wd: pl.BlockSpec((tt, nb, wd), lambda s: (tile(s), 0, 0))
    const = lambda shape: pl.BlockSpec(shape, lambda s: (0,) * len(shape))
    in_specs = [
        pl.BlockSpec((2, nb, pw), lambda s: (jnp.maximum(tile(s) * (tt // 2) - 1, 0), 0, 0)),
        cur(pw),
        pl.BlockSpec((1, nb, pw), lambda s: (jnp.minimum((tile(s) + 1) * tt, rows - 1), 0, 0)),
        const((CONV_W, pw)), const((1, pw)),
        const((LRU_BLOCKS, LANES, 2 * LANES)), const((LRU_BLOCKS, 1, 2 * LANES)), const((1, pw)),
    ]
    args = [u, u, u, cw, cb, wg, bg, lam]
    if reverse:
        in_specs += [cur(pw), cur(pw)]
        args += [hf, g]
    body = functools.partial(_lru_sweep_body, nct=nct, ntiles=ntiles, reverse=reverse)
    return pl.pallas_call(
        body,
        grid=(ntiles,),
        in_specs=in_specs,
        out_specs=cur(pw),
        out_shape=jax.ShapeDtypeStruct(u.shape, BF16 if reverse else F32),
        scratch_shapes=[pltpu.VMEM((tt + 3, nb, pw), F32),
                        pltpu.VMEM((tt * nb, pw), F32),
                        pltpu.VMEM((tt * nb, pw), F32),
                        pltpu.VMEM((nb, pw), F32)],
        compiler_params=_cparams(("arbitrary",)),
        name="lru_bwd" if reverse else "lru_fwd",
    )(*args)


def _pad_blocks(a, axis):
    axis = axis % a.ndim
    shp = a.shape
    a = a.reshape(shp[:axis] + (LRU_BLOCKS, LRU_BW) + shp[axis + 1:])
    pad = [(0, 0)] * a.ndim
    pad[axis + 1] = (0, LANES - LRU_BW)
    a = jnp.pad(a, pad)
    return a.reshape(shp[:axis] + (LRU_PW,) + shp[axis + 1:])


def _load_time_major(ctx_ref, x_ref, *, nct):
    blk = jnp.where(pl.program_id(1) < nct, ctx_ref[...], x_ref[...])
    return pltpu.einshape("btd->tbd", blk)


def _store_batch_major(o_ref, val):
    o_ref[...] = pltpu.einshape("tbd->btd", val)


def _rglru_layer(ctx, x, mod_t, gn, w_in, conv_w, conv_b, wa, ba, wx, bx, lam, w_out):
    nb, n_ctx, _ = ctx.shape
    rows = n_ctx + x.shape[1]
    tt = LRU_TT
    nct = n_ctx // tt
    layout = _tmajor(nb, rows, n_ctx, tt)
    grid, row, mod_spec = layout
    src_specs = [pl.BlockSpec((nb, tt, D), lambda b, i: (0, jnp.minimum(i, nct - 1), 0)),
                 pl.BlockSpec((nb, tt, D), lambda b, i: (0, jnp.maximum(i - nct, 0), 0))]
    load = functools.partial(_load_time_major, nct=nct)
    w_pad = jnp.concatenate([_pad_blocks(w_in[:, :LRU_WIDTH], 1), _pad_blocks(w_in[:, LRU_WIDTH:], 1)],
                            axis=1).astype(BF16)
    out_shapes, out_specs = _rows_out((rows, nb), row, [(LRU_PW, F32), (LRU_PW, BF16)])
    u, g = _in_proj((ctx, x), src_specs, load, mod_t, gn, w_pad, layout, out_shapes, out_specs,
                    _lru_epilogue, name="lru_in_proj")
    cw = _pad_blocks(conv_w, 1)
    cb = _pad_blocks(conv_b.reshape(1, LRU_WIDTH), 1)
    pad_w = ((0, 0), (0, 0), (0, LANES - LRU_BW), (0, LANES - LRU_BW))
    wg = jnp.concatenate([jnp.pad(wa, pad_w), jnp.pad(wx, pad_w)], axis=-1).astype(BF16)
    pad_b = ((0, 0), (0, 0), (0, LANES - LRU_BW))
    bg = jnp.concatenate([jnp.pad(ba.reshape(2, LRU_BLOCKS, LRU_BW), pad_b),
                          jnp.pad(bx.reshape(2, LRU_BLOCKS, LRU_BW), pad_b)], axis=-1)
    bg = bg.reshape(2, LRU_BLOCKS, 1, 2 * LANES)
    lam_p = _pad_blocks(lam, 1)
    hf = _lru_sweep(u, cw, cb, wg[0], bg[0], lam_p[0:1], n_ctx, reverse=False)
    z = _lru_sweep(u, cw, cb, wg[1], bg[1], lam_p[1:2], n_ctx, reverse=True, hf=hf, g=g)
    body = functools.partial(_out_proj_body, load=load, n_h=2, store=_store_batch_major)
    return pl.pallas_call(
        body,
        grid=grid,
        in_specs=[row(LRU_PW), pl.BlockSpec((LRU_PW, D), lambda b, i: (0, 0)), *src_specs, mod_spec],
        out_specs=pl.BlockSpec((nb, tt, D), lambda b, i: (0, i, 0)),
        out_shape=jax.ShapeDtypeStruct((nb, rows, D), F32),
        compiler_params=_cparams(("parallel", "parallel")),
        name="lru_out_proj",
    )(z, _pad_blocks(w_out, 0).astype(BF16), ctx, x, mod_t)


def _rope_tables(n_ctx, t_len):
    pos = jnp.arange(t_len)
    row = (pos // GRID_W).astype(F32)
    col = (pos % GRID_W).astype(F32)
    n_ax = HEAD_DIM // 4
    freqs = ROPE_BASE ** (-jnp.arange(n_ax, dtype=F32) / n_ax)
    ang = jnp.concatenate([row[:, None] * freqs, col[:, None] * freqs], axis=-1)
    cos, sin = jnp.cos(ang), jnp.sin(ang)
    cos_h = jnp.concatenate([cos, cos], axis=-1)
    sin_h = jnp.concatenate([-sin, sin], axis=-1)
    cos_f = jnp.concatenate([jnp.ones((n_ctx, HEAD_DIM), F32), cos_h], axis=0)
    sin_f = jnp.concatenate([jnp.zeros((n_ctx, HEAD_DIM), F32), sin_h], axis=0)
    return jnp.tile(cos_f, (1, 2)), jnp.tile(sin_f, (1, 2))


def _rope(x, cos, sin):
    lane = lax.broadcasted_iota(jnp.int32, x.shape, 1)
    swapped = jnp.where(lane % HEAD_DIM < HEAD_DIM // 2,
                        pltpu.roll(x, LANES - HEAD_DIM // 2, 1), pltpu.roll(x, HEAD_DIM // 2, 1))
    return x * cos + swapped * sin


def _swa_epilogue(j, acc, lead, extra, outs):
    cos_ref, sin_ref = extra
    q_ref, qr_ref, k_ref, v_ref, g_ref = outs
    qd = SWA_HEADS * HEAD_DIM
    kvd = SWA_KV_HEADS * HEAD_DIM
    grp = SWA_HEADS // SWA_KV_HEADS
    width = acc.shape[1]
    lo = j * width
    cos, sin = cos_ref[...], sin_ref[...]

    def roped(x):
        return jnp.concatenate([_rope(x[:, s:s + LANES], cos, sin) for s in range(0, x.shape[1], LANES)], axis=1)

    def head(x, hh):
        return x[:, hh * HEAD_DIM:(hh + 1) * HEAD_DIM]

    if lo < qd:
        q = acc * (HEAD_DIM ** -0.5)
        qr = roped(q)
        for hh in range(width // HEAD_DIM):
            kv, gq = divmod(lo // HEAD_DIM + hh, grp)
            for rt in range(lead[1] // BLOCK_Q):
                rows = slice(rt * BLOCK_Q, (rt + 1) * BLOCK_Q)
                dst = slice(gq * BLOCK_Q, (gq + 1) * BLOCK_Q)
                q_ref[0, kv, rt, dst, :] = head(q, hh)[rows].astype(q_ref.dtype)
                qr_ref[0, kv, rt, dst, :] = head(qr, hh)[rows].astype(qr_ref.dtype)
    elif lo == qd:
        kr = roped(acc[:, :kvd])
        for kv in range(SWA_KV_HEADS):
            k_ref[0, kv] = head(kr, kv).astype(k_ref.dtype)
            v_ref[0, kv] = head(acc[:, kvd:], kv).astype(v_ref.dtype)
    else:
        _put(g_ref, lo - qd - 2 * kvd, acc, lead)


def _swa_mask_table():
    qi = np.arange(BLOCK_Q)[:, None]
    kj = np.arange(3 * BLOCK_Q)[None, :]
    base = np.where(np.abs(kj - BLOCK_Q - qi) <= WINDOW, 0.0, NEG_INF).astype(np.float32)
    first = np.where(kj < BLOCK_Q, NEG_INF, base).astype(np.float32)
    last = np.where(kj >= 2 * BLOCK_Q, NEG_INF, base).astype(np.float32)
    return jnp.asarray(np.stack([np.full_like(base, NEG_INF), first, base, last]))


def _swa_body(sink_ref, mask_ref, q_ref, qr_ref, kc_ref, k0_ref, k1_ref, k2_ref, vc_ref, v0_ref, v1_ref,
              v2_ref, g_ref, o_ref, s_ref, p_ref, l_ref):
    grp = SWA_HEADS // SWA_KV_HEADS
    dims = (((1,), (1,)), ((), ()))
    nc = kc_ref.shape[2]
    for kv in range(SWA_KV_HEADS):
        slot = kv % 2
        kb = jnp.concatenate([k0_ref[0, kv], k1_ref[0, kv], k2_ref[0, kv]], axis=0)
        vb = jnp.concatenate([v0_ref[0, kv], v1_ref[0, kv], v2_ref[0, kv]], axis=0)
        s_ref[slot, :, :nc] = lax.dot_general(q_ref[0, kv, 0], kc_ref[0, kv], dims, preferred_element_type=F32)
        s_ref[slot, :, nc:] = lax.dot_general(qr_ref[0, kv, 0], kb, dims, preferred_element_type=F32)
        for c in range(grp * BLOCK_Q // SWA_CHUNK):
            rows = slice(c * SWA_CHUNK, (c + 1) * SWA_CHUNK)
            mrow = (c * SWA_CHUNK) % BLOCK_Q
            sink = sink_ref[kv * grp + (c * SWA_CHUNK) // BLOCK_Q]
            s_c = s_ref[slot, rows, :nc]
            s_b = s_ref[slot, rows, nc:] + mask_ref[0, mrow:mrow + SWA_CHUNK, :]
            m = jnp.maximum(jnp.maximum(jnp.max(s_c, axis=1, keepdims=True),
                                        jnp.max(s_b, axis=1, keepdims=True)), sink)
            p_c = jnp.exp(s_c - m)
            p_b = jnp.exp(s_b - m)
            l_ref[slot, rows, :] = (jnp.sum(p_c, axis=1, keepdims=True) + jnp.sum(p_b, axis=1, keepdims=True)
                                    + jnp.exp(sink - m))
            p_ref[slot, rows, :nc] = p_c.astype(BF16)
            p_ref[slot, rows, nc:] = p_b.astype(BF16)
        o = (jnp.dot(p_ref[slot, :, :nc], vc_ref[0, kv], preferred_element_type=F32)
             + jnp.dot(p_ref[slot, :, nc:], vb, preferred_element_type=F32)) / l_ref[slot]
        slab = jnp.concatenate([o[gq * BLOCK_Q:(gq + 1) * BLOCK_Q] for gq in range(grp)], axis=1)
        lanes = slice(kv * grp * HEAD_DIM, (kv + 1) * grp * HEAD_DIM)
        o_ref[0, :, lanes] = (slab * _silu(g_ref[0, :, lanes].astype(F32))).astype(o_ref.dtype)


def _swa_attention(q, qr, k, v, g, sink, n_ctx):
    nb, nkv, nt, qrows, hd = q.shape
    rows = k.shape[2]
    nct = n_ctx // BLOCK_Q
    qspec = pl.BlockSpec((1, nkv, 1, qrows, hd), lambda b, n: (b, 0, n, 0, 0))
    ctx = pl.BlockSpec((1, nkv, n_ctx, hd), lambda b, n: (b, 0, 0, 0))
    rowspec = pl.BlockSpec((1, BLOCK_Q, g.shape[2]), lambda b, n: (b, n, 0))

    def band(off):
        return pl.BlockSpec((1, nkv, BLOCK_Q, hd), lambda b, n: (b, 0, jnp.clip(n + off, nct, nt - 1), 0))

    def variant(b, n):
        return (jnp.where(n < nct, 0, jnp.where(n == nct, 1, jnp.where(n == nt - 1, 3, 2))), 0, 0)

    return pl.pallas_call(
        _swa_body,
        grid=(nb, nt),
        in_specs=[pl.BlockSpec(memory_space=pltpu.SMEM),
                  pl.BlockSpec((1, BLOCK_Q, 3 * BLOCK_Q), variant), qspec, qspec,
                  ctx, band(-1), band(0), band(1), ctx, band(-1), band(0), band(1), rowspec],
        out_specs=rowspec,
        out_shape=jax.ShapeDtypeStruct((nb, rows, g.shape[2]), BF16),
        scratch_shapes=[pltpu.VMEM((2, qrows, n_ctx + 3 * BLOCK_Q), F32),
                        pltpu.VMEM((2, qrows, n_ctx + 3 * BLOCK_Q), BF16),
                        pltpu.VMEM((2, qrows, 1), F32)],
        compiler_params=_cparams(("parallel", "parallel")),
        name="swa_attention",
    )(sink, _swa_mask_table(), q, qr, k, k, k, k, v, v, v, v, g)


def _swa_layer(h, mod_b, gn, w_in, sink, w_out, n_ctx):
    nb, rows, _ = h.shape
    tm = ROW_TILE
    layout = _bmajor(nb, rows, n_ctx, tm)
    _, row, _ = layout
    qd = SWA_HEADS * HEAD_DIM
    grp = SWA_HEADS // SWA_KV_HEADS
    nkv = SWA_KV_HEADS
    cos, sin = _rope_tables(n_ctx, rows - n_ctx)
    tab = pl.BlockSpec((tm, LANES), lambda b, i: (i, 0))
    q_shape = jax.ShapeDtypeStruct((nb, nkv, rows // BLOCK_Q, grp * BLOCK_Q, HEAD_DIM), BF16)
    q_spec = pl.BlockSpec((1, nkv, tm // BLOCK_Q, grp * BLOCK_Q, HEAD_DIM), lambda b, i: (b, 0, i, 0, 0))
    kv_shape = jax.ShapeDtypeStruct((nb, nkv, rows, HEAD_DIM), BF16)
    kv_spec = pl.BlockSpec((1, nkv, tm, HEAD_DIM), lambda b, i: (b, 0, i, 0))
    q, qr, k, v, g = _in_proj(
        (h,), [row(D)], _load_block, mod_b, gn, w_in.astype(BF16), layout,
        [q_shape, q_shape, kv_shape, kv_shape, jax.ShapeDtypeStruct((nb, rows, qd), BF16)],
        [q_spec, q_spec, kv_spec, kv_spec, row(qd)], _swa_epilogue,
        extras=(cos, sin), extra_specs=(tab, tab), name="swa_in_proj")
    z = _swa_attention(q, qr, k, v, g, sink, n_ctx)
    return _out_proj(z, w_out.astype(BF16), h, mod_b, layout, name="swa_out_proj")


def _na_epilogue(j, acc, lead, extra, outs):
    q_ref, k_ref, v_ref, g_ref = outs
    wd = NA_HEADS * HEAD_DIM
    lo = j * acc.shape[1]
    which, off = lo // wd, lo % wd
    if which == 0:
        _put(q_ref, off, acc * (HEAD_DIM ** -0.5), lead)
    else:
        _put((k_ref, v_ref, g_ref)[which - 1], off, acc, lead)


def _na_bias_tiles(rpb):
    reach = GRID_W - NA_COLS
    period = 2 * GRID_W
    ndy = 2 * NA_ROWS - 1
    edge = jnp.pad(rpb.astype(F32), ((0, 0), (0, 0), (reach, reach)), mode="edge")
    flat = jnp.tile(jnp.pad(edge, ((0, 0), (0, 0), (0, 1))), (1, 1, GRID_W))[:, :, :GRID_W * (period - 1)]
    e = flat.reshape(NA_HEADS, ndy, GRID_W, period - 1)[:, :, :, GRID_W - 1:]
    qcol = np.arange(GRID_W)
    cstart = np.clip(qcol - NA_COLS // 2, 0, GRID_W - NA_COLS)
    col_ok = (qcol[None, :] >= cstart[:, None]) & (qcol[None, :] < cstart[:, None] + NA_COLS)
    e = jnp.where(jnp.asarray(col_ok)[None, None], e, NEG_INF)
    e = jnp.pad(e, ((0, 0), (0, 1), (0, 0), (0, 0)), constant_values=NEG_INF)
    shape = (NA_HEADS // NA_HSTEP, NA_HSTEP, ndy + 1, GRID_W, period)
    left = jnp.pad(e, ((0, 0), (0, 0), (0, 0), (0, GRID_W)), constant_values=NEG_INF).reshape(shape)
    right = jnp.pad(e, ((0, 0), (0, 0), (0, 0), (GRID_W, 0)), constant_values=NEG_INF).reshape(shape)
    return left, right


def _na_body(q_ref, k_ref, v_ref, g_ref, bl_ref, br_ref, o_ref, *, n_ctx, grid_rows):
    rb = pl.program_id(2)
    r0 = (rb - 1) * NA_QROWS
    kb = jnp.clip(r0 - NA_ROWS // 2, 0, grid_rows - NA_KROWS)
    start = pl.multiple_of(n_ctx + kb * GRID_W, GRID_W)
    nk = NA_KROWS * GRID_W
    q = q_ref[0]
    kc, vc = k_ref[0, 0:n_ctx, :], v_ref[0, 0:n_ctx, :]
    kn, vn = k_ref[0, pl.ds(start, nk), :], v_ref[0, pl.ds(start, nk), :]
    dims = (((1,), (1,)), ((), ()))

    def tile_index(qr, kr):
        qrow, krow = r0 + qr, kb + kr
        first = jnp.clip(qrow - NA_ROWS // 2, 0, grid_rows - NA_ROWS)
        inside = (rb >= 1) & (krow >= first) & (krow < first + NA_ROWS)
        return jnp.where(inside, krow - qrow + NA_ROWS - 1, NA_MASKED)

    idx = [[tile_index(qr, kr) for kr in range(NA_KROWS)] for qr in range(NA_QROWS)]
    left_half = lax.broadcasted_iota(jnp.int32, (GRID_W, 2 * GRID_W), 1) < GRID_W
    for hl in range(NA_HSTEP):
        hs = slice(hl * HEAD_DIM, (hl + 1) * HEAD_DIM)
        bias = jnp.concatenate(
            [jnp.concatenate([jnp.where(left_half, bl_ref[0, hl, idx[qr][2 * kp]], br_ref[0, hl, idx[qr][2 * kp + 1]])
                              for kp in range(NA_KROWS // 2)], axis=1) for qr in range(NA_QROWS)], axis=0)
        s_c = lax.dot_general(q[:, hs], kc[:, hs], dims, preferred_element_type=F32)
        s_n = lax.dot_general(q[:, hs], kn[:, hs], dims, preferred_element_type=F32) + bias
        m = jnp.maximum(jnp.max(s_c, axis=1, keepdims=True), jnp.max(s_n, axis=1, keepdims=True))
        p_c = jnp.exp(s_c - m)
        p_n = jnp.exp(s_n - m)
        denom = jnp.sum(p_c, axis=1, keepdims=True) + jnp.sum(p_n, axis=1, keepdims=True)
        o = (jnp.dot(p_c.astype(BF16), vc[:, hs], preferred_element_type=F32)
             + jnp.dot(p_n.astype(BF16), vn[:, hs], preferred_element_type=F32)) / denom
        gate = g_ref[0, :, hs].astype(F32)
        o_ref[0, :, hs] = (o * _silu(gate)).astype(o_ref.dtype)


def _na_attention(q, k, v, g, bias_left, bias_right, n_ctx):
    nb, rows, wd = q.shape
    grid_rows = (rows - n_ctx) // GRID_W
    nq = NA_QROWS * GRID_W
    assert n_ctx == nq, "context tokens form exactly one query block"
    lanes = NA_HSTEP * HEAD_DIM
    qspec = pl.BlockSpec((1, nq, lanes), lambda b, p, r: (b, r, p))
    kvspec = pl.BlockSpec((1, rows, lanes), lambda b, p, r: (b, 0, p))
    bspec = pl.BlockSpec((1,) + bias_left.shape[1:], lambda b, p, r: (p, 0, 0, 0, 0))
    body = functools.partial(_na_body, n_ctx=n_ctx, grid_rows=grid_rows)
    return pl.pallas_call(
        body,
        grid=(nb, wd // lanes, rows // nq),
        in_specs=[qspec, kvspec, kvspec, qspec, bspec, bspec],
        out_specs=qspec,
        out_shape=jax.ShapeDtypeStruct(q.shape, BF16),
        compiler_params=_cparams(("parallel", "parallel", "parallel")),
        name="na_attention",
    )(q, k, v, g, bias_left, bias_right)


def _na_layer(h, mod_b, gn, w_in, rpb, w_out, n_ctx):
    nb, rows, _ = h.shape
    layout = _bmajor(nb, rows, n_ctx, ROW_TILE)
    _, row, _ = layout
    wd = NA_HEADS * HEAD_DIM
    out_shapes, out_specs = _rows_out((nb, rows), row, [(wd, BF16)] * 4)
    q, k, v, g = _in_proj((h,), [row(D)], _load_block, mod_b, gn, w_in.astype(BF16), layout,
                          out_shapes, out_specs, _na_epilogue, name="na_in_proj")
    bias_left, bias_right = _na_bias_tiles(rpb)
    z = _na_attention(q, k, v, g, bias_left, bias_right, n_ctx)
    return _out_proj(z, w_out.astype(BF16), h, mod_b, layout, name="na_out_proj")


def _s5_epilogue(j, acc, lead, extra, outs):
    u_ref, g_ref = outs
    lo = j * acc.shape[1]
    if lo < S5_WIDTH:
        _put(u_ref, lo, acc, lead)
    else:
        _put(g_ref, lo - S5_WIDTH, acc, lead)


def _s5_matrices(a_re, a_im, log_dt, b_re, b_im, c_re, c_im, d_skip):
    L = S5_CHUNK
    lam = lax.complex(a_re.astype(F32), a_im.astype(F32))
    lam_dt = lam * jnp.exp(log_dt.astype(F32))[..., None]
    lam_bar = jnp.exp(lam_dt)
    b_bar = ((lam_bar - 1.0) / lam)[..., None] * lax.complex(b_re.astype(F32), b_im.astype(F32))
    c_mat = lax.complex(c_re.astype(F32), c_im.astype(F32))
    tau = np.arange(L)

    def power(expo, d):
        return jnp.exp(jnp.asarray(expo, F32).reshape(expo.shape + (1, 1)) * lam_dt[d])

    def in_chunk(d, lag):
        pw = jnp.where(jnp.asarray(lag >= 0)[:, :, None, None], power(np.abs(lag), d), 0.0)
        return jnp.real(jnp.einsum('gip,tugp,gpj->tugij', c_mat[d], pw, b_bar[d], precision=lax.Precision.HIGHEST))

    lag = tau[None, :] - tau[:, None]
    both = in_chunk(0, lag) + in_chunk(1, -lag)
    eye_t = jnp.eye(L, dtype=F32)[:, :, None, None, None]
    skip = jnp.eye(S5_GROUP, dtype=F32)[None] * d_skip.astype(F32).reshape(S5_GROUPS, S5_GROUP, 1)
    both = both + eye_t * skip[None, None]
    toep = both.transpose(2, 0, 4, 1, 3).reshape(S5_GROUPS, L * S5_GROUP, L * S5_GROUP)

    def lanes(z):
        pad = [(0, 0)] * (z.ndim - 1) + [(0, LANES - S5_STATE)]
        return jnp.concatenate([jnp.pad(jnp.real(z), pad), jnp.pad(jnp.imag(z), pad)], axis=-1)

    pf = power(L - 1 - tau, 0)[:, :, :, None] * b_bar[0][None]
    pb = power(tau, 1)[:, :, :, None] * b_bar[1][None]
    pmat = jnp.concatenate([lanes(pf.transpose(1, 0, 3, 2)), lanes(pb.transpose(1, 0, 3, 2))], axis=-1)
    pmat = pmat.reshape(S5_GROUPS, L * S5_GROUP, 4 * LANES)
    qf = c_mat[0][None] * power(tau + 1, 0)[:, :, None, :]
    qb = c_mat[1][None] * power(L - tau, 1)[:, :, None, :]

    def state_rows(z):
        pad = ((0, 0), (0, LANES - S5_STATE), (0, 0), (0, 0))
        zr = jnp.pad(jnp.real(z).transpose(1, 3, 0, 2), pad)
        zi = jnp.pad(-jnp.imag(z).transpose(1, 3, 0, 2), pad)
        return jnp.concatenate([zr, zi], axis=1).reshape(S5_GROUPS, 2 * LANES, L * S5_GROUP)

    qmat = jnp.concatenate([state_rows(qf), state_rows(qb)], axis=1)
    lam_l = jnp.exp(float(L) * lam_dt)
    pad = ((0, 0), (0, LANES - S5_STATE))
    coef = jnp.stack([jnp.pad(jnp.real(lam_l[0]), pad), jnp.pad(jnp.imag(lam_l[0]), pad),
                      jnp.pad(jnp.real(lam_l[1]), pad), jnp.pad(jnp.imag(lam_l[1]), pad)], axis=1)
    coef = jnp.pad(coef, ((0, 0), (0, 4), (0, 0)))
    return toep.astype(BF16), pmat.astype(BF16), qmat.astype(BF16), coef


def _s5_body(u_ref, toep_ref, p_ref, q_ref, coef_ref, y_ref, s_ref, hin_ref, *, nb, nc_ctx, nc):
    for gl in range(S5_SLAB):
        _s5_group(gl, u_ref, toep_ref, p_ref, q_ref, coef_ref, y_ref, s_ref, hin_ref, nb, nc_ctx, nc)


def _s5_group(gl, u_ref, toep_ref, p_ref, q_ref, coef_ref, y_ref, s_ref, hin_ref, nb, nc_ctx, nc):
    L = S5_CHUNK
    nwin = LANES // S5_GROUP
    nrow = u_ref.shape[1]
    lane = lax.broadcasted_iota(jnp.int32, (nrow // 2, LANES), 1)
    halves = []
    for half in range(L // nwin):
        acc = None
        for w in range(nwin):
            x = pltpu.bitcast(u_ref[half * nwin + w], jnp.uint32)
            shift = ((w - gl) * S5_GROUP) % LANES
            x = pltpu.roll(x, shift, 1) if shift else x
            acc = x if acc is None else jnp.where((lane >= w * S5_GROUP) & (lane < (w + 1) * S5_GROUP), x, acc)
        halves.append(pltpu.bitcast(acc, BF16))
    u = jnp.concatenate(halves, axis=1)
    s_ref[...] = jnp.dot(u, p_ref[gl], preferred_element_type=F32)
    coef = coef_ref[gl]
    shape = (nb, LANES)
    lr_f, li_f = jnp.broadcast_to(coef[0:1], shape), jnp.broadcast_to(coef[1:2], shape)
    lr_b, li_b = jnp.broadcast_to(coef[2:3], shape), jnp.broadcast_to(coef[3:4], shape)
    zero = jnp.zeros(shape, F32)

    def fwd(c, carry):
        hr, hi = carry
        rows = pl.ds(pl.multiple_of(c * nb, nb), nb)
        hin_ref[rows, 0:LANES] = hr
        hin_ref[rows, LANES:2 * LANES] = hi
        sr, si = s_ref[rows, 0:LANES], s_ref[rows, LANES:2 * LANES]
        return lr_f * hr - li_f * hi + sr, lr_f * hi + li_f * hr + si

    lax.fori_loop(0, nc, fwd, (zero, zero))

    def bwd(q, carry):
        c = jnp.where(q < nc_ctx, nc_ctx - 1 - q, nc + nc_ctx - 1 - q)
        hr, hi = carry
        rows = pl.ds(pl.multiple_of(c * nb, nb), nb)
        hin_ref[rows, 2 * LANES:3 * LANES] = hr
        hin_ref[rows, 3 * LANES:4 * LANES] = hi
        sr, si = s_ref[rows, 2 * LANES:3 * LANES], s_ref[rows, 3 * LANES:4 * LANES]
        return lr_b * hr - li_b * hi + sr, lr_b * hi + li_b * hr + si

    lax.fori_loop(0, nc, bwd, (zero, zero))

    lat = slice(nc_ctx * nb, nc * nb)
    y = (jnp.dot(u[lat], toep_ref[gl], preferred_element_type=F32)
         + jnp.dot(hin_ref[lat, :].astype(BF16), q_ref[gl], preferred_element_type=F32))
    mine = slice(gl * S5_GROUP, (gl + 1) * S5_GROUP)
    for t in range(L):
        half, w = divmod(t, nwin)
        x = y[:, half * LANES:(half + 1) * LANES]
        shift = ((gl - w) * S5_GROUP) % LANES
        x = pltpu.roll(x, shift, 1) if shift else x
        y_ref[t, :, mine] = x[:, mine]


def _s5_scan(u_t, toep, pmat, qmat, coef, nb, n_ctx):
    L, nrow, wd = u_t.shape
    pk = L * S5_GROUP
    nc = nrow // nb
    nc_ctx = n_ctx // L
    nlat = (nc - nc_ctx) * nb
    slab = lambda shape: pl.BlockSpec((S5_SLAB,) + shape, lambda s: (s, 0, 0))
    body = functools.partial(_s5_body, nb=nb, nc_ctx=nc_ctx, nc=nc)
    return pl.pallas_call(
        body,
        grid=(wd // LANES,),
        in_specs=[pl.BlockSpec((L, nrow, LANES), lambda s: (0, 0, s)),
                  slab((pk, pk)), slab((pk, 4 * LANES)), slab((4 * LANES, pk)), slab((8, LANES))],
        out_specs=pl.BlockSpec((L, nlat, LANES), lambda s: (0, 0, s)),
        out_shape=jax.ShapeDtypeStruct((L, nlat, wd), F32),
        scratch_shapes=[pltpu.VMEM((nrow, 4 * LANES), F32), pltpu.VMEM((nrow, 4 * LANES), F32)],
        compiler_params=_cparams(("parallel",)),
        name="s5_scan",
    )(u_t, toep, pmat, qmat, coef)


def _rms(x, g):
    return x * lax.rsqrt(jnp.mean(x * x, axis=-1, keepdims=True) + EPS) * g


def _s5_readout_body(y_ref, g_ref, gw_ref, gb_ref, w_ref, h_ref, mod_ref, nf_ref, o_ref):
    y = jax.nn.gelu(y_ref[0])
    t = y * jax.nn.sigmoid(jnp.dot(y.astype(BF16), gw_ref[...], preferred_element_type=F32) + gb_ref[...])
    z = (t * _silu(g_ref[0].astype(F32))).astype(BF16)
    out = jnp.dot(z, w_ref[...], preferred_element_type=F32)
    h = h_ref[0] + mod_ref[0, :, 2 * D:3 * D] * out
    o_ref[0] = _rms(h, nf_ref[...])


def _s5_readout(y, g, glu_w, glu_b, w_out, h, mod_b, norm_f, n_ctx):
    nb, t_len, _ = y.shape
    tm = ROW_TILE
    off = n_ctx // tm
    lat = lambda wd: pl.BlockSpec((1, tm, wd), lambda b, i: (b, i, 0))
    full = lambda wd: pl.BlockSpec((1, tm, wd), lambda b, i: (b, i + off, 0))
    const = lambda shape: pl.BlockSpec(shape, lambda b, i: (0, 0))
    return pl.pallas_call(
        _s5_readout_body,
        grid=(nb, t_len // tm),
        in_specs=[lat(S5_WIDTH), full(S5_WIDTH), const((S5_WIDTH, S5_WIDTH)), const((1, S5_WIDTH)),
                  const((S5_WIDTH, D)), full(D),
                  pl.BlockSpec((1, 1, 3 * D), lambda b, i: (b, 0, 0)), const((1, D))],
        out_specs=lat(D),
        out_shape=jax.ShapeDtypeStruct((nb, t_len, D), F32),
        compiler_params=_cparams(("parallel", "parallel")),
        name="s5_readout",
    )(y, g, glu_w.astype(BF16), glu_b.reshape(1, S5_WIDTH), w_out.astype(BF16), h, mod_b,
      norm_f.reshape(1, D))


def _s5_layer(h, mod_b, gn, w_in, a_re, a_im, log_dt, b_re, b_im, c_re, c_im, d_skip, glu_w, glu_b,
              w_out, norm_f, n_ctx):
    nb, rows, _ = h.shape
    L = S5_CHUNK
    layout = _bmajor(nb, rows, n_ctx, ROW_TILE)
    _, row, _ = layout
    out_shapes, out_specs = _rows_out((nb, rows), row, [(S5_WIDTH, BF16)] * 2)
    u, g = _in_proj((h,), [row(D)], _load_block, mod_b, gn, w_in.astype(BF16), layout, out_shapes, out_specs,
                    _s5_epilogue, name="s5_in_proj")
    nc = rows // L
    u_t = u.reshape(nb, nc, L, S5_WIDTH).transpose(2, 1, 0, 3).reshape(L, nc * nb, S5_WIDTH)
    toep, pmat, qmat, coef = _s5_matrices(a_re, a_im, log_dt, b_re, b_im, c_re, c_im, d_skip)
    y_t = _s5_scan(u_t, toep, pmat, qmat, coef, nb, n_ctx)
    t_len = rows - n_ctx
    y = y_t.reshape(L, t_len // L, nb, S5_WIDTH).transpose(2, 1, 0, 3).reshape(nb, t_len, S5_WIDTH)
    return _s5_readout(y, g, glu_w, glu_b, w_out, h, mod_b, norm_f, n_ctx)


def kernel(x, c, ctx, c_ctx, ada_w0, ada_b0, norm0, w_in0, conv_w0, conv_b0, lru_wa0, lru_ba0, lru_wx0, lru_bx0, lru_lam0, w_out0, ada_w1, ada_b1, norm1, w_in1, sink1, w_out1, ada_w2, ada_b2, norm2, w_in2, rpb2, w_out2, ada_w3, ada_b3, norm3, w_in3, s5_a_re3, s5_a_im3, s5_log_dt3, s5_b_re3, s5_b_im3, s5_c_re3, s5_c_im3, s5_d3, glu_w3, glu_b3, w_out3, norm_f):
    nb, t_len, _ = x.shape
    n_ctx = ctx.shape[1]
    cvec = jnp.concatenate([c, c_ctx[None, :], jnp.zeros((16 - nb - 1, D), F32)], axis=0)
    mods = [_modulation(cvec, w, b) for w, b in
            ((ada_w0, ada_b0), (ada_w1, ada_b1), (ada_w2, ada_b2), (ada_w3, ada_b3))]
    mod_b = [m[:, None, :] for m in mods]

    mod_t0 = jnp.stack([jnp.broadcast_to(mods[0][nb], (nb, 3 * D)), mods[0][:nb]], axis=0)
    h = _rglru_layer(ctx, x, mod_t0, norm0, w_in0, conv_w0, conv_b0, lru_wa0, lru_ba0, lru_wx0, lru_bx0,
                     lru_lam0, w_out0)
    h = _swa_layer(h, mod_b[1], norm1, w_in1, sink1, w_out1, n_ctx)
    h = _na_layer(h, mod_b[2], norm2, w_in2, rpb2, w_out2, n_ctx)
    return _s5_layer(h, mod_b[3], norm3, w_in3, s5_a_re3, s5_a_im3, s5_log_dt3, s5_b_re3, s5_b_im3,
                     s5_c_re3, s5_c_im3, s5_d3, glu_w3, glu_b3, w_out3, norm_f, n_ctx)
```

```python
import functools
import math

import jax
import jax.numpy as jnp
import numpy as np
from jax import lax
from jax.experimental import pallas as pl
from jax.experimental.pallas import tpu as pltpu

F32 = jnp.float32
BF16 = jnp.bfloat16

D = 1024
HEAD_DIM = 64
GRID_W = 64
EPS = 1e-6
NEG_INF = -1e30
ROPE_BASE = 10000.0
LANES = 128

LRU_WIDTH = 1408
LRU_BLOCKS = 16
LRU_BW = LRU_WIDTH // LRU_BLOCKS
LRU_PW = LRU_BLOCKS * LANES
CONV_W = 4
LRU_C = 8.0
LRU_TT = 32

SWA_HEADS = 16
SWA_KV_HEADS = 4
WINDOW = 128
BLOCK_Q = 128

NA_HEADS = 16
NA_ROWS = 8
NA_COLS = 16
NA_QROWS = 4
NA_KROWS = 12
NA_HSTEP = 4
NA_MASKED = 2 * NA_ROWS - 1

S5_WIDTH = 1024
S5_GROUP = 16
S5_GROUPS = S5_WIDTH // S5_GROUP
S5_STATE = 64
S5_CHUNK = 16
S5_SLAB = LANES // S5_GROUP

ROW_TILE = 256
VMEM_LIMIT = 56 * 1024 * 1024


def _cparams(sem):
    return pltpu.CompilerParams(dimension_semantics=sem, vmem_limit_bytes=VMEM_LIMIT)


def _silu(v):
    return v * jax.nn.sigmoid(v)


def _bmajor(nb, rows, n_ctx, tm):
    grid = (nb, rows // tm)
    nct = n_ctx // tm

    def row(width, off=0):
        return pl.BlockSpec((1, tm, width), lambda b, i: (b, i + off, 0))

    mod = pl.BlockSpec((1, 1, 3 * D), lambda b, i: (jnp.where(i < nct, nb, b), 0, 0))
    return grid, row, mod


def _tmajor(nb, rows, n_ctx, tt):
    grid = (1, rows // tt)
    nct = n_ctx // tt

    def row(width, off=0):
        return pl.BlockSpec((tt, nb, width), lambda b, i: (i + off, 0, 0))

    mod = pl.BlockSpec((1, nb, 3 * D), lambda b, i: (jnp.where(i < nct, 0, 1), 0, 0))
    return grid, row, mod


def _mod_body(c_ref, w_ref, b_ref, o_ref):
    s = _silu(c_ref[...]).astype(BF16)
    o_ref[...] = jnp.dot(s, w_ref[...].astype(BF16), preferred_element_type=F32) + b_ref[...]


def _modulation(cvec, ada_w, ada_b):
    n = ada_w.shape[1]
    tn = 1024
    return pl.pallas_call(
        _mod_body,
        grid=(n // tn,),
        in_specs=[pl.BlockSpec((16, D), lambda j: (0, 0)),
                  pl.BlockSpec((D, tn), lambda j: (0, j)),
                  pl.BlockSpec((1, tn), lambda j: (0, j))],
        out_specs=pl.BlockSpec((16, tn), lambda j: (0, j)),
        out_shape=jax.ShapeDtypeStruct((16, n), F32),
        compiler_params=_cparams(("parallel",)),
        name="modulation",
    )(cvec, ada_w, ada_b.reshape(1, n))


def _norm_mod(x3, mod3, gn):
    ms = jnp.mean(x3 * x3, axis=-1, keepdims=True)
    y = x3 * lax.rsqrt(ms + EPS) * gn
    return y * (1.0 + mod3[:, :, D:2 * D]) + mod3[:, :, 0:D]


def _in_proj_body(*refs, load, n_x, epilogue, n_extra, chunk):
    xs, (mod_ref, gn_ref, w_ref), rest = refs[:n_x], refs[n_x:n_x + 3], refs[n_x + 3:]
    extra, outs = rest[:n_extra], rest[n_extra:]
    x3 = load(*xs)
    n3 = _norm_mod(x3, mod_ref[...], gn_ref[...])
    lead = x3.shape[:2]
    n = n3.reshape(lead[0] * lead[1], D).astype(BF16)
    for j in range(w_ref.shape[1] // chunk):
        acc = jnp.dot(n, w_ref[:, j * chunk:(j + 1) * chunk], preferred_element_type=F32)
        epilogue(j, acc, lead, extra, outs)


def _load_block(x_ref):
    return x_ref[...]


def _in_proj(xs, x_specs, load, mod, gn, w, layout, out_shapes, out_specs, epilogue, extras=(),
             extra_specs=(), chunk=512, name="in_proj"):
    grid, _, mod_spec = layout
    n = w.shape[1]
    body = functools.partial(_in_proj_body, load=load, n_x=len(xs), epilogue=epilogue,
                             n_extra=len(extras), chunk=chunk)
    return pl.pallas_call(
        body,
        grid=grid,
        in_specs=[*x_specs, mod_spec,
                  pl.BlockSpec((1, D), lambda b, i: (0, 0)),
                  pl.BlockSpec((D, n), lambda b, i: (0, 0)),
                  *extra_specs],
        out_specs=out_specs,
        out_shape=out_shapes,
        compiler_params=_cparams(("parallel", "parallel")),
        name=name,
    )(*xs, mod, gn.reshape(1, D), w, *extras)


def _rows_out(shape2, row, outs):
    return ([jax.ShapeDtypeStruct(shape2 + (wd,), dt) for wd, dt in outs], [row(wd) for wd, _ in outs])


def _put(ref, lo, val, lead):
    ref[:, :, lo:lo + val.shape[1]] = val.reshape(lead + (val.shape[1],)).astype(ref.dtype)


def _out_proj_body(*refs, load, n_h, store):
    (z_ref, w_ref), hs, (mod_ref, o_ref) = refs[:2], refs[2:2 + n_h], refs[2 + n_h:]
    z3 = z_ref[...]
    a, b, wd = z3.shape
    y = jnp.dot(z3.reshape(a * b, wd), w_ref[...], preferred_element_type=F32)
    store(o_ref, load(*hs) + mod_ref[:, :, 2 * D:3 * D] * y.reshape(a, b, D))


def _store_block(o_ref, val):
    o_ref[...] = val


def _out_proj(z, w, h, mod, layout, name):
    grid, row, mod_spec = layout
    wd = w.shape[0]
    body = functools.partial(_out_proj_body, load=_load_block, n_h=1, store=_store_block)
    return pl.pallas_call(
        body,
        grid=grid,
        in_specs=[row(wd), pl.BlockSpec((wd, D), lambda b, i: (0, 0)), row(D), mod_spec],
        out_specs=row(D),
        out_shape=jax.ShapeDtypeStruct(h.shape, F32),
        input_output_aliases={2: 0},
        compiler_params=_cparams(("parallel", "parallel")),
        name=name,
    )(z, w, h, mod)


def _lru_epilogue(j, acc, lead, extra, outs):
    u_ref, g_ref = outs
    lo = j * acc.shape[1]
    if lo < LRU_PW:
        _put(u_ref, lo, acc, lead)
    else:
        _put(g_ref, lo - LRU_PW, acc, lead)


def _lru_tile_index(s, nct, ntiles, reverse):
    if not reverse:
        return s
    return jnp.where(s < nct, nct - 1 - s, ntiles + nct - 1 - s)


def _lru_sweep_body(up_ref, uc_ref, un_ref, cw_ref, cb_ref, wg_ref, bg_ref, lam_ref, *rest,
                    nct, ntiles, reverse):
    if reverse:
        hf_ref, g_ref, o_ref, ext_ref, a_ref, b_ref, carry_ref = rest
    else:
        o_ref, ext_ref, a_ref, b_ref, carry_ref = rest
    s = pl.program_id(0)
    ti = _lru_tile_index(s, nct, ntiles, reverse)
    tt, nb, _ = uc_ref.shape

    @pl.when(s == 0)
    def _():
        carry_ref[...] = jnp.zeros_like(carry_ref)

    seg_start = (ti == 0) | (ti == nct)
    seg_end = (ti == nct - 1) | (ti == ntiles - 1)
    ext_ref[0:2] = jnp.where(seg_start, 0.0, up_ref[...])
    ext_ref[2:2 + tt] = uc_ref[...]
    ext_ref[2 + tt:3 + tt] = jnp.where(seg_end, 0.0, un_ref[...])

    softplus_neg_lam = jax.nn.softplus(-lam_ref[...])
    for k in range(LRU_BLOCKS):
        sl = slice(k * LANES, (k + 1) * LANES)
        u = cb_ref[:, sl]
        for tap in range(CONV_W):
            u = u + cw_ref[tap:tap + 1, sl] * ext_ref[tap:tap + tt, :, sl]
        u = u.reshape(tt * nb, LANES)
        pre = jnp.dot(u.astype(BF16), wg_ref[k], preferred_element_type=F32) + bg_ref[k]
        r = jax.nn.sigmoid(pre[:, :LANES])
        i = jax.nn.sigmoid(pre[:, LANES:])
        a = jnp.exp(-LRU_C * r * softplus_neg_lam[:, sl])
        a_ref[:, sl] = a
        b_ref[:, sl] = jnp.sqrt(1.0 - a * a) * (i * u)

    def step(q, h):
        t = tt - 1 - q if reverse else q
        rows = pl.ds(pl.multiple_of(t * nb, nb), nb)
        h = a_ref[rows, :] * h + b_ref[rows, :]
        if reverse:
            o_ref[t] = ((hf_ref[t] + h) * _silu(g_ref[t].astype(F32))).astype(o_ref.dtype)
        else:
            o_ref[t] = h
        return h

    carry_ref[...] = lax.fori_loop(0, tt, step, carry_ref[...], unroll=4)


def _lru_sweep(u, cw, cb, wg, bg, lam, nct_rows, reverse, hf=None, g=None):
    rows, nb, pw = u.shape
    tt = LRU_TT
    ntiles = rows // tt
    nct = nct_rows // tt
    tile = functools.partial(_lru_tile_index, nct=nct, ntiles=ntiles, reverse=reverse)
    cur = lambda wd: pl.BlockSpec((tt, nb, wd), lambda s: (tile(s), 0, 0))
    const = lambda shape: pl.BlockSpec(shape, lambda s: (0,) * len(shape))
    in_specs = [
        pl.BlockSpec((2, nb, pw), lambda s: (jnp.maximum(tile(s) * (tt // 2) - 1, 0), 0, 0)),
        cur(pw),
        pl.BlockSpec((1, nb, pw), lambda s: (jnp.minimum((tile(s) + 1) * tt, rows - 1), 0, 0)),
        const((CONV_W, pw)), const((1, pw)),
        const((LRU_BLOCKS, LANES, 2 * LANES)), const((LRU_BLOCKS, 1, 2 * LANES)), const((1, pw)),
    ]
    args = [u, u, u, cw, cb, wg, bg, lam]
    if reverse:
        in_specs += [cur(pw), cur(pw)]
        args += [hf, g]
    body = functools.partial(_lru_sweep_body, nct=nct, ntiles=ntiles, reverse=reverse)
    return pl.pallas_call(
        body,
        grid=(ntiles,),
        in_specs=in_specs,
        out_specs=cur(pw),
        out_shape=jax.ShapeDtypeStruct(u.shape, BF16 if reverse else F32),
        scratch_shapes=[pltpu.VMEM((tt + 3, nb, pw), F32),
                        pltpu.VMEM((tt * nb, pw), F32),
                        pltpu.VMEM((tt * nb, pw), F32),
                        pltpu.VMEM((nb, pw), F32)],
        compiler_params=_cparams(("arbitrary",)),
        name="lru_bwd" if reverse else "lru_fwd",
    )(*args)


def _pad_blocks(a, axis):
    axis = axis % a.ndim
    shp = a.shape
    a = a.reshape(shp[:axis] + (LRU_BLOCKS, LRU_BW) + shp[axis + 1:])
    pad = [(0, 0)] * a.ndim
    pad[axis + 1] = (0, LANES - LRU_BW)
    a = jnp.pad(a, pad)
    return a.reshape(shp[:axis] + (LRU_PW,) + shp[axis + 1:])


def _load_time_major(ctx_ref, x_ref, *, nct):
    blk = jnp.where(pl.program_id(1) < nct, ctx_ref[...], x_ref[...])
    return pltpu.einshape("btd->tbd", blk)


def _store_batch_major(o_ref, val):
    o_ref[...] = pltpu.einshape("tbd->btd", val)


def _rglru_layer(ctx, x, mod_t, gn, w_in, conv_w, conv_b, wa, ba, wx, bx, lam, w_out):
    nb, n_ctx, _ = ctx.shape
    rows = n_ctx + x.shape[1]
    tt = LRU_TT
    nct = n_ctx // tt
    layout = _tmajor(nb, rows, n_ctx, tt)
    grid, row, mod_spec = layout
    src_specs = [pl.BlockSpec((nb, tt, D), lambda b, i: (0, jnp.minimum(i, nct - 1), 0)),
                 pl.BlockSpec((nb, tt, D), lambda b, i: (0, jnp.maximum(i - nct, 0), 0))]
    load = functools.partial(_load_time_major, nct=nct)
    w_pad = jnp.concatenate([_pad_blocks(w_in[:, :LRU_WIDTH], 1), _pad_blocks(w_in[:, LRU_WIDTH:], 1)],
                            axis=1).astype(BF16)
    out_shapes, out_specs = _rows_out((rows, nb), row, [(LRU_PW, F32), (LRU_PW, BF16)])
    u, g = _in_proj((ctx, x), src_specs, load, mod_t, gn, w_pad, layout, out_shapes, out_specs,
                    _lru_epilogue, name="lru_in_proj")
    cw = _pad_blocks(conv_w, 1)
    cb = _pad_blocks(conv_b.reshape(1, LRU_WIDTH), 1)
    pad_w = ((0, 0), (0, 0), (0, LANES - LRU_BW), (0, LANES - LRU_BW))
    wg = jnp.concatenate([jnp.pad(wa, pad_w), jnp.pad(wx, pad_w)], axis=-1).astype(BF16)
    pad_b = ((0, 0), (0, 0), (0, LANES - LRU_BW))
    bg = jnp.concatenate([jnp.pad(ba.reshape(2, LRU_BLOCKS, LRU_BW), pad_b),
                          jnp.pad(bx.reshape(2, LRU_BLOCKS, LRU_BW), pad_b)], axis=-1)
    bg = bg.reshape(2, LRU_BLOCKS, 1, 2 * LANES)
    lam_p = _pad_blocks(lam, 1)
    hf = _lru_sweep(u, cw, cb, wg[0], bg[0], lam_p[0:1], n_ctx, reverse=False)
    z = _lru_sweep(u, cw, cb, wg[1], bg[1], lam_p[1:2], n_ctx, reverse=True, hf=hf, g=g)
    body = functools.partial(_out_proj_body, load=load, n_h=2, store=_store_batch_major)
    return pl.pallas_call(
        body,
        grid=grid,
        in_specs=[row(LRU_PW), pl.BlockSpec((LRU_PW, D), lambda b, i: (0, 0)), *src_specs, mod_spec],
        out_specs=pl.BlockSpec((nb, tt, D), lambda b, i: (0, i, 0)),
        out_shape=jax.ShapeDtypeStruct((nb, rows, D), F32),
        compiler_params=_cparams(("parallel", "parallel")),
        name="lru_out_proj",
    )(z, _pad_blocks(w_out, 0).astype(BF16), ctx, x, mod_t)


def _rope_tables(n_ctx, t_len):
    pos = jnp.arange(t_len)
    row = (pos // GRID_W).astype(F32)
    col = (pos % GRID_W).astype(F32)
    n_ax = HEAD_DIM // 4
    freqs = ROPE_BASE ** (-jnp.arange(n_ax, dtype=F32) / n_ax)
    ang = jnp.concatenate([row[:, None] * freqs, col[:, None] * freqs], axis=-1)
    cos, sin = jnp.cos(ang), jnp.sin(ang)
    cos_h = jnp.concatenate([cos, cos], axis=-1)
    sin_h = jnp.concatenate([-sin, sin], axis=-1)
    cos_f = jnp.concatenate([jnp.ones((n_ctx, HEAD_DIM), F32), cos_h], axis=0)
    sin_f = jnp.concatenate([jnp.zeros((n_ctx, HEAD_DIM), F32), sin_h], axis=0)
    return jnp.tile(cos_f, (1, 2)), jnp.tile(sin_f, (1, 2))


def _rope(x, cos, sin):
    lane = lax.broadcasted_iota(jnp.int32, x.shape, 1)
    swapped = jnp.where(lane % HEAD_DIM < HEAD_DIM // 2,
                        pltpu.roll(x, LANES - HEAD_DIM // 2, 1), pltpu.roll(x, HEAD_DIM // 2, 1))
    return x * cos + swapped * sin


def _swa_epilogue(j, acc, lead, extra, outs):
    cos_ref, sin_ref = extra
    q_ref, qr_ref, k_ref, v_ref, g_ref = outs
    qd = SWA_HEADS * HEAD_DIM
    kvd = SWA_KV_HEADS * HEAD_DIM
    grp = SWA_HEADS // SWA_KV_HEADS
    width = acc.shape[1]
    lo = j * width
    cos, sin = cos_ref[...], sin_ref[...]

    def roped(x):
        return jnp.concatenate([_rope(x[:, s:s + LANES], cos, sin) for s in range(0, x.shape[1], LANES)], axis=1)

    def head(x, hh):
        return x[:, hh * HEAD_DIM:(hh + 1) * HEAD_DIM]

    if lo < qd:
        q = acc * (HEAD_DIM ** -0.5)
        qr = roped(q)
        for hh in range(width // HEAD_DIM):
            kv, gq = divmod(lo // HEAD_DIM + hh, grp)
            for rt in range(lead[1] // BLOCK_Q):
                rows = slice(rt * BLOCK_Q, (rt + 1) * BLOCK_Q)
                dst = slice(gq * BLOCK_Q, (gq + 1) * BLOCK_Q)
                q_ref[0, kv, rt, dst, :] = head(q, hh)[rows].astype(q_ref.dtype)
                qr_ref[0, kv, rt, dst, :] = head(qr, hh)[rows].astype(qr_ref.dtype)
    elif lo == qd:
        kr = roped(acc[:, :kvd])
        for kv in range(SWA_KV_HEADS):
            k_ref[0, kv] = head(kr, kv).astype(k_ref.dtype)
            v_ref[0, kv] = head(acc[:, kvd:], kv).T.astype(v_ref.dtype)
    else:
        _put(g_ref, lo - qd - 2 * kvd, acc, lead)


def _swa_mask_table():
    grp = SWA_HEADS // SWA_KV_HEADS
    kj = np.arange(3 * BLOCK_Q)[:, None]
    qi = np.arange(BLOCK_Q)[None, :]
    base = np.where(np.abs(kj - BLOCK_Q - qi) <= WINDOW, 0.0, NEG_INF).astype(np.float32)
    first = np.where(kj < BLOCK_Q, NEG_INF, base).astype(np.float32)
    last = np.where(kj >= 2 * BLOCK_Q, NEG_INF, base).astype(np.float32)
    tab = np.stack([np.full_like(base, NEG_INF), first, base, last])
    return jnp.asarray(np.tile(tab, (1, 1, grp)))


def _swa_body(sink_ref, mask_ref, q_ref, qr_ref, kc_ref, k0_ref, k1_ref, k2_ref, vc_ref, v0_ref, v1_ref,
              v2_ref, g_ref, o_ref, s_ref, p_ref):
    grp = SWA_HEADS // SWA_KV_HEADS
    dims = (((1,), (1,)), ((), ()))
    nc = kc_ref.shape[2]
    for kv in range(SWA_KV_HEADS):
        slot = kv % 2
        kb = jnp.concatenate([k0_ref[0, kv], k1_ref[0, kv], k2_ref[0, kv]], axis=0)
        vt = jnp.concatenate([vc_ref[0, kv], v0_ref[0, kv], v1_ref[0, kv], v2_ref[0, kv]], axis=1)
        s_ref[slot, :nc, :] = lax.dot_general(kc_ref[0, kv], q_ref[0, kv, 0], dims, preferred_element_type=F32)
        s_ref[slot, nc:, :] = (lax.dot_general(kb, qr_ref[0, kv, 0], dims, preferred_element_type=F32)
                               + mask_ref[0])
        inv = []
        for gq in range(grp):
            lanes = slice(gq * BLOCK_Q, (gq + 1) * BLOCK_Q)
            s = s_ref[slot, :, lanes]
            sink = sink_ref[kv, :, lanes]
            m = jnp.maximum(jnp.max(s, axis=0, keepdims=True), sink)
            p = jnp.exp(s - m)
            inv.append(1.0 / (jnp.sum(p, axis=0, keepdims=True) + jnp.exp(sink - m)))
            p_ref[slot, :, lanes] = p.astype(BF16)
        ot = jnp.dot(vt, p_ref[slot], preferred_element_type=F32) * jnp.concatenate(inv, axis=1)
        slab = jnp.concatenate([ot[:, gq * BLOCK_Q:(gq + 1) * BLOCK_Q].T for gq in range(grp)], axis=1)
        lanes = slice(kv * grp * HEAD_DIM, (kv + 1) * grp * HEAD_DIM)
        o_ref[0, :, lanes] = (slab * _silu(g_ref[0, :, lanes].astype(F32))).astype(o_ref.dtype)


def _swa_attention(q, qr, k, vt, g, sink, n_ctx):
    nb, nkv, nt, qrows, hd = q.shape
    rows = k.shape[2]
    grp = SWA_HEADS // SWA_KV_HEADS
    nct = n_ctx // BLOCK_Q
    nkeys = n_ctx + 3 * BLOCK_Q
    qspec = pl.BlockSpec((1, nkv, 1, qrows, hd), lambda b, n: (b, 0, n, 0, 0))
    kctx = pl.BlockSpec((1, nkv, n_ctx, hd), lambda b, n: (b, 0, 0, 0))
    vctx = pl.BlockSpec((1, nkv, hd, n_ctx), lambda b, n: (b, 0, 0, 0))
    rowspec = pl.BlockSpec((1, BLOCK_Q, g.shape[2]), lambda b, n: (b, n, 0))

    def kband(off):
        return pl.BlockSpec((1, nkv, BLOCK_Q, hd), lambda b, n: (b, 0, jnp.clip(n + off, nct, nt - 1), 0))

    def vband(off):
        return pl.BlockSpec((1, nkv, hd, BLOCK_Q), lambda b, n: (b, 0, 0, jnp.clip(n + off, nct, nt - 1)))

    def variant(b, n):
        return (jnp.where(n < nct, 0, jnp.where(n == nct, 1, jnp.where(n == nt - 1, 3, 2))), 0, 0)

    sink_rows = jnp.repeat(sink.astype(F32).reshape(nkv, grp), BLOCK_Q, axis=1).reshape(nkv, 1, qrows)
    return pl.pallas_call(
        _swa_body,
        grid=(nb, nt),
        in_specs=[pl.BlockSpec((nkv, 1, qrows), lambda b, n: (0, 0, 0)),
                  pl.BlockSpec((1, 3 * BLOCK_Q, qrows), variant), qspec, qspec,
                  kctx, kband(-1), kband(0), kband(1), vctx, vband(-1), vband(0), vband(1), rowspec],
        out_specs=rowspec,
        out_shape=jax.ShapeDtypeStruct((nb, rows, g.shape[2]), BF16),
        scratch_shapes=[pltpu.VMEM((2, nkeys, qrows), F32), pltpu.VMEM((2, nkeys, qrows), BF16)],
        compiler_params=_cparams(("parallel", "parallel")),
        name="swa_attention",
    )(sink_rows, _swa_mask_table(), q, qr, k, k, k, k, vt, vt, vt, vt, g)


def _swa_layer(h, mod_b, gn, w_in, sink, w_out, n_ctx):
    nb, rows, _ = h.shape
    tm = ROW_TILE
    layout = _bmajor(nb, rows, n_ctx, tm)
    _, row, _ = layout
    qd = SWA_HEADS * HEAD_DIM
    grp = SWA_HEADS // SWA_KV_HEADS
    nkv = SWA_KV_HEADS
    cos, sin = _rope_tables(n_ctx, rows - n_ctx)
    tab = pl.BlockSpec((tm, LANES), lambda b, i: (i, 0))
    q_shape = jax.ShapeDtypeStruct((nb, nkv, rows // BLOCK_Q, grp * BLOCK_Q, HEAD_DIM), BF16)
    q_spec = pl.BlockSpec((1, nkv, tm // BLOCK_Q, grp * BLOCK_Q, HEAD_DIM), lambda b, i: (b, 0, i, 0, 0))
    k_shape = jax.ShapeDtypeStruct((nb, nkv, rows, HEAD_DIM), BF16)
    k_spec = pl.BlockSpec((1, nkv, tm, HEAD_DIM), lambda b, i: (b, 0, i, 0))
    vt_shape = jax.ShapeDtypeStruct((nb, nkv, HEAD_DIM, rows), BF16)
    vt_spec = pl.BlockSpec((1, nkv, HEAD_DIM, tm), lambda b, i: (b, 0, 0, i))
    q, qr, k, vt, g = _in_proj(
        (h,), [row(D)], _load_block, mod_b, gn, w_in.astype(BF16), layout,
        [q_shape, q_shape, k_shape, vt_shape, jax.ShapeDtypeStruct((nb, rows, qd), BF16)],
        [q_spec, q_spec, k_spec, vt_spec, row(qd)], _swa_epilogue,
        extras=(cos, sin), extra_specs=(tab, tab), name="swa_in_proj")
    z = _swa_attention(q, qr, k, vt, g, sink, n_ctx)
    return _out_proj(z, w_out.astype(BF16), h, mod_b, layout, name="swa_out_proj")


def _na_epilogue(j, acc, lead, extra, outs):
    q_ref, k_ref, v_ref, g_ref = outs
    wd = NA_HEADS * HEAD_DIM
    lo = j * acc.shape[1]
    which, off = lo // wd, lo % wd
    if which == 0:
        _put(q_ref, off, acc * (HEAD_DIM ** -0.5), lead)
    else:
        _put((k_ref, v_ref, g_ref)[which - 1], off, acc, lead)


def _na_bias_tiles(rpb):
    reach = GRID_W - NA_COLS
    period = 2 * GRID_W
    ndy = 2 * NA_ROWS - 1
    edge = jnp.pad(rpb.astype(F32), ((0, 0), (0, 0), (reach, reach)), mode="edge")
    flat = jnp.tile(jnp.pad(edge, ((0, 0), (0, 0), (0, 1))), (1, 1, GRID_W))[:, :, :GRID_W * (period - 1)]
    e = flat.reshape(NA_HEADS, ndy, GRID_W, period - 1)[:, :, :, GRID_W - 1:]
    qcol = np.arange(GRID_W)
    cstart = np.clip(qcol - NA_COLS // 2, 0, GRID_W - NA_COLS)
    col_ok = (qcol[None, :] >= cstart[:, None]) & (qcol[None, :] < cstart[:, None] + NA_COLS)
    e = jnp.where(jnp.asarray(col_ok)[None, None], e, NEG_INF)
    e = jnp.pad(e, ((0, 0), (0, 1), (0, 0), (0, 0)), constant_values=NEG_INF)
    shape = (NA_HEADS // NA_HSTEP, NA_HSTEP, ndy + 1, GRID_W, period)
    left = jnp.pad(e, ((0, 0), (0, 0), (0, 0), (0, GRID_W)), constant_values=NEG_INF).reshape(shape)
    right = jnp.pad(e, ((0, 0), (0, 0), (0, 0), (GRID_W, 0)), constant_values=NEG_INF).reshape(shape)
    return left, right


def _na_body(q_ref, k_ref, v_ref, g_ref, bl_ref, br_ref, o_ref, *, n_ctx, grid_rows):
    rb = pl.program_id(2)
    r0 = (rb - 1) * NA_QROWS
    kb = jnp.clip(r0 - NA_ROWS // 2, 0, grid_rows - NA_KROWS)
    start = pl.multiple_of(n_ctx + kb * GRID_W, GRID_W)
    nk = NA_KROWS * GRID_W
    q = q_ref[0]
    kc, vc = k_ref[0, 0:n_ctx, :], v_ref[0, 0:n_ctx, :]
    kn, vn = k_ref[0, pl.ds(start, nk), :], v_ref[0, pl.ds(start, nk), :]
    dims = (((1,), (1,)), ((), ()))

    def tile_index(qr, kr):
        qrow, krow = r0 + qr, kb + kr
        first = jnp.clip(qrow - NA_ROWS // 2, 0, grid_rows - NA_ROWS)
        inside = (rb >= 1) & (krow >= first) & (krow < first + NA_ROWS)
        return jnp.where(inside, krow - qrow + NA_ROWS - 1, NA_MASKED)

    idx = [[tile_index(qr, kr) for kr in range(NA_KROWS)] for qr in range(NA_QROWS)]
    left_half = lax.broadcasted_iota(jnp.int32, (GRID_W, 2 * GRID_W), 1) < GRID_W
    for hl in range(NA_HSTEP):
        hs = slice(hl * HEAD_DIM, (hl + 1) * HEAD_DIM)
        bias = jnp.concatenate(
            [jnp.concatenate([jnp.where(left_half, bl_ref[0, hl, idx[qr][2 * kp]], br_ref[0, hl, idx[qr][2 * kp + 1]])
                              for kp in range(NA_KROWS // 2)], axis=1) for qr in range(NA_QROWS)], axis=0)
        s_c = lax.dot_general(q[:, hs], kc[:, hs], dims, preferred_element_type=F32)
        s_n = lax.dot_general(q[:, hs], kn[:, hs], dims, preferred_element_type=F32) + bias
        m = jnp.maximum(jnp.max(s_c, axis=1, keepdims=True), jnp.max(s_n, axis=1, keepdims=True))
        p_c = jnp.exp(s_c - m)
        p_n = jnp.exp(s_n - m)
        denom = jnp.sum(p_c, axis=1, keepdims=True) + jnp.sum(p_n, axis=1, keepdims=True)
        o = (jnp.dot(p_c.astype(BF16), vc[:, hs], preferred_element_type=F32)
             + jnp.dot(p_n.astype(BF16), vn[:, hs], preferred_element_type=F32)) / denom
        gate = g_ref[0, :, hs].astype(F32)
        o_ref[0, :, hs] = (o * _silu(gate)).astype(o_ref.dtype)


def _na_attention(q, k, v, g, bias_left, bias_right, n_ctx):
    nb, rows, wd = q.shape
    grid_rows = (rows - n_ctx) // GRID_W
    nq = NA_QROWS * GRID_W
    assert n_ctx == nq, "context tokens form exactly one query block"
    lanes = NA_HSTEP * HEAD_DIM
    qspec = pl.BlockSpec((1, nq, lanes), lambda b, p, r: (b, r, p))
    kvspec = pl.BlockSpec((1, rows, lanes), lambda b, p, r: (b, 0, p))
    bspec = pl.BlockSpec((1,) + bias_left.shape[1:], lambda b, p, r: (p, 0, 0, 0, 0))
    body = functools.partial(_na_body, n_ctx=n_ctx, grid_rows=grid_rows)
    return pl.pallas_call(
        body,
        grid=(nb, wd // lanes, rows // nq),
        in_specs=[qspec, kvspec, kvspec, qspec, bspec, bspec],
        out_specs=qspec,
        out_shape=jax.ShapeDtypeStruct(q.shape, BF16),
        compiler_params=_cparams(("parallel", "parallel", "parallel")),
        name="na_attention",
    )(q, k, v, g, bias_left, bias_right)


def _na_layer(h, mod_b, gn, w_in, rpb, w_out, n_ctx):
    nb, rows, _ = h.shape
    layout = _bmajor(nb, rows, n_ctx, ROW_TILE)
    _, row, _ = layout
    wd = NA_HEADS * HEAD_DIM
    out_shapes, out_specs = _rows_out((nb, rows), row, [(wd, BF16)] * 4)
    q, k, v, g = _in_proj((h,), [row(D)], _load_block, mod_b, gn, w_in.astype(BF16), layout,
                          out_shapes, out_specs, _na_epilogue, name="na_in_proj")
    bias_left, bias_right = _na_bias_tiles(rpb)
    z = _na_attention(q, k, v, g, bias_left, bias_right, n_ctx)
    return _out_proj(z, w_out.astype(BF16), h, mod_b, layout, name="na_out_proj")


def _s5_epilogue(j, acc, lead, extra, outs):
    u_ref, g_ref = outs
    lo = j * acc.shape[1]
    if lo < S5_WIDTH:
        _put(u_ref, lo, acc, lead)
    else:
        _put(g_ref, lo - S5_WIDTH, acc, lead)


def _s5_matrices(a_re, a_im, log_dt, b_re, b_im, c_re, c_im, d_skip):
    L = S5_CHUNK
    lam = lax.complex(a_re.astype(F32), a_im.astype(F32))
    lam_dt = lam * jnp.exp(log_dt.astype(F32))[..., None]
    lam_bar = jnp.exp(lam_dt)
    b_bar = ((lam_bar - 1.0) / lam)[..., None] * lax.complex(b_re.astype(F32), b_im.astype(F32))
    c_mat = lax.complex(c_re.astype(F32), c_im.astype(F32))
    tau = np.arange(L)

    def power(expo, d):
        return jnp.exp(jnp.asarray(expo, F32).reshape(expo.shape + (1, 1)) * lam_dt[d])

    exact = lax.Precision.HIGHEST

    def lag_kernels(d):
        w = c_mat[d][None] * power(tau, d)[:, :, None, :]
        return (jnp.einsum('kgip,gpj->kgij', jnp.real(w), jnp.real(b_bar[d]), precision=exact)
                - jnp.einsum('kgip,gpj->kgij', jnp.imag(w), jnp.imag(b_bar[d]), precision=exact))

    lag = tau[None, :] - tau[:, None]
    sel = np.concatenate([lag[None] == tau[:, None, None], -lag[None] == tau[:, None, None]]).astype(np.float32)
    kern = jnp.concatenate([lag_kernels(0), lag_kernels(1)], axis=0)
    both = jnp.einsum('ktu,kgij->tugij', jnp.asarray(sel), kern, precision=exact)
    eye_t = jnp.eye(L, dtype=F32)[:, :, None, None, None]
    skip = jnp.eye(S5_GROUP, dtype=F32)[None] * d_skip.astype(F32).reshape(S5_GROUPS, S5_GROUP, 1)
    both = both + eye_t * skip[None, None]
    toep = both.transpose(2, 0, 4, 1, 3).reshape(S5_GROUPS, L * S5_GROUP, L * S5_GROUP)

    def lanes(z):
        pad = [(0, 0)] * (z.ndim - 1) + [(0, LANES - S5_STATE)]
        return jnp.concatenate([jnp.pad(jnp.real(z), pad), jnp.pad(jnp.imag(z), pad)], axis=-1)

    pf = power(L - 1 - tau, 0)[:, :, :, None] * b_bar[0][None]
    pb = power(tau, 1)[:, :, :, None] * b_bar[1][None]
    pmat = jnp.concatenate([lanes(pf.transpose(1, 0, 3, 2)), lanes(pb.transpose(1, 0, 3, 2))], axis=-1)
    pmat = pmat.reshape(S5_GROUPS, L * S5_GROUP, 4 * LANES)
    qf = c_mat[0][None] * power(tau + 1, 0)[:, :, None, :]
    qb = c_mat[1][None] * power(L - tau, 1)[:, :, None, :]

    def state_rows(z):
        pad = ((0, 0), (0, LANES - S5_STATE), (0, 0), (0, 0))
        zr = jnp.pad(jnp.real(z).transpose(1, 3, 0, 2), pad)
        zi = jnp.pad(-jnp.imag(z).transpose(1, 3, 0, 2), pad)
        return jnp.concatenate([zr, zi], axis=1).reshape(S5_GROUPS, 2 * LANES, L * S5_GROUP)

    qmat = jnp.concatenate([state_rows(qf), state_rows(qb)], axis=1)
    lam_l = jnp.exp(float(L) * lam_dt)
    pad = ((0, 0), (0, LANES - S5_STATE))
    coef = jnp.stack([jnp.pad(jnp.real(lam_l[0]), pad), jnp.pad(jnp.imag(lam_l[0]), pad),
                      jnp.pad(jnp.real(lam_l[1]), pad), jnp.pad(jnp.imag(lam_l[1]), pad)], axis=1)
    coef = jnp.pad(coef, ((0, 0), (0, 4), (0, 0)))
    return toep.astype(BF16), pmat.astype(BF16), qmat.astype(BF16), coef


def _s5_body(u_ref, toep_ref, p_ref, q_ref, coef_ref, y_ref, s_ref, hin_ref, *, nb, nc_ctx, nc):
    for gl in range(S5_SLAB):
        _s5_group(gl, u_ref, toep_ref, p_ref, q_ref, coef_ref, y_ref, s_ref, hin_ref, nb, nc_ctx, nc)


def _s5_group(gl, u_ref, toep_ref, p_ref, q_ref, coef_ref, y_ref, s_ref, hin_ref, nb, nc_ctx, nc):
    L = S5_CHUNK
    nwin = LANES // S5_GROUP
    nrow = u_ref.shape[1]
    lane = lax.broadcasted_iota(jnp.int32, (nrow // 2, LANES), 1)
    halves = []
    for half in range(L // nwin):
        acc = None
        for w in range(nwin):
            x = pltpu.bitcast(u_ref[half * nwin + w], jnp.uint32)
            shift = ((w - gl) * S5_GROUP) % LANES
            x = pltpu.roll(x, shift, 1) if shift else x
            acc = x if acc is None else jnp.where((lane >= w * S5_GROUP) & (lane < (w + 1) * S5_GROUP), x, acc)
        halves.append(pltpu.bitcast(acc, BF16))
    u = jnp.concatenate(halves, axis=1)
    s_ref[...] = jnp.dot(u, p_ref[gl], preferred_element_type=F32)
    coef = coef_ref[gl]
    shape = (nb, LANES)
    lr_f, li_f = jnp.broadcast_to(coef[0:1], shape), jnp.broadcast_to(coef[1:2], shape)
    lr_b, li_b = jnp.broadcast_to(coef[2:3], shape), jnp.broadcast_to(coef[3:4], shape)
    zero = jnp.zeros(shape, F32)

    def fwd(c, carry):
        hr, hi = carry
        rows = pl.ds(pl.multiple_of(c * nb, nb), nb)
        hin_ref[rows, 0:LANES] = hr
        hin_ref[rows, LANES:2 * LANES] = hi
        sr, si = s_ref[rows, 0:LANES], s_ref[rows, LANES:2 * LANES]
        return lr_f * hr - li_f * hi + sr, lr_f * hi + li_f * hr + si

    lax.fori_loop(0, nc, fwd, (zero, zero))

    def bwd(q, carry):
        c = jnp.where(q < nc_ctx, nc_ctx - 1 - q, nc + nc_ctx - 1 - q)
        hr, hi = carry
        rows = pl.ds(pl.multiple_of(c * nb, nb), nb)
        hin_ref[rows, 2 * LANES:3 * LANES] = hr
        hin_ref[rows, 3 * LANES:4 * LANES] = hi
        sr, si = s_ref[rows, 2 * LANES:3 * LANES], s_ref[rows, 3 * LANES:4 * LANES]
        return lr_b * hr - li_b * hi + sr, lr_b * hi + li_b * hr + si

    lax.fori_loop(0, nc, bwd, (zero, zero))

    lat = slice(nc_ctx * nb, nc * nb)
    y = (jnp.dot(u[lat], toep_ref[gl], preferred_element_type=F32)
         + jnp.dot(hin_ref[lat, :].astype(BF16), q_ref[gl], preferred_element_type=F32))
    mine = slice(gl * S5_GROUP, (gl + 1) * S5_GROUP)
    for t in range(L):
        half, w = divmod(t, nwin)
        x = y[:, half * LANES:(half + 1) * LANES]
        shift = ((gl - w) * S5_GROUP) % LANES
        x = pltpu.roll(x, shift, 1) if shift else x
        y_ref[t, :, mine] = x[:, mine]


def _s5_scan(u_t, toep, pmat, qmat, coef, nb, n_ctx):
    L, nrow, wd = u_t.shape
    pk = L * S5_GROUP
    nc = nrow // nb
    nc_ctx = n_ctx // L
    nlat = (nc - nc_ctx) * nb
    slab = lambda shape: pl.BlockSpec((S5_SLAB,) + shape, lambda s: (s, 0, 0))
    body = functools.partial(_s5_body, nb=nb, nc_ctx=nc_ctx, nc=nc)
    return pl.pallas_call(
        body,
        grid=(wd // LANES,),
        in_specs=[pl.BlockSpec((L, nrow, LANES), lambda s: (0, 0, s)),
                  slab((pk, pk)), slab((pk, 4 * LANES)), slab((4 * LANES, pk)), slab((8, LANES))],
        out_specs=pl.BlockSpec((L, nlat, LANES), lambda s: (0, 0, s)),
        out_shape=jax.ShapeDtypeStruct((L, nlat, wd), F32),
        scratch_shapes=[pltpu.VMEM((nrow, 4 * LANES), F32), pltpu.VMEM((nrow, 4 * LANES), F32)],
        compiler_params=_cparams(("parallel",)),
        name="s5_scan",
    )(u_t, toep, pmat, qmat, coef)


def _rms(x, g):
    return x * lax.rsqrt(jnp.mean(x * x, axis=-1, keepdims=True) + EPS) * g


def _s5_readout_body(y_ref, g_ref, gw_ref, gb_ref, w_ref, h_ref, mod_ref, nf_ref, o_ref):
    y = jax.nn.gelu(y_ref[0])
    t = y * jax.nn.sigmoid(jnp.dot(y.astype(BF16), gw_ref[...], preferred_element_type=F32) + gb_ref[...])
    z = (t * _silu(g_ref[0].astype(F32))).astype(BF16)
    out = jnp.dot(z, w_ref[...], preferred_element_type=F32)
    h = h_ref[0] + mod_ref[0, :, 2 * D:3 * D] * out
    o_ref[0] = _rms(h, nf_ref[...])


def _s5_readout(y, g, glu_w, glu_b, w_out, h, mod_b, norm_f, n_ctx):
    nb, t_len, _ = y.shape
    tm = ROW_TILE
    off = n_ctx // tm
    lat = lambda wd: pl.BlockSpec((1, tm, wd), lambda b, i: (b, i, 0))
    full = lambda wd: pl.BlockSpec((1, tm, wd), lambda b, i: (b, i + off, 0))
    const = lambda shape: pl.BlockSpec(shape, lambda b, i: (0, 0))
    return pl.pallas_call(
        _s5_readout_body,
        grid=(nb, t_len // tm),
        in_specs=[lat(S5_WIDTH), full(S5_WIDTH), const((S5_WIDTH, S5_WIDTH)), const((1, S5_WIDTH)),
                  const((S5_WIDTH, D)), full(D),
                  pl.BlockSpec((1, 1, 3 * D), lambda b, i: (b, 0, 0)), const((1, D))],
        out_specs=lat(D),
        out_shape=jax.ShapeDtypeStruct((nb, t_len, D), F32),
        compiler_params=_cparams(("parallel", "parallel")),
        name="s5_readout",
    )(y, g, glu_w.astype(BF16), glu_b.reshape(1, S5_WIDTH), w_out.astype(BF16), h, mod_b,
      norm_f.reshape(1, D))


def _s5_layer(h, mod_b, gn, w_in, a_re, a_im, log_dt, b_re, b_im, c_re, c_im, d_skip, glu_w, glu_b,
              w_out, norm_f, n_ctx):
    nb, rows, _ = h.shape
    L = S5_CHUNK
    layout = _bmajor(nb, rows, n_ctx, ROW_TILE)
    _, row, _ = layout
    out_shapes, out_specs = _rows_out((nb, rows), row, [(S5_WIDTH, BF16)] * 2)
    u, g = _in_proj((h,), [row(D)], _load_block, mod_b, gn, w_in.astype(BF16), layout, out_shapes, out_specs,
                    _s5_epilogue, name="s5_in_proj")
    nc = rows // L
    u_t = u.reshape(nb, nc, L, S5_WIDTH).transpose(2, 1, 0, 3).reshape(L, nc * nb, S5_WIDTH)
    toep, pmat, qmat, coef = _s5_matrices(a_re, a_im, log_dt, b_re, b_im, c_re, c_im, d_skip)
    y_t = _s5_scan(u_t, toep, pmat, qmat, coef, nb, n_ctx)
    t_len = rows - n_ctx
    y = y_t.reshape(L, t_len // L, nb, S5_WIDTH).transpose(2, 1, 0, 3).reshape(nb, t_len, S5_WIDTH)
    return _s5_readout(y, g, glu_w, glu_b, w_out, h, mod_b, norm_f, n_ctx)


def kernel(x, c, ctx, c_ctx, ada_w0, ada_b0, norm0, w_in0, conv_w0, conv_b0, lru_wa0, lru_ba0, lru_wx0, lru_bx0, lru_lam0, w_out0, ada_w1, ada_b1, norm1, w_in1, sink1, w_out1, ada_w2, ada_b2, norm2, w_in2, rpb2, w_out2, ada_w3, ada_b3, norm3, w_in3, s5_a_re3, s5_a_im3, s5_log_dt3, s5_b_re3, s5_b_im3, s5_c_re3, s5_c_im3, s5_d3, glu_w3, glu_b3, w_out3, norm_f):
    nb, t_len, _ = x.shape
    n_ctx = ctx.shape[1]
    cvec = jnp.concatenate([c, c_ctx[None, :], jnp.zeros((16 - nb - 1, D), F32)], axis=0)
    mods = [_modulation(cvec, w, b) for w, b in
            ((ada_w0, ada_b0), (ada_w1, ada_b1), (ada_w2, ada_b2), (ada_w3, ada_b3))]
    mod_b = [m[:, None, :] for m in mods]

    mod_t0 = jnp.stack([jnp.broadcast_to(mods[0][nb], (nb, 3 * D)), mods[0][:nb]], axis=0)
    h = _rglru_layer(ctx, x, mod_t0, norm0, w_in0, conv_w0, conv_b0, lru_wa0, lru_ba0, lru_wx0, lru_bx0,
                     lru_lam0, w_out0)
    h = _swa_layer(h, mod_b[1], norm1, w_in1, sink1, w_out1, n_ctx)
    h = _na_layer(h, mod_b[2], norm2, w_in2, rpb2, w_out2, n_ctx)
    return _s5_layer(h, mod_b[3], norm3, w_in3, s5_a_re3, s5_a_im3, s5_log_dt3, s5_b_re3, s5_b_im3,
                     s5_c_re3, s5_c_im3, s5_d3, glu_w3, glu_b3, w_out3, norm_f, n_ctx)
```

```python
import functools
import math

import jax
import jax.numpy as jnp
import numpy as np
from jax import lax
from jax.experimental import pallas as pl
from jax.experimental.pallas import tpu as pltpu

F32 = jnp.float32
BF16 = jnp.bfloat16

D = 1024
HEAD_DIM = 64
GRID_W = 64
EPS = 1e-6
NEG_INF = -1e30
ROPE_BASE = 10000.0
LANES = 128
LOG2E = math.log2(math.e)
QK_SCALE = HEAD_DIM ** -0.5 * LOG2E

LRU_WIDTH = 1408
LRU_BLOCKS = 16
LRU_BW = LRU_WIDTH // LRU_BLOCKS
LRU_WIN = 3
CONV_W = 4
LRU_C = 8.0
LRU_TT = 64

SWA_HEADS = 16
SWA_KV_HEADS = 4
WINDOW = 128
BLOCK_Q = 128

NA_HEADS = 16
NA_ROWS = 8
NA_COLS = 16
NA_QROWS = 4
NA_KROWS = 12
NA_HSTEP = 4
NA_MASKED = 2 * NA_ROWS - 1

S5_WIDTH = 1024
S5_GROUP = 16
S5_GROUPS = S5_WIDTH // S5_GROUP
S5_STATE = 64
S5_CHUNK = 16
S5_SLAB = LANES // S5_GROUP

ROW_TILE = 256
VMEM_LIMIT = 56 * 1024 * 1024


def _cparams(sem):
    return pltpu.CompilerParams(dimension_semantics=sem, vmem_limit_bytes=VMEM_LIMIT)


def _sigmoid(v):
    return 0.5 * jnp.tanh(0.5 * v) + 0.5


def _silu(v):
    return v * _sigmoid(v)


def _bmajor(nb, rows, n_ctx, tm):
    grid = (nb, rows // tm)
    nct = n_ctx // tm

    def row(width, off=0):
        return pl.BlockSpec((1, tm, width), lambda b, i: (b, i + off, 0))

    mod = pl.BlockSpec((1, 1, 3 * D), lambda b, i: (jnp.where(i < nct, nb, b), 0, 0))
    return grid, row, mod


def _tmajor(nb, rows, n_ctx, tt):
    grid = (1, rows // tt)
    nct = n_ctx // tt

    def row(width, off=0):
        return pl.BlockSpec((tt, nb, width), lambda b, i: (i + off, 0, 0))

    mod = pl.BlockSpec((1, nb, 3 * D), lambda b, i: (jnp.where(i < nct, 0, 1), 0, 0))
    return grid, row, mod


def _mod_body(c_ref, w_ref, b_ref, o_ref):
    s = _silu(c_ref[...]).astype(BF16)
    o_ref[...] = jnp.dot(s, w_ref[...].astype(BF16), preferred_element_type=F32) + b_ref[...]


def _modulation(cvec, ada_w, ada_b):
    n = ada_w.shape[1]
    tn = 1024
    return pl.pallas_call(
        _mod_body,
        grid=(n // tn,),
        in_specs=[pl.BlockSpec((16, D), lambda j: (0, 0)),
                  pl.BlockSpec((D, tn), lambda j: (0, j)),
                  pl.BlockSpec((1, tn), lambda j: (0, j))],
        out_specs=pl.BlockSpec((16, tn), lambda j: (0, j)),
        out_shape=jax.ShapeDtypeStruct((16, n), F32),
        compiler_params=_cparams(("parallel",)),
        name="modulation",
    )(cvec, ada_w, ada_b.reshape(1, n))


def _norm_mod(x3, mod3, gn):
    ms = jnp.mean(x3 * x3, axis=-1, keepdims=True)
    y = x3 * lax.rsqrt(ms + EPS) * gn
    return y * (1.0 + mod3[:, :, D:2 * D]) + mod3[:, :, 0:D]


def _in_proj_body(*refs, load, n_x, epilogue, n_extra, chunk):
    xs, (mod_ref, gn_ref, w_ref), rest = refs[:n_x], refs[n_x:n_x + 3], refs[n_x + 3:]
    extra, outs = rest[:n_extra], rest[n_extra:]
    x3 = load(*xs)
    n3 = _norm_mod(x3, mod_ref[...], gn_ref[...])
    lead = x3.shape[:2]
    n = n3.reshape(lead[0] * lead[1], D).astype(BF16)
    for j in range(w_ref.shape[1] // chunk):
        acc = jnp.dot(n, w_ref[:, j * chunk:(j + 1) * chunk], preferred_element_type=F32)
        epilogue(j, acc, lead, extra, outs)


def _load_block(x_ref):
    return x_ref[...]


def _in_proj(xs, x_specs, load, mod, gn, w, layout, out_shapes, out_specs, epilogue, extras=(),
             extra_specs=(), chunk=512, name="in_proj"):
    grid, _, mod_spec = layout
    n = w.shape[1]
    body = functools.partial(_in_proj_body, load=load, n_x=len(xs), epilogue=epilogue,
                             n_extra=len(extras), chunk=chunk)
    return pl.pallas_call(
        body,
        grid=grid,
        in_specs=[*x_specs, mod_spec,
                  pl.BlockSpec((1, D), lambda b, i: (0, 0)),
                  pl.BlockSpec((D, n), lambda b, i: (0, 0)),
                  *extra_specs],
        out_specs=out_specs,
        out_shape=out_shapes,
        compiler_params=_cparams(("parallel", "parallel")),
        name=name,
    )(*xs, mod, gn.reshape(1, D), w, *extras)


def _rows_out(shape2, row, outs):
    return ([jax.ShapeDtypeStruct(shape2 + (wd,), dt) for wd, dt in outs], [row(wd) for wd, _ in outs])


def _put(ref, lo, val, lead):
    ref[:, :, lo:lo + val.shape[1]] = val.reshape(lead + (val.shape[1],)).astype(ref.dtype)


def _out_proj_body(*refs, load, n_h, store):
    (z_ref, w_ref), hs, (mod_ref, o_ref) = refs[:2], refs[2:2 + n_h], refs[2 + n_h:]
    z3 = z_ref[...]
    a, b, wd = z3.shape
    y = jnp.dot(z3.reshape(a * b, wd), w_ref[...], preferred_element_type=F32)
    store(o_ref, load(*hs) + mod_ref[:, :, 2 * D:3 * D] * y.reshape(a, b, D))


def _store_block(o_ref, val):
    o_ref[...] = val


def _out_proj(z, w, h, mod, layout, name):
    grid, row, mod_spec = layout
    wd = w.shape[0]
    body = functools.partial(_out_proj_body, load=_load_block, n_h=1, store=_store_block)
    return pl.pallas_call(
        body,
        grid=grid,
        in_specs=[row(wd), pl.BlockSpec((wd, D), lambda b, i: (0, 0)), row(D), mod_spec],
        out_specs=row(D),
        out_shape=jax.ShapeDtypeStruct(h.shape, F32),
        input_output_aliases={2: 0},
        compiler_params=_cparams(("parallel", "parallel")),
        name=name,
    )(z, w, h, mod)


def _lru_epilogue(j, acc, lead, extra, outs):
    _put(outs[j], 0, acc, lead)


def _lru_tile_index(s, nct, ntiles, reverse):
    if not reverse:
        return s
    return jnp.where(s < nct, nct - 1 - s, ntiles + nct - 1 - s)


def _lru_sweep_body(up_ref, uc_ref, un_ref, cw_ref, cb_ref, wg_ref, bg_ref, lam_ref, *rest,
                    nct, ntiles, reverse):
    if reverse:
        hf_ref, g_ref, o_ref, ext_ref, cv_ref, a_ref, b_ref, carry_ref = rest
    else:
        o_ref, ext_ref, cv_ref, a_ref, b_ref, carry_ref = rest
    s = pl.program_id(0)
    ti = _lru_tile_index(s, nct, ntiles, reverse)
    tt, nb, pw = uc_ref.shape

    @pl.when(s == 0)
    def _():
        carry_ref[...] = jnp.zeros_like(carry_ref)

    seg_start = (ti == 0) | (ti == nct)
    seg_end = (ti == nct - 1) | (ti == ntiles - 1)
    ext_ref[0:2] = jnp.where(seg_start, 0.0, up_ref[...])
    ext_ref[2:2 + tt] = uc_ref[...]
    ext_ref[2 + tt:3 + tt] = jnp.where(seg_end, 0.0, un_ref[...])

    softplus_neg_lam = jax.nn.softplus(-lam_ref[...])
    tiles = [slice(k * LANES, (k + 1) * LANES) for k in range(pw // LANES)]
    for sl in tiles:
        u = cb_ref[:, sl]
        for tap in range(CONV_W):
            u = u + cw_ref[tap:tap + 1, sl] * ext_ref[tap:tap + tt, :, sl]
        cv_ref[:, sl] = u.reshape(tt * nb, LANES)
    for k, sl in enumerate(tiles):
        win = slice(_lru_window(k) * LANES, (_lru_window(k) + LRU_WIN) * LANES)
        pre = jnp.dot(cv_ref[:, win].astype(BF16), wg_ref[k], preferred_element_type=F32) + bg_ref[k]
        r = _sigmoid(pre[:, :LANES])
        i = _sigmoid(pre[:, LANES:])
        u = cv_ref[:, sl]
        a = jnp.exp(-LRU_C * r * softplus_neg_lam[:, sl])
        a_ref[:, sl] = a
        b_ref[:, sl] = jnp.sqrt(1.0 - a * a) * (i * u)

    def step(q, h):
        t = tt - 1 - q if reverse else q
        rows = pl.ds(pl.multiple_of(t * nb, nb), nb)
        h = a_ref[rows, :] * h + b_ref[rows, :]
        if reverse:
            o_ref[t] = ((hf_ref[t] + h) * _silu(g_ref[t].astype(F32))).astype(o_ref.dtype)
        else:
            o_ref[t] = h
        return h

    carry_ref[...] = lax.fori_loop(0, tt, step, carry_ref[...], unroll=4)


def _lru_sweep(u, cw, cb, wg, bg, lam, nct_rows, reverse, hf=None, g=None):
    rows, nb, pw = u.shape
    tt = LRU_TT
    ntiles = rows // tt
    nct = nct_rows // tt
    tile = functools.partial(_lru_tile_index, nct=nct, ntiles=ntiles, reverse=reverse)
    cur = lambda wd: pl.BlockSpec((tt, nb, wd), lambda s: (tile(s), 0, 0))
    const = lambda shape: pl.BlockSpec(shape, lambda s: (0,) * len(shape))
    in_specs = [
        pl.BlockSpec((2, nb, pw), lambda s: (jnp.maximum(tile(s) * (tt // 2) - 1, 0), 0, 0)),
        cur(pw),
        pl.BlockSpec((1, nb, pw), lambda s: (jnp.minimum((tile(s) + 1) * tt, rows - 1), 0, 0)),
        const((CONV_W, pw)), const((1, pw)),
        const(wg.shape), const(bg.shape), const((1, pw)),
    ]
    args = [u, u, u, cw, cb, wg, bg, lam]
    if reverse:
        in_specs += [cur(pw), cur(pw)]
        args += [hf, g]
    body = functools.partial(_lru_sweep_body, nct=nct, ntiles=ntiles, reverse=reverse)
    return pl.pallas_call(
        body,
        grid=(ntiles,),
        in_specs=in_specs,
        out_specs=cur(pw),
        out_shape=jax.ShapeDtypeStruct(u.shape, BF16 if reverse else F32),
        scratch_shapes=[pltpu.VMEM((tt + 3, nb, pw), F32),
                        pltpu.VMEM((tt * nb, pw), F32),
                        pltpu.VMEM((tt * nb, pw), F32),
                        pltpu.VMEM((tt * nb, pw), F32),
                        pltpu.VMEM((nb, pw), F32)],
        compiler_params=_cparams(("arbitrary",)),
        name="lru_bwd" if reverse else "lru_fwd",
    )(*args)


def _lru_window(k):
    first_block = (k * LANES) // LRU_BW
    return min((first_block * LRU_BW) // LANES, LRU_WIDTH // LANES - LRU_WIN)


def _lru_gate_weights(wa, wx, ba, bx):
    eye = jnp.eye(LRU_BLOCKS, dtype=F32)

    def dense(w):
        return (w[:, :, None, :] * eye[:, None, :, None]).reshape(LRU_WIDTH, LRU_WIDTH)

    da, dx = dense(wa), dense(wx)
    wg, bg = [], []
    for k in range(LRU_WIDTH // LANES):
        rows = slice(_lru_window(k) * LANES, (_lru_window(k) + LRU_WIN) * LANES)
        cols = slice(k * LANES, (k + 1) * LANES)
        wg.append(jnp.concatenate([da[rows, cols], dx[rows, cols]], axis=1))
        bg.append(jnp.concatenate([ba[cols], bx[cols]])[None])
    return jnp.stack(wg).astype(BF16), jnp.stack(bg)


def _load_time_major(ctx_ref, x_ref, *, nct):
    blk = jnp.where(pl.program_id(1) < nct, ctx_ref[...], x_ref[...])
    return pltpu.einshape("btd->tbd", blk)


def _store_batch_major(o_ref, val):
    o_ref[...] = pltpu.einshape("tbd->btd", val)


def _rglru_layer(ctx, x, mod_t, gn, w_in, conv_w, conv_b, wa, ba, wx, bx, lam, w_out):
    nb, n_ctx, _ = ctx.shape
    rows = n_ctx + x.shape[1]
    tt = LRU_TT
    nct = n_ctx // tt
    layout = _tmajor(nb, rows, n_ctx, tt)
    grid, row, mod_spec = layout
    src_specs = [pl.BlockSpec((nb, tt, D), lambda b, i: (0, jnp.minimum(i, nct - 1), 0)),
                 pl.BlockSpec((nb, tt, D), lambda b, i: (0, jnp.maximum(i - nct, 0), 0))]
    load = functools.partial(_load_time_major, nct=nct)
    wd = LRU_WIDTH
    out_shapes, out_specs = _rows_out((rows, nb), row, [(wd, F32), (wd, BF16)])
    u, g = _in_proj((ctx, x), src_specs, load, mod_t, gn, w_in.astype(BF16), layout, out_shapes, out_specs,
                    _lru_epilogue, chunk=wd, name="lru_in_proj")
    cb = conv_b.reshape(1, wd)
    sweeps = []
    for d in range(2):
        wg, bg = _lru_gate_weights(wa[d], wx[d], ba[d], bx[d])
        sweeps.append((conv_w, cb, wg, bg, lam[d:d + 1]))
    hf = _lru_sweep(u, *sweeps[0], n_ctx, reverse=False)
    z = _lru_sweep(u, *sweeps[1], n_ctx, reverse=True, hf=hf, g=g)
    body = functools.partial(_out_proj_body, load=load, n_h=2, store=_store_batch_major)
    return pl.pallas_call(
        body,
        grid=grid,
        in_specs=[row(wd), pl.BlockSpec((wd, D), lambda b, i: (0, 0)), *src_specs, mod_spec],
        out_specs=pl.BlockSpec((nb, tt, D), lambda b, i: (0, i, 0)),
        out_shape=jax.ShapeDtypeStruct((nb, rows, D), F32),
        compiler_params=_cparams(("parallel", "parallel")),
        name="lru_out_proj",
    )(z, w_out.astype(BF16), ctx, x, mod_t)


def _rope_tables(n_ctx, t_len):
    pos = jnp.arange(t_len)
    row = (pos // GRID_W).astype(F32)
    col = (pos % GRID_W).astype(F32)
    n_ax = HEAD_DIM // 4
    freqs = ROPE_BASE ** (-jnp.arange(n_ax, dtype=F32) / n_ax)
    ang = jnp.concatenate([row[:, None] * freqs, col[:, None] * freqs], axis=-1)
    cos, sin = jnp.cos(ang), jnp.sin(ang)
    cos_h = jnp.concatenate([cos, cos], axis=-1)
    sin_h = jnp.concatenate([-sin, sin], axis=-1)
    cos_f = jnp.concatenate([jnp.ones((n_ctx, HEAD_DIM), F32), cos_h], axis=0)
    sin_f = jnp.concatenate([jnp.zeros((n_ctx, HEAD_DIM), F32), sin_h], axis=0)
    return jnp.tile(cos_f, (1, 2)), jnp.tile(sin_f, (1, 2))


def _rope(x, cos, sin):
    lane = lax.broadcasted_iota(jnp.int32, x.shape, 1)
    swapped = jnp.where(lane % HEAD_DIM < HEAD_DIM // 2,
                        pltpu.roll(x, LANES - HEAD_DIM // 2, 1), pltpu.roll(x, HEAD_DIM // 2, 1))
    return x * cos + swapped * sin


def _swa_epilogue(j, acc, lead, extra, outs):
    cos_ref, sin_ref = extra
    q_ref, qr_ref, k_ref, v_ref, g_ref = outs
    qd = SWA_HEADS * HEAD_DIM
    kvd = SWA_KV_HEADS * HEAD_DIM
    grp = SWA_HEADS // SWA_KV_HEADS
    width = acc.shape[1]
    lo = j * width
    cos, sin = cos_ref[...], sin_ref[...]

    def roped(x):
        return jnp.concatenate([_rope(x[:, s:s + LANES], cos, sin) for s in range(0, x.shape[1], LANES)], axis=1)

    def head(x, hh):
        return x[:, hh * HEAD_DIM:(hh + 1) * HEAD_DIM]

    if lo < qd:
        q = acc * QK_SCALE
        qr = roped(q)
        for hh in range(width // HEAD_DIM):
            kv, gq = divmod(lo // HEAD_DIM + hh, grp)
            for rt in range(lead[1] // BLOCK_Q):
                rows = slice(rt * BLOCK_Q, (rt + 1) * BLOCK_Q)
                dst = slice(gq * BLOCK_Q, (gq + 1) * BLOCK_Q)
                q_ref[0, kv, rt, dst, :] = head(q, hh)[rows].astype(q_ref.dtype)
                qr_ref[0, kv, rt, dst, :] = head(qr, hh)[rows].astype(qr_ref.dtype)
    elif lo == qd:
        kr = roped(acc[:, :kvd])
        for kv in range(SWA_KV_HEADS):
            k_ref[0, kv] = head(kr, kv).astype(k_ref.dtype)
            v_ref[0, kv] = head(acc[:, kvd:], kv).T.astype(v_ref.dtype)
    else:
        _put(g_ref, lo - qd - 2 * kvd, acc, lead)


def _swa_mask_table():
    grp = SWA_HEADS // SWA_KV_HEADS
    kj = np.arange(3 * BLOCK_Q)[:, None]
    qi = np.arange(BLOCK_Q)[None, :]
    base = np.where(np.abs(kj - BLOCK_Q - qi) <= WINDOW, 0.0, NEG_INF).astype(np.float32)
    first = np.where(kj < BLOCK_Q, NEG_INF, base).astype(np.float32)
    last = np.where(kj >= 2 * BLOCK_Q, NEG_INF, base).astype(np.float32)
    tab = np.stack([np.full_like(base, NEG_INF), first, base, last])
    return jnp.asarray(np.tile(tab, (1, 1, grp)))


def _swa_body(sink_ref, mask_ref, q_ref, qr_ref, kc_ref, k0_ref, k1_ref, k2_ref, vc_ref, v0_ref, v1_ref,
              v2_ref, g_ref, o_ref, s_ref, p_ref):
    grp = SWA_HEADS // SWA_KV_HEADS
    dims = (((1,), (1,)), ((), ()))
    nc = kc_ref.shape[2]
    for kv in range(SWA_KV_HEADS):
        slot = kv % 2
        kb = jnp.concatenate([k0_ref[0, kv], k1_ref[0, kv], k2_ref[0, kv]], axis=0)
        vt = jnp.concatenate([vc_ref[0, kv], v0_ref[0, kv], v1_ref[0, kv], v2_ref[0, kv]], axis=1)
        s_ref[slot, :nc, :] = lax.dot_general(kc_ref[0, kv], q_ref[0, kv, 0], dims, preferred_element_type=F32)
        s_ref[slot, nc:, :] = (lax.dot_general(kb, qr_ref[0, kv, 0], dims, preferred_element_type=F32)
                               + mask_ref[0])
        inv = []
        for gq in range(grp):
            lanes = slice(gq * BLOCK_Q, (gq + 1) * BLOCK_Q)
            s = s_ref[slot, :, lanes]
            sink = sink_ref[kv, :, lanes]
            m = jnp.maximum(jnp.max(s, axis=0, keepdims=True), sink)
            p = jnp.exp2(s - m)
            inv.append(1.0 / (jnp.sum(p, axis=0, keepdims=True) + jnp.exp2(sink - m)))
            p_ref[slot, :, lanes] = p.astype(BF16)
        ot = jnp.dot(vt, p_ref[slot], preferred_element_type=F32) * jnp.concatenate(inv, axis=1)
        slab = jnp.concatenate([ot[:, gq * BLOCK_Q:(gq + 1) * BLOCK_Q].T for gq in range(grp)], axis=1)
        lanes = slice(kv * grp * HEAD_DIM, (kv + 1) * grp * HEAD_DIM)
        o_ref[0, :, lanes] = (slab * _silu(g_ref[0, :, lanes].astype(F32))).astype(o_ref.dtype)


def _swa_attention(q, qr, k, vt, g, sink, n_ctx):
    nb, nkv, nt, qrows, hd = q.shape
    rows = k.shape[2]
    grp = SWA_HEADS // SWA_KV_HEADS
    nct = n_ctx // BLOCK_Q
    nkeys = n_ctx + 3 * BLOCK_Q
    qspec = pl.BlockSpec((1, nkv, 1, qrows, hd), lambda b, n: (b, 0, n, 0, 0))
    kctx = pl.BlockSpec((1, nkv, n_ctx, hd), lambda b, n: (b, 0, 0, 0))
    vctx = pl.BlockSpec((1, nkv, hd, n_ctx), lambda b, n: (b, 0, 0, 0))
    rowspec = pl.BlockSpec((1, BLOCK_Q, g.shape[2]), lambda b, n: (b, n, 0))

    def kband(off):
        return pl.BlockSpec((1, nkv, BLOCK_Q, hd), lambda b, n: (b, 0, jnp.clip(n + off, nct, nt - 1), 0))

    def vband(off):
        return pl.BlockSpec((1, nkv, hd, BLOCK_Q), lambda b, n: (b, 0, 0, jnp.clip(n + off, nct, nt - 1)))

    def variant(b, n):
        return (jnp.where(n < nct, 0, jnp.where(n == nct, 1, jnp.where(n == nt - 1, 3, 2))), 0, 0)

    sink_rows = jnp.repeat(sink.astype(F32).reshape(nkv, grp) * LOG2E, BLOCK_Q, axis=1).reshape(nkv, 1, qrows)
    return pl.pallas_call(
        _swa_body,
        grid=(nb, nt),
        in_specs=[pl.BlockSpec((nkv, 1, qrows), lambda b, n: (0, 0, 0)),
                  pl.BlockSpec((1, 3 * BLOCK_Q, qrows), variant), qspec, qspec,
                  kctx, kband(-1), kband(0), kband(1), vctx, vband(-1), vband(0), vband(1), rowspec],
        out_specs=rowspec,
        out_shape=jax.ShapeDtypeStruct((nb, rows, g.shape[2]), BF16),
        scratch_shapes=[pltpu.VMEM((2, nkeys, qrows), F32), pltpu.VMEM((2, nkeys, qrows), BF16)],
        compiler_params=_cparams(("parallel", "parallel")),
        name="swa_attention",
    )(sink_rows, _swa_mask_table(), q, qr, k, k, k, k, vt, vt, vt, vt, g)


def _swa_layer(h, mod_b, gn, w_in, sink, w_out, n_ctx):
    nb, rows, _ = h.shape
    tm = ROW_TILE
    layout = _bmajor(nb, rows, n_ctx, tm)
    _, row, _ = layout
    qd = SWA_HEADS * HEAD_DIM
    grp = SWA_HEADS // SWA_KV_HEADS
    nkv = SWA_KV_HEADS
    cos, sin = _rope_tables(n_ctx, rows - n_ctx)
    tab = pl.BlockSpec((tm, LANES), lambda b, i: (i, 0))
    q_shape = jax.ShapeDtypeStruct((nb, nkv, rows // BLOCK_Q, grp * BLOCK_Q, HEAD_DIM), BF16)
    q_spec = pl.BlockSpec((1, nkv, tm // BLOCK_Q, grp * BLOCK_Q, HEAD_DIM), lambda b, i: (b, 0, i, 0, 0))
    k_shape = jax.ShapeDtypeStruct((nb, nkv, rows, HEAD_DIM), BF16)
    k_spec = pl.BlockSpec((1, nkv, tm, HEAD_DIM), lambda b, i: (b, 0, i, 0))
    vt_shape = jax.ShapeDtypeStruct((nb, nkv, HEAD_DIM, rows), BF16)
    vt_spec = pl.BlockSpec((1, nkv, HEAD_DIM, tm), lambda b, i: (b, 0, 0, i))
    q, qr, k, vt, g = _in_proj(
        (h,), [row(D)], _load_block, mod_b, gn, w_in.astype(BF16), layout,
        [q_shape, q_shape, k_shape, vt_shape, jax.ShapeDtypeStruct((nb, rows, qd), BF16)],
        [q_spec, q_spec, k_spec, vt_spec, row(qd)], _swa_epilogue,
        extras=(cos, sin), extra_specs=(tab, tab), name="swa_in_proj")
    z = _swa_attention(q, qr, k, vt, g, sink, n_ctx)
    return _out_proj(z, w_out.astype(BF16), h, mod_b, layout, name="swa_out_proj")


def _na_epilogue(j, acc, lead, extra, outs):
    q_ref, k_ref, v_ref, g_ref = outs
    wd = NA_HEADS * HEAD_DIM
    lo = j * acc.shape[1]
    which, off = lo // wd, lo % wd
    if which == 0:
        _put(q_ref, off, acc * QK_SCALE, lead)
    else:
        _put((k_ref, v_ref, g_ref)[which - 1], off, acc, lead)


def _na_bias_tiles(rpb):
    reach = GRID_W - NA_COLS
    period = 2 * GRID_W
    ndy = 2 * NA_ROWS - 1
    edge = jnp.pad(rpb.astype(F32) * LOG2E, ((0, 0), (0, 0), (reach, reach)), mode="edge")
    flat = jnp.tile(jnp.pad(edge, ((0, 0), (0, 0), (0, 1))), (1, 1, GRID_W))[:, :, :GRID_W * (period - 1)]
    e = flat.reshape(NA_HEADS, ndy, GRID_W, period - 1)[:, :, :, GRID_W - 1:]
    qcol = np.arange(GRID_W)
    cstart = np.clip(qcol - NA_COLS // 2, 0, GRID_W - NA_COLS)
    col_ok = (qcol[None, :] >= cstart[:, None]) & (qcol[None, :] < cstart[:, None] + NA_COLS)
    e = jnp.where(jnp.asarray(col_ok)[None, None], e, NEG_INF)
    e = jnp.pad(e, ((0, 0), (0, 1), (0, 0), (0, 0)), constant_values=NEG_INF)
    shape = (NA_HEADS // NA_HSTEP, NA_HSTEP, ndy + 1, GRID_W, period)
    left = jnp.pad(e, ((0, 0), (0, 0), (0, 0), (0, GRID_W)), constant_values=NEG_INF).reshape(shape)
    right = jnp.pad(e, ((0, 0), (0, 0), (0, 0), (GRID_W, 0)), constant_values=NEG_INF).reshape(shape)
    return left, right


def _na_body(q_ref, k_ref, v_ref, g_ref, bl_ref, br_ref, o_ref, *, n_ctx, grid_rows):
    rb = pl.program_id(2)
    r0 = (rb - 1) * NA_QROWS
    kb = jnp.clip(r0 - NA_ROWS // 2, 0, grid_rows - NA_KROWS)
    start = pl.multiple_of(n_ctx + kb * GRID_W, GRID_W)
    nk = NA_KROWS * GRID_W
    q = q_ref[0]
    kc, vc = k_ref[0, 0:n_ctx, :], v_ref[0, 0:n_ctx, :]
    kn, vn = k_ref[0, pl.ds(start, nk), :], v_ref[0, pl.ds(start, nk), :]
    dims = (((1,), (1,)), ((), ()))

    def tile_index(qr, kr):
        qrow, krow = r0 + qr, kb + kr
        first = jnp.clip(qrow - NA_ROWS // 2, 0, grid_rows - NA_ROWS)
        inside = (rb >= 1) & (krow >= first) & (krow < first + NA_ROWS)
        return jnp.where(inside, krow - qrow + NA_ROWS - 1, NA_MASKED)

    idx = [[tile_index(qr, kr) for kr in range(NA_KROWS)] for qr in range(NA_QROWS)]
    left_half = lax.broadcasted_iota(jnp.int32, (GRID_W, 2 * GRID_W), 1) < GRID_W
    for hl in range(NA_HSTEP):
        hs = slice(hl * HEAD_DIM, (hl + 1) * HEAD_DIM)
        bias = jnp.concatenate(
            [jnp.concatenate([jnp.where(left_half, bl_ref[0, hl, idx[qr][2 * kp]], br_ref[0, hl, idx[qr][2 * kp + 1]])
                              for kp in range(NA_KROWS // 2)], axis=1) for qr in range(NA_QROWS)], axis=0)
        s_c = lax.dot_general(q[:, hs], kc[:, hs], dims, preferred_element_type=F32)
        s_n = lax.dot_general(q[:, hs], kn[:, hs], dims, preferred_element_type=F32) + bias
        m = jnp.maximum(jnp.max(s_c, axis=1, keepdims=True), jnp.max(s_n, axis=1, keepdims=True))
        p_c = jnp.exp2(s_c - m)
        p_n = jnp.exp2(s_n - m)
        denom = jnp.sum(p_c, axis=1, keepdims=True) + jnp.sum(p_n, axis=1, keepdims=True)
        o = (jnp.dot(p_c.astype(BF16), vc[:, hs], preferred_element_type=F32)
             + jnp.dot(p_n.astype(BF16), vn[:, hs], preferred_element_type=F32)) / denom
        gate = g_ref[0, :, hs].astype(F32)
        o_ref[0, :, hs] = (o * _silu(gate)).astype(o_ref.dtype)


def _na_attention(q, k, v, g, bias_left, bias_right, n_ctx):
    nb, rows, wd = q.shape
    grid_rows = (rows - n_ctx) // GRID_W
    nq = NA_QROWS * GRID_W
    assert n_ctx == nq, "context tokens form exactly one query block"
    lanes = NA_HSTEP * HEAD_DIM
    qspec = pl.BlockSpec((1, nq, lanes), lambda b, p, r: (b, r, p))
    kvspec = pl.BlockSpec((1, rows, lanes), lambda b, p, r: (b, 0, p))
    bspec = pl.BlockSpec((1,) + bias_left.shape[1:], lambda b, p, r: (p, 0, 0, 0, 0))
    body = functools.partial(_na_body, n_ctx=n_ctx, grid_rows=grid_rows)
    return pl.pallas_call(
        body,
        grid=(nb, wd // lanes, rows // nq),
        in_specs=[qspec, kvspec, kvspec, qspec, bspec, bspec],
        out_specs=qspec,
        out_shape=jax.ShapeDtypeStruct(q.shape, BF16),
        compiler_params=_cparams(("parallel", "parallel", "parallel")),
        name="na_attention",
    )(q, k, v, g, bias_left, bias_right)


def _na_layer(h, mod_b, gn, w_in, rpb, w_out, n_ctx):
    nb, rows, _ = h.shape
    layout = _bmajor(nb, rows, n_ctx, ROW_TILE)
    _, row, _ = layout
    wd = NA_HEADS * HEAD_DIM
    out_shapes, out_specs = _rows_out((nb, rows), row, [(wd, BF16)] * 4)
    q, k, v, g = _in_proj((h,), [row(D)], _load_block, mod_b, gn, w_in.astype(BF16), layout,
                          out_shapes, out_specs, _na_epilogue, name="na_in_proj")
    bias_left, bias_right = _na_bias_tiles(rpb)
    z = _na_attention(q, k, v, g, bias_left, bias_right, n_ctx)
    return _out_proj(z, w_out.astype(BF16), h, mod_b, layout, name="na_out_proj")


def _s5_epilogue(j, acc, lead, extra, outs):
    u_ref, g_ref = outs
    lo = j * acc.shape[1]
    if lo < S5_WIDTH:
        _put(u_ref, lo, acc, lead)
    else:
        _put(g_ref, lo - S5_WIDTH, acc, lead)


def _s5_matrices(a_re, a_im, log_dt, b_re, b_im, c_re, c_im, d_skip):
    L = S5_CHUNK
    lam = lax.complex(a_re.astype(F32), a_im.astype(F32))
    lam_dt = lam * jnp.exp(log_dt.astype(F32))[..., None]
    lam_bar = jnp.exp(lam_dt)
    b_bar = ((lam_bar - 1.0) / lam)[..., None] * lax.complex(b_re.astype(F32), b_im.astype(F32))
    c_mat = lax.complex(c_re.astype(F32), c_im.astype(F32))
    tau = np.arange(L)

    def power(expo, d):
        return jnp.exp(jnp.asarray(expo, F32).reshape(expo.shape + (1, 1)) * lam_dt[d])

    w = jnp.stack([c_mat[d][None] * power(tau, d)[:, :, None, :] for d in range(2)])
    w2 = jnp.concatenate([jnp.real(w), -jnp.imag(w)], axis=-1)
    b2 = jnp.concatenate([jnp.real(b_bar), jnp.imag(b_bar)], axis=-2)
    kern = jnp.einsum('dkgip,dgpj->dkgij', w2, b2, precision=lax.Precision.HIGH).reshape(2 * L, S5_GROUPS,
                                                                                         S5_GROUP, S5_GROUP)
    lag = tau[None, :] - tau[:, None]
    sel = np.concatenate([lag[None] == tau[:, None, None], -lag[None] == tau[:, None, None]]).astype(np.float32)
    both = jnp.einsum('ktu,kgij->tugij', jnp.asarray(sel), kern, precision=lax.Precision.HIGHEST)
    eye_t = jnp.eye(L, dtype=F32)[:, :, None, None, None]
    skip = jnp.eye(S5_GROUP, dtype=F32)[None] * d_skip.astype(F32).reshape(S5_GROUPS, S5_GROUP, 1)
    both = both + eye_t * skip[None, None]
    toep = both.transpose(2, 0, 4, 1, 3).reshape(S5_GROUPS, L * S5_GROUP, L * S5_GROUP)

    def lanes(z):
        pad = [(0, 0)] * (z.ndim - 1) + [(0, LANES - S5_STATE)]
        return jnp.concatenate([jnp.pad(jnp.real(z), pad), jnp.pad(jnp.imag(z), pad)], axis=-1)

    pf = power(L - 1 - tau, 0)[:, :, :, None] * b_bar[0][None]
    pb = power(tau, 1)[:, :, :, None] * b_bar[1][None]
    pmat = jnp.concatenate([lanes(pf.transpose(1, 0, 3, 2)), lanes(pb.transpose(1, 0, 3, 2))], axis=-1)
    pmat = pmat.reshape(S5_GROUPS, L * S5_GROUP, 4 * LANES)
    qf = c_mat[0][None] * power(tau + 1, 0)[:, :, None, :]
    qb = c_mat[1][None] * power(L - tau, 1)[:, :, None, :]

    def state_rows(z):
        pad = ((0, 0), (0, LANES - S5_STATE), (0, 0), (0, 0))
        zr = jnp.pad(jnp.real(z).transpose(1, 3, 0, 2), pad)
        zi = jnp.pad(-jnp.imag(z).transpose(1, 3, 0, 2), pad)
        return jnp.concatenate([zr, zi], axis=1).reshape(S5_GROUPS, 2 * LANES, L * S5_GROUP)

    qmat = jnp.concatenate([state_rows(qf), state_rows(qb)], axis=1)
    lam_l = jnp.exp(float(L) * lam_dt)
    pad = ((0, 0), (0, LANES - S5_STATE))
    coef = jnp.stack([jnp.pad(jnp.real(lam_l[0]), pad), jnp.pad(jnp.imag(lam_l[0]), pad),
                      jnp.pad(jnp.real(lam_l[1]), pad), jnp.pad(jnp.imag(lam_l[1]), pad)], axis=1)
    coef = jnp.pad(coef, ((0, 0), (0, 4), (0, 0)))
    return toep.astype(BF16), pmat.astype(BF16), qmat.astype(BF16), coef


def _s5_body(u_ref, toep_ref, p_ref, q_ref, coef_ref, y_ref, s_ref, hin_ref, *, nb, nc_ctx, nc):
    for gl in range(S5_SLAB):
        _s5_group(gl, u_ref, toep_ref, p_ref, q_ref, coef_ref, y_ref, s_ref, hin_ref, nb, nc_ctx, nc)


def _s5_group(gl, u_ref, toep_ref, p_ref, q_ref, coef_ref, y_ref, s_ref, hin_ref, nb, nc_ctx, nc):
    L = S5_CHUNK
    nwin = LANES // S5_GROUP
    nrow = u_ref.shape[1]
    lane = lax.broadcasted_iota(jnp.int32, (nrow // 2, LANES), 1)
    halves = []
    for half in range(L // nwin):
        acc = None
        for w in range(nwin):
            x = pltpu.bitcast(u_ref[half * nwin + w], jnp.uint32)
            shift = ((w - gl) * S5_GROUP) % LANES
            x = pltpu.roll(x, shift, 1) if shift else x
            acc = x if acc is None else jnp.where((lane >= w * S5_GROUP) & (lane < (w + 1) * S5_GROUP), x, acc)
        halves.append(pltpu.bitcast(acc, BF16))
    u = jnp.concatenate(halves, axis=1)
    s_ref[...] = jnp.dot(u, p_ref[gl], preferred_element_type=F32)
    coef = coef_ref[gl]
    shape = (nb, LANES)
    lr_f, li_f = jnp.broadcast_to(coef[0:1], shape), jnp.broadcast_to(coef[1:2], shape)
    lr_b, li_b = jnp.broadcast_to(coef[2:3], shape), jnp.broadcast_to(coef[3:4], shape)
    zero = jnp.zeros(shape, F32)

    def fwd(c, carry):
        hr, hi = carry
        rows = pl.ds(pl.multiple_of(c * nb, nb), nb)
        hin_ref[rows, 0:LANES] = hr
        hin_ref[rows, LANES:2 * LANES] = hi
        sr, si = s_ref[rows, 0:LANES], s_ref[rows, LANES:2 * LANES]
        return lr_f * hr - li_f * hi + sr, lr_f * hi + li_f * hr + si

    lax.fori_loop(0, nc, fwd, (zero, zero))

    def bwd(q, carry):
        c = jnp.where(q < nc_ctx, nc_ctx - 1 - q, nc + nc_ctx - 1 - q)
        hr, hi = carry
        rows = pl.ds(pl.multiple_of(c * nb, nb), nb)
        hin_ref[rows, 2 * LANES:3 * LANES] = hr
        hin_ref[rows, 3 * LANES:4 * LANES] = hi
        sr, si = s_ref[rows, 2 * LANES:3 * LANES], s_ref[rows, 3 * LANES:4 * LANES]
        return lr_b * hr - li_b * hi + sr, lr_b * hi + li_b * hr + si

    lax.fori_loop(0, nc, bwd, (zero, zero))

    lat = slice(nc_ctx * nb, nc * nb)
    y = (jnp.dot(u[lat], toep_ref[gl], preferred_element_type=F32)
         + jnp.dot(hin_ref[lat, :].astype(BF16), q_ref[gl], preferred_element_type=F32))
    mine = slice(gl * S5_GROUP, (gl + 1) * S5_GROUP)
    for t in range(L):
        half, w = divmod(t, nwin)
        x = y[:, half * LANES:(half + 1) * LANES]
        shift = ((gl - w) * S5_GROUP) % LANES
        x = pltpu.roll(x, shift, 1) if shift else x
        y_ref[t, :, mine] = x[:, mine]


def _s5_scan(u_t, toep, pmat, qmat, coef, nb, n_ctx):
    L, nrow, wd = u_t.shape
    pk = L * S5_GROUP
    nc = nrow // nb
    nc_ctx = n_ctx // L
    nlat = (nc - nc_ctx) * nb
    slab = lambda shape: pl.BlockSpec((S5_SLAB,) + shape, lambda s: (s, 0, 0))
    body = functools.partial(_s5_body, nb=nb, nc_ctx=nc_ctx, nc=nc)
    return pl.pallas_call(
        body,
        grid=(wd // LANES,),
        in_specs=[pl.BlockSpec((L, nrow, LANES), lambda s: (0, 0, s)),
                  slab((pk, pk)), slab((pk, 4 * LANES)), slab((4 * LANES, pk)), slab((8, LANES))],
        out_specs=pl.BlockSpec((L, nlat, LANES), lambda s: (0, 0, s)),
        out_shape=jax.ShapeDtypeStruct((L, nlat, wd), F32),
        scratch_shapes=[pltpu.VMEM((nrow, 4 * LANES), F32), pltpu.VMEM((nrow, 4 * LANES), F32)],
        compiler_params=_cparams(("parallel",)),
        name="s5_scan",
    )(u_t, toep, pmat, qmat, coef)


def _rms(x, g):
    return x * lax.rsqrt(jnp.mean(x * x, axis=-1, keepdims=True) + EPS) * g


def _s5_readout_body(y_ref, g_ref, gw_ref, gb_ref, w_ref, h_ref, mod_ref, nf_ref, o_ref):
    y = jax.nn.gelu(y_ref[0])
    t = y * _sigmoid(jnp.dot(y.astype(BF16), gw_ref[...], preferred_element_type=F32) + gb_ref[...])
    z = (t * _silu(g_ref[0].astype(F32))).astype(BF16)
    out = jnp.dot(z, w_ref[...], preferred_element_type=F32)
    h = h_ref[0] + mod_ref[0, :, 2 * D:3 * D] * out
    o_ref[0] = _rms(h, nf_ref[...])


def _s5_readout(y, g, glu_w, glu_b, w_out, h, mod_b, norm_f, n_ctx):
    nb, t_len, _ = y.shape
    tm = ROW_TILE
    off = n_ctx // tm
    lat = lambda wd: pl.BlockSpec((1, tm, wd), lambda b, i: (b, i, 0))
    full = lambda wd: pl.BlockSpec((1, tm, wd), lambda b, i: (b, i + off, 0))
    const = lambda shape: pl.BlockSpec(shape, lambda b, i: (0, 0))
    return pl.pallas_call(
        _s5_readout_body,
        grid=(nb, t_len // tm),
        in_specs=[lat(S5_WIDTH), full(S5_WIDTH), const((S5_WIDTH, S5_WIDTH)), const((1, S5_WIDTH)),
                  const((S5_WIDTH, D)), full(D),
                  pl.BlockSpec((1, 1, 3 * D), lambda b, i: (b, 0, 0)), const((1, D))],
        out_specs=lat(D),
        out_shape=jax.ShapeDtypeStruct((nb, t_len, D), F32),
        compiler_params=_cparams(("parallel", "parallel")),
        name="s5_readout",
    )(y, g, glu_w.astype(BF16), glu_b.reshape(1, S5_WIDTH), w_out.astype(BF16), h, mod_b,
      norm_f.reshape(1, D))


def _s5_layer(h, mod_b, gn, w_in, a_re, a_im, log_dt, b_re, b_im, c_re, c_im, d_skip, glu_w, glu_b,
              w_out, norm_f, n_ctx):
    nb, rows, _ = h.shape
    L = S5_CHUNK
    layout = _bmajor(nb, rows, n_ctx, ROW_TILE)
    _, row, _ = layout
    out_shapes, out_specs = _rows_out((nb, rows), row, [(S5_WIDTH, BF16)] * 2)
    u, g = _in_proj((h,), [row(D)], _load_block, mod_b, gn, w_in.astype(BF16), layout, out_shapes, out_specs,
                    _s5_epilogue, name="s5_in_proj")
    nc = rows // L
    u_t = u.reshape(nb, nc, L, S5_WIDTH).transpose(2, 1, 0, 3).reshape(L, nc * nb, S5_WIDTH)
    toep, pmat, qmat, coef = _s5_matrices(a_re, a_im, log_dt, b_re, b_im, c_re, c_im, d_skip)
    y_t = _s5_scan(u_t, toep, pmat, qmat, coef, nb, n_ctx)
    t_len = rows - n_ctx
    y = y_t.reshape(L, t_len // L, nb, S5_WIDTH).transpose(2, 1, 0, 3).reshape(nb, t_len, S5_WIDTH)
    return _s5_readout(y, g, glu_w, glu_b, w_out, h, mod_b, norm_f, n_ctx)


def kernel(x, c, ctx, c_ctx, ada_w0, ada_b0, norm0, w_in0, conv_w0, conv_b0, lru_wa0, lru_ba0, lru_wx0, lru_bx0, lru_lam0, w_out0, ada_w1, ada_b1, norm1, w_in1, sink1, w_out1, ada_w2, ada_b2, norm2, w_in2, rpb2, w_out2, ada_w3, ada_b3, norm3, w_in3, s5_a_re3, s5_a_im3, s5_log_dt3, s5_b_re3, s5_b_im3, s5_c_re3, s5_c_im3, s5_d3, glu_w3, glu_b3, w_out3, norm_f):
    nb, t_len, _ = x.shape
    n_ctx = ctx.shape[1]
    cvec = jnp.concatenate([c, c_ctx[None, :], jnp.zeros((16 - nb - 1, D), F32)], axis=0)
    mods = [_modulation(cvec, w, b) for w, b in
            ((ada_w0, ada_b0), (ada_w1, ada_b1), (ada_w2, ada_b2), (ada_w3, ada_b3))]
    mod_b = [m[:, None, :] for m in mods]

    mod_t0 = jnp.stack([jnp.broadcast_to(mods[0][nb], (nb, 3 * D)), mods[0][:nb]], axis=0)
    h = _rglru_layer(ctx, x, mod_t0, norm0, w_in0, conv_w0, conv_b0, lru_wa0, lru_ba0, lru_wx0, lru_bx0,
                     lru_lam0, w_out0)
    h = _swa_layer(h, mod_b[1], norm1, w_in1, sink1, w_out1, n_ctx)
    h = _na_layer(h, mod_b[2], norm2, w_in2, rpb2, w_out2, n_ctx)
    return _s5_layer(h, mod_b[3], norm3, w_in3, s5_a_re3, s5_a_im3, s5_log_dt3, s5_b_re3, s5_b_im3,
                     s5_c_re3, s5_c_im3, s5_d3, glu_w3, glu_b3, w_out3, norm_f, n_ctx)
```

```python
import functools
import math

import jax
import jax.numpy as jnp
import numpy as np
from jax import lax
from jax.experimental import pallas as pl
from jax.experimental.pallas import tpu as pltpu

F32 = jnp.float32
BF16 = jnp.bfloat16

D = 1024
HEAD_DIM = 64
GRID_W = 64
EPS = 1e-6
NEG_INF = -1e30
ROPE_BASE = 10000.0
LANES = 128
LOG2E = math.log2(math.e)
QK_SCALE = HEAD_DIM ** -0.5 * LOG2E
ONES_ROWS = 16

LRU_WIDTH = 1408
LRU_BLOCKS = 16
LRU_BW = LRU_WIDTH // LRU_BLOCKS
LRU_WIN = 3
CONV_W = 4
LRU_C = 8.0
LRU_TT = 64

SWA_HEADS = 16
SWA_KV_HEADS = 4
WINDOW = 128
BLOCK_Q = 128

NA_HEADS = 16
NA_ROWS = 8
NA_COLS = 16
NA_QROWS = 4
NA_KROWS = 12
NA_HSTEP = 4
NA_MASKED = 2 * NA_ROWS - 1

S5_WIDTH = 1024
S5_GROUP = 16
S5_GROUPS = S5_WIDTH // S5_GROUP
S5_STATE = 64
S5_CHUNK = 16
S5_SLAB = LANES // S5_GROUP
S5_ROWS = 128

ROW_TILE = 768
VMEM_LIMIT = 56 * 1024 * 1024


def _cparams(sem):
    return pltpu.CompilerParams(dimension_semantics=sem, vmem_limit_bytes=VMEM_LIMIT)


def _sigmoid(v):
    return 0.5 * jnp.tanh(0.5 * v) + 0.5


def _silu(v):
    return v * _sigmoid(v)


def _bmajor(nb, rows, n_ctx, tm):
    grid = (nb, rows // tm)
    per_batch = rows // tm

    def row(width, off=0):
        return pl.BlockSpec((1, tm, width), lambda b, i: (b, i + off, 0))

    mod = pl.BlockSpec((tm // n_ctx, 1, 3 * D), lambda b, i: (b * per_batch + i, 0, 0))
    return grid, row, mod


def _bmajor_mod_table(mod16, nb, rows, n_ctx):
    groups = rows // n_ctx
    tab = jnp.broadcast_to(mod16[:nb, None, :], (nb, groups, 3 * D))
    tab = tab.at[:, 0, :].set(mod16[nb])
    return tab.reshape(nb * groups, 1, 3 * D)


def _tmajor(nb, rows, n_ctx, tt):
    grid = (1, rows // tt)
    nct = n_ctx // tt

    def row(width, off=0):
        return pl.BlockSpec((tt, nb, width), lambda b, i: (i + off, 0, 0))

    mod = pl.BlockSpec((1, nb, 3 * D), lambda b, i: (jnp.where(i < nct, 0, 1), 0, 0))
    return grid, row, mod


def _mod_body(c_ref, w_ref, b_ref, o_ref):
    s = _silu(c_ref[...]).astype(BF16)
    o_ref[...] = jnp.dot(s, w_ref[...].astype(BF16), preferred_element_type=F32) + b_ref[...]


def _modulation(cvec, ada_w, ada_b):
    n = ada_w.shape[1]
    tn = 1024
    return pl.pallas_call(
        _mod_body,
        grid=(n // tn,),
        in_specs=[pl.BlockSpec((16, D), lambda j: (0, 0)),
                  pl.BlockSpec((D, tn), lambda j: (0, j)),
                  pl.BlockSpec((1, tn), lambda j: (0, j))],
        out_specs=pl.BlockSpec((16, tn), lambda j: (0, j)),
        out_shape=jax.ShapeDtypeStruct((16, n), F32),
        compiler_params=_cparams(("parallel",)),
        name="modulation",
    )(cvec, ada_w, ada_b.reshape(1, n))


def _by_mod_group(x3, mod3):
    groups = mod3.shape[0]
    return x3 if groups == 1 else x3.reshape(groups, x3.shape[1] // groups, x3.shape[2])


def _norm_mod(x3, mod3, gn):
    xg = _by_mod_group(x3, mod3)
    ms = jnp.mean(xg * xg, axis=-1, keepdims=True)
    y = xg * lax.rsqrt(ms + EPS) * gn
    return (y * (1.0 + mod3[:, :, D:2 * D]) + mod3[:, :, 0:D]).reshape(x3.shape)


def _in_proj_body(*refs, load, n_x, epilogue, n_extra, chunk):
    xs, (mod_ref, gn_ref, w_ref), rest = refs[:n_x], refs[n_x:n_x + 3], refs[n_x + 3:]
    extra, outs = rest[:n_extra], rest[n_extra:]
    x3 = load(*xs)
    n3 = _norm_mod(x3, mod_ref[...], gn_ref[...])
    lead = x3.shape[:2]
    n = n3.reshape(lead[0] * lead[1], D).astype(BF16)
    for j in range(w_ref.shape[1] // chunk):
        acc = jnp.dot(n, w_ref[:, j * chunk:(j + 1) * chunk], preferred_element_type=F32)
        epilogue(j, acc, lead, extra, outs)


def _load_block(x_ref):
    return x_ref[...]


def _in_proj(xs, x_specs, load, mod, gn, w, layout, out_shapes, out_specs, epilogue, extras=(),
             extra_specs=(), chunk=512, name="in_proj"):
    grid, _, mod_spec = layout
    n = w.shape[1]
    body = functools.partial(_in_proj_body, load=load, n_x=len(xs), epilogue=epilogue,
                             n_extra=len(extras), chunk=chunk)
    return pl.pallas_call(
        body,
        grid=grid,
        in_specs=[*x_specs, mod_spec,
                  pl.BlockSpec((1, D), lambda b, i: (0, 0)),
                  pl.BlockSpec((D, n), lambda b, i: (0, 0)),
                  *extra_specs],
        out_specs=out_specs,
        out_shape=out_shapes,
        compiler_params=_cparams(("parallel", "parallel")),
        name=name,
    )(*xs, mod, gn.reshape(1, D), w, *extras)


def _rows_out(shape2, row, outs):
    return ([jax.ShapeDtypeStruct(shape2 + (wd,), dt) for wd, dt in outs], [row(wd) for wd, _ in outs])


def _put(ref, lo, val, lead):
    ref[:, :, lo:lo + val.shape[1]] = val.reshape(lead + (val.shape[1],)).astype(ref.dtype)


def _out_proj_body(*refs, load, n_h, store):
    (z_ref, w_ref), hs, (mod_ref, o_ref) = refs[:2], refs[2:2 + n_h], refs[2 + n_h:]
    z3 = z_ref[...]
    a, b, wd = z3.shape
    y = jnp.dot(z3.reshape(a * b, wd), w_ref[...], preferred_element_type=F32).reshape(a, b, D)
    mod3 = mod_ref[...]
    h = _by_mod_group(load(*hs), mod3) + mod3[:, :, 2 * D:3 * D] * _by_mod_group(y, mod3)
    store(o_ref, h.reshape(a, b, D))


def _store_block(o_ref, val):
    o_ref[...] = val


def _out_proj(z, w, h, mod, layout, name):
    grid, row, mod_spec = layout
    wd = w.shape[0]
    body = functools.partial(_out_proj_body, load=_load_block, n_h=1, store=_store_block)
    return pl.pallas_call(
        body,
        grid=grid,
        in_specs=[row(wd), pl.BlockSpec((wd, D), lambda b, i: (0, 0)), row(D), mod_spec],
        out_specs=row(D),
        out_shape=jax.ShapeDtypeStruct(h.shape, F32),
        input_output_aliases={2: 0},
        compiler_params=_cparams(("parallel", "parallel")),
        name=name,
    )(z, w, h, mod)


def _lru_epilogue(j, acc, lead, extra, outs):
    _put(outs[j], 0, acc, lead)


def _lru_tile_index(s, nct, ntiles, reverse):
    if not reverse:
        return s
    return jnp.where(s < nct, nct - 1 - s, ntiles + nct - 1 - s)


def _lru_sweep_body(up_ref, uc_ref, un_ref, cw_ref, cb_ref, wg_ref, bg_ref, lam_ref, *rest,
                    nct, ntiles, reverse):
    if reverse:
        hf_ref, g_ref, o_ref, ext_ref, cv_ref, a_ref, b_ref, carry_ref = rest
    else:
        o_ref, ext_ref, cv_ref, a_ref, b_ref, carry_ref = rest
    s = pl.program_id(0)
    ti = _lru_tile_index(s, nct, ntiles, reverse)
    tt, nb, pw = uc_ref.shape

    @pl.when(s == 0)
    def _():
        carry_ref[...] = jnp.zeros_like(carry_ref)

    seg_start = (ti == 0) | (ti == nct)
    seg_end = (ti == nct - 1) | (ti == ntiles - 1)
    ext_ref[0:2] = jnp.where(seg_start, 0.0, up_ref[...])
    ext_ref[2:2 + tt] = uc_ref[...]
    ext_ref[2 + tt:3 + tt] = jnp.where(seg_end, 0.0, un_ref[...])

    softplus_neg_lam = jax.nn.softplus(-lam_ref[...])
    tiles = [slice(k * LANES, (k + 1) * LANES) for k in range(pw // LANES)]
    for sl in tiles:
        u = cb_ref[:, sl]
        for tap in range(CONV_W):
            u = u + cw_ref[tap:tap + 1, sl] * ext_ref[tap:tap + tt, :, sl]
        cv_ref[:, sl] = u.reshape(tt * nb, LANES)
    for k, sl in enumerate(tiles):
        win = slice(_lru_window(k) * LANES, (_lru_window(k) + LRU_WIN) * LANES)
        pre = jnp.dot(cv_ref[:, win].astype(BF16), wg_ref[k], preferred_element_type=F32) + bg_ref[k]
        r = _sigmoid(pre[:, :LANES])
        i = _sigmoid(pre[:, LANES:])
        u = cv_ref[:, sl]
        a = jnp.exp(-LRU_C * r * softplus_neg_lam[:, sl])
        a_ref[:, sl] = a
        b_ref[:, sl] = jnp.sqrt(1.0 - a * a) * (i * u)

    def step(q, h):
        t = tt - 1 - q if reverse else q
        rows = pl.ds(pl.multiple_of(t * nb, nb), nb)
        h = a_ref[rows, :] * h + b_ref[rows, :]
        if reverse:
            o_ref[t] = ((hf_ref[t] + h) * _silu(g_ref[t].astype(F32))).astype(o_ref.dtype)
        else:
            o_ref[t] = h
        return h

    carry_ref[...] = lax.fori_loop(0, tt, step, carry_ref[...], unroll=4)


def _lru_sweep(u, cw, cb, wg, bg, lam, nct_rows, reverse, hf=None, g=None):
    rows, nb, pw = u.shape
    tt = LRU_TT
    ntiles = rows // tt
    nct = nct_rows // tt
    tile = functools.partial(_lru_tile_index, nct=nct, ntiles=ntiles, reverse=reverse)
    cur = lambda wd: pl.BlockSpec((tt, nb, wd), lambda s: (tile(s), 0, 0))
    const = lambda shape: pl.BlockSpec(shape, lambda s: (0,) * len(shape))
    in_specs = [
        pl.BlockSpec((2, nb, pw), lambda s: (jnp.maximum(tile(s) * (tt // 2) - 1, 0), 0, 0)),
        cur(pw),
        pl.BlockSpec((1, nb, pw), lambda s: (jnp.minimum((tile(s) + 1) * tt, rows - 1), 0, 0)),
        const((CONV_W, pw)), const((1, pw)),
        const(wg.shape), const(bg.shape), const((1, pw)),
    ]
    args = [u, u, u, cw, cb, wg, bg, lam]
    if reverse:
        in_specs += [cur(pw), cur(pw)]
        args += [hf, g]
    body = functools.partial(_lru_sweep_body, nct=nct, ntiles=ntiles, reverse=reverse)
    return pl.pallas_call(
        body,
        grid=(ntiles,),
        in_specs=in_specs,
        out_specs=cur(pw),
        out_shape=jax.ShapeDtypeStruct(u.shape, BF16 if reverse else F32),
        scratch_shapes=[pltpu.VMEM((tt + 3, nb, pw), F32),
                        pltpu.VMEM((tt * nb, pw), F32),
                        pltpu.VMEM((tt * nb, pw), F32),
                        pltpu.VMEM((tt * nb, pw), F32),
                        pltpu.VMEM((nb, pw), F32)],
        compiler_params=_cparams(("arbitrary",)),
        name="lru_bwd" if reverse else "lru_fwd",
    )(*args)


def _lru_window(k):
    first_block = (k * LANES) // LRU_BW
    return min((first_block * LRU_BW) // LANES, LRU_WIDTH // LANES - LRU_WIN)


def _lru_gate_weights(wa, wx, ba, bx):
    eye = jnp.eye(LRU_BLOCKS, dtype=F32)

    def dense(w):
        return (w[:, :, None, :] * eye[:, None, :, None]).reshape(LRU_WIDTH, LRU_WIDTH)

    da, dx = dense(wa), dense(wx)
    wg, bg = [], []
    for k in range(LRU_WIDTH // LANES):
        rows = slice(_lru_window(k) * LANES, (_lru_window(k) + LRU_WIN) * LANES)
        cols = slice(k * LANES, (k + 1) * LANES)
        wg.append(jnp.concatenate([da[rows, cols], dx[rows, cols]], axis=1))
        bg.append(jnp.concatenate([ba[cols], bx[cols]])[None])
    return jnp.stack(wg).astype(BF16), jnp.stack(bg)


def _load_time_major(ctx_ref, x_ref, *, nct):
    blk = jnp.where(pl.program_id(1) < nct, ctx_ref[...], x_ref[...])
    return pltpu.einshape("btd->tbd", blk)


def _store_batch_major(o_ref, val):
    o_ref[...] = pltpu.einshape("tbd->btd", val)


def _rglru_layer(ctx, x, mod_t, gn, w_in, conv_w, conv_b, wa, ba, wx, bx, lam, w_out):
    nb, n_ctx, _ = ctx.shape
    rows = n_ctx + x.shape[1]
    tt = LRU_TT
    nct = n_ctx // tt
    layout = _tmajor(nb, rows, n_ctx, tt)
    grid, row, mod_spec = layout
    src_specs = [pl.BlockSpec((nb, tt, D), lambda b, i: (0, jnp.minimum(i, nct - 1), 0)),
                 pl.BlockSpec((nb, tt, D), lambda b, i: (0, jnp.maximum(i - nct, 0), 0))]
    load = functools.partial(_load_time_major, nct=nct)
    wd = LRU_WIDTH
    out_shapes, out_specs = _rows_out((rows, nb), row, [(wd, F32), (wd, BF16)])
    u, g = _in_proj((ctx, x), src_specs, load, mod_t, gn, w_in.astype(BF16), layout, out_shapes, out_specs,
                    _lru_epilogue, chunk=wd, name="lru_in_proj")
    cb = conv_b.reshape(1, wd)
    sweeps = []
    for d in range(2):
        wg, bg = _lru_gate_weights(wa[d], wx[d], ba[d], bx[d])
        sweeps.append((conv_w, cb, wg, bg, lam[d:d + 1]))
    hf = _lru_sweep(u, *sweeps[0], n_ctx, reverse=False)
    z = _lru_sweep(u, *sweeps[1], n_ctx, reverse=True, hf=hf, g=g)
    body = functools.partial(_out_proj_body, load=load, n_h=2, store=_store_batch_major)
    return pl.pallas_call(
        body,
        grid=grid,
        in_specs=[row(wd), pl.BlockSpec((wd, D), lambda b, i: (0, 0)), *src_specs, mod_spec],
        out_specs=pl.BlockSpec((nb, tt, D), lambda b, i: (0, i, 0)),
        out_shape=jax.ShapeDtypeStruct((nb, rows, D), F32),
        compiler_params=_cparams(("parallel", "parallel")),
        name="lru_out_proj",
    )(z, w_out.astype(BF16), ctx, x, mod_t)


def _rope_tables(n_ctx, t_len):
    pos = jnp.arange(t_len)
    row = (pos // GRID_W).astype(F32)
    col = (pos % GRID_W).astype(F32)
    n_ax = HEAD_DIM // 4
    freqs = ROPE_BASE ** (-jnp.arange(n_ax, dtype=F32) / n_ax)
    ang = jnp.concatenate([row[:, None] * freqs, col[:, None] * freqs], axis=-1)
    cos, sin = jnp.cos(ang), jnp.sin(ang)
    cos_h = jnp.concatenate([cos, cos], axis=-1)
    sin_h = jnp.concatenate([-sin, sin], axis=-1)
    cos_f = jnp.concatenate([jnp.ones((n_ctx, HEAD_DIM), F32), cos_h], axis=0)
    sin_f = jnp.concatenate([jnp.zeros((n_ctx, HEAD_DIM), F32), sin_h], axis=0)
    return jnp.tile(cos_f, (1, 2)), jnp.tile(sin_f, (1, 2))


def _rope(x, cos, sin):
    lane = lax.broadcasted_iota(jnp.int32, x.shape, 1)
    swapped = jnp.where(lane % HEAD_DIM < HEAD_DIM // 2,
                        pltpu.roll(x, LANES - HEAD_DIM // 2, 1), pltpu.roll(x, HEAD_DIM // 2, 1))
    return x * cos + swapped * sin


def _swa_epilogue(j, acc, lead, extra, outs):
    cos_ref, sin_ref = extra
    q_ref, qr_ref, k_ref, v_ref, g_ref = outs
    qd = SWA_HEADS * HEAD_DIM
    kvd = SWA_KV_HEADS * HEAD_DIM
    grp = SWA_HEADS // SWA_KV_HEADS
    width = acc.shape[1]
    lo = j * width
    cos, sin = cos_ref[...], sin_ref[...]

    def roped(x):
        return jnp.concatenate([_rope(x[:, s:s + LANES], cos, sin) for s in range(0, x.shape[1], LANES)], axis=1)

    def head(x, hh):
        return x[:, hh * HEAD_DIM:(hh + 1) * HEAD_DIM]

    if lo < qd:
        q = acc * QK_SCALE
        qr = roped(q)
        for hh in range(width // HEAD_DIM):
            kv, gq = divmod(lo // HEAD_DIM + hh, grp)
            for rt in range(lead[1] // BLOCK_Q):
                rows = slice(rt * BLOCK_Q, (rt + 1) * BLOCK_Q)
                dst = slice(gq * BLOCK_Q, (gq + 1) * BLOCK_Q)
                q_ref[0, kv, rt, dst, :] = head(q, hh)[rows].astype(q_ref.dtype)
                qr_ref[0, kv, rt, dst, :] = head(qr, hh)[rows].astype(qr_ref.dtype)
    elif lo == qd:
        kr = roped(acc[:, :kvd])
        for kv in range(SWA_KV_HEADS):
            k_ref[0, kv] = head(kr, kv).astype(k_ref.dtype)
            v_ref[0, kv] = head(acc[:, kvd:], kv).T.astype(v_ref.dtype)
    else:
        _put(g_ref, lo - qd - 2 * kvd, acc, lead)


def _swa_mask_table():
    grp = SWA_HEADS // SWA_KV_HEADS
    kj = np.arange(3 * BLOCK_Q)[:, None]
    qi = np.arange(BLOCK_Q)[None, :]
    base = np.where(np.abs(kj - BLOCK_Q - qi) <= WINDOW, 0.0, NEG_INF).astype(np.float32)
    first = np.where(kj < BLOCK_Q, NEG_INF, base).astype(np.float32)
    last = np.where(kj >= 2 * BLOCK_Q, NEG_INF, base).astype(np.float32)
    tab = np.stack([np.full_like(base, NEG_INF), first, base, last])
    return jnp.asarray(np.tile(tab, (1, 1, grp)))


def _swa_body(sink_ref, mask_ref, q_ref, qr_ref, kc_ref, k0_ref, k1_ref, k2_ref, vc_ref, v0_ref, v1_ref,
              v2_ref, g_ref, o_ref, s_ref, p_ref):
    grp = SWA_HEADS // SWA_KV_HEADS
    dims = (((1,), (1,)), ((), ()))
    nc = kc_ref.shape[2]
    for kv in range(SWA_KV_HEADS):
        slot = kv % 2
        kb = jnp.concatenate([k0_ref[0, kv], k1_ref[0, kv], k2_ref[0, kv]], axis=0)
        vt = jnp.concatenate([vc_ref[0, kv], v0_ref[0, kv], v1_ref[0, kv], v2_ref[0, kv]], axis=1)
        vt = jnp.concatenate([vt, jnp.ones((ONES_ROWS, vt.shape[1]), BF16)], axis=0)
        s_ref[slot, :nc, :] = lax.dot_general(kc_ref[0, kv], q_ref[0, kv, 0], dims, preferred_element_type=F32)
        s_ref[slot, nc:, :] = (lax.dot_general(kb, qr_ref[0, kv, 0], dims, preferred_element_type=F32)
                               + mask_ref[0])
        sink_p = []
        for gq in range(grp):
            lanes = slice(gq * BLOCK_Q, (gq + 1) * BLOCK_Q)
            s = s_ref[slot, :, lanes]
            sink = sink_ref[kv, :, lanes]
            m = jnp.maximum(jnp.max(s, axis=0, keepdims=True), sink)
            sink_p.append(jnp.exp2(sink - m))
            p_ref[slot, :, lanes] = jnp.exp2(s - m).astype(BF16)
        ot = jnp.dot(vt, p_ref[slot], preferred_element_type=F32)
        denom = ot[HEAD_DIM:HEAD_DIM + 1] + jnp.concatenate(sink_p, axis=1)
        ot = ot[:HEAD_DIM] * (1.0 / denom)
        slab = jnp.concatenate([ot[:, gq * BLOCK_Q:(gq + 1) * BLOCK_Q].T for gq in range(grp)], axis=1)
        lanes = slice(kv * grp * HEAD_DIM, (kv + 1) * grp * HEAD_DIM)
        o_ref[0, :, lanes] = (slab * _silu(g_ref[0, :, lanes].astype(F32))).astype(o_ref.dtype)


def _swa_attention(q, qr, k, vt, g, sink, n_ctx):
    nb, nkv, nt, qrows, hd = q.shape
    rows = k.shape[2]
    grp = SWA_HEADS // SWA_KV_HEADS
    nct = n_ctx // BLOCK_Q
    nkeys = n_ctx + 3 * BLOCK_Q
    qspec = pl.BlockSpec((1, nkv, 1, qrows, hd), lambda b, n: (b, 0, n, 0, 0))
    kctx = pl.BlockSpec((1, nkv, n_ctx, hd), lambda b, n: (b, 0, 0, 0))
    vctx = pl.BlockSpec((1, nkv, hd, n_ctx), lambda b, n: (b, 0, 0, 0))
    rowspec = pl.BlockSpec((1, BLOCK_Q, g.shape[2]), lambda b, n: (b, n, 0))

    def kband(off):
        return pl.BlockSpec((1, nkv, BLOCK_Q, hd), lambda b, n: (b, 0, jnp.clip(n + off, nct, nt - 1), 0))

    def vband(off):
        return pl.BlockSpec((1, nkv, hd, BLOCK_Q), lambda b, n: (b, 0, 0, jnp.clip(n + off, nct, nt - 1)))

    def variant(b, n):
        return (jnp.where(n < nct, 0, jnp.where(n == nct, 1, jnp.where(n == nt - 1, 3, 2))), 0, 0)

    sink_rows = jnp.repeat(sink.astype(F32).reshape(nkv, grp) * LOG2E, BLOCK_Q, axis=1).reshape(nkv, 1, qrows)
    return pl.pallas_call(
        _swa_body,
        grid=(nb, nt),
        in_specs=[pl.BlockSpec((nkv, 1, qrows), lambda b, n: (0, 0, 0)),
                  pl.BlockSpec((1, 3 * BLOCK_Q, qrows), variant), qspec, qspec,
                  kctx, kband(-1), kband(0), kband(1), vctx, vband(-1), vband(0), vband(1), rowspec],
        out_specs=rowspec,
        out_shape=jax.ShapeDtypeStruct((nb, rows, g.shape[2]), BF16),
        scratch_shapes=[pltpu.VMEM((2, nkeys, qrows), F32), pltpu.VMEM((2, nkeys, qrows), BF16)],
        compiler_params=_cparams(("parallel", "parallel")),
        name="swa_attention",
    )(sink_rows, _swa_mask_table(), q, qr, k, k, k, k, vt, vt, vt, vt, g)


def _swa_layer(h, mod_b, gn, w_in, sink, w_out, n_ctx):
    nb, rows, _ = h.shape
    tm = ROW_TILE
    layout = _bmajor(nb, rows, n_ctx, tm)
    _, row, _ = layout
    qd = SWA_HEADS * HEAD_DIM
    grp = SWA_HEADS // SWA_KV_HEADS
    nkv = SWA_KV_HEADS
    cos, sin = _rope_tables(n_ctx, rows - n_ctx)
    tab = pl.BlockSpec((tm, LANES), lambda b, i: (i, 0))
    q_shape = jax.ShapeDtypeStruct((nb, nkv, rows // BLOCK_Q, grp * BLOCK_Q, HEAD_DIM), BF16)
    q_spec = pl.BlockSpec((1, nkv, tm // BLOCK_Q, grp * BLOCK_Q, HEAD_DIM), lambda b, i: (b, 0, i, 0, 0))
    k_shape = jax.ShapeDtypeStruct((nb, nkv, rows, HEAD_DIM), BF16)
    k_spec = pl.BlockSpec((1, nkv, tm, HEAD_DIM), lambda b, i: (b, 0, i, 0))
    vt_shape = jax.ShapeDtypeStruct((nb, nkv, HEAD_DIM, rows), BF16)
    vt_spec = pl.BlockSpec((1, nkv, HEAD_DIM, tm), lambda b, i: (b, 0, 0, i))
    q, qr, k, vt, g = _in_proj(
        (h,), [row(D)], _load_block, mod_b, gn, w_in.astype(BF16), layout,
        [q_shape, q_shape, k_shape, vt_shape, jax.ShapeDtypeStruct((nb, rows, qd), BF16)],
        [q_spec, q_spec, k_spec, vt_spec, row(qd)], _swa_epilogue,
        extras=(cos, sin), extra_specs=(tab, tab), name="swa_in_proj")
    z = _swa_attention(q, qr, k, vt, g, sink, n_ctx)
    return _out_proj(z, w_out.astype(BF16), h, mod_b, layout, name="swa_out_proj")


def _na_epilogue(j, acc, lead, extra, outs):
    q_ref, k_ref, v_ref, g_ref = outs
    wd = NA_HEADS * HEAD_DIM
    lo = j * acc.shape[1]
    which, off = lo // wd, lo % wd
    if which == 0:
        _put(q_ref, off, acc * QK_SCALE, lead)
    else:
        _put((k_ref, v_ref, g_ref)[which - 1], off, acc, lead)


def _na_bias_tiles(rpb):
    reach = GRID_W - NA_COLS
    period = 2 * GRID_W
    ndy = 2 * NA_ROWS - 1
    edge = jnp.pad(rpb.astype(F32) * LOG2E, ((0, 0), (0, 0), (reach, reach)), mode="edge")
    flat = jnp.tile(jnp.pad(edge, ((0, 0), (0, 0), (0, 1))), (1, 1, GRID_W))[:, :, :GRID_W * (period - 1)]
    e = flat.reshape(NA_HEADS, ndy, GRID_W, period - 1)[:, :, :, GRID_W - 1:]
    qcol = np.arange(GRID_W)
    cstart = np.clip(qcol - NA_COLS // 2, 0, GRID_W - NA_COLS)
    col_ok = (qcol[None, :] >= cstart[:, None]) & (qcol[None, :] < cstart[:, None] + NA_COLS)
    e = jnp.where(jnp.asarray(col_ok)[None, None], e, NEG_INF)
    e = jnp.pad(e, ((0, 0), (0, 1), (0, 0), (0, 0)), constant_values=NEG_INF)
    shape = (NA_HEADS // NA_HSTEP, NA_HSTEP, ndy + 1, GRID_W, period)
    left = jnp.pad(e, ((0, 0), (0, 0), (0, 0), (0, GRID_W)), constant_values=NEG_INF).reshape(shape)
    right = jnp.pad(e, ((0, 0), (0, 0), (0, 0), (GRID_W, 0)), constant_values=NEG_INF).reshape(shape)
    return left, right


def _na_body(q_ref, k_ref, v_ref, g_ref, bl_ref, br_ref, o_ref, *, n_ctx, grid_rows):
    rb = pl.program_id(2)
    r0 = (rb - 1) * NA_QROWS
    kb = jnp.clip(r0 - NA_ROWS // 2, 0, grid_rows - NA_KROWS)
    start = pl.multiple_of(n_ctx + kb * GRID_W, GRID_W)
    nk = NA_KROWS * GRID_W
    q = q_ref[0]
    kc, vc = k_ref[0, 0:n_ctx, :], v_ref[0, 0:n_ctx, :]
    kn, vn = k_ref[0, pl.ds(start, nk), :], v_ref[0, pl.ds(start, nk), :]
    dims = (((1,), (1,)), ((), ()))

    def tile_index(qr, kr):
        qrow, krow = r0 + qr, kb + kr
        first = jnp.clip(qrow - NA_ROWS // 2, 0, grid_rows - NA_ROWS)
        inside = (rb >= 1) & (krow >= first) & (krow < first + NA_ROWS)
        return jnp.where(inside, krow - qrow + NA_ROWS - 1, NA_MASKED)

    idx = [[tile_index(qr, kr) for kr in range(NA_KROWS)] for qr in range(NA_QROWS)]
    left_half = lax.broadcasted_iota(jnp.int32, (GRID_W, 2 * GRID_W), 1) < GRID_W
    for hl in range(NA_HSTEP):
        hs = slice(hl * HEAD_DIM, (hl + 1) * HEAD_DIM)
        bias = jnp.concatenate(
            [jnp.concatenate([jnp.where(left_half, bl_ref[0, hl, idx[qr][2 * kp]], br_ref[0, hl, idx[qr][2 * kp + 1]])
                              for kp in range(NA_KROWS // 2)], axis=1) for qr in range(NA_QROWS)], axis=0)
        s_c = lax.dot_general(q[:, hs], kc[:, hs], dims, preferred_element_type=F32)
        s_n = lax.dot_general(q[:, hs], kn[:, hs], dims, preferred_element_type=F32) + bias
        m = jnp.maximum(jnp.max(s_c, axis=1, keepdims=True), jnp.max(s_n, axis=1, keepdims=True))
        p_c = jnp.exp2(s_c - m).astype(BF16)
        p_n = jnp.exp2(s_n - m).astype(BF16)
        vc1 = jnp.concatenate([vc[:, hs], jnp.ones((n_ctx, HEAD_DIM), BF16)], axis=1)
        vn1 = jnp.concatenate([vn[:, hs], jnp.ones((nk, HEAD_DIM), BF16)], axis=1)
        o = jnp.dot(p_c, vc1, preferred_element_type=F32) + jnp.dot(p_n, vn1, preferred_element_type=F32)
        o = o[:, :HEAD_DIM] / o[:, HEAD_DIM:HEAD_DIM + 1]
        gate = g_ref[0, :, hs].astype(F32)
        o_ref[0, :, hs] = (o * _silu(gate)).astype(o_ref.dtype)


def _na_attention(q, k, v, g, bias_left, bias_right, n_ctx):
    nb, rows, wd = q.shape
    grid_rows = (rows - n_ctx) // GRID_W
    nq = NA_QROWS * GRID_W
    assert n_ctx == nq, "context tokens form exactly one query block"
    lanes = NA_HSTEP * HEAD_DIM
    qspec = pl.BlockSpec((1, nq, lanes), lambda b, p, r: (b, r, p))
    kvspec = pl.BlockSpec((1, rows, lanes), lambda b, p, r: (b, 0, p))
    bspec = pl.BlockSpec((1,) + bias_left.shape[1:], lambda b, p, r: (p, 0, 0, 0, 0))
    body = functools.partial(_na_body, n_ctx=n_ctx, grid_rows=grid_rows)
    return pl.pallas_call(
        body,
        grid=(nb, wd // lanes, rows // nq),
        in_specs=[qspec, kvspec, kvspec, qspec, bspec, bspec],
        out_specs=qspec,
        out_shape=jax.ShapeDtypeStruct(q.shape, BF16),
        compiler_params=_cparams(("parallel", "parallel", "parallel")),
        name="na_attention",
    )(q, k, v, g, bias_left, bias_right)


def _na_layer(h, mod_b, gn, w_in, rpb, w_out, n_ctx):
    nb, rows, _ = h.shape
    layout = _bmajor(nb, rows, n_ctx, ROW_TILE)
    _, row, _ = layout
    wd = NA_HEADS * HEAD_DIM
    out_shapes, out_specs = _rows_out((nb, rows), row, [(wd, BF16)] * 4)
    q, k, v, g = _in_proj((h,), [row(D)], _load_block, mod_b, gn, w_in.astype(BF16), layout,
                          out_shapes, out_specs, _na_epilogue, name="na_in_proj")
    bias_left, bias_right = _na_bias_tiles(rpb)
    z = _na_attention(q, k, v, g, bias_left, bias_right, n_ctx)
    return _out_proj(z, w_out.astype(BF16), h, mod_b, layout, name="na_out_proj")


def _s5_epilogue(j, acc, lead, extra, outs):
    u_ref, g_ref = outs
    lo = j * acc.shape[1]
    if lo < S5_WIDTH:
        _put(u_ref, lo, acc, lead)
    else:
        _put(g_ref, lo - S5_WIDTH, acc, lead)


def _s5_matrices(a_re, a_im, log_dt, b_re, b_im, c_re, c_im, d_skip):
    L = S5_CHUNK
    lam = lax.complex(a_re.astype(F32), a_im.astype(F32))
    lam_dt = lam * jnp.exp(log_dt.astype(F32))[..., None]
    lam_bar = jnp.exp(lam_dt)
    b_bar = ((lam_bar - 1.0) / lam)[..., None] * lax.complex(b_re.astype(F32), b_im.astype(F32))
    c_mat = lax.complex(c_re.astype(F32), c_im.astype(F32))
    tau = np.arange(L)

    def power(expo, d):
        return jnp.exp(jnp.asarray(expo, F32).reshape(expo.shape + (1, 1)) * lam_dt[d])

    w = jnp.stack([c_mat[d][None] * power(tau, d)[:, :, None, :] for d in range(2)])
    w2 = jnp.concatenate([jnp.real(w), -jnp.imag(w)], axis=-1)
    b2 = jnp.concatenate([jnp.real(b_bar), jnp.imag(b_bar)], axis=-2)
    kern = jnp.einsum('dkgip,dgpj->dkgij', w2, b2, precision=lax.Precision.HIGH).reshape(2 * L, S5_GROUPS,
                                                                                         S5_GROUP, S5_GROUP)
    lag = tau[None, :] - tau[:, None]
    sel = np.concatenate([lag[None] == tau[:, None, None], -lag[None] == tau[:, None, None]]).astype(np.float32)
    both = jnp.einsum('ktu,kgij->tugij', jnp.asarray(sel), kern, precision=lax.Precision.HIGHEST)
    eye_t = jnp.eye(L, dtype=F32)[:, :, None, None, None]
    skip = jnp.eye(S5_GROUP, dtype=F32)[None] * d_skip.astype(F32).reshape(S5_GROUPS, S5_GROUP, 1)
    both = both + eye_t * skip[None, None]
    toep = both.transpose(2, 0, 4, 1, 3).reshape(S5_GROUPS, L * S5_GROUP, L * S5_GROUP)

    def lanes(z):
        pad = [(0, 0)] * (z.ndim - 1) + [(0, LANES - S5_STATE)]
        return jnp.concatenate([jnp.pad(jnp.real(z), pad), jnp.pad(jnp.imag(z), pad)], axis=-1)

    pf = power(L - 1 - tau, 0)[:, :, :, None] * b_bar[0][None]
    pb = power(tau, 1)[:, :, :, None] * b_bar[1][None]
    pmat = jnp.concatenate([lanes(pf.transpose(1, 0, 3, 2)), lanes(pb.transpose(1, 0, 3, 2))], axis=-1)
    pmat = pmat.reshape(S5_GROUPS, L * S5_GROUP, 4 * LANES)
    qf = c_mat[0][None] * power(tau + 1, 0)[:, :, None, :]
    qb = c_mat[1][None] * power(L - tau, 1)[:, :, None, :]

    def state_rows(z):
        pad = ((0, 0), (0, LANES - S5_STATE), (0, 0), (0, 0))
        zr = jnp.pad(jnp.real(z).transpose(1, 3, 0, 2), pad)
        zi = jnp.pad(-jnp.imag(z).transpose(1, 3, 0, 2), pad)
        return jnp.concatenate([zr, zi], axis=1).reshape(S5_GROUPS, 2 * LANES, L * S5_GROUP)

    qmat = jnp.concatenate([state_rows(qf), state_rows(qb)], axis=1)
    lam_l = jnp.exp(float(L) * lam_dt)
    pad = ((0, 0), (0, LANES - S5_STATE))
    coef = jnp.stack([jnp.pad(jnp.real(lam_l[0]), pad), jnp.pad(jnp.imag(lam_l[0]), pad),
                      jnp.pad(jnp.real(lam_l[1]), pad), jnp.pad(jnp.imag(lam_l[1]), pad)], axis=1)
    coef = jnp.pad(coef, ((0, 0), (0, 4), (0, 0)))
    return toep.astype(BF16), pmat.astype(BF16), qmat.astype(BF16), coef


def _atom_transpose(xs, atom):
    n = len(xs)
    group = lax.broadcasted_iota(jnp.int32, xs[0].shape, 1) // atom
    xs = list(xs)
    d = n // 2
    while d:
        keep = (group & d) == 0
        for i in range(n):
            if not i & d:
                lo, hi = xs[i], xs[i + d]
                xs[i] = jnp.where(keep, lo, pltpu.roll(hi, d * atom, 1))
                xs[i + d] = jnp.where(keep, pltpu.roll(lo, LANES - d * atom, 1), hi)
        d //= 2
    return xs


def _s5_body(u_ref, toep_ref, p_ref, q_ref, coef_ref, y_ref, u_scr, y_scr, s_ref, hin_ref, *, nb, nc_ctx, nc):
    nwin = LANES // S5_GROUP
    nrow, nlat = u_ref.shape[1], y_ref.shape[1]
    blk = S5_ROWS
    for half in range(S5_CHUNK // nwin):
        cols = slice(half * LANES, (half + 1) * LANES)
        for r in range(0, nrow, blk):
            words = [pltpu.bitcast(u_ref[half * nwin + w, r:r + blk, :], jnp.uint32) for w in range(nwin)]
            for gl, x in enumerate(_atom_transpose(words, S5_GROUP)):
                u_scr[gl, r:r + blk, cols] = pltpu.bitcast(x, BF16)
    for gl in range(S5_SLAB):
        _s5_group(gl, u_scr, toep_ref, p_ref, q_ref, coef_ref, y_scr, s_ref, hin_ref, nb, nc_ctx, nc)
    for half in range(S5_CHUNK // nwin):
        cols = slice(half * LANES, (half + 1) * LANES)
        for r in range(0, nlat, blk):
            ys = [y_scr[gl, r:r + blk, cols] for gl in range(S5_SLAB)]
            for w, x in enumerate(_atom_transpose(ys, S5_GROUP)):
                y_ref[half * nwin + w, r:r + blk, :] = x


def _s5_group(gl, u_scr, toep_ref, p_ref, q_ref, coef_ref, y_scr, s_ref, hin_ref, nb, nc_ctx, nc):
    u = u_scr[gl]
    s_ref[...] = jnp.dot(u, p_ref[gl], preferred_element_type=F32)
    coef = coef_ref[gl]
    shape = (nb, LANES)
    lr_f, li_f = jnp.broadcast_to(coef[0:1], shape), jnp.broadcast_to(coef[1:2], shape)
    lr_b, li_b = jnp.broadcast_to(coef[2:3], shape), jnp.broadcast_to(coef[3:4], shape)
    zero = jnp.zeros(shape, F32)

    def step(q, carry):
        fr, fi, br, bi = carry
        rows_f = pl.ds(pl.multiple_of(q * nb, nb), nb)
        cb = jnp.where(q < nc_ctx, nc_ctx - 1 - q, nc + nc_ctx - 1 - q)
        rows_b = pl.ds(pl.multiple_of(cb * nb, nb), nb)
        hin_ref[rows_f, 0:LANES] = fr
        hin_ref[rows_f, LANES:2 * LANES] = fi
        hin_ref[rows_b, 2 * LANES:3 * LANES] = br
        hin_ref[rows_b, 3 * LANES:4 * LANES] = bi
        sfr, sfi = s_ref[rows_f, 0:LANES], s_ref[rows_f, LANES:2 * LANES]
        sbr, sbi = s_ref[rows_b, 2 * LANES:3 * LANES], s_ref[rows_b, 3 * LANES:4 * LANES]
        return (lr_f * fr - li_f * fi + sfr, lr_f * fi + li_f * fr + sfi,
                lr_b * br - li_b * bi + sbr, lr_b * bi + li_b * br + sbi)

    lax.fori_loop(0, nc, step, (zero, zero, zero, zero), unroll=2)

    lat = slice(nc_ctx * nb, nc * nb)
    y_scr[gl] = (jnp.dot(u[lat], toep_ref[gl], preferred_element_type=F32)
                 + jnp.dot(hin_ref[lat, :].astype(BF16), q_ref[gl], preferred_element_type=F32))


def _s5_scan(u_t, toep, pmat, qmat, coef, nb, n_ctx):
    L, nrow, wd = u_t.shape
    pk = L * S5_GROUP
    nc = nrow // nb
    nc_ctx = n_ctx // L
    nlat = (nc - nc_ctx) * nb
    assert nrow % S5_ROWS == 0 and nlat % S5_ROWS == 0
    once = pl.Buffered(1)
    slab = lambda shape: pl.BlockSpec((S5_SLAB,) + shape, lambda s: (s, 0, 0), pipeline_mode=once)
    body = functools.partial(_s5_body, nb=nb, nc_ctx=nc_ctx, nc=nc)
    return pl.pallas_call(
        body,
        grid=(wd // LANES,),
        in_specs=[pl.BlockSpec((L, nrow, LANES), lambda s: (0, 0, s)),
                  slab((pk, pk)), slab((pk, 4 * LANES)), slab((4 * LANES, pk)), slab((8, LANES))],
        out_specs=pl.BlockSpec((L, nlat, LANES), lambda s: (0, 0, s)),
        out_shape=jax.ShapeDtypeStruct((L, nlat, wd), F32),
        scratch_shapes=[pltpu.VMEM((S5_SLAB, nrow, pk), BF16), pltpu.VMEM((S5_SLAB, nlat, pk), F32),
                        pltpu.VMEM((nrow, 4 * LANES), F32), pltpu.VMEM((nrow, 4 * LANES), F32)],
        compiler_params=_cparams(("parallel",)),
        name="s5_scan",
    )(u_t, toep, pmat, qmat, coef)


def _rms(x, g):
    return x * lax.rsqrt(jnp.mean(x * x, axis=-1, keepdims=True) + EPS) * g


def _s5_readout_body(y_ref, g_ref, gw_ref, gb_ref, w_ref, h_ref, mod_ref, nf_ref, o_ref):
    y = jax.nn.gelu(y_ref[0])
    t = y * _sigmoid(jnp.dot(y.astype(BF16), gw_ref[...], preferred_element_type=F32) + gb_ref[...])
    z = (t * _silu(g_ref[0].astype(F32))).astype(BF16)
    out = jnp.dot(z, w_ref[...], preferred_element_type=F32)
    h = h_ref[0] + mod_ref[0, :, 2 * D:3 * D] * out
    o_ref[0] = _rms(h, nf_ref[...])


def _s5_readout(y, g, glu_w, glu_b, w_out, h, mod_b, norm_f, n_ctx):
    nb, t_len, _ = y.shape
    tm = n_ctx
    off = n_ctx // tm
    lat = lambda wd: pl.BlockSpec((1, tm, wd), lambda b, i: (b, i, 0))
    full = lambda wd: pl.BlockSpec((1, tm, wd), lambda b, i: (b, i + off, 0))
    const = lambda shape: pl.BlockSpec(shape, lambda b, i: (0, 0))
    return pl.pallas_call(
        _s5_readout_body,
        grid=(nb, t_len // tm),
        in_specs=[lat(S5_WIDTH), full(S5_WIDTH), const((S5_WIDTH, S5_WIDTH)), const((1, S5_WIDTH)),
                  const((S5_WIDTH, D)), full(D),
                  pl.BlockSpec((1, 1, 3 * D), lambda b, i: (b, 0, 0)), const((1, D))],
        out_specs=lat(D),
        out_shape=jax.ShapeDtypeStruct((nb, t_len, D), F32),
        compiler_params=_cparams(("parallel", "parallel")),
        name="s5_readout",
    )(y, g, glu_w.astype(BF16), glu_b.reshape(1, S5_WIDTH), w_out.astype(BF16), h, mod_b,
      norm_f.reshape(1, D))


def _s5_layer(h, mod_b, mod_lat, gn, w_in, a_re, a_im, log_dt, b_re, b_im, c_re, c_im, d_skip, glu_w, glu_b,
              w_out, norm_f, n_ctx):
    nb, rows, _ = h.shape
    L = S5_CHUNK
    layout = _bmajor(nb, rows, n_ctx, ROW_TILE)
    _, row, _ = layout
    out_shapes, out_specs = _rows_out((nb, rows), row, [(S5_WIDTH, BF16)] * 2)
    u, g = _in_proj((h,), [row(D)], _load_block, mod_b, gn, w_in.astype(BF16), layout, out_shapes, out_specs,
                    _s5_epilogue, name="s5_in_proj")
    nc = rows // L
    u_t = u.reshape(nb, nc, L, S5_WIDTH).transpose(2, 1, 0, 3).reshape(L, nc * nb, S5_WIDTH)
    toep, pmat, qmat, coef = _s5_matrices(a_re, a_im, log_dt, b_re, b_im, c_re, c_im, d_skip)
    y_t = _s5_scan(u_t, toep, pmat, qmat, coef, nb, n_ctx)
    t_len = rows - n_ctx
    y = y_t.reshape(L, t_len // L, nb, S5_WIDTH).transpose(2, 1, 0, 3).reshape(nb, t_len, S5_WIDTH)
    return _s5_readout(y, g, glu_w, glu_b, w_out, h, mod_lat, norm_f, n_ctx)


def kernel(x, c, ctx, c_ctx, ada_w0, ada_b0, norm0, w_in0, conv_w0, conv_b0, lru_wa0, lru_ba0, lru_wx0, lru_bx0, lru_lam0, w_out0, ada_w1, ada_b1, norm1, w_in1, sink1, w_out1, ada_w2, ada_b2, norm2, w_in2, rpb2, w_out2, ada_w3, ada_b3, norm3, w_in3, s5_a_re3, s5_a_im3, s5_log_dt3, s5_b_re3, s5_b_im3, s5_c_re3, s5_c_im3, s5_d3, glu_w3, glu_b3, w_out3, norm_f):
    nb, t_len, _ = x.shape
    n_ctx = ctx.shape[1]
    cvec = jnp.concatenate([c, c_ctx[None, :], jnp.zeros((16 - nb - 1, D), F32)], axis=0)
    mods = [_modulation(cvec, w, b) for w, b in
            ((ada_w0, ada_b0), (ada_w1, ada_b1), (ada_w2, ada_b2), (ada_w3, ada_b3))]
    mod_b = [_bmajor_mod_table(m, nb, n_ctx + t_len, n_ctx) for m in mods]

    mod_t0 = jnp.stack([jnp.broadcast_to(mods[0][nb], (nb, 3 * D)), mods[0][:nb]], axis=0)
    h = _rglru_layer(ctx, x, mod_t0, norm0, w_in0, conv_w0, conv_b0, lru_wa0, lru_ba0, lru_wx0, lru_bx0,
                     lru_lam0, w_out0)
    h = _swa_layer(h, mod_b[1], norm1, w_in1, sink1, w_out1, n_ctx)
    h = _na_layer(h, mod_b[2], norm2, w_in2, rpb2, w_out2, n_ctx)
    return _s5_layer(h, mod_b[3], mods[3][:nb, None, :], norm3, w_in3, s5_a_re3, s5_a_im3, s5_log_dt3,
                     s5_b_re3, s5_b_im3, s5_c_re3, s5_c_im3, s5_d3, glu_w3, glu_b3, w_out3, norm_f, n_ctx)
```

```python
import functools
import math

import jax
import jax.numpy as jnp
import numpy as np
from jax import lax
from jax.experimental import pallas as pl
from jax.experimental.pallas import tpu as pltpu

F32 = jnp.float32
BF16 = jnp.bfloat16

D = 1024
HEAD_DIM = 64
GRID_W = 64
EPS = 1e-6
NEG_INF = -1e30
ROPE_BASE = 10000.0
LANES = 128
LOG2E = math.log2(math.e)
QK_SCALE = HEAD_DIM ** -0.5 * LOG2E
ONES_ROWS = 16

LRU_WIDTH = 1408
LRU_BLOCKS = 16
LRU_BW = LRU_WIDTH // LRU_BLOCKS
LRU_WIN = 3
CONV_W = 4
LRU_C = 8.0
LRU_TT = 64

SWA_HEADS = 16
SWA_KV_HEADS = 4
WINDOW = 128
BLOCK_Q = 128

NA_HEADS = 16
NA_ROWS = 8
NA_COLS = 16
NA_QROWS = 4
NA_KROWS = 12
NA_HSTEP = 8
NA_MASKED = 2 * NA_ROWS - 1

S5_WIDTH = 1024
S5_GROUP = 16
S5_GROUPS = S5_WIDTH // S5_GROUP
S5_STATE = 64
S5_CHUNK = 16
S5_SLAB = LANES // S5_GROUP
S5_ROWS = 128

ROW_TILE = 768
VMEM_LIMIT = 56 * 1024 * 1024


def _cparams(sem):
    return pltpu.CompilerParams(dimension_semantics=sem, vmem_limit_bytes=VMEM_LIMIT)


def _sigmoid(v):
    return 0.5 * jnp.tanh(0.5 * v) + 0.5


def _silu(v):
    return v * _sigmoid(v)


def _bmajor(nb, rows, n_ctx, tm):
    grid = (nb, rows // tm)
    per_batch = rows // tm

    def row(width, off=0):
        return pl.BlockSpec((1, tm, width), lambda b, i: (b, i + off, 0))

    mod = pl.BlockSpec((tm // n_ctx, 1, 3 * D), lambda b, i: (b * per_batch + i, 0, 0))
    return grid, row, mod


def _bmajor_mod_table(mod16, nb, rows, n_ctx):
    groups = rows // n_ctx
    tab = jnp.broadcast_to(mod16[:nb, None, :], (nb, groups, 3 * D))
    tab = tab.at[:, 0, :].set(mod16[nb])
    return tab.reshape(nb * groups, 1, 3 * D)


def _tmajor(nb, rows, n_ctx, tt):
    grid = (1, rows // tt)
    nct = n_ctx // tt

    def row(width, off=0):
        return pl.BlockSpec((tt, nb, width), lambda b, i: (i + off, 0, 0))

    mod = pl.BlockSpec((1, nb, 3 * D), lambda b, i: (jnp.where(i < nct, 0, 1), 0, 0))
    return grid, row, mod


def _mod_body(c_ref, w_ref, b_ref, o_ref):
    s = _silu(c_ref[...]).astype(BF16)
    o_ref[...] = jnp.dot(s, w_ref[...].astype(BF16), preferred_element_type=F32) + b_ref[...]


def _modulation(cvec, ada_w, ada_b):
    n = ada_w.shape[1]
    tn = 1024
    return pl.pallas_call(
        _mod_body,
        grid=(n // tn,),
        in_specs=[pl.BlockSpec((16, D), lambda j: (0, 0)),
                  pl.BlockSpec((D, tn), lambda j: (0, j)),
                  pl.BlockSpec((1, tn), lambda j: (0, j))],
        out_specs=pl.BlockSpec((16, tn), lambda j: (0, j)),
        out_shape=jax.ShapeDtypeStruct((16, n), F32),
        compiler_params=_cparams(("parallel",)),
        name="modulation",
    )(cvec, ada_w, ada_b.reshape(1, n))


def _by_mod_group(x3, mod3):
    groups = mod3.shape[0]
    return x3 if groups == 1 else x3.reshape(groups, x3.shape[1] // groups, x3.shape[2])


def _norm_mod(x3, mod3, gn):
    xg = _by_mod_group(x3, mod3)
    ms = jnp.mean(xg * xg, axis=-1, keepdims=True)
    y = xg * lax.rsqrt(ms + EPS) * gn
    return (y * (1.0 + mod3[:, :, D:2 * D]) + mod3[:, :, 0:D]).reshape(x3.shape)


def _in_proj_body(*refs, load, n_x, epilogue, n_extra, chunk):
    xs, (mod_ref, gn_ref, w_ref), rest = refs[:n_x], refs[n_x:n_x + 3], refs[n_x + 3:]
    extra, outs = rest[:n_extra], rest[n_extra:]
    x3 = load(*xs)
    n3 = _norm_mod(x3, mod_ref[...], gn_ref[...])
    lead = x3.shape[:2]
    n = n3.reshape(lead[0] * lead[1], D).astype(BF16)
    for j in range(w_ref.shape[1] // chunk):
        acc = jnp.dot(n, w_ref[:, j * chunk:(j + 1) * chunk], preferred_element_type=F32)
        epilogue(j, acc, lead, extra, outs)


def _load_block(x_ref):
    return x_ref[...]


def _in_proj(xs, x_specs, load, mod, gn, w, layout, out_shapes, out_specs, epilogue, extras=(),
             extra_specs=(), chunk=512, name="in_proj"):
    grid, _, mod_spec = layout
    n = w.shape[1]
    body = functools.partial(_in_proj_body, load=load, n_x=len(xs), epilogue=epilogue,
                             n_extra=len(extras), chunk=chunk)
    return pl.pallas_call(
        body,
        grid=grid,
        in_specs=[*x_specs, mod_spec,
                  pl.BlockSpec((1, D), lambda b, i: (0, 0)),
                  pl.BlockSpec((D, n), lambda b, i: (0, 0)),
                  *extra_specs],
        out_specs=out_specs,
        out_shape=out_shapes,
        compiler_params=_cparams(("parallel", "parallel")),
        name=name,
    )(*xs, mod, gn.reshape(1, D), w, *extras)


def _rows_out(shape2, row, outs):
    return ([jax.ShapeDtypeStruct(shape2 + (wd,), dt) for wd, dt in outs], [row(wd) for wd, _ in outs])


def _put(ref, lo, val, lead):
    ref[:, :, lo:lo + val.shape[1]] = val.reshape(lead + (val.shape[1],)).astype(ref.dtype)


def _out_proj_body(*refs, load, n_h, store):
    (z_ref, w_ref), hs, (mod_ref, o_ref) = refs[:2], refs[2:2 + n_h], refs[2 + n_h:]
    z3 = z_ref[...]
    a, b, wd = z3.shape
    y = jnp.dot(z3.reshape(a * b, wd), w_ref[...], preferred_element_type=F32).reshape(a, b, D)
    mod3 = mod_ref[...]
    h = _by_mod_group(load(*hs), mod3) + mod3[:, :, 2 * D:3 * D] * _by_mod_group(y, mod3)
    store(o_ref, h.reshape(a, b, D))


def _store_block(o_ref, val):
    o_ref[...] = val


def _out_proj(z, w, h, mod, layout, name):
    grid, row, mod_spec = layout
    wd = w.shape[0]
    body = functools.partial(_out_proj_body, load=_load_block, n_h=1, store=_store_block)
    return pl.pallas_call(
        body,
        grid=grid,
        in_specs=[row(wd), pl.BlockSpec((wd, D), lambda b, i: (0, 0)), row(D), mod_spec],
        out_specs=row(D),
        out_shape=jax.ShapeDtypeStruct(h.shape, F32),
        input_output_aliases={2: 0},
        compiler_params=_cparams(("parallel", "parallel")),
        name=name,
    )(z, w, h, mod)


def _lru_epilogue(j, acc, lead, extra, outs):
    _put(outs[j], 0, acc, lead)


def _lru_tile_index(s, nct, ntiles, reverse):
    if not reverse:
        return s
    return jnp.where(s < nct, nct - 1 - s, ntiles + nct - 1 - s)


def _lru_sweep_body(up_ref, uc_ref, un_ref, cw_ref, cb_ref, wg_ref, bg_ref, lam_ref, *rest,
                    nct, ntiles, reverse):
    if reverse:
        hf_ref, g_ref, o_ref, ext_ref, cv_ref, a_ref, b_ref, carry_ref = rest
    else:
        o_ref, ext_ref, cv_ref, a_ref, b_ref, carry_ref = rest
    s = pl.program_id(0)
    ti = _lru_tile_index(s, nct, ntiles, reverse)
    tt, nb, pw = uc_ref.shape

    @pl.when(s == 0)
    def _():
        carry_ref[...] = jnp.zeros_like(carry_ref)

    seg_start = (ti == 0) | (ti == nct)
    seg_end = (ti == nct - 1) | (ti == ntiles - 1)
    ext_ref[0:2] = jnp.where(seg_start, 0.0, up_ref[...])
    ext_ref[2:2 + tt] = uc_ref[...]
    ext_ref[2 + tt:3 + tt] = jnp.where(seg_end, 0.0, un_ref[...])

    softplus_neg_lam = jax.nn.softplus(-lam_ref[...])
    tiles = [slice(k * LANES, (k + 1) * LANES) for k in range(pw // LANES)]
    for sl in tiles:
        u = cb_ref[:, sl]
        for tap in range(CONV_W):
            u = u + cw_ref[tap:tap + 1, sl] * ext_ref[tap:tap + tt, :, sl]
        cv_ref[:, sl] = u.reshape(tt * nb, LANES)
    for k, sl in enumerate(tiles):
        win = slice(_lru_window(k) * LANES, (_lru_window(k) + LRU_WIN) * LANES)
        pre = jnp.dot(cv_ref[:, win].astype(BF16), wg_ref[k], preferred_element_type=F32) + bg_ref[k]
        r = _sigmoid(pre[:, :LANES])
        i = _sigmoid(pre[:, LANES:])
        u = cv_ref[:, sl]
        a = jnp.exp(-LRU_C * r * softplus_neg_lam[:, sl])
        a_ref[:, sl] = a
        b_ref[:, sl] = jnp.sqrt(1.0 - a * a) * (i * u)

    def step(q, h):
        t = tt - 1 - q if reverse else q
        rows = pl.ds(pl.multiple_of(t * nb, nb), nb)
        h = a_ref[rows, :] * h + b_ref[rows, :]
        if reverse:
            o_ref[t] = ((hf_ref[t] + h) * _silu(g_ref[t].astype(F32))).astype(o_ref.dtype)
        else:
            o_ref[t] = h
        return h

    carry_ref[...] = lax.fori_loop(0, tt, step, carry_ref[...], unroll=4)


def _lru_sweep(u, cw, cb, wg, bg, lam, nct_rows, reverse, hf=None, g=None):
    rows, nb, pw = u.shape
    tt = LRU_TT
    ntiles = rows // tt
    nct = nct_rows // tt
    tile = functools.partial(_lru_tile_index, nct=nct, ntiles=ntiles, reverse=reverse)
    cur = lambda wd: pl.BlockSpec((tt, nb, wd), lambda s: (tile(s), 0, 0))
    const = lambda shape: pl.BlockSpec(shape, lambda s: (0,) * len(shape))
    in_specs = [
        pl.BlockSpec((2, nb, pw), lambda s: (jnp.maximum(tile(s) * (tt // 2) - 1, 0), 0, 0)),
        cur(pw),
        pl.BlockSpec((1, nb, pw), lambda s: (jnp.minimum((tile(s) + 1) * tt, rows - 1), 0, 0)),
        const((CONV_W, pw)), const((1, pw)),
        const(wg.shape), const(bg.shape), const((1, pw)),
    ]
    args = [u, u, u, cw, cb, wg, bg, lam]
    if reverse:
        in_specs += [cur(pw), cur(pw)]
        args += [hf, g]
    body = functools.partial(_lru_sweep_body, nct=nct, ntiles=ntiles, reverse=reverse)
    return pl.pallas_call(
        body,
        grid=(ntiles,),
        in_specs=in_specs,
        out_specs=cur(pw),
        out_shape=jax.ShapeDtypeStruct(u.shape, BF16 if reverse else F32),
        scratch_shapes=[pltpu.VMEM((tt + 3, nb, pw), F32),
                        pltpu.VMEM((tt * nb, pw), F32),
                        pltpu.VMEM((tt * nb, pw), F32),
                        pltpu.VMEM((tt * nb, pw), F32),
                        pltpu.VMEM((nb, pw), F32)],
        compiler_params=_cparams(("arbitrary",)),
        name="lru_bwd" if reverse else "lru_fwd",
    )(*args)


def _lru_window(k):
    first_block = (k * LANES) // LRU_BW
    return min((first_block * LRU_BW) // LANES, LRU_WIDTH // LANES - LRU_WIN)


def _lru_gate_weights(wa, wx, ba, bx):
    ntile = LRU_WIDTH // LANES
    eye = jnp.eye(LRU_BLOCKS, dtype=F32)
    w = jnp.stack([wa, wx], axis=1)
    dense = (w[:, :, :, :, None, :] * eye[:, None, :, None]).reshape(2, 2, LRU_WIDTH, LRU_WIDTH)
    tiles = dense.reshape(2, 2, ntile, LANES, ntile, LANES).transpose(0, 1, 4, 2, 3, 5)
    window = np.array([[_lru_window(k) + i for i in range(LRU_WIN)] for k in range(ntile)])
    picked = tiles[:, :, np.arange(ntile)[:, None], window]
    wg = picked.transpose(0, 2, 3, 4, 1, 5).reshape(2, ntile, LRU_WIN * LANES, 2 * LANES)
    bg = jnp.stack([ba, bx], axis=1).reshape(2, 2, ntile, LANES).transpose(0, 2, 1, 3)
    return wg.astype(BF16), bg.reshape(2, ntile, 1, 2 * LANES)


def _load_time_major(ctx_ref, x_ref, *, nct):
    blk = jnp.where(pl.program_id(1) < nct, ctx_ref[...], x_ref[...])
    return pltpu.einshape("btd->tbd", blk)


def _store_batch_major(o_ref, val):
    o_ref[...] = pltpu.einshape("tbd->btd", val)


def _rglru_layer(ctx, x, mod_t, gn, w_in, conv_w, conv_b, wa, ba, wx, bx, lam, w_out):
    nb, n_ctx, _ = ctx.shape
    rows = n_ctx + x.shape[1]
    tt = LRU_TT
    nct = n_ctx // tt
    layout = _tmajor(nb, rows, n_ctx, tt)
    grid, row, mod_spec = layout
    src_specs = [pl.BlockSpec((nb, tt, D), lambda b, i: (0, jnp.minimum(i, nct - 1), 0)),
                 pl.BlockSpec((nb, tt, D), lambda b, i: (0, jnp.maximum(i - nct, 0), 0))]
    load = functools.partial(_load_time_major, nct=nct)
    wd = LRU_WIDTH
    out_shapes, out_specs = _rows_out((rows, nb), row, [(wd, F32), (wd, BF16)])
    u, g = _in_proj((ctx, x), src_specs, load, mod_t, gn, w_in.astype(BF16), layout, out_shapes, out_specs,
                    _lru_epilogue, chunk=wd, name="lru_in_proj")
    cb = conv_b.reshape(1, wd)
    wg, bg = _lru_gate_weights(wa, wx, ba, bx)
    sweeps = [(conv_w, cb, wg[d], bg[d], lam[d:d + 1]) for d in range(2)]
    hf = _lru_sweep(u, *sweeps[0], n_ctx, reverse=False)
    z = _lru_sweep(u, *sweeps[1], n_ctx, reverse=True, hf=hf, g=g)
    body = functools.partial(_out_proj_body, load=load, n_h=2, store=_store_batch_major)
    return pl.pallas_call(
        body,
        grid=grid,
        in_specs=[row(wd), pl.BlockSpec((wd, D), lambda b, i: (0, 0)), *src_specs, mod_spec],
        out_specs=pl.BlockSpec((nb, tt, D), lambda b, i: (0, i, 0)),
        out_shape=jax.ShapeDtypeStruct((nb, rows, D), F32),
        compiler_params=_cparams(("parallel", "parallel")),
        name="lru_out_proj",
    )(z, w_out.astype(BF16), ctx, x, mod_t)


def _rope_tables(n_ctx, t_len):
    pos = jnp.arange(t_len)
    row = (pos // GRID_W).astype(F32)
    col = (pos % GRID_W).astype(F32)
    n_ax = HEAD_DIM // 4
    freqs = ROPE_BASE ** (-jnp.arange(n_ax, dtype=F32) / n_ax)
    ang = jnp.concatenate([row[:, None] * freqs, col[:, None] * freqs], axis=-1)
    cos, sin = jnp.cos(ang), jnp.sin(ang)
    cos_h = jnp.concatenate([cos, cos], axis=-1)
    sin_h = jnp.concatenate([-sin, sin], axis=-1)
    cos_f = jnp.concatenate([jnp.ones((n_ctx, HEAD_DIM), F32), cos_h], axis=0)
    sin_f = jnp.concatenate([jnp.zeros((n_ctx, HEAD_DIM), F32), sin_h], axis=0)
    return jnp.tile(cos_f, (1, 2)), jnp.tile(sin_f, (1, 2))


def _rope(x, cos, sin):
    lane = lax.broadcasted_iota(jnp.int32, x.shape, 1)
    swapped = jnp.where(lane % HEAD_DIM < HEAD_DIM // 2,
                        pltpu.roll(x, LANES - HEAD_DIM // 2, 1), pltpu.roll(x, HEAD_DIM // 2, 1))
    return x * cos + swapped * sin


def _swa_epilogue(j, acc, lead, extra, outs):
    cos_ref, sin_ref = extra
    q_ref, qr_ref, k_ref, v_ref, g_ref = outs
    qd = SWA_HEADS * HEAD_DIM
    kvd = SWA_KV_HEADS * HEAD_DIM
    grp = SWA_HEADS // SWA_KV_HEADS
    width = acc.shape[1]
    lo = j * width
    cos, sin = cos_ref[...], sin_ref[...]

    def roped(x):
        return jnp.concatenate([_rope(x[:, s:s + LANES], cos, sin) for s in range(0, x.shape[1], LANES)], axis=1)

    def head(x, hh):
        return x[:, hh * HEAD_DIM:(hh + 1) * HEAD_DIM]

    if lo < qd:
        q = acc * QK_SCALE
        qr = roped(q)
        for hh in range(width // HEAD_DIM):
            kv, gq = divmod(lo // HEAD_DIM + hh, grp)
            for rt in range(lead[1] // BLOCK_Q):
                rows = slice(rt * BLOCK_Q, (rt + 1) * BLOCK_Q)
                dst = slice(gq * BLOCK_Q, (gq + 1) * BLOCK_Q)
                q_ref[0, kv, rt, dst, :] = head(q, hh)[rows].astype(q_ref.dtype)
                qr_ref[0, kv, rt, dst, :] = head(qr, hh)[rows].astype(qr_ref.dtype)
    elif lo == qd:
        kr = roped(acc[:, :kvd])
        for kv in range(SWA_KV_HEADS):
            k_ref[0, kv] = head(kr, kv).astype(k_ref.dtype)
            v_ref[0, kv] = head(acc[:, kvd:], kv).T.astype(v_ref.dtype)
    else:
        _put(g_ref, lo - qd - 2 * kvd, acc, lead)


def _swa_mask_table():
    grp = SWA_HEADS // SWA_KV_HEADS
    kj = np.arange(3 * BLOCK_Q)[:, None]
    qi = np.arange(BLOCK_Q)[None, :]
    base = np.where(np.abs(kj - BLOCK_Q - qi) <= WINDOW, 0.0, NEG_INF).astype(np.float32)
    first = np.where(kj < BLOCK_Q, NEG_INF, base).astype(np.float32)
    last = np.where(kj >= 2 * BLOCK_Q, NEG_INF, base).astype(np.float32)
    tab = np.stack([np.full_like(base, NEG_INF), first, base, last])
    return jnp.asarray(np.tile(tab, (1, 1, grp)))


def _swa_body(sink_ref, mask_ref, q_ref, qr_ref, kc_ref, k0_ref, k1_ref, k2_ref, vc_ref, v0_ref, v1_ref,
              v2_ref, g_ref, o_ref, s_ref, p_ref):
    grp = SWA_HEADS // SWA_KV_HEADS
    dims = (((1,), (1,)), ((), ()))
    nc = kc_ref.shape[2]
    for kv in range(SWA_KV_HEADS):
        slot = kv % 2
        kb = jnp.concatenate([k0_ref[0, kv], k1_ref[0, kv], k2_ref[0, kv]], axis=0)
        vt = jnp.concatenate([vc_ref[0, kv], v0_ref[0, kv], v1_ref[0, kv], v2_ref[0, kv]], axis=1)
        vt = jnp.concatenate([vt, jnp.ones((ONES_ROWS, vt.shape[1]), BF16)], axis=0)
        s_ref[slot, :nc, :] = lax.dot_general(kc_ref[0, kv], q_ref[0, kv, 0], dims, preferred_element_type=F32)
        s_ref[slot, nc:, :] = (lax.dot_general(kb, qr_ref[0, kv, 0], dims, preferred_element_type=F32)
                               + mask_ref[0])
        sink_p = []
        for gq in range(grp):
            lanes = slice(gq * BLOCK_Q, (gq + 1) * BLOCK_Q)
            s = s_ref[slot, :, lanes]
            sink = sink_ref[kv, :, lanes]
            m = jnp.maximum(jnp.max(s, axis=0, keepdims=True), sink)
            sink_p.append(jnp.exp2(sink - m))
            p_ref[slot, :, lanes] = jnp.exp2(s - m).astype(BF16)
        ot = jnp.dot(vt, p_ref[slot], preferred_element_type=F32)
        denom = ot[HEAD_DIM:HEAD_DIM + 1] + jnp.concatenate(sink_p, axis=1)
        ot = ot[:HEAD_DIM] * (1.0 / denom)
        slab = jnp.concatenate([ot[:, gq * BLOCK_Q:(gq + 1) * BLOCK_Q].T for gq in range(grp)], axis=1)
        lanes = slice(kv * grp * HEAD_DIM, (kv + 1) * grp * HEAD_DIM)
        o_ref[0, :, lanes] = (slab * _silu(g_ref[0, :, lanes].astype(F32))).astype(o_ref.dtype)


def _swa_attention(q, qr, k, vt, g, sink, n_ctx):
    nb, nkv, nt, qrows, hd = q.shape
    rows = k.shape[2]
    grp = SWA_HEADS // SWA_KV_HEADS
    nct = n_ctx // BLOCK_Q
    nkeys = n_ctx + 3 * BLOCK_Q
    qspec = pl.BlockSpec((1, nkv, 1, qrows, hd), lambda b, n: (b, 0, n, 0, 0))
    kctx = pl.BlockSpec((1, nkv, n_ctx, hd), lambda b, n: (b, 0, 0, 0))
    vctx = pl.BlockSpec((1, nkv, hd, n_ctx), lambda b, n: (b, 0, 0, 0))
    rowspec = pl.BlockSpec((1, BLOCK_Q, g.shape[2]), lambda b, n: (b, n, 0))

    def kband(off):
        return pl.BlockSpec((1, nkv, BLOCK_Q, hd), lambda b, n: (b, 0, jnp.clip(n + off, nct, nt - 1), 0))

    def vband(off):
        return pl.BlockSpec((1, nkv, hd, BLOCK_Q), lambda b, n: (b, 0, 0, jnp.clip(n + off, nct, nt - 1)))

    def variant(b, n):
        return (jnp.where(n < nct, 0, jnp.where(n == nct, 1, jnp.where(n == nt - 1, 3, 2))), 0, 0)

    sink_rows = jnp.repeat(sink.astype(F32).reshape(nkv, grp) * LOG2E, BLOCK_Q, axis=1).reshape(nkv, 1, qrows)
    return pl.pallas_call(
        _swa_body,
        grid=(nb, nt),
        in_specs=[pl.BlockSpec((nkv, 1, qrows), lambda b, n: (0, 0, 0)),
                  pl.BlockSpec((1, 3 * BLOCK_Q, qrows), variant), qspec, qspec,
                  kctx, kband(-1), kband(0), kband(1), vctx, vband(-1), vband(0), vband(1), rowspec],
        out_specs=rowspec,
        out_shape=jax.ShapeDtypeStruct((nb, rows, g.shape[2]), BF16),
        scratch_shapes=[pltpu.VMEM((2, nkeys, qrows), F32), pltpu.VMEM((2, nkeys, qrows), BF16)],
        compiler_params=_cparams(("parallel", "parallel")),
        name="swa_attention",
    )(sink_rows, _swa_mask_table(), q, qr, k, k, k, k, vt, vt, vt, vt, g)


def _swa_layer(h, mod_b, gn, w_in, sink, w_out, n_ctx):
    nb, rows, _ = h.shape
    tm = ROW_TILE
    layout = _bmajor(nb, rows, n_ctx, tm)
    _, row, _ = layout
    qd = SWA_HEADS * HEAD_DIM
    grp = SWA_HEADS // SWA_KV_HEADS
    nkv = SWA_KV_HEADS
    cos, sin = _rope_tables(n_ctx, rows - n_ctx)
    tab = pl.BlockSpec((tm, LANES), lambda b, i: (i, 0))
    q_shape = jax.ShapeDtypeStruct((nb, nkv, rows // BLOCK_Q, grp * BLOCK_Q, HEAD_DIM), BF16)
    q_spec = pl.BlockSpec((1, nkv, tm // BLOCK_Q, grp * BLOCK_Q, HEAD_DIM), lambda b, i: (b, 0, i, 0, 0))
    k_shape = jax.ShapeDtypeStruct((nb, nkv, rows, HEAD_DIM), BF16)
    k_spec = pl.BlockSpec((1, nkv, tm, HEAD_DIM), lambda b, i: (b, 0, i, 0))
    vt_shape = jax.ShapeDtypeStruct((nb, nkv, HEAD_DIM, rows), BF16)
    vt_spec = pl.BlockSpec((1, nkv, HEAD_DIM, tm), lambda b, i: (b, 0, 0, i))
    q, qr, k, vt, g = _in_proj(
        (h,), [row(D)], _load_block, mod_b, gn, w_in.astype(BF16), layout,
        [q_shape, q_shape, k_shape, vt_shape, jax.ShapeDtypeStruct((nb, rows, qd), BF16)],
        [q_spec, q_spec, k_spec, vt_spec, row(qd)], _swa_epilogue,
        extras=(cos, sin), extra_specs=(tab, tab), name="swa_in_proj")
    z = _swa_attention(q, qr, k, vt, g, sink, n_ctx)
    return _out_proj(z, w_out.astype(BF16), h, mod_b, layout, name="swa_out_proj")


def _na_epilogue(j, acc, lead, extra, outs):
    q_ref, k_ref, v_ref, g_ref = outs
    wd = NA_HEADS * HEAD_DIM
    width = acc.shape[1]
    lo = j * width
    which, off = lo // wd, lo % wd
    if which == 3:
        _put(g_ref, off, acc, lead)
        return
    ref = (q_ref, k_ref, v_ref)[which]
    val = acc * QK_SCALE if which == 0 else acc
    for hh in range(width // HEAD_DIM):
        ref[0, off // HEAD_DIM + hh] = val[:, hh * HEAD_DIM:(hh + 1) * HEAD_DIM].astype(ref.dtype)


def _na_bias_tiles(rpb):
    reach = GRID_W - NA_COLS
    period = 2 * GRID_W
    ndy = 2 * NA_ROWS - 1
    edge = jnp.pad(rpb.astype(F32) * LOG2E, ((0, 0), (0, 0), (reach, reach)), mode="edge")
    flat = jnp.tile(jnp.pad(edge[:, :, ::-1], ((0, 0), (0, 0), (0, 1))), (1, 1, GRID_W))[:, :, :GRID_W * (period - 1)]
    e = flat.reshape(NA_HEADS, ndy, GRID_W, period - 1)[:, :, :, GRID_W - 1:]
    col = np.arange(GRID_W)
    cstart = np.clip(col - NA_COLS // 2, 0, GRID_W - NA_COLS)
    col_ok = (col[:, None] >= cstart[None, :]) & (col[:, None] < cstart[None, :] + NA_COLS)
    e = jnp.where(jnp.asarray(col_ok)[None, None], e, NEG_INF)
    e = jnp.pad(e, ((0, 0), (0, 1), (0, 0), (0, 0)), constant_values=NEG_INF)
    shape = (NA_HEADS // NA_HSTEP, NA_HSTEP, ndy + 1, GRID_W, period)
    left = jnp.pad(e, ((0, 0), (0, 0), (0, 0), (0, GRID_W)), constant_values=NEG_INF).reshape(shape)
    right = jnp.pad(e, ((0, 0), (0, 0), (0, 0), (GRID_W, 0)), constant_values=NEG_INF).reshape(shape)
    return left, right


def _na_body(q_ref, k_ref, v_ref, g_ref, bl_ref, br_ref, o_ref, s_ref, p_ref, *, n_ctx, grid_rows):
    rb = pl.program_id(2)
    r0 = (rb - 1) * NA_QROWS
    kb = jnp.clip(r0 - NA_ROWS // 2, 0, grid_rows - NA_KROWS)
    start = pl.multiple_of(n_ctx + kb * GRID_W, GRID_W)
    nk = NA_KROWS * GRID_W
    nq = q_ref.shape[2]
    a_bt = (((1,), (1,)), ((), ()))
    at_b = (((0,), (0,)), ((), ()))

    def tile_index(qr, kr):
        qrow, krow = r0 + qr, kb + kr
        first = jnp.clip(qrow - NA_ROWS // 2, 0, grid_rows - NA_ROWS)
        inside = (rb >= 1) & (krow >= first) & (krow < first + NA_ROWS)
        return jnp.where(inside, krow - qrow + NA_ROWS - 1, NA_MASKED)

    idx = [[tile_index(qr, kr) for kr in range(NA_KROWS)] for qr in range(NA_QROWS)]
    left_half = lax.broadcasted_iota(jnp.int32, (GRID_W, 2 * GRID_W), 1) < GRID_W
    ones = jnp.ones((n_ctx + nk, HEAD_DIM), BF16)
    outs = []
    for hl in range(NA_HSTEP):
        slot = hl % 2
        qh = q_ref[0, hl]
        bias = jnp.concatenate(
            [jnp.concatenate([jnp.where(left_half, bl_ref[0, hl, idx[2 * m][kr]], br_ref[0, hl, idx[2 * m + 1][kr]])
                              for m in range(NA_QROWS // 2)], axis=1) for kr in range(NA_KROWS)], axis=0)
        s_ref[slot, :n_ctx, :] = lax.dot_general(k_ref[0, hl, 0:n_ctx, :], qh, a_bt, preferred_element_type=F32)
        s_ref[slot, n_ctx:, :] = (lax.dot_general(k_ref[0, hl, pl.ds(start, nk), :], qh, a_bt,
                                                  preferred_element_type=F32) + bias)
        for c in range(nq // LANES):
            lanes = slice(c * LANES, (c + 1) * LANES)
            s = s_ref[slot, :, lanes]
            p_ref[slot, :, lanes] = jnp.exp2(s - jnp.max(s, axis=0, keepdims=True)).astype(BF16)
        v1 = jnp.concatenate(
            [jnp.concatenate([v_ref[0, hl, 0:n_ctx, :], v_ref[0, hl, pl.ds(start, nk), :]], axis=0), ones], axis=1)
        ot = lax.dot_general(v1, p_ref[slot], at_b, preferred_element_type=F32)
        outs.append((ot[:HEAD_DIM] * (1.0 / ot[HEAD_DIM:HEAD_DIM + 1])).T)
    slab = jnp.concatenate(outs, axis=1)
    o_ref[0] = (slab * _silu(g_ref[0].astype(F32))).astype(o_ref.dtype)


def _na_attention(q, k, v, g, bias_left, bias_right, n_ctx):
    nb, nh, rows, hd = q.shape
    grid_rows = (rows - n_ctx) // GRID_W
    nq = NA_QROWS * GRID_W
    assert n_ctx == nq, "context tokens form exactly one query block"
    nkeys = n_ctx + NA_KROWS * GRID_W
    lanes = NA_HSTEP * HEAD_DIM
    qspec = pl.BlockSpec((1, NA_HSTEP, nq, hd), lambda p, b, r: (b, p, r, 0))
    kvspec = pl.BlockSpec((1, NA_HSTEP, rows, hd), lambda p, b, r: (b, p, 0, 0))
    rowspec = pl.BlockSpec((1, nq, lanes), lambda p, b, r: (b, r, p))
    bspec = pl.BlockSpec((1,) + bias_left.shape[1:], lambda p, b, r: (p, 0, 0, 0, 0))
    body = functools.partial(_na_body, n_ctx=n_ctx, grid_rows=grid_rows)
    return pl.pallas_call(
        body,
        grid=(nh // NA_HSTEP, nb, rows // nq),
        in_specs=[qspec, kvspec, kvspec, rowspec, bspec, bspec],
        out_specs=rowspec,
        out_shape=jax.ShapeDtypeStruct(g.shape, BF16),
        scratch_shapes=[pltpu.VMEM((2, nkeys, nq), F32), pltpu.VMEM((2, nkeys, nq), BF16)],
        compiler_params=_cparams(("parallel", "parallel", "parallel")),
        name="na_attention",
    )(q, k, v, g, bias_left, bias_right)


def _na_layer(h, mod_b, gn, w_in, rpb, w_out, n_ctx):
    nb, rows, _ = h.shape
    tm = ROW_TILE
    layout = _bmajor(nb, rows, n_ctx, tm)
    _, row, _ = layout
    wd = NA_HEADS * HEAD_DIM
    head_shape = jax.ShapeDtypeStruct((nb, NA_HEADS, rows, HEAD_DIM), BF16)
    head_spec = pl.BlockSpec((1, NA_HEADS, tm, HEAD_DIM), lambda b, i: (b, 0, i, 0))
    q, k, v, g = _in_proj((h,), [row(D)], _load_block, mod_b, gn, w_in.astype(BF16), layout,
                          [head_shape] * 3 + [jax.ShapeDtypeStruct((nb, rows, wd), BF16)],
                          [head_spec] * 3 + [row(wd)], _na_epilogue, name="na_in_proj")
    bias_left, bias_right = _na_bias_tiles(rpb)
    z = _na_attention(q, k, v, g, bias_left, bias_right, n_ctx)
    return _out_proj(z, w_out.astype(BF16), h, mod_b, layout, name="na_out_proj")


def _s5_epilogue(j, acc, lead, extra, outs):
    u_ref, g_ref = outs
    lo = j * acc.shape[1]
    if lo < S5_WIDTH:
        _put(u_ref, lo, acc, lead)
    else:
        _put(g_ref, lo - S5_WIDTH, acc, lead)


def _s5_matrices(a_re, a_im, log_dt, b_re, b_im, c_re, c_im, d_skip):
    L = S5_CHUNK
    lam = lax.complex(a_re.astype(F32), a_im.astype(F32))
    lam_dt = lam * jnp.exp(log_dt.astype(F32))[..., None]
    lam_bar = jnp.exp(lam_dt)
    b_bar = ((lam_bar - 1.0) / lam)[..., None] * lax.complex(b_re.astype(F32), b_im.astype(F32))
    c_mat = lax.complex(c_re.astype(F32), c_im.astype(F32))
    tau = np.arange(L)

    def power(expo, d):
        return jnp.exp(jnp.asarray(expo, F32).reshape(expo.shape + (1, 1)) * lam_dt[d])

    w = jnp.stack([c_mat[d][None] * power(tau, d)[:, :, None, :] for d in range(2)])
    w2 = jnp.concatenate([jnp.real(w), -jnp.imag(w)], axis=-1)
    b2 = jnp.concatenate([jnp.real(b_bar), jnp.imag(b_bar)], axis=-2)
    kern = jnp.einsum('dkgip,dgpj->dkgij', w2, b2, precision=lax.Precision.HIGH).reshape(2 * L, S5_GROUPS,
                                                                                         S5_GROUP, S5_GROUP)
    lag = tau[None, :] - tau[:, None]
    sel = np.concatenate([lag[None] == tau[:, None, None], -lag[None] == tau[:, None, None]]).astype(np.float32)
    both = jnp.einsum('ktu,kgij->tugij', jnp.asarray(sel), kern, precision=lax.Precision.HIGHEST)
    eye_t = jnp.eye(L, dtype=F32)[:, :, None, None, None]
    skip = jnp.eye(S5_GROUP, dtype=F32)[None] * d_skip.astype(F32).reshape(S5_GROUPS, S5_GROUP, 1)
    both = both + eye_t * skip[None, None]
    toep = both.transpose(2, 0, 4, 1, 3).reshape(S5_GROUPS, L * S5_GROUP, L * S5_GROUP)

    def lanes(z):
        pad = [(0, 0)] * (z.ndim - 1) + [(0, LANES - S5_STATE)]
        return jnp.concatenate([jnp.pad(jnp.real(z), pad), jnp.pad(jnp.imag(z), pad)], axis=-1)

    pf = power(L - 1 - tau, 0)[:, :, :, None] * b_bar[0][None]
    pb = power(tau, 1)[:, :, :, None] * b_bar[1][None]
    pmat = jnp.concatenate([lanes(pf.transpose(1, 0, 3, 2)), lanes(pb.transpose(1, 0, 3, 2))], axis=-1)
    pmat = pmat.reshape(S5_GROUPS, L * S5_GROUP, 4 * LANES)
    qf = c_mat[0][None] * power(tau + 1, 0)[:, :, None, :]
    qb = c_mat[1][None] * power(L - tau, 1)[:, :, None, :]

    def state_rows(z):
        pad = ((0, 0), (0, LANES - S5_STATE), (0, 0), (0, 0))
        zr = jnp.pad(jnp.real(z).transpose(1, 3, 0, 2), pad)
        zi = jnp.pad(-jnp.imag(z).transpose(1, 3, 0, 2), pad)
        return jnp.concatenate([zr, zi], axis=1).reshape(S5_GROUPS, 2 * LANES, L * S5_GROUP)

    qmat = jnp.concatenate([state_rows(qf), state_rows(qb)], axis=1)
    lam_l = jnp.exp(float(L) * lam_dt)
    pad = ((0, 0), (0, LANES - S5_STATE))
    coef = jnp.stack([jnp.pad(jnp.real(lam_l[0]), pad), jnp.pad(jnp.imag(lam_l[0]), pad),
                      jnp.pad(jnp.real(lam_l[1]), pad), jnp.pad(jnp.imag(lam_l[1]), pad)], axis=1)
    coef = jnp.pad(coef, ((0, 0), (0, 4), (0, 0)))
    return toep.astype(BF16), pmat.astype(BF16), qmat.astype(BF16), coef


def _atom_transpose(xs, atom):
    n = len(xs)
    group = lax.broadcasted_iota(jnp.int32, xs[0].shape, 1) // atom
    xs = list(xs)
    d = n // 2
    while d:
        keep = (group & d) == 0
        for i in range(n):
            if not i & d:
                lo, hi = xs[i], xs[i + d]
                xs[i] = jnp.where(keep, lo, pltpu.roll(hi, d * atom, 1))
                xs[i + d] = jnp.where(keep, pltpu.roll(lo, LANES - d * atom, 1), hi)
        d //= 2
    return xs


def _s5_body(u_ref, toep_ref, p_ref, q_ref, coef_ref, y_ref, u_scr, y_scr, s_ref, hin_ref, *, nb, nc_ctx, nc):
    nwin = LANES // S5_GROUP
    nrow, nlat = u_ref.shape[1], y_ref.shape[1]
    blk = S5_ROWS
    for half in range(S5_CHUNK // nwin):
        cols = slice(half * LANES, (half + 1) * LANES)
        for r in range(0, nrow, blk):
            words = [pltpu.bitcast(u_ref[half * nwin + w, r:r + blk, :], jnp.uint32) for w in range(nwin)]
            for gl, x in enumerate(_atom_transpose(words, S5_GROUP)):
                u_scr[gl, r:r + blk, cols] = pltpu.bitcast(x, BF16)
    for gl in range(S5_SLAB):
        _s5_group(gl, u_scr, toep_ref, p_ref, q_ref, coef_ref, y_scr, s_ref, hin_ref, nb, nc_ctx, nc)
    for half in range(S5_CHUNK // nwin):
        cols = slice(half * LANES, (half + 1) * LANES)
        for r in range(0, nlat, blk):
            ys = [y_scr[gl, r:r + blk, cols] for gl in range(S5_SLAB)]
            for w, x in enumerate(_atom_transpose(ys, S5_GROUP)):
                y_ref[half * nwin + w, r:r + blk, :] = x


def _s5_group(gl, u_scr, toep_ref, p_ref, q_ref, coef_ref, y_scr, s_ref, hin_ref, nb, nc_ctx, nc):
    u = u_scr[gl]
    s_ref[...] = jnp.dot(u, p_ref[gl], preferred_element_type=F32)
    coef = coef_ref[gl]
    shape = (nb, LANES)
    lr_f, li_f = jnp.broadcast_to(coef[0:1], shape), jnp.broadcast_to(coef[1:2], shape)
    lr_b, li_b = jnp.broadcast_to(coef[2:3], shape), jnp.broadcast_to(coef[3:4], shape)
    zero = jnp.zeros(shape, F32)

    def step(q, carry):
        fr, fi, br, bi = carry
        rows_f = pl.ds(pl.multiple_of(q * nb, nb), nb)
        cb = jnp.where(q < nc_ctx, nc_ctx - 1 - q, nc + nc_ctx - 1 - q)
        rows_b = pl.ds(pl.multiple_of(cb * nb, nb), nb)
        hin_ref[rows_f, 0:LANES] = fr
        hin_ref[rows_f, LANES:2 * LANES] = fi
        hin_ref[rows_b, 2 * LANES:3 * LANES] = br
        hin_ref[rows_b, 3 * LANES:4 * LANES] = bi
        sfr, sfi = s_ref[rows_f, 0:LANES], s_ref[rows_f, LANES:2 * LANES]
        sbr, sbi = s_ref[rows_b, 2 * LANES:3 * LANES], s_ref[rows_b, 3 * LANES:4 * LANES]
        return (lr_f * fr - li_f * fi + sfr, lr_f * fi + li_f * fr + sfi,
                lr_b * br - li_b * bi + sbr, lr_b * bi + li_b * br + sbi)

    lax.fori_loop(0, nc, step, (zero, zero, zero, zero), unroll=2)

    lat = slice(nc_ctx * nb, nc * nb)
    y_scr[gl] = (jnp.dot(u[lat], toep_ref[gl], preferred_element_type=F32)
                 + jnp.dot(hin_ref[lat, :].astype(BF16), q_ref[gl], preferred_element_type=F32))


def _s5_scan(u_t, toep, pmat, qmat, coef, nb, n_ctx):
    L, nrow, wd = u_t.shape
    pk = L * S5_GROUP
    nc = nrow // nb
    nc_ctx = n_ctx // L
    nlat = (nc - nc_ctx) * nb
    assert nrow % S5_ROWS == 0 and nlat % S5_ROWS == 0
    once = pl.Buffered(1)
    slab = lambda shape: pl.BlockSpec((S5_SLAB,) + shape, lambda s: (s, 0, 0), pipeline_mode=once)
    body = functools.partial(_s5_body, nb=nb, nc_ctx=nc_ctx, nc=nc)
    return pl.pallas_call(
        body,
        grid=(wd // LANES,),
        in_specs=[pl.BlockSpec((L, nrow, LANES), lambda s: (0, 0, s)),
                  slab((pk, pk)), slab((pk, 4 * LANES)), slab((4 * LANES, pk)), slab((8, LANES))],
        out_specs=pl.BlockSpec((L, nlat, LANES), lambda s: (0, 0, s)),
        out_shape=jax.ShapeDtypeStruct((L, nlat, wd), F32),
        scratch_shapes=[pltpu.VMEM((S5_SLAB, nrow, pk), BF16), pltpu.VMEM((S5_SLAB, nlat, pk), F32),
                        pltpu.VMEM((nrow, 4 * LANES), F32), pltpu.VMEM((nrow, 4 * LANES), F32)],
        compiler_params=_cparams(("parallel",)),
        name="s5_scan",
    )(u_t, toep, pmat, qmat, coef)


def _rms(x, g):
    return x * lax.rsqrt(jnp.mean(x * x, axis=-1, keepdims=True) + EPS) * g


def _s5_readout_body(y_ref, g_ref, gw_ref, gb_ref, w_ref, h_ref, mod_ref, nf_ref, o_ref):
    y = jax.nn.gelu(y_ref[0])
    t = y * _sigmoid(jnp.dot(y.astype(BF16), gw_ref[...], preferred_element_type=F32) + gb_ref[...])
    z = (t * _silu(g_ref[0].astype(F32))).astype(BF16)
    out = jnp.dot(z, w_ref[...], preferred_element_type=F32)
    h = h_ref[0] + mod_ref[0, :, 2 * D:3 * D] * out
    o_ref[0] = _rms(h, nf_ref[...])


def _s5_readout(y, g, glu_w, glu_b, w_out, h, mod_b, norm_f, n_ctx):
    nb, t_len, _ = y.shape
    tm = n_ctx
    off = n_ctx // tm
    lat = lambda wd: pl.BlockSpec((1, tm, wd), lambda b, i: (b, i, 0))
    full = lambda wd: pl.BlockSpec((1, tm, wd), lambda b, i: (b, i + off, 0))
    const = lambda shape: pl.BlockSpec(shape, lambda b, i: (0, 0))
    return pl.pallas_call(
        _s5_readout_body,
        grid=(nb, t_len // tm),
        in_specs=[lat(S5_WIDTH), full(S5_WIDTH), const((S5_WIDTH, S5_WIDTH)), const((1, S5_WIDTH)),
                  const((S5_WIDTH, D)), full(D),
                  pl.BlockSpec((1, 1, 3 * D), lambda b, i: (b, 0, 0)), const((1, D))],
        out_specs=lat(D),
        out_shape=jax.ShapeDtypeStruct((nb, t_len, D), F32),
        compiler_params=_cparams(("parallel", "parallel")),
        name="s5_readout",
    )(y, g, glu_w.astype(BF16), glu_b.reshape(1, S5_WIDTH), w_out.astype(BF16), h, mod_b,
      norm_f.reshape(1, D))


def _s5_layer(h, mod_b, mod_lat, gn, w_in, a_re, a_im, log_dt, b_re, b_im, c_re, c_im, d_skip, glu_w, glu_b,
              w_out, norm_f, n_ctx):
    nb, rows, _ = h.shape
    L = S5_CHUNK
    layout = _bmajor(nb, rows, n_ctx, ROW_TILE)
    _, row, _ = layout
    out_shapes, out_specs = _rows_out((nb, rows), row, [(S5_WIDTH, BF16)] * 2)
    u, g = _in_proj((h,), [row(D)], _load_block, mod_b, gn, w_in.astype(BF16), layout, out_shapes, out_specs,
                    _s5_epilogue, name="s5_in_proj")
    nc = rows // L
    u_t = u.reshape(nb, nc, L, S5_WIDTH).transpose(2, 1, 0, 3).reshape(L, nc * nb, S5_WIDTH)
    toep, pmat, qmat, coef = _s5_matrices(a_re, a_im, log_dt, b_re, b_im, c_re, c_im, d_skip)
    y_t = _s5_scan(u_t, toep, pmat, qmat, coef, nb, n_ctx)
    t_len = rows - n_ctx
    y = y_t.reshape(L, t_len // L, nb, S5_WIDTH).transpose(2, 1, 0, 3).reshape(nb, t_len, S5_WIDTH)
    return _s5_readout(y, g, glu_w, glu_b, w_out, h, mod_lat, norm_f, n_ctx)


def kernel(x, c, ctx, c_ctx, ada_w0, ada_b0, norm0, w_in0, conv_w0, conv_b0, lru_wa0, lru_ba0, lru_wx0, lru_bx0, lru_lam0, w_out0, ada_w1, ada_b1, norm1, w_in1, sink1, w_out1, ada_w2, ada_b2, norm2, w_in2, rpb2, w_out2, ada_w3, ada_b3, norm3, w_in3, s5_a_re3, s5_a_im3, s5_log_dt3, s5_b_re3, s5_b_im3, s5_c_re3, s5_c_im3, s5_d3, glu_w3, glu_b3, w_out3, norm_f):
    nb, t_len, _ = x.shape
    n_ctx = ctx.shape[1]
    cvec = jnp.concatenate([c, c_ctx[None, :], jnp.zeros((16 - nb - 1, D), F32)], axis=0)
    mods = [_modulation(cvec, w, b) for w, b in
            ((ada_w0, ada_b0), (ada_w1, ada_b1), (ada_w2, ada_b2), (ada_w3, ada_b3))]
    mod_b = [_bmajor_mod_table(m, nb, n_ctx + t_len, n_ctx) for m in mods]

    mod_t0 = jnp.stack([jnp.broadcast_to(mods[0][nb], (nb, 3 * D)), mods[0][:nb]], axis=0)
    h = _rglru_layer(ctx, x, mod_t0, norm0, w_in0, conv_w0, conv_b0, lru_wa0, lru_ba0, lru_wx0, lru_bx0,
                     lru_lam0, w_out0)
    h = _swa_layer(h, mod_b[1], norm1, w_in1, sink1, w_out1, n_ctx)
    h = _na_layer(h, mod_b[2], norm2, w_in2, rpb2, w_out2, n_ctx)
    return _s5_layer(h, mod_b[3], mods[3][:nb, None, :], norm3, w_in3, s5_a_re3, s5_a_im3, s5_log_dt3,
                     s5_b_re3, s5_b_im3, s5_c_re3, s5_c_im3, s5_d3, glu_w3, glu_b3, w_out3, norm_f, n_ctx)
```

```python
import functools
import math

import jax
import jax.numpy as jnp
import numpy as np
from jax import lax
from jax.experimental import pallas as pl
from jax.experimental.pallas import tpu as pltpu

F32 = jnp.float32
BF16 = jnp.bfloat16

D = 1024
HEAD_DIM = 64
GRID_W = 64
EPS = 1e-6
NEG_INF = -1e30
ROPE_BASE = 10000.0
LANES = 128
LOG2E = math.log2(math.e)
QK_SCALE = HEAD_DIM ** -0.5 * LOG2E
ONES_ROWS = 16

LRU_WIDTH = 1408
LRU_BLOCKS = 16
LRU_BW = LRU_WIDTH // LRU_BLOCKS
LRU_WIN = 3
CONV_W = 4
LRU_C = 8.0
LRU_TT = 64

SWA_HEADS = 16
SWA_KV_HEADS = 4
WINDOW = 128
BLOCK_Q = 128

NA_HEADS = 16
NA_ROWS = 8
NA_COLS = 16
NA_QROWS = 4
NA_KROWS = 12
NA_HSTEP = 8
NA_SLOTS = 4
NA_MASKED = 2 * NA_ROWS - 1

S5_WIDTH = 1024
S5_GROUP = 16
S5_GROUPS = S5_WIDTH // S5_GROUP
S5_STATE = 64
S5_CHUNK = 16
S5_SLAB = LANES // S5_GROUP
S5_ROWS = 128

ROW_TILE = 768
VMEM_LIMIT = 56 * 1024 * 1024


def _cparams(sem):
    return pltpu.CompilerParams(dimension_semantics=sem, vmem_limit_bytes=VMEM_LIMIT)


def _sigmoid(v):
    return 0.5 * jnp.tanh(0.5 * v) + 0.5


def _silu(v):
    half = 0.5 * v
    return half * (jnp.tanh(half) + 1.0)


def _bmajor(nb, rows, n_ctx, tm):
    grid = (nb, rows // tm)
    per_batch = rows // tm

    def row(width, off=0):
        return pl.BlockSpec((1, tm, width), lambda b, i: (b, i + off, 0))

    mod = pl.BlockSpec((tm // n_ctx, 1, 3 * D), lambda b, i: (b * per_batch + i, 0, 0))
    return grid, row, mod


def _bmajor_mod_table(mod16, nb, rows, n_ctx):
    groups = rows // n_ctx
    tab = jnp.broadcast_to(mod16[:nb, None, :], (nb, groups, 3 * D))
    tab = tab.at[:, 0, :].set(mod16[nb])
    return tab.reshape(nb * groups, 1, 3 * D)


def _tmajor(nb, rows, n_ctx, tt):
    grid = (1, rows // tt)
    nct = n_ctx // tt

    def row(width, off=0):
        return pl.BlockSpec((tt, nb, width), lambda b, i: (i + off, 0, 0))

    mod = pl.BlockSpec((1, nb, 3 * D), lambda b, i: (jnp.where(i < nct, 0, 1), 0, 0))
    return grid, row, mod


def _mod_body(c_ref, w_ref, b_ref, o_ref):
    s = _silu(c_ref[...]).astype(BF16)
    o_ref[...] = jnp.dot(s, w_ref[...].astype(BF16), preferred_element_type=F32) + b_ref[...]


def _modulation(cvec, ada_w, ada_b):
    n = ada_w.shape[1]
    tn = 1024
    return pl.pallas_call(
        _mod_body,
        grid=(n // tn,),
        in_specs=[pl.BlockSpec((16, D), lambda j: (0, 0)),
                  pl.BlockSpec((D, tn), lambda j: (0, j)),
                  pl.BlockSpec((1, tn), lambda j: (0, j))],
        out_specs=pl.BlockSpec((16, tn), lambda j: (0, j)),
        out_shape=jax.ShapeDtypeStruct((16, n), F32),
        compiler_params=_cparams(("parallel",)),
        name="modulation",
    )(cvec, ada_w, ada_b.reshape(1, n))


def _by_mod_group(x3, mod3):
    groups = mod3.shape[0]
    return x3 if groups == 1 else x3.reshape(groups, x3.shape[1] // groups, x3.shape[2])


def _norm_mod(x3, mod3, gn):
    xg = _by_mod_group(x3, mod3)
    ms = jnp.mean(xg * xg, axis=-1, keepdims=True)
    y = xg * lax.rsqrt(ms + EPS) * gn
    return (y * (1.0 + mod3[:, :, D:2 * D]) + mod3[:, :, 0:D]).reshape(x3.shape)


def _in_proj_body(*refs, load, n_x, epilogue, n_extra, chunk):
    xs, (mod_ref, gn_ref, w_ref), rest = refs[:n_x], refs[n_x:n_x + 3], refs[n_x + 3:]
    extra, outs = rest[:n_extra], rest[n_extra:]
    x3 = load(*xs)
    n3 = _norm_mod(x3, mod_ref[...], gn_ref[...])
    lead = x3.shape[:2]
    n = n3.reshape(lead[0] * lead[1], D).astype(BF16)
    for j in range(w_ref.shape[1] // chunk):
        acc = jnp.dot(n, w_ref[:, j * chunk:(j + 1) * chunk], preferred_element_type=F32)
        epilogue(j, acc, lead, extra, outs)


def _load_block(x_ref):
    return x_ref[...]


def _in_proj(xs, x_specs, load, mod, gn, w, layout, out_shapes, out_specs, epilogue, extras=(),
             extra_specs=(), chunk=512, name="in_proj"):
    grid, _, mod_spec = layout
    n = w.shape[1]
    body = functools.partial(_in_proj_body, load=load, n_x=len(xs), epilogue=epilogue,
                             n_extra=len(extras), chunk=chunk)
    return pl.pallas_call(
        body,
        grid=grid,
        in_specs=[*x_specs, mod_spec,
                  pl.BlockSpec((1, D), lambda b, i: (0, 0)),
                  pl.BlockSpec((D, n), lambda b, i: (0, 0)),
                  *extra_specs],
        out_specs=out_specs,
        out_shape=out_shapes,
        compiler_params=_cparams(("parallel", "parallel")),
        name=name,
    )(*xs, mod, gn.reshape(1, D), w, *extras)


def _rows_out(shape2, row, outs):
    return ([jax.ShapeDtypeStruct(shape2 + (wd,), dt) for wd, dt in outs], [row(wd) for wd, _ in outs])


def _put(ref, lo, val, lead):
    ref[:, :, lo:lo + val.shape[1]] = val.reshape(lead + (val.shape[1],)).astype(ref.dtype)


def _out_proj_body(*refs, load, n_h, store):
    (z_ref, w_ref), hs, (mod_ref, o_ref) = refs[:2], refs[2:2 + n_h], refs[2 + n_h:]
    z3 = z_ref[...]
    a, b, wd = z3.shape
    y = jnp.dot(z3.reshape(a * b, wd), w_ref[...], preferred_element_type=F32).reshape(a, b, D)
    mod3 = mod_ref[...]
    h = _by_mod_group(load(*hs), mod3) + mod3[:, :, 2 * D:3 * D] * _by_mod_group(y, mod3)
    store(o_ref, h.reshape(a, b, D))


def _store_block(o_ref, val):
    o_ref[...] = val


def _out_proj(z, w, h, mod, layout, name):
    grid, row, mod_spec = layout
    wd = w.shape[0]
    body = functools.partial(_out_proj_body, load=_load_block, n_h=1, store=_store_block)
    return pl.pallas_call(
        body,
        grid=grid,
        in_specs=[row(wd), pl.BlockSpec((wd, D), lambda b, i: (0, 0)), row(D), mod_spec],
        out_specs=row(D),
        out_shape=jax.ShapeDtypeStruct(h.shape, F32),
        input_output_aliases={2: 0},
        compiler_params=_cparams(("parallel", "parallel")),
        name=name,
    )(z, w, h, mod)


def _lru_epilogue(j, acc, lead, extra, outs):
    _put(outs[j], 0, acc, lead)


def _lru_tile_index(s, nct, ntiles, reverse):
    if not reverse:
        return s
    return jnp.where(s < nct, nct - 1 - s, ntiles + nct - 1 - s)


def _lru_sweep_body(up_ref, uc_ref, un_ref, cw_ref, cb_ref, wg_ref, bg_ref, lam_ref, *rest,
                    nct, ntiles, reverse):
    if reverse:
        hf_ref, g_ref, o_ref, ext_ref, cv_ref, a_ref, b_ref, carry_ref = rest
    else:
        o_ref, ext_ref, cv_ref, a_ref, b_ref, carry_ref = rest
    s = pl.program_id(0)
    ti = _lru_tile_index(s, nct, ntiles, reverse)
    tt, nb, pw = uc_ref.shape

    @pl.when(s == 0)
    def _():
        carry_ref[...] = jnp.zeros_like(carry_ref)

    seg_start = (ti == 0) | (ti == nct)
    seg_end = (ti == nct - 1) | (ti == ntiles - 1)
    ext_ref[0:2] = jnp.where(seg_start, 0.0, up_ref[...])
    ext_ref[2:2 + tt] = uc_ref[...]
    ext_ref[2 + tt:3 + tt] = jnp.where(seg_end, 0.0, un_ref[...])

    rate = (-0.5 * LRU_C * LOG2E) * jax.nn.softplus(-lam_ref[...])
    tiles = [slice(k * LANES, (k + 1) * LANES) for k in range(pw // LANES)]
    for sl in tiles:
        u = cb_ref[:, sl]
        for tap in range(CONV_W):
            u = u + cw_ref[tap:tap + 1, sl] * ext_ref[tap:tap + tt, :, sl]
        cv_ref[:, sl] = u.reshape(tt * nb, LANES)
    for k, sl in enumerate(tiles):
        win = slice(_lru_window(k) * LANES, (_lru_window(k) + LRU_WIN) * LANES)
        pre = jnp.dot(cv_ref[:, win].astype(BF16), wg_ref[k], preferred_element_type=F32) + bg_ref[k]
        tr = jnp.tanh(pre[:, :LANES]) + 1.0
        ti = jnp.tanh(pre[:, LANES:]) + 1.0
        a = jnp.exp2(rate[:, sl] * tr)
        a_ref[:, sl] = a
        b_ref[:, sl] = (0.5 * jnp.sqrt(1.0 - a * a)) * (ti * cv_ref[:, sl])

    def step(q, h):
        t = tt - 1 - q if reverse else q
        rows = pl.ds(pl.multiple_of(t * nb, nb), nb)
        h = a_ref[rows, :] * h + b_ref[rows, :]
        if reverse:
            o_ref[t] = ((hf_ref[t] + h) * _silu(g_ref[t].astype(F32))).astype(o_ref.dtype)
        else:
            o_ref[t] = h
        return h

    carry_ref[...] = lax.fori_loop(0, tt, step, carry_ref[...], unroll=4)


def _lru_sweep(u, cw, cb, wg, bg, lam, nct_rows, reverse, hf=None, g=None):
    rows, nb, pw = u.shape
    tt = LRU_TT
    ntiles = rows // tt
    nct = nct_rows // tt
    tile = functools.partial(_lru_tile_index, nct=nct, ntiles=ntiles, reverse=reverse)
    cur = lambda wd: pl.BlockSpec((tt, nb, wd), lambda s: (tile(s), 0, 0))
    const = lambda shape: pl.BlockSpec(shape, lambda s: (0,) * len(shape))
    in_specs = [
        pl.BlockSpec((2, nb, pw), lambda s: (jnp.maximum(tile(s) * (tt // 2) - 1, 0), 0, 0)),
        cur(pw),
        pl.BlockSpec((1, nb, pw), lambda s: (jnp.minimum((tile(s) + 1) * tt, rows - 1), 0, 0)),
        const((CONV_W, pw)), const((1, pw)),
        const(wg.shape), const(bg.shape), const((1, pw)),
    ]
    args = [u, u, u, cw, cb, wg, bg, lam]
    if reverse:
        in_specs += [cur(pw), cur(pw)]
        args += [hf, g]
    body = functools.partial(_lru_sweep_body, nct=nct, ntiles=ntiles, reverse=reverse)
    return pl.pallas_call(
        body,
        grid=(ntiles,),
        in_specs=in_specs,
        out_specs=cur(pw),
        out_shape=jax.ShapeDtypeStruct(u.shape, BF16 if reverse else F32),
        scratch_shapes=[pltpu.VMEM((tt + 3, nb, pw), F32),
                        pltpu.VMEM((tt * nb, pw), F32),
                        pltpu.VMEM((tt * nb, pw), F32),
                        pltpu.VMEM((tt * nb, pw), F32),
                        pltpu.VMEM((nb, pw), F32)],
        compiler_params=_cparams(("arbitrary",)),
        name="lru_bwd" if reverse else "lru_fwd",
    )(*args)


def _lru_window(k):
    first_block = (k * LANES) // LRU_BW
    return min((first_block * LRU_BW) // LANES, LRU_WIDTH // LANES - LRU_WIN)


def _lru_gate_weights(wa, wx, ba, bx):
    eye = 0.5 * jnp.eye(LRU_BLOCKS, dtype=F32)

    def dense(w):
        return (w[:, :, None, :] * eye[:, None, :, None]).reshape(LRU_WIDTH, LRU_WIDTH)

    da, dx = dense(wa), dense(wx)
    wg, bg = [], []
    for k in range(LRU_WIDTH // LANES):
        rows = slice(_lru_window(k) * LANES, (_lru_window(k) + LRU_WIN) * LANES)
        cols = slice(k * LANES, (k + 1) * LANES)
        wg.append(jnp.concatenate([da[rows, cols], dx[rows, cols]], axis=1))
        bg.append(0.5 * jnp.concatenate([ba[cols], bx[cols]])[None])
    return jnp.stack(wg).astype(BF16), jnp.stack(bg)


def _load_time_major(ctx_ref, x_ref, *, nct):
    blk = jnp.where(pl.program_id(1) < nct, ctx_ref[...], x_ref[...])
    return pltpu.einshape("btd->tbd", blk)


def _store_batch_major(o_ref, val):
    o_ref[...] = pltpu.einshape("tbd->btd", val)


def _rglru_layer(ctx, x, mod_t, gn, w_in, conv_w, conv_b, wa, ba, wx, bx, lam, w_out):
    nb, n_ctx, _ = ctx.shape
    rows = n_ctx + x.shape[1]
    tt = LRU_TT
    nct = n_ctx // tt
    layout = _tmajor(nb, rows, n_ctx, tt)
    grid, row, mod_spec = layout
    src_specs = [pl.BlockSpec((nb, tt, D), lambda b, i: (0, jnp.minimum(i, nct - 1), 0)),
                 pl.BlockSpec((nb, tt, D), lambda b, i: (0, jnp.maximum(i - nct, 0), 0))]
    load = functools.partial(_load_time_major, nct=nct)
    wd = LRU_WIDTH
    out_shapes, out_specs = _rows_out((rows, nb), row, [(wd, F32), (wd, BF16)])
    u, g = _in_proj((ctx, x), src_specs, load, mod_t, gn, w_in.astype(BF16), layout, out_shapes, out_specs,
                    _lru_epilogue, chunk=wd, name="lru_in_proj")
    cb = conv_b.reshape(1, wd)
    sweeps = []
    for d in range(2):
        wg, bg = _lru_gate_weights(wa[d], wx[d], ba[d], bx[d])
        sweeps.append((conv_w, cb, wg, bg, lam[d:d + 1]))
    hf = _lru_sweep(u, *sweeps[0], n_ctx, reverse=False)
    z = _lru_sweep(u, *sweeps[1], n_ctx, reverse=True, hf=hf, g=g)
    body = functools.partial(_out_proj_body, load=load, n_h=2, store=_store_batch_major)
    return pl.pallas_call(
        body,
        grid=grid,
        in_specs=[row(wd), pl.BlockSpec((wd, D), lambda b, i: (0, 0)), *src_specs, mod_spec],
        out_specs=pl.BlockSpec((nb, tt, D), lambda b, i: (0, i, 0)),
        out_shape=jax.ShapeDtypeStruct((nb, rows, D), F32),
        compiler_params=_cparams(("parallel", "parallel")),
        name="lru_out_proj",
    )(z, w_out.astype(BF16), ctx, x, mod_t)


def _rope_tables(n_ctx, t_len):
    pos = jnp.arange(t_len)
    row = (pos // GRID_W).astype(F32)
    col = (pos % GRID_W).astype(F32)
    n_ax = HEAD_DIM // 4
    freqs = ROPE_BASE ** (-jnp.arange(n_ax, dtype=F32) / n_ax)
    ang = jnp.concatenate([row[:, None] * freqs, col[:, None] * freqs], axis=-1)
    cos, sin = jnp.cos(ang), jnp.sin(ang)
    cos_h = jnp.concatenate([cos, cos], axis=-1)
    sin_h = jnp.concatenate([-sin, sin], axis=-1)
    cos_f = jnp.concatenate([jnp.ones((n_ctx, HEAD_DIM), F32), cos_h], axis=0)
    sin_f = jnp.concatenate([jnp.zeros((n_ctx, HEAD_DIM), F32), sin_h], axis=0)
    return jnp.tile(cos_f, (1, 2)), jnp.tile(sin_f, (1, 2))


def _rope(x, cos, sin):
    lane = lax.broadcasted_iota(jnp.int32, x.shape, 1)
    swapped = jnp.where(lane % HEAD_DIM < HEAD_DIM // 2,
                        pltpu.roll(x, LANES - HEAD_DIM // 2, 1), pltpu.roll(x, HEAD_DIM // 2, 1))
    return x * cos + swapped * sin


def _swa_epilogue(j, acc, lead, extra, outs):
    cos_ref, sin_ref = extra
    q_ref, qr_ref, k_ref, v_ref, g_ref = outs
    qd = SWA_HEADS * HEAD_DIM
    kvd = SWA_KV_HEADS * HEAD_DIM
    grp = SWA_HEADS // SWA_KV_HEADS
    width = acc.shape[1]
    lo = j * width
    cos, sin = cos_ref[...], sin_ref[...]

    def roped(x):
        return jnp.concatenate([_rope(x[:, s:s + LANES], cos, sin) for s in range(0, x.shape[1], LANES)], axis=1)

    def head(x, hh):
        return x[:, hh * HEAD_DIM:(hh + 1) * HEAD_DIM]

    if lo < qd:
        q = acc * QK_SCALE
        qr = roped(q)
        for hh in range(width // HEAD_DIM):
            kv, gq = divmod(lo // HEAD_DIM + hh, grp)
            for rt in range(lead[1] // BLOCK_Q):
                rows = slice(rt * BLOCK_Q, (rt + 1) * BLOCK_Q)
                dst = slice(gq * BLOCK_Q, (gq + 1) * BLOCK_Q)
                q_ref[0, kv, rt, dst, :] = head(q, hh)[rows].astype(q_ref.dtype)
                qr_ref[0, kv, rt, dst, :] = head(qr, hh)[rows].astype(qr_ref.dtype)
    elif lo == qd:
        kr = roped(acc[:, :kvd])
        for kv in range(SWA_KV_HEADS):
            k_ref[0, kv] = head(kr, kv).astype(k_ref.dtype)
            v_ref[0, kv] = head(acc[:, kvd:], kv).T.astype(v_ref.dtype)
    else:
        _put(g_ref, lo - qd - 2 * kvd, acc, lead)


def _swa_mask_table():
    grp = SWA_HEADS // SWA_KV_HEADS
    kj = np.arange(3 * BLOCK_Q)[:, None]
    qi = np.arange(BLOCK_Q)[None, :]
    base = np.where(np.abs(kj - BLOCK_Q - qi) <= WINDOW, 0.0, NEG_INF).astype(np.float32)
    first = np.where(kj < BLOCK_Q, NEG_INF, base).astype(np.float32)
    last = np.where(kj >= 2 * BLOCK_Q, NEG_INF, base).astype(np.float32)
    tab = np.stack([np.full_like(base, NEG_INF), first, base, last])
    return jnp.asarray(np.tile(tab, (1, 1, grp)))


def _swa_body(sink_ref, mask_ref, q_ref, qr_ref, kc_ref, k0_ref, k1_ref, k2_ref, vc_ref, v0_ref, v1_ref,
              v2_ref, g_ref, o_ref, s_ref, p_ref):
    grp = SWA_HEADS // SWA_KV_HEADS
    dims = (((1,), (1,)), ((), ()))
    nc = kc_ref.shape[2]
    sink_p = {}

    def scores(kv):
        kb = jnp.concatenate([k0_ref[0, kv], k1_ref[0, kv], k2_ref[0, kv]], axis=0)
        s_ref[kv, :nc, :] = lax.dot_general(kc_ref[0, kv], q_ref[0, kv, 0], dims, preferred_element_type=F32)
        s_ref[kv, nc:, :] = (lax.dot_general(kb, qr_ref[0, kv, 0], dims, preferred_element_type=F32)
                             + mask_ref[0])

    def softmax(kv):
        sink_p[kv] = []
        for gq in range(grp):
            lanes = slice(gq * BLOCK_Q, (gq + 1) * BLOCK_Q)
            s = s_ref[kv, :, lanes]
            sink = sink_ref[kv, :, lanes]
            m = jnp.maximum(jnp.max(s, axis=0, keepdims=True), sink)
            sink_p[kv].append(jnp.exp2(sink - m))
            p_ref[kv, :, lanes] = jnp.exp2(s - m).astype(BF16)

    def values(kv):
        vt = jnp.concatenate([vc_ref[0, kv], v0_ref[0, kv], v1_ref[0, kv], v2_ref[0, kv]], axis=1)
        vt = jnp.concatenate([vt, jnp.ones((ONES_ROWS, vt.shape[1]), BF16)], axis=0)
        ot = jnp.dot(vt, p_ref[kv], preferred_element_type=F32)
        denom = ot[HEAD_DIM:HEAD_DIM + 1] + jnp.concatenate(sink_p[kv], axis=1)
        ot = ot[:HEAD_DIM] * (1.0 / denom)
        slab = jnp.concatenate([ot[:, gq * BLOCK_Q:(gq + 1) * BLOCK_Q].T for gq in range(grp)], axis=1)
        lanes = slice(kv * grp * HEAD_DIM, (kv + 1) * grp * HEAD_DIM)
        o_ref[0, :, lanes] = (slab * _silu(g_ref[0, :, lanes].astype(F32))).astype(o_ref.dtype)

    scores(0)
    scores(1)
    for kv in range(SWA_KV_HEADS):
        softmax(kv)
        if kv + 2 < SWA_KV_HEADS:
            scores(kv + 2)
        values(kv)


def _swa_attention(q, qr, k, vt, g, sink, n_ctx):
    nb, nkv, nt, qrows, hd = q.shape
    rows = k.shape[2]
    grp = SWA_HEADS // SWA_KV_HEADS
    nct = n_ctx // BLOCK_Q
    nkeys = n_ctx + 3 * BLOCK_Q
    qspec = pl.BlockSpec((1, nkv, 1, qrows, hd), lambda b, n: (b, 0, n, 0, 0))
    kctx = pl.BlockSpec((1, nkv, n_ctx, hd), lambda b, n: (b, 0, 0, 0))
    vctx = pl.BlockSpec((1, nkv, hd, n_ctx), lambda b, n: (b, 0, 0, 0))
    rowspec = pl.BlockSpec((1, BLOCK_Q, g.shape[2]), lambda b, n: (b, n, 0))

    def kband(off):
        return pl.BlockSpec((1, nkv, BLOCK_Q, hd), lambda b, n: (b, 0, jnp.clip(n + off, nct, nt - 1), 0))

    def vband(off):
        return pl.BlockSpec((1, nkv, hd, BLOCK_Q), lambda b, n: (b, 0, 0, jnp.clip(n + off, nct, nt - 1)))

    def variant(b, n):
        return (jnp.where(n < nct, 0, jnp.where(n == nct, 1, jnp.where(n == nt - 1, 3, 2))), 0, 0)

    sink_rows = jnp.repeat(sink.astype(F32).reshape(nkv, grp) * LOG2E, BLOCK_Q, axis=1).reshape(nkv, 1, qrows)
    return pl.pallas_call(
        _swa_body,
        grid=(nb, nt),
        in_specs=[pl.BlockSpec((nkv, 1, qrows), lambda b, n: (0, 0, 0)),
                  pl.BlockSpec((1, 3 * BLOCK_Q, qrows), variant), qspec, qspec,
                  kctx, kband(-1), kband(0), kband(1), vctx, vband(-1), vband(0), vband(1), rowspec],
        out_specs=rowspec,
        out_shape=jax.ShapeDtypeStruct((nb, rows, g.shape[2]), BF16),
        scratch_shapes=[pltpu.VMEM((nkv, nkeys, qrows), F32), pltpu.VMEM((nkv, nkeys, qrows), BF16)],
        compiler_params=_cparams(("parallel", "parallel")),
        name="swa_attention",
    )(sink_rows, _swa_mask_table(), q, qr, k, k, k, k, vt, vt, vt, vt, g)


def _swa_layer(h, mod_b, gn, w_in, sink, w_out, n_ctx):
    nb, rows, _ = h.shape
    tm = ROW_TILE
    layout = _bmajor(nb, rows, n_ctx, tm)
    _, row, _ = layout
    qd = SWA_HEADS * HEAD_DIM
    grp = SWA_HEADS // SWA_KV_HEADS
    nkv = SWA_KV_HEADS
    cos, sin = _rope_tables(n_ctx, rows - n_ctx)
    tab = pl.BlockSpec((tm, LANES), lambda b, i: (i, 0))
    q_shape = jax.ShapeDtypeStruct((nb, nkv, rows // BLOCK_Q, grp * BLOCK_Q, HEAD_DIM), BF16)
    q_spec = pl.BlockSpec((1, nkv, tm // BLOCK_Q, grp * BLOCK_Q, HEAD_DIM), lambda b, i: (b, 0, i, 0, 0))
    k_shape = jax.ShapeDtypeStruct((nb, nkv, rows, HEAD_DIM), BF16)
    k_spec = pl.BlockSpec((1, nkv, tm, HEAD_DIM), lambda b, i: (b, 0, i, 0))
    vt_shape = jax.ShapeDtypeStruct((nb, nkv, HEAD_DIM, rows), BF16)
    vt_spec = pl.BlockSpec((1, nkv, HEAD_DIM, tm), lambda b, i: (b, 0, 0, i))
    q, qr, k, vt, g = _in_proj(
        (h,), [row(D)], _load_block, mod_b, gn, w_in.astype(BF16), layout,
        [q_shape, q_shape, k_shape, vt_shape, jax.ShapeDtypeStruct((nb, rows, qd), BF16)],
        [q_spec, q_spec, k_spec, vt_spec, row(qd)], _swa_epilogue,
        extras=(cos, sin), extra_specs=(tab, tab), name="swa_in_proj")
    z = _swa_attention(q, qr, k, vt, g, sink, n_ctx)
    return _out_proj(z, w_out.astype(BF16), h, mod_b, layout, name="swa_out_proj")


def _na_epilogue(j, acc, lead, extra, outs):
    q_ref, k_ref, v_ref, g_ref = outs
    wd = NA_HEADS * HEAD_DIM
    width = acc.shape[1]
    lo = j * width
    which, off = lo // wd, lo % wd
    if which == 3:
        _put(g_ref, off, acc, lead)
        return
    ref = (q_ref, k_ref, v_ref)[which]
    val = acc * QK_SCALE if which == 0 else acc
    for hh in range(width // HEAD_DIM):
        ref[0, off // HEAD_DIM + hh] = val[:, hh * HEAD_DIM:(hh + 1) * HEAD_DIM].astype(ref.dtype)


def _na_bias_tiles(rpb):
    reach = GRID_W - NA_COLS
    period = 2 * GRID_W
    ndy = 2 * NA_ROWS - 1
    edge = jnp.pad(rpb.astype(F32) * LOG2E, ((0, 0), (0, 0), (reach, reach)), mode="edge")
    flat = jnp.tile(jnp.pad(edge[:, :, ::-1], ((0, 0), (0, 0), (0, 1))), (1, 1, GRID_W))[:, :, :GRID_W * (period - 1)]
    e = flat.reshape(NA_HEADS, ndy, GRID_W, period - 1)[:, :, :, GRID_W - 1:]
    col = np.arange(GRID_W)
    cstart = np.clip(col - NA_COLS // 2, 0, GRID_W - NA_COLS)
    col_ok = (col[:, None] >= cstart[None, :]) & (col[:, None] < cstart[None, :] + NA_COLS)
    e = jnp.where(jnp.asarray(col_ok)[None, None], e, NEG_INF)
    e = jnp.pad(e, ((0, 0), (0, 1), (0, 0), (0, 0)), constant_values=NEG_INF)
    shape = (NA_HEADS // NA_HSTEP, NA_HSTEP, ndy + 1, GRID_W, period)
    left = jnp.pad(e, ((0, 0), (0, 0), (0, 0), (0, GRID_W)), constant_values=NEG_INF).reshape(shape)
    right = jnp.pad(e, ((0, 0), (0, 0), (0, 0), (GRID_W, 0)), constant_values=NEG_INF).reshape(shape)
    return left, right


def _na_body(q_ref, k_ref, v_ref, g_ref, bl_ref, br_ref, o_ref, s_ref, p_ref, *, n_ctx, grid_rows):
    rb = pl.program_id(2)
    r0 = (rb - 1) * NA_QROWS
    kb = jnp.clip(r0 - NA_ROWS // 2, 0, grid_rows - NA_KROWS)
    start = pl.multiple_of(n_ctx + kb * GRID_W, GRID_W)
    nk = NA_KROWS * GRID_W
    nq = q_ref.shape[2]
    a_bt = (((1,), (1,)), ((), ()))
    at_b = (((0,), (0,)), ((), ()))

    def tile_index(qr, kr):
        qrow, krow = r0 + qr, kb + kr
        first = jnp.clip(qrow - NA_ROWS // 2, 0, grid_rows - NA_ROWS)
        inside = (rb >= 1) & (krow >= first) & (krow < first + NA_ROWS)
        return jnp.where(inside, krow - qrow + NA_ROWS - 1, NA_MASKED)

    idx = [[tile_index(qr, kr) for kr in range(NA_KROWS)] for qr in range(NA_QROWS)]
    left_half = lax.broadcasted_iota(jnp.int32, (GRID_W, 2 * GRID_W), 1) < GRID_W
    ones = jnp.ones((n_ctx + nk, HEAD_DIM), BF16)
    outs = [None] * NA_HSTEP

    def scores(hl):
        slot = hl % NA_SLOTS
        qh = q_ref[0, hl]
        bias = jnp.concatenate(
            [jnp.concatenate([jnp.where(left_half, bl_ref[0, hl, idx[2 * m][kr]], br_ref[0, hl, idx[2 * m + 1][kr]])
                              for m in range(NA_QROWS // 2)], axis=1) for kr in range(NA_KROWS)], axis=0)
        s_ref[slot, :n_ctx, :] = lax.dot_general(k_ref[0, hl, 0:n_ctx, :], qh, a_bt, preferred_element_type=F32)
        s_ref[slot, n_ctx:, :] = (lax.dot_general(k_ref[0, hl, pl.ds(start, nk), :], qh, a_bt,
                                                  preferred_element_type=F32) + bias)

    def softmax(hl):
        slot = hl % NA_SLOTS
        for c in range(nq // LANES):
            lanes = slice(c * LANES, (c + 1) * LANES)
            s = s_ref[slot, :, lanes]
            p_ref[slot, :, lanes] = jnp.exp2(s - jnp.max(s, axis=0, keepdims=True)).astype(BF16)

    def values(hl):
        v1 = jnp.concatenate(
            [jnp.concatenate([v_ref[0, hl, 0:n_ctx, :], v_ref[0, hl, pl.ds(start, nk), :]], axis=0), ones], axis=1)
        ot = lax.dot_general(v1, p_ref[hl % NA_SLOTS], at_b, preferred_element_type=F32)
        outs[hl] = (ot[:HEAD_DIM] * (1.0 / ot[HEAD_DIM:HEAD_DIM + 1])).T

    scores(0)
    scores(1)
    for hl in range(NA_HSTEP):
        softmax(hl)
        if hl + 2 < NA_HSTEP:
            scores(hl + 2)
        values(hl)
    slab = jnp.concatenate(outs, axis=1)
    o_ref[0] = (slab * _silu(g_ref[0].astype(F32))).astype(o_ref.dtype)


def _na_attention(q, k, v, g, bias_left, bias_right, n_ctx):
    nb, nh, rows, hd = q.shape
    grid_rows = (rows - n_ctx) // GRID_W
    nq = NA_QROWS * GRID_W
    assert n_ctx == nq, "context tokens form exactly one query block"
    nkeys = n_ctx + NA_KROWS * GRID_W
    lanes = NA_HSTEP * HEAD_DIM
    qspec = pl.BlockSpec((1, NA_HSTEP, nq, hd), lambda p, b, r: (b, p, r, 0))
    kvspec = pl.BlockSpec((1, NA_HSTEP, rows, hd), lambda p, b, r: (b, p, 0, 0))
    rowspec = pl.BlockSpec((1, nq, lanes), lambda p, b, r: (b, r, p))
    bspec = pl.BlockSpec((1,) + bias_left.shape[1:], lambda p, b, r: (p, 0, 0, 0, 0))
    body = functools.partial(_na_body, n_ctx=n_ctx, grid_rows=grid_rows)
    return pl.pallas_call(
        body,
        grid=(nh // NA_HSTEP, nb, rows // nq),
        in_specs=[qspec, kvspec, kvspec, rowspec, bspec, bspec],
        out_specs=rowspec,
        out_shape=jax.ShapeDtypeStruct(g.shape, BF16),
        scratch_shapes=[pltpu.VMEM((NA_SLOTS, nkeys, nq), F32), pltpu.VMEM((NA_SLOTS, nkeys, nq), BF16)],
        compiler_params=_cparams(("parallel", "parallel", "parallel")),
        name="na_attention",
    )(q, k, v, g, bias_left, bias_right)


def _na_layer(h, mod_b, gn, w_in, rpb, w_out, n_ctx):
    nb, rows, _ = h.shape
    tm = ROW_TILE
    layout = _bmajor(nb, rows, n_ctx, tm)
    _, row, _ = layout
    wd = NA_HEADS * HEAD_DIM
    head_shape = jax.ShapeDtypeStruct((nb, NA_HEADS, rows, HEAD_DIM), BF16)
    head_spec = pl.BlockSpec((1, NA_HEADS, tm, HEAD_DIM), lambda b, i: (b, 0, i, 0))
    q, k, v, g = _in_proj((h,), [row(D)], _load_block, mod_b, gn, w_in.astype(BF16), layout,
                          [head_shape] * 3 + [jax.ShapeDtypeStruct((nb, rows, wd), BF16)],
                          [head_spec] * 3 + [row(wd)], _na_epilogue, name="na_in_proj")
    bias_left, bias_right = _na_bias_tiles(rpb)
    z = _na_attention(q, k, v, g, bias_left, bias_right, n_ctx)
    return _out_proj(z, w_out.astype(BF16), h, mod_b, layout, name="na_out_proj")


def _s5_epilogue(j, acc, lead, extra, outs):
    u_ref, g_ref = outs
    lo = j * acc.shape[1]
    if lo < S5_WIDTH:
        _put(u_ref, lo, acc, lead)
    else:
        _put(g_ref, lo - S5_WIDTH, acc, lead)


def _s5_matrices(a_re, a_im, log_dt, b_re, b_im, c_re, c_im, d_skip):
    L = S5_CHUNK
    lam = lax.complex(a_re.astype(F32), a_im.astype(F32))
    lam_dt = lam * jnp.exp(log_dt.astype(F32))[..., None]
    lam_bar = jnp.exp(lam_dt)
    b_bar = ((lam_bar - 1.0) / lam)[..., None] * lax.complex(b_re.astype(F32), b_im.astype(F32))
    c_mat = lax.complex(c_re.astype(F32), c_im.astype(F32))
    tau = np.arange(L)

    def power(expo, d):
        return jnp.exp(jnp.asarray(expo, F32).reshape(expo.shape + (1, 1)) * lam_dt[d])

    w = jnp.stack([c_mat[d][None] * power(tau, d)[:, :, None, :] for d in range(2)])
    w2 = jnp.concatenate([jnp.real(w), -jnp.imag(w)], axis=-1)
    b2 = jnp.concatenate([jnp.real(b_bar), jnp.imag(b_bar)], axis=-2)
    kern = jnp.einsum('dkgip,dgpj->dkgij', w2, b2).reshape(2 * L, S5_GROUPS, S5_GROUP, S5_GROUP)
    lag = tau[None, :] - tau[:, None]
    sel = np.concatenate([lag[None] == tau[:, None, None], -lag[None] == tau[:, None, None]]).astype(np.float32)
    both = jnp.einsum('ktu,kgij->tugij', jnp.asarray(sel), kern)
    eye_t = jnp.eye(L, dtype=F32)[:, :, None, None, None]
    skip = jnp.eye(S5_GROUP, dtype=F32)[None] * d_skip.astype(F32).reshape(S5_GROUPS, S5_GROUP, 1)
    both = both + eye_t * skip[None, None]
    toep = both.transpose(2, 0, 4, 1, 3).reshape(S5_GROUPS, L * S5_GROUP, L * S5_GROUP)

    def lanes(z):
        pad = [(0, 0)] * (z.ndim - 1) + [(0, LANES - S5_STATE)]
        return jnp.concatenate([jnp.pad(jnp.real(z), pad), jnp.pad(jnp.imag(z), pad)], axis=-1)

    pf = power(L - 1 - tau, 0)[:, :, :, None] * b_bar[0][None]
    pb = power(tau, 1)[:, :, :, None] * b_bar[1][None]
    pmat = jnp.concatenate([lanes(pf.transpose(1, 0, 3, 2)), lanes(pb.transpose(1, 0, 3, 2))], axis=-1)
    pmat = pmat.reshape(S5_GROUPS, L * S5_GROUP, 4 * LANES)
    qf = c_mat[0][None] * power(tau + 1, 0)[:, :, None, :]
    qb = c_mat[1][None] * power(L - tau, 1)[:, :, None, :]

    def state_rows(z):
        pad = ((0, 0), (0, LANES - S5_STATE), (0, 0), (0, 0))
        zr = jnp.pad(jnp.real(z).transpose(1, 3, 0, 2), pad)
        zi = jnp.pad(-jnp.imag(z).transpose(1, 3, 0, 2), pad)
        return jnp.concatenate([zr, zi], axis=1).reshape(S5_GROUPS, 2 * LANES, L * S5_GROUP)

    qmat = jnp.concatenate([state_rows(qf), state_rows(qb)], axis=1)
    lam_l = jnp.exp(float(L) * lam_dt)
    pad = ((0, 0), (0, LANES - S5_STATE))
    coef = jnp.stack([jnp.pad(jnp.real(lam_l[0]), pad), jnp.pad(jnp.imag(lam_l[0]), pad),
                      jnp.pad(jnp.real(lam_l[1]), pad), jnp.pad(jnp.imag(lam_l[1]), pad)], axis=1)
    coef = jnp.pad(coef, ((0, 0), (0, 4), (0, 0)))
    return toep.astype(BF16), pmat.astype(BF16), qmat.astype(BF16), coef


def _atom_transpose(xs, atom):
    n = len(xs)
    group = lax.broadcasted_iota(jnp.int32, xs[0].shape, 1) // atom
    xs = list(xs)
    d = n // 2
    while d:
        keep = (group & d) == 0
        for i in range(n):
            if not i & d:
                lo, hi = xs[i], xs[i + d]
                xs[i] = jnp.where(keep, lo, pltpu.roll(hi, d * atom, 1))
                xs[i + d] = jnp.where(keep, pltpu.roll(lo, LANES - d * atom, 1), hi)
        d //= 2
    return xs


def _s5_body(u_ref, toep_ref, p_ref, q_ref, coef_ref, y_ref, u_scr, y_scr, s_ref, hin_ref, *, nb, nc_ctx, nc):
    nwin = LANES // S5_GROUP
    nrow, nlat = u_ref.shape[1], y_ref.shape[1]
    blk = S5_ROWS
    for half in range(S5_CHUNK // nwin):
        cols = slice(half * LANES, (half + 1) * LANES)
        for r in range(0, nrow, blk):
            words = [pltpu.bitcast(u_ref[half * nwin + w, r:r + blk, :], jnp.uint32) for w in range(nwin)]
            for gl, x in enumerate(_atom_transpose(words, S5_GROUP)):
                u_scr[gl, r:r + blk, cols] = pltpu.bitcast(x, BF16)
    for gl in range(S5_SLAB):
        _s5_group(gl, u_scr, toep_ref, p_ref, q_ref, coef_ref, y_scr, s_ref, hin_ref, nb, nc_ctx, nc)
    for half in range(S5_CHUNK // nwin):
        cols = slice(half * LANES, (half + 1) * LANES)
        for r in range(0, nlat, blk):
            ys = [y_scr[gl, r:r + blk, cols] for gl in range(S5_SLAB)]
            for w, x in enumerate(_atom_transpose(ys, S5_GROUP)):
                y_ref[half * nwin + w, r:r + blk, :] = x


def _s5_group(gl, u_scr, toep_ref, p_ref, q_ref, coef_ref, y_scr, s_ref, hin_ref, nb, nc_ctx, nc):
    u = u_scr[gl]
    s_ref[...] = jnp.dot(u, p_ref[gl], preferred_element_type=F32)
    coef = coef_ref[gl]
    shape = (nb, LANES)
    lr_f, li_f = jnp.broadcast_to(coef[0:1], shape), jnp.broadcast_to(coef[1:2], shape)
    lr_b, li_b = jnp.broadcast_to(coef[2:3], shape), jnp.broadcast_to(coef[3:4], shape)
    zero = jnp.zeros(shape, F32)

    def step(q, carry):
        fr, fi, br, bi = carry
        rows_f = pl.ds(pl.multiple_of(q * nb, nb), nb)
        cb = jnp.where(q < nc_ctx, nc_ctx - 1 - q, nc + nc_ctx - 1 - q)
        rows_b = pl.ds(pl.multiple_of(cb * nb, nb), nb)
        hin_ref[rows_f, 0:LANES] = fr
        hin_ref[rows_f, LANES:2 * LANES] = fi
        hin_ref[rows_b, 2 * LANES:3 * LANES] = br
        hin_ref[rows_b, 3 * LANES:4 * LANES] = bi
        sfr, sfi = s_ref[rows_f, 0:LANES], s_ref[rows_f, LANES:2 * LANES]
        sbr, sbi = s_ref[rows_b, 2 * LANES:3 * LANES], s_ref[rows_b, 3 * LANES:4 * LANES]
        return (lr_f * fr - li_f * fi + sfr, lr_f * fi + li_f * fr + sfi,
                lr_b * br - li_b * bi + sbr, lr_b * bi + li_b * br + sbi)

    lax.fori_loop(0, nc, step, (zero, zero, zero, zero), unroll=2)

    lat = slice(nc_ctx * nb, nc * nb)
    y_scr[gl] = (jnp.dot(u[lat], toep_ref[gl], preferred_element_type=F32)
                 + jnp.dot(hin_ref[lat, :].astype(BF16), q_ref[gl], preferred_element_type=F32))


def _s5_scan(u_t, toep, pmat, qmat, coef, nb, n_ctx):
    L, nrow, wd = u_t.shape
    pk = L * S5_GROUP
    nc = nrow // nb
    nc_ctx = n_ctx // L
    nlat = (nc - nc_ctx) * nb
    assert nrow % S5_ROWS == 0 and nlat % S5_ROWS == 0
    once = pl.Buffered(1)
    slab = lambda shape: pl.BlockSpec((S5_SLAB,) + shape, lambda s: (s, 0, 0), pipeline_mode=once)
    body = functools.partial(_s5_body, nb=nb, nc_ctx=nc_ctx, nc=nc)
    return pl.pallas_call(
        body,
        grid=(wd // LANES,),
        in_specs=[pl.BlockSpec((L, nrow, LANES), lambda s: (0, 0, s)),
                  slab((pk, pk)), slab((pk, 4 * LANES)), slab((4 * LANES, pk)), slab((8, LANES))],
        out_specs=pl.BlockSpec((L, nlat, LANES), lambda s: (0, 0, s)),
        out_shape=jax.ShapeDtypeStruct((L, nlat, wd), F32),
        scratch_shapes=[pltpu.VMEM((S5_SLAB, nrow, pk), BF16), pltpu.VMEM((S5_SLAB, nlat, pk), F32),
                        pltpu.VMEM((nrow, 4 * LANES), F32), pltpu.VMEM((nrow, 4 * LANES), F32)],
        compiler_params=_cparams(("parallel",)),
        name="s5_scan",
    )(u_t, toep, pmat, qmat, coef)


def _rms(x, g):
    return x * lax.rsqrt(jnp.mean(x * x, axis=-1, keepdims=True) + EPS) * g


def _s5_readout_body(y_ref, g_ref, gw_ref, gb_ref, w_ref, h_ref, mod_ref, nf_ref, o_ref):
    y = jax.nn.gelu(y_ref[0])
    t = y * _sigmoid(jnp.dot(y.astype(BF16), gw_ref[...], preferred_element_type=F32) + gb_ref[...])
    z = (t * _silu(g_ref[0].astype(F32))).astype(BF16)
    out = jnp.dot(z, w_ref[...], preferred_element_type=F32)
    h = h_ref[0] + mod_ref[0, :, 2 * D:3 * D] * out
    o_ref[0] = _rms(h, nf_ref[...])


def _s5_readout(y, g, glu_w, glu_b, w_out, h, mod_b, norm_f, n_ctx):
    nb, t_len, _ = y.shape
    tm = n_ctx
    off = n_ctx // tm
    lat = lambda wd: pl.BlockSpec((1, tm, wd), lambda b, i: (b, i, 0))
    full = lambda wd: pl.BlockSpec((1, tm, wd), lambda b, i: (b, i + off, 0))
    const = lambda shape: pl.BlockSpec(shape, lambda b, i: (0, 0))
    return pl.pallas_call(
        _s5_readout_body,
        grid=(nb, t_len // tm),
        in_specs=[lat(S5_WIDTH), full(S5_WIDTH), const((S5_WIDTH, S5_WIDTH)), const((1, S5_WIDTH)),
                  const((S5_WIDTH, D)), full(D),
                  pl.BlockSpec((1, 1, 3 * D), lambda b, i: (b, 0, 0)), const((1, D))],
        out_specs=lat(D),
        out_shape=jax.ShapeDtypeStruct((nb, t_len, D), F32),
        compiler_params=_cparams(("parallel", "parallel")),
        name="s5_readout",
    )(y, g, glu_w.astype(BF16), glu_b.reshape(1, S5_WIDTH), w_out.astype(BF16), h, mod_b,
      norm_f.reshape(1, D))


def _s5_layer(h, mod_b, mod_lat, gn, w_in, a_re, a_im, log_dt, b_re, b_im, c_re, c_im, d_skip, glu_w, glu_b,
              w_out, norm_f, n_ctx):
    nb, rows, _ = h.shape
    L = S5_CHUNK
    layout = _bmajor(nb, rows, n_ctx, ROW_TILE)
    _, row, _ = layout
    out_shapes, out_specs = _rows_out((nb, rows), row, [(S5_WIDTH, BF16)] * 2)
    u, g = _in_proj((h,), [row(D)], _load_block, mod_b, gn, w_in.astype(BF16), layout, out_shapes, out_specs,
                    _s5_epilogue, name="s5_in_proj")
    nc = rows // L
    u_t = u.reshape(nb, nc, L, S5_WIDTH).transpose(2, 1, 0, 3).reshape(L, nc * nb, S5_WIDTH)
    toep, pmat, qmat, coef = _s5_matrices(a_re, a_im, log_dt, b_re, b_im, c_re, c_im, d_skip)
    y_t = _s5_scan(u_t, toep, pmat, qmat, coef, nb, n_ctx)
    t_len = rows - n_ctx
    y = y_t.reshape(L, t_len // L, nb, S5_WIDTH).transpose(2, 1, 0, 3).reshape(nb, t_len, S5_WIDTH)
    return _s5_readout(y, g, glu_w, glu_b, w_out, h, mod_lat, norm_f, n_ctx)


def kernel(x, c, ctx, c_ctx, ada_w0, ada_b0, norm0, w_in0, conv_w0, conv_b0, lru_wa0, lru_ba0, lru_wx0, lru_bx0, lru_lam0, w_out0, ada_w1, ada_b1, norm1, w_in1, sink1, w_out1, ada_w2, ada_b2, norm2, w_in2, rpb2, w_out2, ada_w3, ada_b3, norm3, w_in3, s5_a_re3, s5_a_im3, s5_log_dt3, s5_b_re3, s5_b_im3, s5_c_re3, s5_c_im3, s5_d3, glu_w3, glu_b3, w_out3, norm_f):
    nb, t_len, _ = x.shape
    n_ctx = ctx.shape[1]
    cvec = jnp.concatenate([c, c_ctx[None, :], jnp.zeros((16 - nb - 1, D), F32)], axis=0)
    mods = [_modulation(cvec, w, b) for w, b in
            ((ada_w0, ada_b0), (ada_w1, ada_b1), (ada_w2, ada_b2), (ada_w3, ada_b3))]
    mod_b = [_bmajor_mod_table(m, nb, n_ctx + t_len, n_ctx) for m in mods]

    mod_t0 = jnp.stack([jnp.broadcast_to(mods[0][nb], (nb, 3 * D)), mods[0][:nb]], axis=0)
    h = _rglru_layer(ctx, x, mod_t0, norm0, w_in0, conv_w0, conv_b0, lru_wa0, lru_ba0, lru_wx0, lru_bx0,
                     lru_lam0, w_out0)
    h = _swa_layer(h, mod_b[1], norm1, w_in1, sink1, w_out1, n_ctx)
    h = _na_layer(h, mod_b[2], norm2, w_in2, rpb2, w_out2, n_ctx)
    return _s5_layer(h, mod_b[3], mods[3][:nb, None, :], norm3, w_in3, s5_a_re3, s5_a_im3, s5_log_dt3,
                     s5_b_re3, s5_b_im3, s5_c_re3, s5_c_im3, s5_d3, glu_w3, glu_b3, w_out3, norm_f, n_ctx)
```

```python
import functools
import math

import jax
import jax.numpy as jnp
import numpy as np
from jax import lax
from jax.experimental import pallas as pl
from jax.experimental.pallas import tpu as pltpu

F32 = jnp.float32
BF16 = jnp.bfloat16

D = 1024
HEAD_DIM = 64
GRID_W = 64
EPS = 1e-6
NEG_INF = -1e30
ROPE_BASE = 10000.0
LANES = 128
LOG2E = math.log2(math.e)
QK_SCALE = HEAD_DIM ** -0.5 * LOG2E
ONES_ROWS = 16

LRU_WIDTH = 1408
LRU_BLOCKS = 16
LRU_BW = LRU_WIDTH // LRU_BLOCKS
LRU_WIN = 3
CONV_W = 4
LRU_C = 8.0
LRU_TT = 64

SWA_HEADS = 16
SWA_KV_HEADS = 4
WINDOW = 128
BLOCK_Q = 128

NA_HEADS = 16
NA_ROWS = 8
NA_COLS = 16
NA_QROWS = 4
NA_KROWS = 12
NA_HSTEP = 8
NA_SLOTS = 4
NA_MASKED = 2 * NA_ROWS - 1

S5_WIDTH = 1024
S5_GROUP = 16
S5_GROUPS = S5_WIDTH // S5_GROUP
S5_STATE = 64
S5_CHUNK = 16
S5_SLAB = LANES // S5_GROUP
S5_ROWS = 128

ROW_TILE = 768
VMEM_LIMIT = 56 * 1024 * 1024


def _cparams(sem):
    return pltpu.CompilerParams(dimension_semantics=sem, vmem_limit_bytes=VMEM_LIMIT)


def _sigmoid(v):
    return 0.5 * jnp.tanh(0.5 * v) + 0.5


def _silu(v):
    half = 0.5 * v
    return half * (jnp.tanh(half) + 1.0)


def _bmajor(nb, rows, n_ctx, tm):
    grid = (nb, rows // tm)
    per_batch = rows // tm

    def row(width, off=0):
        return pl.BlockSpec((1, tm, width), lambda b, i: (b, i + off, 0))

    mod = pl.BlockSpec((tm // n_ctx, 1, 3 * D), lambda b, i: (b * per_batch + i, 0, 0))
    return grid, row, mod


def _bmajor_mod_table(mod16, nb, rows, n_ctx):
    groups = rows // n_ctx
    tab = jnp.broadcast_to(mod16[:nb, None, :], (nb, groups, 3 * D))
    tab = tab.at[:, 0, :].set(mod16[nb])
    return tab.reshape(nb * groups, 1, 3 * D)


def _tmajor(nb, rows, n_ctx, tt):
    grid = (1, rows // tt)
    nct = n_ctx // tt

    def row(width, off=0):
        return pl.BlockSpec((tt, nb, width), lambda b, i: (i + off, 0, 0))

    mod = pl.BlockSpec((1, nb, 3 * D), lambda b, i: (jnp.where(i < nct, 0, 1), 0, 0))
    return grid, row, mod


def _mod_body(c_ref, w_ref, b_ref, o_ref):
    s = _silu(c_ref[...]).astype(BF16)
    o_ref[...] = jnp.dot(s, w_ref[...].astype(BF16), preferred_element_type=F32) + b_ref[...]


def _modulation(cvec, ada_w, ada_b):
    n = ada_w.shape[1]
    tn = 1024
    return pl.pallas_call(
        _mod_body,
        grid=(n // tn,),
        in_specs=[pl.BlockSpec((16, D), lambda j: (0, 0)),
                  pl.BlockSpec((D, tn), lambda j: (0, j)),
                  pl.BlockSpec((1, tn), lambda j: (0, j))],
        out_specs=pl.BlockSpec((16, tn), lambda j: (0, j)),
        out_shape=jax.ShapeDtypeStruct((16, n), F32),
        compiler_params=_cparams(("parallel",)),
        name="modulation",
    )(cvec, ada_w, ada_b.reshape(1, n))


def _by_mod_group(x3, mod3):
    groups = mod3.shape[0]
    return x3 if groups == 1 else x3.reshape(groups, x3.shape[1] // groups, x3.shape[2])


def _norm_mod(x3, mod3, gn):
    xg = _by_mod_group(x3, mod3)
    ms = jnp.mean(xg * xg, axis=-1, keepdims=True)
    y = xg * lax.rsqrt(ms + EPS) * gn
    return (y * (1.0 + mod3[:, :, D:2 * D]) + mod3[:, :, 0:D]).reshape(x3.shape)


def _in_proj_body(*refs, load, n_x, epilogue, n_extra, chunk, n_side=0):
    xs, (mod_ref, gn_ref, w_ref), rest = refs[:n_x], refs[n_x:n_x + 3], refs[n_x + 3:]
    extra, outs = rest[:n_extra], rest[n_extra:]
    x3 = load(*xs, *outs[:n_side])
    outs = outs[n_side:]
    n3 = _norm_mod(x3, mod_ref[...], gn_ref[...])
    lead = x3.shape[:2]
    n = n3.reshape(lead[0] * lead[1], D).astype(BF16)
    for j in range(w_ref.shape[1] // chunk):
        acc = jnp.dot(n, w_ref[:, j * chunk:(j + 1) * chunk], preferred_element_type=F32)
        epilogue(j, acc, lead, extra, outs)


def _load_block(x_ref):
    return x_ref[...]


def _load_after_out_proj(z_ref, w_ref, h_ref, mod_ref, h_out_ref):
    z3 = z_ref[...]
    a, b, wd = z3.shape
    y = jnp.dot(z3.reshape(a * b, wd), w_ref[...], preferred_element_type=F32).reshape(a, b, D)
    mod3 = mod_ref[...]
    h = (_by_mod_group(h_ref[...], mod3) + mod3[:, :, 2 * D:3 * D] * _by_mod_group(y, mod3)).reshape(a, b, D)
    h_out_ref[...] = h
    return h


def _in_proj(xs, x_specs, load, mod, gn, w, layout, out_shapes, out_specs, epilogue, extras=(),
             extra_specs=(), chunk=512, name="in_proj", prev=None):
    grid, row, mod_spec = layout
    n = w.shape[1]
    n_side, aliases = 0, {}
    if prev is not None:
        z, w_out, mod_prev = prev
        wd = w_out.shape[0]
        (h,) = xs
        xs = (z, w_out, h, mod_prev)
        x_specs = [row(wd), pl.BlockSpec((wd, D), lambda b, i: (0, 0)), row(D), mod_spec]
        load, n_side, aliases = _load_after_out_proj, 1, {2: 0}
        out_shapes = [jax.ShapeDtypeStruct(h.shape, F32), *out_shapes]
        out_specs = [row(D), *out_specs]
    body = functools.partial(_in_proj_body, load=load, n_x=len(xs), epilogue=epilogue,
                             n_extra=len(extras), chunk=chunk, n_side=n_side)
    return pl.pallas_call(
        body,
        grid=grid,
        in_specs=[*x_specs, mod_spec,
                  pl.BlockSpec((1, D), lambda b, i: (0, 0)),
                  pl.BlockSpec((D, n), lambda b, i: (0, 0)),
                  *extra_specs],
        out_specs=out_specs,
        out_shape=out_shapes,
        input_output_aliases=aliases,
        compiler_params=_cparams(("parallel", "parallel")),
        name=name,
    )(*xs, mod, gn.reshape(1, D), w, *extras)


def _rows_out(shape2, row, outs):
    return ([jax.ShapeDtypeStruct(shape2 + (wd,), dt) for wd, dt in outs], [row(wd) for wd, _ in outs])


def _put(ref, lo, val, lead):
    ref[:, :, lo:lo + val.shape[1]] = val.reshape(lead + (val.shape[1],)).astype(ref.dtype)


def _out_proj_body(*refs, load, n_h, store):
    (z_ref, w_ref), hs, (mod_ref, o_ref) = refs[:2], refs[2:2 + n_h], refs[2 + n_h:]
    z3 = z_ref[...]
    a, b, wd = z3.shape
    y = jnp.dot(z3.reshape(a * b, wd), w_ref[...], preferred_element_type=F32).reshape(a, b, D)
    mod3 = mod_ref[...]
    h = _by_mod_group(load(*hs), mod3) + mod3[:, :, 2 * D:3 * D] * _by_mod_group(y, mod3)
    store(o_ref, h.reshape(a, b, D))


def _lru_epilogue(j, acc, lead, extra, outs):
    _put(outs[j], 0, acc, lead)


def _lru_tile_index(s, nct, ntiles, reverse):
    if not reverse:
        return s
    return jnp.where(s < nct, nct - 1 - s, ntiles + nct - 1 - s)


def _lru_sweep_body(up_ref, uc_ref, un_ref, cw_ref, cb_ref, wg_ref, bg_ref, lam_ref, *rest,
                    nct, ntiles, reverse):
    if reverse:
        hf_ref, g_ref, o_ref, ext_ref, cv_ref, a_ref, b_ref, carry_ref = rest
    else:
        o_ref, ext_ref, cv_ref, a_ref, b_ref, carry_ref = rest
    s = pl.program_id(0)
    ti = _lru_tile_index(s, nct, ntiles, reverse)
    tt, nb, pw = uc_ref.shape

    @pl.when(s == 0)
    def _():
        carry_ref[...] = jnp.zeros_like(carry_ref)

    seg_start = (ti == 0) | (ti == nct)
    seg_end = (ti == nct - 1) | (ti == ntiles - 1)
    ext_ref[0:2] = jnp.where(seg_start, 0.0, up_ref[...])
    ext_ref[2:2 + tt] = uc_ref[...]
    ext_ref[2 + tt:3 + tt] = jnp.where(seg_end, 0.0, un_ref[...])

    rate = (-0.5 * LRU_C * LOG2E) * jax.nn.softplus(-lam_ref[...])
    tiles = [slice(k * LANES, (k + 1) * LANES) for k in range(pw // LANES)]
    for sl in tiles:
        u = cb_ref[:, sl]
        for tap in range(CONV_W):
            u = u + cw_ref[tap:tap + 1, sl] * ext_ref[tap:tap + tt, :, sl]
        cv_ref[:, sl] = u.reshape(tt * nb, LANES)
    for k, sl in enumerate(tiles):
        win = slice(_lru_window(k) * LANES, (_lru_window(k) + LRU_WIN) * LANES)
        pre = jnp.dot(cv_ref[:, win].astype(BF16), wg_ref[k], preferred_element_type=F32) + bg_ref[k]
        tr = jnp.tanh(pre[:, :LANES]) + 1.0
        ti = jnp.tanh(pre[:, LANES:]) + 1.0
        a = jnp.exp2(rate[:, sl] * tr)
        a_ref[:, sl] = a
        b_ref[:, sl] = (0.5 * jnp.sqrt(1.0 - a * a)) * (ti * cv_ref[:, sl])

    def step(q, h):
        t = tt - 1 - q if reverse else q
        rows = pl.ds(pl.multiple_of(t * nb, nb), nb)
        h = a_ref[rows, :] * h + b_ref[rows, :]
        if reverse:
            o_ref[t] = ((hf_ref[t] + h) * _silu(g_ref[t].astype(F32))).astype(o_ref.dtype)
        else:
            o_ref[t] = h
        return h

    carry_ref[...] = lax.fori_loop(0, tt, step, carry_ref[...], unroll=4)


def _lru_sweep(u, cw, cb, wg, bg, lam, nct_rows, reverse, hf=None, g=None):
    rows, nb, pw = u.shape
    tt = LRU_TT
    ntiles = rows // tt
    nct = nct_rows // tt
    tile = functools.partial(_lru_tile_index, nct=nct, ntiles=ntiles, reverse=reverse)
    cur = lambda wd: pl.BlockSpec((tt, nb, wd), lambda s: (tile(s), 0, 0))
    const = lambda shape: pl.BlockSpec(shape, lambda s: (0,) * len(shape))
    in_specs = [
        pl.BlockSpec((2, nb, pw), lambda s: (jnp.maximum(tile(s) * (tt // 2) - 1, 0), 0, 0)),
        cur(pw),
        pl.BlockSpec((1, nb, pw), lambda s: (jnp.minimum((tile(s) + 1) * tt, rows - 1), 0, 0)),
        const((CONV_W, pw)), const((1, pw)),
        const(wg.shape), const(bg.shape), const((1, pw)),
    ]
    args = [u, u, u, cw, cb, wg, bg, lam]
    if reverse:
        in_specs += [cur(pw), cur(pw)]
        args += [hf, g]
    body = functools.partial(_lru_sweep_body, nct=nct, ntiles=ntiles, reverse=reverse)
    return pl.pallas_call(
        body,
        grid=(ntiles,),
        in_specs=in_specs,
        out_specs=cur(pw),
        out_shape=jax.ShapeDtypeStruct(u.shape, BF16 if reverse else F32),
        scratch_shapes=[pltpu.VMEM((tt + 3, nb, pw), F32),
                        pltpu.VMEM((tt * nb, pw), F32),
                        pltpu.VMEM((tt * nb, pw), F32),
                        pltpu.VMEM((tt * nb, pw), F32),
                        pltpu.VMEM((nb, pw), F32)],
        compiler_params=_cparams(("arbitrary",)),
        name="lru_bwd" if reverse else "lru_fwd",
    )(*args)


def _lru_window(k):
    first_block = (k * LANES) // LRU_BW
    return min((first_block * LRU_BW) // LANES, LRU_WIDTH // LANES - LRU_WIN)


def _lru_gate_weights(wa, wx, ba, bx):
    eye = 0.5 * jnp.eye(LRU_BLOCKS, dtype=BF16)

    def dense(w):
        return (w.astype(BF16)[:, :, None, :] * eye[:, None, :, None]).reshape(LRU_WIDTH, LRU_WIDTH)

    da, dx = dense(wa), dense(wx)
    wg, bg = [], []
    for k in range(LRU_WIDTH // LANES):
        rows = slice(_lru_window(k) * LANES, (_lru_window(k) + LRU_WIN) * LANES)
        cols = slice(k * LANES, (k + 1) * LANES)
        wg.append(jnp.concatenate([da[rows, cols], dx[rows, cols]], axis=1))
        bg.append(0.5 * jnp.concatenate([ba[cols], bx[cols]])[None])
    return jnp.stack(wg).astype(BF16), jnp.stack(bg)


def _load_time_major(ctx_ref, x_ref, *, nct):
    blk = jnp.where(pl.program_id(1) < nct, ctx_ref[...], x_ref[...])
    return pltpu.einshape("btd->tbd", blk)


def _store_batch_major(o_ref, val):
    o_ref[...] = pltpu.einshape("tbd->btd", val)


def _rglru_layer(ctx, x, mod_t, gn, w_in, conv_w, conv_b, wa, ba, wx, bx, lam, w_out):
    nb, n_ctx, _ = ctx.shape
    rows = n_ctx + x.shape[1]
    tt = LRU_TT
    nct = n_ctx // tt
    layout = _tmajor(nb, rows, n_ctx, tt)
    grid, row, mod_spec = layout
    src_specs = [pl.BlockSpec((nb, tt, D), lambda b, i: (0, jnp.minimum(i, nct - 1), 0)),
                 pl.BlockSpec((nb, tt, D), lambda b, i: (0, jnp.maximum(i - nct, 0), 0))]
    load = functools.partial(_load_time_major, nct=nct)
    wd = LRU_WIDTH
    out_shapes, out_specs = _rows_out((rows, nb), row, [(wd, F32), (wd, BF16)])
    u, g = _in_proj((ctx, x), src_specs, load, mod_t, gn, w_in.astype(BF16), layout, out_shapes, out_specs,
                    _lru_epilogue, chunk=wd, name="lru_in_proj")
    cb = conv_b.reshape(1, wd)
    sweeps = []
    for d in range(2):
        wg, bg = _lru_gate_weights(wa[d], wx[d], ba[d], bx[d])
        sweeps.append((conv_w, cb, wg, bg, lam[d:d + 1]))
    hf = _lru_sweep(u, *sweeps[0], n_ctx, reverse=False)
    z = _lru_sweep(u, *sweeps[1], n_ctx, reverse=True, hf=hf, g=g)
    body = functools.partial(_out_proj_body, load=load, n_h=2, store=_store_batch_major)
    return pl.pallas_call(
        body,
        grid=grid,
        in_specs=[row(wd), pl.BlockSpec((wd, D), lambda b, i: (0, 0)), *src_specs, mod_spec],
        out_specs=pl.BlockSpec((nb, tt, D), lambda b, i: (0, i, 0)),
        out_shape=jax.ShapeDtypeStruct((nb, rows, D), F32),
        compiler_params=_cparams(("parallel", "parallel")),
        name="lru_out_proj",
    )(z, w_out.astype(BF16), ctx, x, mod_t)


def _rope_tables(n_ctx, t_len):
    pos = jnp.arange(t_len)
    row = (pos // GRID_W).astype(F32)
    col = (pos % GRID_W).astype(F32)
    n_ax = HEAD_DIM // 4
    freqs = ROPE_BASE ** (-jnp.arange(n_ax, dtype=F32) / n_ax)
    ang = jnp.concatenate([row[:, None] * freqs, col[:, None] * freqs], axis=-1)
    cos, sin = jnp.cos(ang), jnp.sin(ang)
    cos_h = jnp.concatenate([cos, cos], axis=-1)
    sin_h = jnp.concatenate([-sin, sin], axis=-1)
    cos_f = jnp.concatenate([jnp.ones((n_ctx, HEAD_DIM), F32), cos_h], axis=0)
    sin_f = jnp.concatenate([jnp.zeros((n_ctx, HEAD_DIM), F32), sin_h], axis=0)
    return jnp.tile(cos_f, (1, 2)), jnp.tile(sin_f, (1, 2))


def _rope(x, cos, sin):
    lane = lax.broadcasted_iota(jnp.int32, x.shape, 1)
    swapped = jnp.where(lane % HEAD_DIM < HEAD_DIM // 2,
                        pltpu.roll(x, LANES - HEAD_DIM // 2, 1), pltpu.roll(x, HEAD_DIM // 2, 1))
    return x * cos + swapped * sin


def _swa_epilogue(j, acc, lead, extra, outs):
    cos_ref, sin_ref = extra
    q_ref, qr_ref, k_ref, v_ref, g_ref = outs
    qd = SWA_HEADS * HEAD_DIM
    kvd = SWA_KV_HEADS * HEAD_DIM
    grp = SWA_HEADS // SWA_KV_HEADS
    width = acc.shape[1]
    lo = j * width
    cos, sin = cos_ref[...], sin_ref[...]

    def roped(x):
        return jnp.concatenate([_rope(x[:, s:s + LANES], cos, sin) for s in range(0, x.shape[1], LANES)], axis=1)

    def head(x, hh):
        return x[:, hh * HEAD_DIM:(hh + 1) * HEAD_DIM]

    if lo < qd:
        q = acc * QK_SCALE
        qr = roped(q)
        for hh in range(width // HEAD_DIM):
            kv, gq = divmod(lo // HEAD_DIM + hh, grp)
            for rt in range(lead[1] // BLOCK_Q):
                rows = slice(rt * BLOCK_Q, (rt + 1) * BLOCK_Q)
                dst = slice(gq * BLOCK_Q, (gq + 1) * BLOCK_Q)
                q_ref[0, kv, rt, dst, :] = head(q, hh)[rows].astype(q_ref.dtype)
                qr_ref[0, kv, rt, dst, :] = head(qr, hh)[rows].astype(qr_ref.dtype)
    elif lo == qd:
        kr = roped(acc[:, :kvd])
        for kv in range(SWA_KV_HEADS):
            k_ref[0, kv] = head(kr, kv).astype(k_ref.dtype)
            v_ref[0, kv] = head(acc[:, kvd:], kv).T.astype(v_ref.dtype)
    else:
        _put(g_ref, lo - qd - 2 * kvd, acc, lead)


def _swa_mask_table():
    grp = SWA_HEADS // SWA_KV_HEADS
    kj = np.arange(3 * BLOCK_Q)[:, None]
    qi = np.arange(BLOCK_Q)[None, :]
    base = np.where(np.abs(kj - BLOCK_Q - qi) <= WINDOW, 0.0, NEG_INF).astype(np.float32)
    first = np.where(kj < BLOCK_Q, NEG_INF, base).astype(np.float32)
    last = np.where(kj >= 2 * BLOCK_Q, NEG_INF, base).astype(np.float32)
    tab = np.stack([np.full_like(base, NEG_INF), first, base, last])
    return jnp.asarray(np.tile(tab, (1, 1, grp)))


def _swa_body(sink_ref, mask_ref, q_ref, qr_ref, kc_ref, k0_ref, k1_ref, k2_ref, vc_ref, v0_ref, v1_ref,
              v2_ref, g_ref, o_ref, s_ref, p_ref):
    grp = SWA_HEADS // SWA_KV_HEADS
    dims = (((1,), (1,)), ((), ()))
    nc = kc_ref.shape[2]
    sink_p = {}

    def scores(kv):
        kb = jnp.concatenate([k0_ref[0, kv], k1_ref[0, kv], k2_ref[0, kv]], axis=0)
        s_ref[kv, :nc, :] = lax.dot_general(kc_ref[0, kv], q_ref[0, kv, 0], dims, preferred_element_type=F32)
        s_ref[kv, nc:, :] = (lax.dot_general(kb, qr_ref[0, kv, 0], dims, preferred_element_type=F32)
                             + mask_ref[0])

    def softmax(kv):
        sink_p[kv] = []
        for gq in range(grp):
            lanes = slice(gq * BLOCK_Q, (gq + 1) * BLOCK_Q)
            s = s_ref[kv, :, lanes]
            sink = sink_ref[kv, :, lanes]
            m = jnp.maximum(jnp.max(s, axis=0, keepdims=True), sink)
            sink_p[kv].append(jnp.exp2(sink - m))
            p_ref[kv, :, lanes] = jnp.exp2(s - m).astype(BF16)

    def values(kv):
        vt = jnp.concatenate([vc_ref[0, kv], v0_ref[0, kv], v1_ref[0, kv], v2_ref[0, kv]], axis=1)
        vt = jnp.concatenate([vt, jnp.ones((ONES_ROWS, vt.shape[1]), BF16)], axis=0)
        ot = jnp.dot(vt, p_ref[kv], preferred_element_type=F32)
        denom = ot[HEAD_DIM:HEAD_DIM + 1] + jnp.concatenate(sink_p[kv], axis=1)
        ot = ot[:HEAD_DIM] * (1.0 / denom)
        slab = jnp.concatenate([ot[:, gq * BLOCK_Q:(gq + 1) * BLOCK_Q].T for gq in range(grp)], axis=1)
        lanes = slice(kv * grp * HEAD_DIM, (kv + 1) * grp * HEAD_DIM)
        o_ref[0, :, lanes] = (slab * _silu(g_ref[0, :, lanes].astype(F32))).astype(o_ref.dtype)

    scores(0)
    scores(1)
    for kv in range(SWA_KV_HEADS):
        softmax(kv)
        if kv + 2 < SWA_KV_HEADS:
            scores(kv + 2)
        values(kv)


def _swa_attention(q, qr, k, vt, g, sink, n_ctx):
    nb, nkv, nt, qrows, hd = q.shape
    rows = k.shape[2]
    grp = SWA_HEADS // SWA_KV_HEADS
    nct = n_ctx // BLOCK_Q
    nkeys = n_ctx + 3 * BLOCK_Q
    qspec = pl.BlockSpec((1, nkv, 1, qrows, hd), lambda b, n: (b, 0, n, 0, 0))
    kctx = pl.BlockSpec((1, nkv, n_ctx, hd), lambda b, n: (b, 0, 0, 0))
    vctx = pl.BlockSpec((1, nkv, hd, n_ctx), lambda b, n: (b, 0, 0, 0))
    rowspec = pl.BlockSpec((1, BLOCK_Q, g.shape[2]), lambda b, n: (b, n, 0))

    def kband(off):
        return pl.BlockSpec((1, nkv, BLOCK_Q, hd), lambda b, n: (b, 0, jnp.clip(n + off, nct, nt - 1), 0))

    def vband(off):
        return pl.BlockSpec((1, nkv, hd, BLOCK_Q), lambda b, n: (b, 0, 0, jnp.clip(n + off, nct, nt - 1)))

    def variant(b, n):
        return (jnp.where(n < nct, 0, jnp.where(n == nct, 1, jnp.where(n == nt - 1, 3, 2))), 0, 0)

    sink_rows = jnp.repeat(sink.astype(F32).reshape(nkv, grp) * LOG2E, BLOCK_Q, axis=1).reshape(nkv, 1, qrows)
    return pl.pallas_call(
        _swa_body,
        grid=(nb, nt),
        in_specs=[pl.BlockSpec((nkv, 1, qrows), lambda b, n: (0, 0, 0)),
                  pl.BlockSpec((1, 3 * BLOCK_Q, qrows), variant), qspec, qspec,
                  kctx, kband(-1), kband(0), kband(1), vctx, vband(-1), vband(0), vband(1), rowspec],
        out_specs=rowspec,
        out_shape=jax.ShapeDtypeStruct((nb, rows, g.shape[2]), BF16),
        scratch_shapes=[pltpu.VMEM((nkv, nkeys, qrows), F32), pltpu.VMEM((nkv, nkeys, qrows), BF16)],
        compiler_params=_cparams(("parallel", "parallel")),
        name="swa_attention",
    )(sink_rows, _swa_mask_table(), q, qr, k, k, k, k, vt, vt, vt, vt, g)


def _swa_layer(h, mod_b, gn, w_in, sink, n_ctx):
    nb, rows, _ = h.shape
    tm = ROW_TILE
    layout = _bmajor(nb, rows, n_ctx, tm)
    _, row, _ = layout
    qd = SWA_HEADS * HEAD_DIM
    grp = SWA_HEADS // SWA_KV_HEADS
    nkv = SWA_KV_HEADS
    cos, sin = _rope_tables(n_ctx, rows - n_ctx)
    tab = pl.BlockSpec((tm, LANES), lambda b, i: (i, 0))
    q_shape = jax.ShapeDtypeStruct((nb, nkv, rows // BLOCK_Q, grp * BLOCK_Q, HEAD_DIM), BF16)
    q_spec = pl.BlockSpec((1, nkv, tm // BLOCK_Q, grp * BLOCK_Q, HEAD_DIM), lambda b, i: (b, 0, i, 0, 0))
    k_shape = jax.ShapeDtypeStruct((nb, nkv, rows, HEAD_DIM), BF16)
    k_spec = pl.BlockSpec((1, nkv, tm, HEAD_DIM), lambda b, i: (b, 0, i, 0))
    vt_shape = jax.ShapeDtypeStruct((nb, nkv, HEAD_DIM, rows), BF16)
    vt_spec = pl.BlockSpec((1, nkv, HEAD_DIM, tm), lambda b, i: (b, 0, 0, i))
    q, qr, k, vt, g = _in_proj(
        (h,), [row(D)], _load_block, mod_b, gn, w_in.astype(BF16), layout,
        [q_shape, q_shape, k_shape, vt_shape, jax.ShapeDtypeStruct((nb, rows, qd), BF16)],
        [q_spec, q_spec, k_spec, vt_spec, row(qd)], _swa_epilogue,
        extras=(cos, sin), extra_specs=(tab, tab), name="swa_in_proj")
    return _swa_attention(q, qr, k, vt, g, sink, n_ctx)


def _na_epilogue(j, acc, lead, extra, outs):
    q_ref, k_ref, v_ref, g_ref = outs
    wd = NA_HEADS * HEAD_DIM
    width = acc.shape[1]
    lo = j * width
    which, off = lo // wd, lo % wd
    if which == 3:
        _put(g_ref, off, acc, lead)
        return
    ref = (q_ref, k_ref, v_ref)[which]
    val = acc * QK_SCALE if which == 0 else acc
    for hh in range(width // HEAD_DIM):
        ref[0, off // HEAD_DIM + hh] = val[:, hh * HEAD_DIM:(hh + 1) * HEAD_DIM].astype(ref.dtype)


def _na_bias_tiles(rpb):
    reach = GRID_W - NA_COLS
    period = 2 * GRID_W
    ndy = 2 * NA_ROWS - 1
    edge = jnp.pad(rpb.astype(F32) * LOG2E, ((0, 0), (0, 0), (reach, reach)), mode="edge")
    flat = jnp.tile(jnp.pad(edge[:, :, ::-1], ((0, 0), (0, 0), (0, 1))), (1, 1, GRID_W))[:, :, :GRID_W * (period - 1)]
    e = flat.reshape(NA_HEADS, ndy, GRID_W, period - 1)[:, :, :, GRID_W - 1:]
    col = np.arange(GRID_W)
    cstart = np.clip(col - NA_COLS // 2, 0, GRID_W - NA_COLS)
    col_ok = (col[:, None] >= cstart[None, :]) & (col[:, None] < cstart[None, :] + NA_COLS)
    e = jnp.where(jnp.asarray(col_ok)[None, None], e, NEG_INF)
    e = jnp.pad(e, ((0, 0), (0, 1), (0, 0), (0, 0)), constant_values=NEG_INF)
    shape = (NA_HEADS // NA_HSTEP, NA_HSTEP, ndy + 1, GRID_W, period)
    left = jnp.pad(e, ((0, 0), (0, 0), (0, 0), (0, GRID_W)), constant_values=NEG_INF).reshape(shape)
    right = jnp.pad(e, ((0, 0), (0, 0), (0, 0), (GRID_W, 0)), constant_values=NEG_INF).reshape(shape)
    return left, right


def _na_body(q_ref, k_ref, v_ref, g_ref, bl_ref, br_ref, o_ref, s_ref, p_ref, *, n_ctx, grid_rows):
    rb = pl.program_id(2)
    r0 = (rb - 1) * NA_QROWS
    kb = jnp.clip(r0 - NA_ROWS // 2, 0, grid_rows - NA_KROWS)
    start = pl.multiple_of(n_ctx + kb * GRID_W, GRID_W)
    nk = NA_KROWS * GRID_W
    nq = q_ref.shape[2]
    a_bt = (((1,), (1,)), ((), ()))
    at_b = (((0,), (0,)), ((), ()))

    def tile_index(qr, kr):
        qrow, krow = r0 + qr, kb + kr
        first = jnp.clip(qrow - NA_ROWS // 2, 0, grid_rows - NA_ROWS)
        inside = (rb >= 1) & (krow >= first) & (krow < first + NA_ROWS)
        return jnp.where(inside, krow - qrow + NA_ROWS - 1, NA_MASKED)

    idx = [[tile_index(qr, kr) for kr in range(NA_KROWS)] for qr in range(NA_QROWS)]
    left_half = lax.broadcasted_iota(jnp.int32, (GRID_W, 2 * GRID_W), 1) < GRID_W
    ones = jnp.ones((n_ctx + nk, HEAD_DIM), BF16)
    outs = [None] * NA_HSTEP

    def scores(hl):
        slot = hl % NA_SLOTS
        qh = q_ref[0, hl]
        bias = jnp.concatenate(
            [jnp.concatenate([jnp.where(left_half, bl_ref[0, hl, idx[2 * m][kr]], br_ref[0, hl, idx[2 * m + 1][kr]])
                              for m in range(NA_QROWS // 2)], axis=1) for kr in range(NA_KROWS)], axis=0)
        s_ref[slot, :n_ctx, :] = lax.dot_general(k_ref[0, hl, 0:n_ctx, :], qh, a_bt, preferred_element_type=F32)
        s_ref[slot, n_ctx:, :] = (lax.dot_general(k_ref[0, hl, pl.ds(start, nk), :], qh, a_bt,
                                                  preferred_element_type=F32) + bias)

    def softmax(hl):
        slot = hl % NA_SLOTS
        for c in range(nq // LANES):
            lanes = slice(c * LANES, (c + 1) * LANES)
            s = s_ref[slot, :, lanes]
            p_ref[slot, :, lanes] = jnp.exp2(s - jnp.max(s, axis=0, keepdims=True)).astype(BF16)

    def values(hl):
        v1 = jnp.concatenate(
            [jnp.concatenate([v_ref[0, hl, 0:n_ctx, :], v_ref[0, hl, pl.ds(start, nk), :]], axis=0), ones], axis=1)
        ot = lax.dot_general(v1, p_ref[hl % NA_SLOTS], at_b, preferred_element_type=F32)
        outs[hl] = (ot[:HEAD_DIM] * (1.0 / ot[HEAD_DIM:HEAD_DIM + 1])).T

    scores(0)
    scores(1)
    for hl in range(NA_HSTEP):
        softmax(hl)
        if hl + 2 < NA_HSTEP:
            scores(hl + 2)
        values(hl)
    slab = jnp.concatenate(outs, axis=1)
    o_ref[0] = (slab * _silu(g_ref[0].astype(F32))).astype(o_ref.dtype)


def _na_attention(q, k, v, g, bias_left, bias_right, n_ctx):
    nb, nh, rows, hd = q.shape
    grid_rows = (rows - n_ctx) // GRID_W
    nq = NA_QROWS * GRID_W
    assert n_ctx == nq, "context tokens form exactly one query block"
    nkeys = n_ctx + NA_KROWS * GRID_W
    lanes = NA_HSTEP * HEAD_DIM
    qspec = pl.BlockSpec((1, NA_HSTEP, nq, hd), lambda p, b, r: (b, p, r, 0))
    kvspec = pl.BlockSpec((1, NA_HSTEP, rows, hd), lambda p, b, r: (b, p, 0, 0))
    rowspec = pl.BlockSpec((1, nq, lanes), lambda p, b, r: (b, r, p))
    bspec = pl.BlockSpec((1,) + bias_left.shape[1:], lambda p, b, r: (p, 0, 0, 0, 0))
    body = functools.partial(_na_body, n_ctx=n_ctx, grid_rows=grid_rows)
    return pl.pallas_call(
        body,
        grid=(nh // NA_HSTEP, nb, rows // nq),
        in_specs=[qspec, kvspec, kvspec, rowspec, bspec, bspec],
        out_specs=rowspec,
        out_shape=jax.ShapeDtypeStruct(g.shape, BF16),
        scratch_shapes=[pltpu.VMEM((NA_SLOTS, nkeys, nq), F32), pltpu.VMEM((NA_SLOTS, nkeys, nq), BF16)],
        compiler_params=_cparams(("parallel", "parallel", "parallel")),
        name="na_attention",
    )(q, k, v, g, bias_left, bias_right)


def _na_layer(h, prev, mod_b, gn, w_in, rpb, n_ctx):
    nb, rows, _ = h.shape
    tm = ROW_TILE
    layout = _bmajor(nb, rows, n_ctx, tm)
    _, row, _ = layout
    wd = NA_HEADS * HEAD_DIM
    head_shape = jax.ShapeDtypeStruct((nb, NA_HEADS, rows, HEAD_DIM), BF16)
    head_spec = pl.BlockSpec((1, NA_HEADS, tm, HEAD_DIM), lambda b, i: (b, 0, i, 0))
    h, q, k, v, g = _in_proj((h,), [row(D)], _load_block, mod_b, gn, w_in.astype(BF16), layout,
                             [head_shape] * 3 + [jax.ShapeDtypeStruct((nb, rows, wd), BF16)],
                             [head_spec] * 3 + [row(wd)], _na_epilogue, name="swa_out_na_in_proj", prev=prev)
    bias_left, bias_right = _na_bias_tiles(rpb)
    return h, _na_attention(q, k, v, g, bias_left, bias_right, n_ctx)


def _s5_epilogue(j, acc, lead, extra, outs):
    u_ref, g_ref = outs
    lo = j * acc.shape[1]
    if lo < S5_WIDTH:
        _put(u_ref, lo, acc, lead)
    else:
        _put(g_ref, lo - S5_WIDTH, acc, lead)


def _s5_matrices(a_re, a_im, log_dt, b_re, b_im, c_re, c_im, d_skip):
    L = S5_CHUNK
    lam = lax.complex(a_re.astype(F32), a_im.astype(F32))
    lam_dt = lam * jnp.exp(log_dt.astype(F32))[..., None]
    lam_bar = jnp.exp(lam_dt)
    b_bar = ((lam_bar - 1.0) / lam)[..., None] * lax.complex(b_re.astype(F32), b_im.astype(F32))
    c_mat = lax.complex(c_re.astype(F32), c_im.astype(F32))
    tau = np.arange(L)

    def power(expo, d):
        return jnp.exp(jnp.asarray(expo, F32).reshape(expo.shape + (1, 1)) * lam_dt[d])

    w = jnp.stack([c_mat[d][None] * power(tau, d)[:, :, None, :] for d in range(2)])
    w2 = jnp.concatenate([jnp.real(w), -jnp.imag(w)], axis=-1)
    b2 = jnp.concatenate([jnp.real(b_bar), jnp.imag(b_bar)], axis=-2)
    kern = jnp.einsum('dkgip,dgpj->dkgij', w2, b2).reshape(2 * L, S5_GROUPS, S5_GROUP, S5_GROUP)
    lag = tau[None, :] - tau[:, None]
    sel = np.concatenate([lag[None] == tau[:, None, None], -lag[None] == tau[:, None, None]]).astype(np.float32)
    both = jnp.einsum('ktu,kgij->tugij', jnp.asarray(sel), kern)
    eye_t = jnp.eye(L, dtype=F32)[:, :, None, None, None]
    skip = jnp.eye(S5_GROUP, dtype=F32)[None] * d_skip.astype(F32).reshape(S5_GROUPS, S5_GROUP, 1)
    both = both + eye_t * skip[None, None]
    toep = both.transpose(2, 0, 4, 1, 3).reshape(S5_GROUPS, L * S5_GROUP, L * S5_GROUP)

    def lanes(z):
        pad = [(0, 0)] * (z.ndim - 1) + [(0, LANES - S5_STATE)]
        return jnp.concatenate([jnp.pad(jnp.real(z), pad), jnp.pad(jnp.imag(z), pad)], axis=-1)

    pf = power(L - 1 - tau, 0)[:, :, :, None] * b_bar[0][None]
    pb = power(tau, 1)[:, :, :, None] * b_bar[1][None]
    pmat = jnp.concatenate([lanes(pf.transpose(1, 0, 3, 2)), lanes(pb.transpose(1, 0, 3, 2))], axis=-1)
    pmat = pmat.reshape(S5_GROUPS, L * S5_GROUP, 4 * LANES)
    qf = c_mat[0][None] * power(tau + 1, 0)[:, :, None, :]
    qb = c_mat[1][None] * power(L - tau, 1)[:, :, None, :]

    def state_rows(z):
        pad = ((0, 0), (0, LANES - S5_STATE), (0, 0), (0, 0))
        zr = jnp.pad(jnp.real(z).transpose(1, 3, 0, 2), pad)
        zi = jnp.pad(-jnp.imag(z).transpose(1, 3, 0, 2), pad)
        return jnp.concatenate([zr, zi], axis=1).reshape(S5_GROUPS, 2 * LANES, L * S5_GROUP)

    qmat = jnp.concatenate([state_rows(qf), state_rows(qb)], axis=1)
    lam_l = jnp.exp(float(L) * lam_dt)
    pad = ((0, 0), (0, LANES - S5_STATE))
    coef = jnp.stack([jnp.pad(jnp.real(lam_l[0]), pad), jnp.pad(jnp.imag(lam_l[0]), pad),
                      jnp.pad(jnp.real(lam_l[1]), pad), jnp.pad(jnp.imag(lam_l[1]), pad)], axis=1)
    coef = jnp.pad(coef, ((0, 0), (0, 4), (0, 0)))
    return toep.astype(BF16), pmat.astype(BF16), qmat.astype(BF16), coef


def _atom_transpose(xs, atom):
    n = len(xs)
    group = lax.broadcasted_iota(jnp.int32, xs[0].shape, 1) // atom
    xs = list(xs)
    d = n // 2
    while d:
        keep = (group & d) == 0
        for i in range(n):
            if not i & d:
                lo, hi = xs[i], xs[i + d]
                xs[i] = jnp.where(keep, lo, pltpu.roll(hi, d * atom, 1))
                xs[i + d] = jnp.where(keep, pltpu.roll(lo, LANES - d * atom, 1), hi)
        d //= 2
    return xs


def _s5_body(u_ref, toep_ref, p_ref, q_ref, coef_ref, y_ref, u_scr, y_scr, s_ref, hin_ref, *, nb, nc_ctx, nc):
    nwin = LANES // S5_GROUP
    nrow, nlat = u_ref.shape[1], y_ref.shape[1]
    blk = S5_ROWS
    for half in range(S5_CHUNK // nwin):
        cols = slice(half * LANES, (half + 1) * LANES)
        for r in range(0, nrow, blk):
            words = [pltpu.bitcast(u_ref[half * nwin + w, r:r + blk, :], jnp.uint32) for w in range(nwin)]
            for gl, x in enumerate(_atom_transpose(words, S5_GROUP)):
                u_scr[gl, r:r + blk, cols] = pltpu.bitcast(x, BF16)
    for gl in range(S5_SLAB):
        _s5_group(gl, u_scr, toep_ref, p_ref, q_ref, coef_ref, y_scr, s_ref, hin_ref, nb, nc_ctx, nc)
    for half in range(S5_CHUNK // nwin):
        cols = slice(half * LANES, (half + 1) * LANES)
        for r in range(0, nlat, blk):
            ys = [y_scr[gl, r:r + blk, cols] for gl in range(S5_SLAB)]
            for w, x in enumerate(_atom_transpose(ys, S5_GROUP)):
                y_ref[half * nwin + w, r:r + blk, :] = x


def _s5_group(gl, u_scr, toep_ref, p_ref, q_ref, coef_ref, y_scr, s_ref, hin_ref, nb, nc_ctx, nc):
    u = u_scr[gl]
    s_ref[...] = jnp.dot(u, p_ref[gl], preferred_element_type=F32)
    coef = coef_ref[gl]
    shape = (nb, LANES)
    lr_f, li_f = jnp.broadcast_to(coef[0:1], shape), jnp.broadcast_to(coef[1:2], shape)
    lr_b, li_b = jnp.broadcast_to(coef[2:3], shape), jnp.broadcast_to(coef[3:4], shape)
    zero = jnp.zeros(shape, F32)

    def step(q, carry):
        fr, fi, br, bi = carry
        rows_f = pl.ds(pl.multiple_of(q * nb, nb), nb)
        cb = jnp.where(q < nc_ctx, nc_ctx - 1 - q, nc + nc_ctx - 1 - q)
        rows_b = pl.ds(pl.multiple_of(cb * nb, nb), nb)
        hin_ref[rows_f, 0:LANES] = fr
        hin_ref[rows_f, LANES:2 * LANES] = fi
        hin_ref[rows_b, 2 * LANES:3 * LANES] = br
        hin_ref[rows_b, 3 * LANES:4 * LANES] = bi
        sfr, sfi = s_ref[rows_f, 0:LANES], s_ref[rows_f, LANES:2 * LANES]
        sbr, sbi = s_ref[rows_b, 2 * LANES:3 * LANES], s_ref[rows_b, 3 * LANES:4 * LANES]
        return (lr_f * fr - li_f * fi + sfr, lr_f * fi + li_f * fr + sfi,
                lr_b * br - li_b * bi + sbr, lr_b * bi + li_b * br + sbi)

    lax.fori_loop(0, nc, step, (zero, zero, zero, zero), unroll=True)

    lat = slice(nc_ctx * nb, nc * nb)
    y_scr[gl] = (jnp.dot(u[lat], toep_ref[gl], preferred_element_type=F32)
                 + jnp.dot(hin_ref[lat, :].astype(BF16), q_ref[gl], preferred_element_type=F32))


def _s5_scan(u_t, toep, pmat, qmat, coef, nb, n_ctx):
    L, nrow, wd = u_t.shape
    pk = L * S5_GROUP
    nc = nrow // nb
    nc_ctx = n_ctx // L
    nlat = (nc - nc_ctx) * nb
    assert nrow % S5_ROWS == 0 and nlat % S5_ROWS == 0
    once = pl.Buffered(1)
    slab = lambda shape: pl.BlockSpec((S5_SLAB,) + shape, lambda s: (s, 0, 0), pipeline_mode=once)
    body = functools.partial(_s5_body, nb=nb, nc_ctx=nc_ctx, nc=nc)
    return pl.pallas_call(
        body,
        grid=(wd // LANES,),
        in_specs=[pl.BlockSpec((L, nrow, LANES), lambda s: (0, 0, s)),
                  slab((pk, pk)), slab((pk, 4 * LANES)), slab((4 * LANES, pk)), slab((8, LANES))],
        out_specs=pl.BlockSpec((L, nlat, LANES), lambda s: (0, 0, s)),
        out_shape=jax.ShapeDtypeStruct((L, nlat, wd), F32),
        scratch_shapes=[pltpu.VMEM((S5_SLAB, nrow, pk), BF16), pltpu.VMEM((S5_SLAB, nlat, pk), F32),
                        pltpu.VMEM((nrow, 4 * LANES), F32), pltpu.VMEM((nrow, 4 * LANES), F32)],
        compiler_params=_cparams(("parallel",)),
        name="s5_scan",
    )(u_t, toep, pmat, qmat, coef)


def _rms(x, g):
    return x * lax.rsqrt(jnp.mean(x * x, axis=-1, keepdims=True) + EPS) * g


def _s5_readout_body(y_ref, g_ref, gw_ref, gb_ref, w_ref, h_ref, mod_ref, nf_ref, o_ref):
    y = jax.nn.gelu(y_ref[0])
    t = y * _sigmoid(jnp.dot(y.astype(BF16), gw_ref[...], preferred_element_type=F32) + gb_ref[...])
    z = (t * _silu(g_ref[0].astype(F32))).astype(BF16)
    out = jnp.dot(z, w_ref[...], preferred_element_type=F32)
    h = h_ref[0] + mod_ref[0, :, 2 * D:3 * D] * out
    o_ref[0] = _rms(h, nf_ref[...])


def _s5_readout(y, g, glu_w, glu_b, w_out, h, mod_b, norm_f, n_ctx):
    nb, t_len, _ = y.shape
    tm = n_ctx
    off = n_ctx // tm
    lat = lambda wd: pl.BlockSpec((1, tm, wd), lambda b, i: (b, i, 0))
    full = lambda wd: pl.BlockSpec((1, tm, wd), lambda b, i: (b, i + off, 0))
    const = lambda shape: pl.BlockSpec(shape, lambda b, i: (0, 0))
    return pl.pallas_call(
        _s5_readout_body,
        grid=(nb, t_len // tm),
        in_specs=[lat(S5_WIDTH), full(S5_WIDTH), const((S5_WIDTH, S5_WIDTH)), const((1, S5_WIDTH)),
                  const((S5_WIDTH, D)), full(D),
                  pl.BlockSpec((1, 1, 3 * D), lambda b, i: (b, 0, 0)), const((1, D))],
        out_specs=lat(D),
        out_shape=jax.ShapeDtypeStruct((nb, t_len, D), F32),
        compiler_params=_cparams(("parallel", "parallel")),
        name="s5_readout",
    )(y, g, glu_w.astype(BF16), glu_b.reshape(1, S5_WIDTH), w_out.astype(BF16), h, mod_b,
      norm_f.reshape(1, D))


def _s5_layer(h, prev, mod_b, mod_lat, gn, w_in, a_re, a_im, log_dt, b_re, b_im, c_re, c_im, d_skip, glu_w, glu_b,
              w_out, norm_f, n_ctx):
    nb, rows, _ = h.shape
    L = S5_CHUNK
    layout = _bmajor(nb, rows, n_ctx, ROW_TILE)
    _, row, _ = layout
    out_shapes, out_specs = _rows_out((nb, rows), row, [(S5_WIDTH, BF16)] * 2)
    h, u, g = _in_proj((h,), [row(D)], _load_block, mod_b, gn, w_in.astype(BF16), layout, out_shapes, out_specs,
                       _s5_epilogue, name="na_out_s5_in_proj", prev=prev)
    nc = rows // L
    u_t = u.reshape(nb, nc, L, S5_WIDTH).transpose(2, 1, 0, 3).reshape(L, nc * nb, S5_WIDTH)
    toep, pmat, qmat, coef = _s5_matrices(a_re, a_im, log_dt, b_re, b_im, c_re, c_im, d_skip)
    y_t = _s5_scan(u_t, toep, pmat, qmat, coef, nb, n_ctx)
    t_len = rows - n_ctx
    y = y_t.reshape(L, t_len // L, nb, S5_WIDTH).transpose(2, 1, 0, 3).reshape(nb, t_len, S5_WIDTH)
    return _s5_readout(y, g, glu_w, glu_b, w_out, h, mod_lat, norm_f, n_ctx)


def kernel(x, c, ctx, c_ctx, ada_w0, ada_b0, norm0, w_in0, conv_w0, conv_b0, lru_wa0, lru_ba0, lru_wx0, lru_bx0, lru_lam0, w_out0, ada_w1, ada_b1, norm1, w_in1, sink1, w_out1, ada_w2, ada_b2, norm2, w_in2, rpb2, w_out2, ada_w3, ada_b3, norm3, w_in3, s5_a_re3, s5_a_im3, s5_log_dt3, s5_b_re3, s5_b_im3, s5_c_re3, s5_c_im3, s5_d3, glu_w3, glu_b3, w_out3, norm_f):
    nb, t_len, _ = x.shape
    n_ctx = ctx.shape[1]
    cvec = jnp.concatenate([c, c_ctx[None, :], jnp.zeros((16 - nb - 1, D), F32)], axis=0)
    mods = [_modulation(cvec, w, b) for w, b in
            ((ada_w0, ada_b0), (ada_w1, ada_b1), (ada_w2, ada_b2), (ada_w3, ada_b3))]
    mod_b = [_bmajor_mod_table(m, nb, n_ctx + t_len, n_ctx) for m in mods]

    mod_t0 = jnp.stack([jnp.broadcast_to(mods[0][nb], (nb, 3 * D)), mods[0][:nb]], axis=0)
    h = _rglru_layer(ctx, x, mod_t0, norm0, w_in0, conv_w0, conv_b0, lru_wa0, lru_ba0, lru_wx0, lru_bx0,
                     lru_lam0, w_out0)
    z = _swa_layer(h, mod_b[1], norm1, w_in1, sink1, n_ctx)
    h, z = _na_layer(h, (z, w_out1.astype(BF16), mod_b[1]), mod_b[2], norm2, w_in2, rpb2, n_ctx)
    return _s5_layer(h, (z, w_out2.astype(BF16), mod_b[2]), mod_b[3], mods[3][:nb, None, :], norm3, w_in3, s5_a_re3, s5_a_im3, s5_log_dt3,
                     s5_b_re3, s5_b_im3, s5_c_re3, s5_c_im3, s5_d3, glu_w3, glu_b3, w_out3, norm_f, n_ctx)
```

```python
import functools
import math

import jax
import jax.numpy as jnp
import numpy as np
from jax import lax
from jax.experimental import pallas as pl
from jax.experimental.pallas import tpu as pltpu

F32 = jnp.float32
BF16 = jnp.bfloat16

D = 1024
HEAD_DIM = 64
GRID_W = 64
EPS = 1e-6
NEG_INF = -1e30
ROPE_BASE = 10000.0
LANES = 128
LOG2E = math.log2(math.e)
QK_SCALE = HEAD_DIM ** -0.5 * LOG2E
ONES_ROWS = 16

LRU_WIDTH = 1408
LRU_BLOCKS = 16
LRU_BW = LRU_WIDTH // LRU_BLOCKS
LRU_WIN = 3
CONV_W = 4
LRU_C = 8.0
LRU_TT = 64

SWA_HEADS = 16
SWA_KV_HEADS = 4
WINDOW = 128
BLOCK_Q = 128
SWA_QTILES = 2
SWA_SLOTS = 4

NA_HEADS = 16
NA_ROWS = 8
NA_COLS = 16
NA_QROWS = 4
NA_KROWS = 12
NA_HSTEP = 8
NA_SLOTS = 4
NA_MASKED = 2 * NA_ROWS - 1

S5_WIDTH = 1024
S5_GROUP = 16
S5_GROUPS = S5_WIDTH // S5_GROUP
S5_STATE = 64
S5_CHUNK = 16
S5_SLAB = LANES // S5_GROUP
S5_ROWS = 128

ROW_TILE = 768
VMEM_LIMIT = 56 * 1024 * 1024


def _cparams(sem):
    return pltpu.CompilerParams(dimension_semantics=sem, vmem_limit_bytes=VMEM_LIMIT)


def _sigmoid(v):
    return 0.5 * jnp.tanh(0.5 * v) + 0.5


def _silu(v):
    half = 0.5 * v
    return half * (jnp.tanh(half) + 1.0)


def _bmajor(nb, rows, n_ctx, tm):
    grid = (nb, rows // tm)
    per_batch = rows // tm

    def row(width, off=0):
        return pl.BlockSpec((1, tm, width), lambda b, i: (b, i + off, 0))

    mod = pl.BlockSpec((tm // n_ctx, 1, 3 * D), lambda b, i: (b * per_batch + i, 0, 0))
    return grid, row, mod


def _bmajor_mod_table(mod16, nb, rows, n_ctx):
    groups = rows // n_ctx
    tab = jnp.broadcast_to(mod16[:nb, None, :], (nb, groups, 3 * D))
    tab = tab.at[:, 0, :].set(mod16[nb])
    return tab.reshape(nb * groups, 1, 3 * D)


def _tmajor(nb, rows, n_ctx, tt):
    grid = (1, rows // tt)
    nct = n_ctx // tt

    def row(width, off=0):
        return pl.BlockSpec((tt, nb, width), lambda b, i: (i + off, 0, 0))

    mod = pl.BlockSpec((1, nb, 3 * D), lambda b, i: (jnp.where(i < nct, 0, 1), 0, 0))
    return grid, row, mod


def _mod_body(c_ref, w_ref, b_ref, o_ref):
    s = _silu(c_ref[...]).astype(BF16)
    o_ref[...] = jnp.dot(s, w_ref[...].astype(BF16), preferred_element_type=F32) + b_ref[...]


def _modulation(cvec, ada_w, ada_b):
    n = ada_w.shape[1]
    tn = 1024
    return pl.pallas_call(
        _mod_body,
        grid=(n // tn,),
        in_specs=[pl.BlockSpec((16, D), lambda j: (0, 0)),
                  pl.BlockSpec((D, tn), lambda j: (0, j)),
                  pl.BlockSpec((1, tn), lambda j: (0, j))],
        out_specs=pl.BlockSpec((16, tn), lambda j: (0, j)),
        out_shape=jax.ShapeDtypeStruct((16, n), F32),
        compiler_params=_cparams(("parallel",)),
        name="modulation",
    )(cvec, ada_w, ada_b.reshape(1, n))


def _by_mod_group(x3, mod3):
    groups = mod3.shape[0]
    return x3 if groups == 1 else x3.reshape(groups, x3.shape[1] // groups, x3.shape[2])


def _norm_mod(x3, mod3, gn):
    xg = _by_mod_group(x3, mod3)
    ms = jnp.mean(xg * xg, axis=-1, keepdims=True)
    y = xg * lax.rsqrt(ms + EPS) * gn
    return (y * (1.0 + mod3[:, :, D:2 * D]) + mod3[:, :, 0:D]).reshape(x3.shape)


def _in_proj_body(*refs, load, n_x, epilogue, n_extra, chunk, n_side=0):
    xs, (mod_ref, gn_ref, w_ref), rest = refs[:n_x], refs[n_x:n_x + 3], refs[n_x + 3:]
    extra, outs = rest[:n_extra], rest[n_extra:]
    x3 = load(*xs, *outs[:n_side])
    outs = outs[n_side:]
    n3 = _norm_mod(x3, mod_ref[...], gn_ref[...])
    lead = x3.shape[:2]
    n = n3.reshape(lead[0] * lead[1], D).astype(BF16)
    for j in range(w_ref.shape[1] // chunk):
        acc = jnp.dot(n, w_ref[:, j * chunk:(j + 1) * chunk], preferred_element_type=F32)
        epilogue(j, acc, lead, extra, outs)


def _load_block(x_ref):
    return x_ref[...]


def _load_after_out_proj(z_ref, w_ref, h_ref, mod_ref, h_out_ref):
    z3 = z_ref[...]
    a, b, wd = z3.shape
    y = jnp.dot(z3.reshape(a * b, wd), w_ref[...], preferred_element_type=F32).reshape(a, b, D)
    mod3 = mod_ref[...]
    h = (_by_mod_group(h_ref[...], mod3) + mod3[:, :, 2 * D:3 * D] * _by_mod_group(y, mod3)).reshape(a, b, D)
    h_out_ref[...] = h
    return h


def _in_proj(xs, x_specs, load, mod, gn, w, layout, out_shapes, out_specs, epilogue, extras=(),
             extra_specs=(), chunk=512, name="in_proj", prev=None):
    grid, row, mod_spec = layout
    n = w.shape[1]
    n_side, aliases = 0, {}
    if prev is not None:
        z, w_out, mod_prev = prev
        wd = w_out.shape[0]
        (h,) = xs
        xs = (z, w_out, h, mod_prev)
        x_specs = [row(wd), pl.BlockSpec((wd, D), lambda b, i: (0, 0)), row(D), mod_spec]
        load, n_side, aliases = _load_after_out_proj, 1, {2: 0}
        out_shapes = [jax.ShapeDtypeStruct(h.shape, F32), *out_shapes]
        out_specs = [row(D), *out_specs]
    body = functools.partial(_in_proj_body, load=load, n_x=len(xs), epilogue=epilogue,
                             n_extra=len(extras), chunk=chunk, n_side=n_side)
    return pl.pallas_call(
        body,
        grid=grid,
        in_specs=[*x_specs, mod_spec,
                  pl.BlockSpec((1, D), lambda b, i: (0, 0)),
                  pl.BlockSpec((D, n), lambda b, i: (0, 0)),
                  *extra_specs],
        out_specs=out_specs,
        out_shape=out_shapes,
        input_output_aliases=aliases,
        compiler_params=_cparams(("parallel", "parallel")),
        name=name,
    )(*xs, mod, gn.reshape(1, D), w, *extras)


def _rows_out(shape2, row, outs):
    return ([jax.ShapeDtypeStruct(shape2 + (wd,), dt) for wd, dt in outs], [row(wd) for wd, _ in outs])


def _put(ref, lo, val, lead):
    ref[:, :, lo:lo + val.shape[1]] = val.reshape(lead + (val.shape[1],)).astype(ref.dtype)


def _out_proj_body(*refs, load, n_h, store):
    (z_ref, w_ref), hs, (mod_ref, o_ref) = refs[:2], refs[2:2 + n_h], refs[2 + n_h:]
    z3 = z_ref[...]
    a, b, wd = z3.shape
    y = jnp.dot(z3.reshape(a * b, wd), w_ref[...], preferred_element_type=F32).reshape(a, b, D)
    mod3 = mod_ref[...]
    h = _by_mod_group(load(*hs), mod3) + mod3[:, :, 2 * D:3 * D] * _by_mod_group(y, mod3)
    store(o_ref, h.reshape(a, b, D))


def _lru_epilogue(j, acc, lead, extra, outs):
    _put(outs[j], 0, acc, lead)


def _lru_tile_index(s, nct, ntiles, reverse):
    if not reverse:
        return s
    return jnp.where(s < nct, nct - 1 - s, ntiles + nct - 1 - s)


def _lru_sweep_body(up_ref, uc_ref, un_ref, cw_ref, cb_ref, wg_ref, bg_ref, lam_ref, *rest,
                    nct, ntiles, reverse):
    if reverse:
        hf_ref, g_ref, o_ref, ext_ref, cv_ref, a_ref, b_ref, carry_ref = rest
    else:
        o_ref, ext_ref, cv_ref, a_ref, b_ref, carry_ref = rest
    s = pl.program_id(0)
    ti = _lru_tile_index(s, nct, ntiles, reverse)
    tt, nb, pw = uc_ref.shape

    @pl.when(s == 0)
    def _():
        carry_ref[...] = jnp.zeros_like(carry_ref)

    seg_start = (ti == 0) | (ti == nct)
    seg_end = (ti == nct - 1) | (ti == ntiles - 1)
    ext_ref[0:2] = jnp.where(seg_start, 0.0, up_ref[...])
    ext_ref[2:2 + tt] = uc_ref[...]
    ext_ref[2 + tt:3 + tt] = jnp.where(seg_end, 0.0, un_ref[...])

    rate = (-0.5 * LRU_C * LOG2E) * jax.nn.softplus(-lam_ref[...])
    tiles = [slice(k * LANES, (k + 1) * LANES) for k in range(pw // LANES)]
    for sl in tiles:
        u = cb_ref[:, sl]
        for tap in range(CONV_W):
            u = u + cw_ref[tap:tap + 1, sl] * ext_ref[tap:tap + tt, :, sl]
        cv_ref[:, sl] = u.reshape(tt * nb, LANES)
    for k, sl in enumerate(tiles):
        win = slice(_lru_window(k) * LANES, (_lru_window(k) + LRU_WIN) * LANES)
        pre = jnp.dot(cv_ref[:, win].astype(BF16), wg_ref[k], preferred_element_type=F32) + bg_ref[k]
        tr = jnp.tanh(pre[:, :LANES]) + 1.0
        ti = jnp.tanh(pre[:, LANES:]) + 1.0
        a = jnp.exp2(rate[:, sl] * tr)
        a_ref[:, sl] = a
        b_ref[:, sl] = (0.5 * jnp.sqrt(1.0 - a * a)) * (ti * cv_ref[:, sl])

    def step(q, h):
        t = tt - 1 - q if reverse else q
        rows = pl.ds(pl.multiple_of(t * nb, nb), nb)
        h = a_ref[rows, :] * h + b_ref[rows, :]
        if reverse:
            o_ref[t] = ((hf_ref[t] + h) * _silu(g_ref[t].astype(F32))).astype(o_ref.dtype)
        else:
            o_ref[t] = h
        return h

    carry_ref[...] = lax.fori_loop(0, tt, step, carry_ref[...], unroll=4)


def _lru_sweep(u, cw, cb, wg, bg, lam, nct_rows, reverse, hf=None, g=None):
    rows, nb, pw = u.shape
    tt = LRU_TT
    ntiles = rows // tt
    nct = nct_rows // tt
    tile = functools.partial(_lru_tile_index, nct=nct, ntiles=ntiles, reverse=reverse)
    cur = lambda wd: pl.BlockSpec((tt, nb, wd), lambda s: (tile(s), 0, 0))
    const = lambda shape: pl.BlockSpec(shape, lambda s: (0,) * len(shape))
    in_specs = [
        pl.BlockSpec((2, nb, pw), lambda s: (jnp.maximum(tile(s) * (tt // 2) - 1, 0), 0, 0)),
        cur(pw),
        pl.BlockSpec((1, nb, pw), lambda s: (jnp.minimum((tile(s) + 1) * tt, rows - 1), 0, 0)),
        const((CONV_W, pw)), const((1, pw)),
        const(wg.shape), const(bg.shape), const((1, pw)),
    ]
    args = [u, u, u, cw, cb, wg, bg, lam]
    if reverse:
        in_specs += [cur(pw), cur(pw)]
        args += [hf, g]
    body = functools.partial(_lru_sweep_body, nct=nct, ntiles=ntiles, reverse=reverse)
    return pl.pallas_call(
        body,
        grid=(ntiles,),
        in_specs=in_specs,
        out_specs=cur(pw),
        out_shape=jax.ShapeDtypeStruct(u.shape, BF16 if reverse else F32),
        scratch_shapes=[pltpu.VMEM((tt + 3, nb, pw), F32),
                        pltpu.VMEM((tt * nb, pw), F32),
                        pltpu.VMEM((tt * nb, pw), F32),
                        pltpu.VMEM((tt * nb, pw), F32),
                        pltpu.VMEM((nb, pw), F32)],
        compiler_params=_cparams(("arbitrary",)),
        name="lru_bwd" if reverse else "lru_fwd",
    )(*args)


def _lru_window(k):
    first_block = (k * LANES) // LRU_BW
    return min((first_block * LRU_BW) // LANES, LRU_WIDTH // LANES - LRU_WIN)


def _lru_gate_weights(wa, wx, ba, bx):
    eye = 0.5 * jnp.eye(LRU_BLOCKS, dtype=BF16)

    def dense(w):
        return (w.astype(BF16)[:, :, None, :] * eye[:, None, :, None]).reshape(LRU_WIDTH, LRU_WIDTH)

    da, dx = dense(wa), dense(wx)
    wg, bg = [], []
    for k in range(LRU_WIDTH // LANES):
        rows = slice(_lru_window(k) * LANES, (_lru_window(k) + LRU_WIN) * LANES)
        cols = slice(k * LANES, (k + 1) * LANES)
        wg.append(jnp.concatenate([da[rows, cols], dx[rows, cols]], axis=1))
        bg.append(0.5 * jnp.concatenate([ba[cols], bx[cols]])[None])
    return jnp.stack(wg).astype(BF16), jnp.stack(bg)


def _load_time_major(ctx_ref, x_ref, *, nct):
    blk = jnp.where(pl.program_id(1) < nct, ctx_ref[...], x_ref[...])
    return pltpu.einshape("btd->tbd", blk)


def _store_batch_major(o_ref, val):
    o_ref[...] = pltpu.einshape("tbd->btd", val)


def _rglru_layer(ctx, x, mod_t, gn, w_in, conv_w, conv_b, wa, ba, wx, bx, lam, w_out):
    nb, n_ctx, _ = ctx.shape
    rows = n_ctx + x.shape[1]
    tt = LRU_TT
    nct = n_ctx // tt
    layout = _tmajor(nb, rows, n_ctx, tt)
    grid, row, mod_spec = layout
    src_specs = [pl.BlockSpec((nb, tt, D), lambda b, i: (0, jnp.minimum(i, nct - 1), 0)),
                 pl.BlockSpec((nb, tt, D), lambda b, i: (0, jnp.maximum(i - nct, 0), 0))]
    load = functools.partial(_load_time_major, nct=nct)
    wd = LRU_WIDTH
    out_shapes, out_specs = _rows_out((rows, nb), row, [(wd, F32), (wd, BF16)])
    u, g = _in_proj((ctx, x), src_specs, load, mod_t, gn, w_in.astype(BF16), layout, out_shapes, out_specs,
                    _lru_epilogue, chunk=wd, name="lru_in_proj")
    cb = conv_b.reshape(1, wd)
    sweeps = []
    for d in range(2):
        wg, bg = _lru_gate_weights(wa[d], wx[d], ba[d], bx[d])
        sweeps.append((conv_w, cb, wg, bg, lam[d:d + 1]))
    hf = _lru_sweep(u, *sweeps[0], n_ctx, reverse=False)
    z = _lru_sweep(u, *sweeps[1], n_ctx, reverse=True, hf=hf, g=g)
    body = functools.partial(_out_proj_body, load=load, n_h=2, store=_store_batch_major)
    return pl.pallas_call(
        body,
        grid=grid,
        in_specs=[row(wd), pl.BlockSpec((wd, D), lambda b, i: (0, 0)), *src_specs, mod_spec],
        out_specs=pl.BlockSpec((nb, tt, D), lambda b, i: (0, i, 0)),
        out_shape=jax.ShapeDtypeStruct((nb, rows, D), F32),
        compiler_params=_cparams(("parallel", "parallel")),
        name="lru_out_proj",
    )(z, w_out.astype(BF16), ctx, x, mod_t)


def _rope_tables(n_ctx, t_len):
    pos = jnp.arange(t_len)
    row = (pos // GRID_W).astype(F32)
    col = (pos % GRID_W).astype(F32)
    n_ax = HEAD_DIM // 4
    freqs = ROPE_BASE ** (-jnp.arange(n_ax, dtype=F32) / n_ax)
    ang = jnp.concatenate([row[:, None] * freqs, col[:, None] * freqs], axis=-1)
    cos, sin = jnp.cos(ang), jnp.sin(ang)
    cos_h = jnp.concatenate([cos, cos], axis=-1)
    sin_h = jnp.concatenate([-sin, sin], axis=-1)
    cos_f = jnp.concatenate([jnp.ones((n_ctx, HEAD_DIM), F32), cos_h], axis=0)
    sin_f = jnp.concatenate([jnp.zeros((n_ctx, HEAD_DIM), F32), sin_h], axis=0)
    return jnp.tile(cos_f, (1, 2)), jnp.tile(sin_f, (1, 2))


def _rope(x, cos, sin):
    lane = lax.broadcasted_iota(jnp.int32, x.shape, 1)
    swapped = jnp.where(lane % HEAD_DIM < HEAD_DIM // 2,
                        pltpu.roll(x, LANES - HEAD_DIM // 2, 1), pltpu.roll(x, HEAD_DIM // 2, 1))
    return x * cos + swapped * sin


def _swa_epilogue(j, acc, lead, extra, outs):
    cos_ref, sin_ref = extra
    q_ref, qr_ref, k_ref, v_ref, g_ref = outs
    qd = SWA_HEADS * HEAD_DIM
    kvd = SWA_KV_HEADS * HEAD_DIM
    grp = SWA_HEADS // SWA_KV_HEADS
    width = acc.shape[1]
    lo = j * width
    cos, sin = cos_ref[...], sin_ref[...]

    def roped(x):
        return jnp.concatenate([_rope(x[:, s:s + LANES], cos, sin) for s in range(0, x.shape[1], LANES)], axis=1)

    def head(x, hh):
        return x[:, hh * HEAD_DIM:(hh + 1) * HEAD_DIM]

    if lo < qd:
        q = acc * QK_SCALE
        qr = roped(q)
        for hh in range(width // HEAD_DIM):
            kv, gq = divmod(lo // HEAD_DIM + hh, grp)
            for rt in range(lead[1] // BLOCK_Q):
                rows = slice(rt * BLOCK_Q, (rt + 1) * BLOCK_Q)
                dst = slice(gq * BLOCK_Q, (gq + 1) * BLOCK_Q)
                q_ref[0, kv, rt, dst, :] = head(q, hh)[rows].astype(q_ref.dtype)
                qr_ref[0, kv, rt, dst, :] = head(qr, hh)[rows].astype(qr_ref.dtype)
    elif lo == qd:
        kr = roped(acc[:, :kvd])
        for kv in range(SWA_KV_HEADS):
            k_ref[0, kv] = head(kr, kv).astype(k_ref.dtype)
            v_ref[0, kv] = head(acc[:, kvd:], kv).T.astype(v_ref.dtype)
    else:
        _put(g_ref, lo - qd - 2 * kvd, acc, lead)


def _swa_mask_table():
    grp = SWA_HEADS // SWA_KV_HEADS
    kj = np.arange(3 * BLOCK_Q)[:, None]
    qi = np.arange(BLOCK_Q)[None, :]
    base = np.where(np.abs(kj - BLOCK_Q - qi) <= WINDOW, 0.0, NEG_INF).astype(np.float32)
    first = np.where(kj < BLOCK_Q, NEG_INF, base).astype(np.float32)
    last = np.where(kj >= 2 * BLOCK_Q, NEG_INF, base).astype(np.float32)
    tab = np.stack([np.full_like(base, NEG_INF), first, base, last])
    return jnp.asarray(np.tile(tab, (1, 1, grp)))


def _swa_body(sink_ref, *refs):
    nq, nband = SWA_QTILES, SWA_QTILES + 2
    mask_refs, refs = refs[:nq], refs[nq:]
    (q_ref, qr_ref, kc_ref), refs = refs[:3], refs[3:]
    kb_refs, refs = refs[:nband], refs[nband:]
    vc_ref, refs = refs[0], refs[1:]
    vb_refs, (g_ref, o_ref, s_ref, p_ref) = refs[:nband], refs[nband:]
    grp = SWA_HEADS // SWA_KV_HEADS
    dims = (((1,), (1,)), ((), ()))
    nc = kc_ref.shape[2]
    units = [(t, kv) for t in range(nq) for kv in range(SWA_KV_HEADS)]
    sink_p = {}

    def scores(u):
        t, kv = units[u]
        slot = u % SWA_SLOTS
        kb = jnp.concatenate([kb_refs[t + i][0, kv] for i in range(3)], axis=0)
        s_ref[slot, :nc, :] = lax.dot_general(kc_ref[0, kv], q_ref[0, kv, t], dims, preferred_element_type=F32)
        s_ref[slot, nc:, :] = (lax.dot_general(kb, qr_ref[0, kv, t], dims, preferred_element_type=F32)
                               + mask_refs[t][0])

    def softmax(u):
        _, kv = units[u]
        slot = u % SWA_SLOTS
        sink_p[u] = []
        for gq in range(grp):
            lanes = slice(gq * BLOCK_Q, (gq + 1) * BLOCK_Q)
            s = s_ref[slot, :, lanes]
            sink = sink_ref[kv, :, lanes]
            m = jnp.maximum(jnp.max(s, axis=0, keepdims=True), sink)
            sink_p[u].append(jnp.exp2(sink - m))
            p_ref[slot, :, lanes] = jnp.exp2(s - m).astype(BF16)

    def values(u):
        t, kv = units[u]
        vt = jnp.concatenate([vc_ref[0, kv]] + [vb_refs[t + i][0, kv] for i in range(3)], axis=1)
        vt = jnp.concatenate([vt, jnp.ones((ONES_ROWS, vt.shape[1]), BF16)], axis=0)
        ot = jnp.dot(vt, p_ref[u % SWA_SLOTS], preferred_element_type=F32)
        denom = ot[HEAD_DIM:HEAD_DIM + 1] + jnp.concatenate(sink_p[u], axis=1)
        ot = ot[:HEAD_DIM] * (1.0 / denom)
        slab = jnp.concatenate([ot[:, gq * BLOCK_Q:(gq + 1) * BLOCK_Q].T for gq in range(grp)], axis=1)
        rows = slice(t * BLOCK_Q, (t + 1) * BLOCK_Q)
        lanes = slice(kv * grp * HEAD_DIM, (kv + 1) * grp * HEAD_DIM)
        o_ref[0, rows, lanes] = (slab * _silu(g_ref[0, rows, lanes].astype(F32))).astype(o_ref.dtype)

    scores(0)
    scores(1)
    for u in range(len(units)):
        softmax(u)
        if u + 2 < len(units):
            scores(u + 2)
        values(u)


def _swa_attention(q, qr, k, vt, g, sink, n_ctx):
    nb, nkv, nt, qrows, hd = q.shape
    rows = k.shape[2]
    grp = SWA_HEADS // SWA_KV_HEADS
    nct = n_ctx // BLOCK_Q
    nkeys = n_ctx + 3 * BLOCK_Q
    nq = SWA_QTILES
    assert nt % nq == 0 and nct % nq == 0
    qspec = pl.BlockSpec((1, nkv, nq, qrows, hd), lambda b, j: (b, 0, j, 0, 0))
    kctx = pl.BlockSpec((1, nkv, n_ctx, hd), lambda b, j: (b, 0, 0, 0))
    vctx = pl.BlockSpec((1, nkv, hd, n_ctx), lambda b, j: (b, 0, 0, 0))
    rowspec = pl.BlockSpec((1, nq * BLOCK_Q, g.shape[2]), lambda b, j: (b, j, 0))

    def tile(j, off):
        return jnp.clip(j * nq - 1 + off, nct, nt - 1)

    def kband(off):
        return pl.BlockSpec((1, nkv, BLOCK_Q, hd), lambda b, j: (b, 0, tile(j, off), 0))

    def vband(off):
        return pl.BlockSpec((1, nkv, hd, BLOCK_Q), lambda b, j: (b, 0, 0, tile(j, off)))

    def mask(t):
        def variant(b, j):
            n = j * nq + t
            return (jnp.where(n < nct, 0, jnp.where(n == nct, 1, jnp.where(n == nt - 1, 3, 2))), 0, 0)
        return pl.BlockSpec((1, 3 * BLOCK_Q, qrows), variant)

    sink_rows = jnp.repeat(sink.astype(F32).reshape(nkv, grp) * LOG2E, BLOCK_Q, axis=1).reshape(nkv, 1, qrows)
    masks = _swa_mask_table()
    return pl.pallas_call(
        _swa_body,
        grid=(nb, nt // nq),
        in_specs=[pl.BlockSpec((nkv, 1, qrows), lambda b, j: (0, 0, 0)),
                  *[mask(t) for t in range(nq)], qspec, qspec,
                  kctx, *[kband(i) for i in range(nq + 2)], vctx, *[vband(i) for i in range(nq + 2)], rowspec],
        out_specs=rowspec,
        out_shape=jax.ShapeDtypeStruct((nb, rows, g.shape[2]), BF16),
        scratch_shapes=[pltpu.VMEM((SWA_SLOTS, nkeys, qrows), F32), pltpu.VMEM((SWA_SLOTS, nkeys, qrows), BF16)],
        compiler_params=_cparams(("parallel", "parallel")),
        name="swa_attention",
    )(sink_rows, *[masks] * nq, q, qr, k, *[k] * (nq + 2), vt, *[vt] * (nq + 2), g)


def _swa_layer(h, mod_b, gn, w_in, sink, n_ctx):
    nb, rows, _ = h.shape
    tm = ROW_TILE
    layout = _bmajor(nb, rows, n_ctx, tm)
    _, row, _ = layout
    qd = SWA_HEADS * HEAD_DIM
    grp = SWA_HEADS // SWA_KV_HEADS
    nkv = SWA_KV_HEADS
    cos, sin = _rope_tables(n_ctx, rows - n_ctx)
    tab = pl.BlockSpec((tm, LANES), lambda b, i: (i, 0))
    q_shape = jax.ShapeDtypeStruct((nb, nkv, rows // BLOCK_Q, grp * BLOCK_Q, HEAD_DIM), BF16)
    q_spec = pl.BlockSpec((1, nkv, tm // BLOCK_Q, grp * BLOCK_Q, HEAD_DIM), lambda b, i: (b, 0, i, 0, 0))
    k_shape = jax.ShapeDtypeStruct((nb, nkv, rows, HEAD_DIM), BF16)
    k_spec = pl.BlockSpec((1, nkv, tm, HEAD_DIM), lambda b, i: (b, 0, i, 0))
    vt_shape = jax.ShapeDtypeStruct((nb, nkv, HEAD_DIM, rows), BF16)
    vt_spec = pl.BlockSpec((1, nkv, HEAD_DIM, tm), lambda b, i: (b, 0, 0, i))
    q, qr, k, vt, g = _in_proj(
        (h,), [row(D)], _load_block, mod_b, gn, w_in.astype(BF16), layout,
        [q_shape, q_shape, k_shape, vt_shape, jax.ShapeDtypeStruct((nb, rows, qd), BF16)],
        [q_spec, q_spec, k_spec, vt_spec, row(qd)], _swa_epilogue,
        extras=(cos, sin), extra_specs=(tab, tab), name="swa_in_proj")
    return _swa_attention(q, qr, k, vt, g, sink, n_ctx)


def _na_epilogue(j, acc, lead, extra, outs):
    q_ref, k_ref, v_ref, g_ref = outs
    wd = NA_HEADS * HEAD_DIM
    width = acc.shape[1]
    lo = j * width
    which, off = lo // wd, lo % wd
    if which == 3:
        _put(g_ref, off, acc, lead)
        return
    ref = (q_ref, k_ref, v_ref)[which]
    val = acc * QK_SCALE if which == 0 else acc
    for hh in range(width // HEAD_DIM):
        ref[0, off // HEAD_DIM + hh] = val[:, hh * HEAD_DIM:(hh + 1) * HEAD_DIM].astype(ref.dtype)


def _na_bias_tiles(rpb):
    reach = GRID_W - NA_COLS
    period = 2 * GRID_W
    ndy = 2 * NA_ROWS - 1
    edge = jnp.pad(rpb.astype(F32) * LOG2E, ((0, 0), (0, 0), (reach, reach)), mode="edge")
    flat = jnp.tile(jnp.pad(edge[:, :, ::-1], ((0, 0), (0, 0), (0, 1))), (1, 1, GRID_W))[:, :, :GRID_W * (period - 1)]
    e = flat.reshape(NA_HEADS, ndy, GRID_W, period - 1)[:, :, :, GRID_W - 1:]
    col = np.arange(GRID_W)
    cstart = np.clip(col - NA_COLS // 2, 0, GRID_W - NA_COLS)
    col_ok = (col[:, None] >= cstart[None, :]) & (col[:, None] < cstart[None, :] + NA_COLS)
    e = jnp.where(jnp.asarray(col_ok)[None, None], e, NEG_INF)
    e = jnp.pad(e, ((0, 0), (0, 1), (0, 0), (0, 0)), constant_values=NEG_INF)
    shape = (NA_HEADS // NA_HSTEP, NA_HSTEP, ndy + 1, GRID_W, period)
    left = jnp.pad(e, ((0, 0), (0, 0), (0, 0), (0, GRID_W)), constant_values=NEG_INF).reshape(shape)
    right = jnp.pad(e, ((0, 0), (0, 0), (0, 0), (GRID_W, 0)), constant_values=NEG_INF).reshape(shape)
    return left, right


def _na_body(q_ref, k_ref, v_ref, g_ref, bl_ref, br_ref, o_ref, s_ref, p_ref, *, n_ctx, grid_rows):
    rb = pl.program_id(2)
    r0 = (rb - 1) * NA_QROWS
    kb = jnp.clip(r0 - NA_ROWS // 2, 0, grid_rows - NA_KROWS)
    start = pl.multiple_of(n_ctx + kb * GRID_W, GRID_W)
    nk = NA_KROWS * GRID_W
    nq = q_ref.shape[2]
    a_bt = (((1,), (1,)), ((), ()))
    at_b = (((0,), (0,)), ((), ()))

    def tile_index(qr, kr):
        qrow, krow = r0 + qr, kb + kr
        first = jnp.clip(qrow - NA_ROWS // 2, 0, grid_rows - NA_ROWS)
        inside = (rb >= 1) & (krow >= first) & (krow < first + NA_ROWS)
        return jnp.where(inside, krow - qrow + NA_ROWS - 1, NA_MASKED)

    idx = [[tile_index(qr, kr) for kr in range(NA_KROWS)] for qr in range(NA_QROWS)]
    left_half = lax.broadcasted_iota(jnp.int32, (GRID_W, 2 * GRID_W), 1) < GRID_W
    ones = jnp.ones((n_ctx + nk, HEAD_DIM), BF16)
    outs = [None] * NA_HSTEP

    def scores(hl):
        slot = hl % NA_SLOTS
        qh = q_ref[0, hl]
        bias = jnp.concatenate(
            [jnp.concatenate([jnp.where(left_half, bl_ref[0, hl, idx[2 * m][kr]], br_ref[0, hl, idx[2 * m + 1][kr]])
                              for m in range(NA_QROWS // 2)], axis=1) for kr in range(NA_KROWS)], axis=0)
        s_ref[slot, :n_ctx, :] = lax.dot_general(k_ref[0, hl, 0:n_ctx, :], qh, a_bt, preferred_element_type=F32)
        s_ref[slot, n_ctx:, :] = (lax.dot_general(k_ref[0, hl, pl.ds(start, nk), :], qh, a_bt,
                                                  preferred_element_type=F32) + bias)

    def softmax(hl):
        slot = hl % NA_SLOTS
        for c in range(nq // LANES):
            lanes = slice(c * LANES, (c + 1) * LANES)
            s = s_ref[slot, :, lanes]
            p_ref[slot, :, lanes] = jnp.exp2(s - jnp.max(s, axis=0, keepdims=True)).astype(BF16)

    def values(hl):
        v1 = jnp.concatenate(
            [jnp.concatenate([v_ref[0, hl, 0:n_ctx, :], v_ref[0, hl, pl.ds(start, nk), :]], axis=0), ones], axis=1)
        ot = lax.dot_general(v1, p_ref[hl % NA_SLOTS], at_b, preferred_element_type=F32)
        outs[hl] = (ot[:HEAD_DIM] * (1.0 / ot[HEAD_DIM:HEAD_DIM + 1])).T

    scores(0)
    scores(1)
    for hl in range(NA_HSTEP):
        softmax(hl)
        if hl + 2 < NA_HSTEP:
            scores(hl + 2)
        values(hl)
    slab = jnp.concatenate(outs, axis=1)
    o_ref[0] = (slab * _silu(g_ref[0].astype(F32))).astype(o_ref.dtype)


def _na_attention(q, k, v, g, bias_left, bias_right, n_ctx):
    nb, nh, rows, hd = q.shape
    grid_rows = (rows - n_ctx) // GRID_W
    nq = NA_QROWS * GRID_W
    assert n_ctx == nq, "context tokens form exactly one query block"
    nkeys = n_ctx + NA_KROWS * GRID_W
    lanes = NA_HSTEP * HEAD_DIM
    qspec = pl.BlockSpec((1, NA_HSTEP, nq, hd), lambda p, b, r: (b, p, r, 0))
    kvspec = pl.BlockSpec((1, NA_HSTEP, rows, hd), lambda p, b, r: (b, p, 0, 0))
    rowspec = pl.BlockSpec((1, nq, lanes), lambda p, b, r: (b, r, p))
    bspec = pl.BlockSpec((1,) + bias_left.shape[1:], lambda p, b, r: (p, 0, 0, 0, 0))
    body = functools.partial(_na_body, n_ctx=n_ctx, grid_rows=grid_rows)
    return pl.pallas_call(
        body,
        grid=(nh // NA_HSTEP, nb, rows // nq),
        in_specs=[qspec, kvspec, kvspec, rowspec, bspec, bspec],
        out_specs=rowspec,
        out_shape=jax.ShapeDtypeStruct(g.shape, BF16),
        scratch_shapes=[pltpu.VMEM((NA_SLOTS, nkeys, nq), F32), pltpu.VMEM((NA_SLOTS, nkeys, nq), BF16)],
        compiler_params=_cparams(("parallel", "parallel", "parallel")),
        name="na_attention",
    )(q, k, v, g, bias_left, bias_right)


def _na_layer(h, prev, mod_b, gn, w_in, rpb, n_ctx):
    nb, rows, _ = h.shape
    tm = ROW_TILE
    layout = _bmajor(nb, rows, n_ctx, tm)
    _, row, _ = layout
    wd = NA_HEADS * HEAD_DIM
    head_shape = jax.ShapeDtypeStruct((nb, NA_HEADS, rows, HEAD_DIM), BF16)
    head_spec = pl.BlockSpec((1, NA_HEADS, tm, HEAD_DIM), lambda b, i: (b, 0, i, 0))
    h, q, k, v, g = _in_proj((h,), [row(D)], _load_block, mod_b, gn, w_in.astype(BF16), layout,
                             [head_shape] * 3 + [jax.ShapeDtypeStruct((nb, rows, wd), BF16)],
                             [head_spec] * 3 + [row(wd)], _na_epilogue, name="swa_out_na_in_proj", prev=prev)
    bias_left, bias_right = _na_bias_tiles(rpb)
    return h, _na_attention(q, k, v, g, bias_left, bias_right, n_ctx)


def _s5_epilogue(j, acc, lead, extra, outs):
    u_ref, g_ref = outs
    lo = j * acc.shape[1]
    if lo < S5_WIDTH:
        _put(u_ref, lo, acc, lead)
    else:
        _put(g_ref, lo - S5_WIDTH, acc, lead)


def _s5_matrices(a_re, a_im, log_dt, b_re, b_im, c_re, c_im, d_skip):
    L = S5_CHUNK
    lam = lax.complex(a_re.astype(F32), a_im.astype(F32))
    lam_dt = lam * jnp.exp(log_dt.astype(F32))[..., None]
    lam_bar = jnp.exp(lam_dt)
    b_bar = ((lam_bar - 1.0) / lam)[..., None] * lax.complex(b_re.astype(F32), b_im.astype(F32))
    c_mat = lax.complex(c_re.astype(F32), c_im.astype(F32))
    tau = np.arange(L)

    def power(expo, d):
        return jnp.exp(jnp.asarray(expo, F32).reshape(expo.shape + (1, 1)) * lam_dt[d])

    w = jnp.stack([c_mat[d][None] * power(tau, d)[:, :, None, :] for d in range(2)])
    w2 = jnp.concatenate([jnp.real(w), -jnp.imag(w)], axis=-1)
    b2 = jnp.concatenate([jnp.real(b_bar), jnp.imag(b_bar)], axis=-2)
    kern = jnp.einsum('dkgip,dgpj->dkgij', w2, b2).reshape(2 * L, S5_GROUPS, S5_GROUP, S5_GROUP)
    lag = tau[None, :] - tau[:, None]
    sel = np.concatenate([lag[None] == tau[:, None, None], -lag[None] == tau[:, None, None]]).astype(np.float32)
    both = jnp.einsum('ktu,kgij->tugij', jnp.asarray(sel), kern)
    eye_t = jnp.eye(L, dtype=F32)[:, :, None, None, None]
    skip = jnp.eye(S5_GROUP, dtype=F32)[None] * d_skip.astype(F32).reshape(S5_GROUPS, S5_GROUP, 1)
    both = both + eye_t * skip[None, None]
    toep = both.transpose(2, 0, 4, 1, 3).reshape(S5_GROUPS, L * S5_GROUP, L * S5_GROUP)

    def lanes(z):
        pad = [(0, 0)] * (z.ndim - 1) + [(0, LANES - S5_STATE)]
        return jnp.concatenate([jnp.pad(jnp.real(z), pad), jnp.pad(jnp.imag(z), pad)], axis=-1)

    pf = power(L - 1 - tau, 0)[:, :, :, None] * b_bar[0][None]
    pb = power(tau, 1)[:, :, :, None] * b_bar[1][None]
    pmat = jnp.concatenate([lanes(pf.transpose(1, 0, 3, 2)), lanes(pb.transpose(1, 0, 3, 2))], axis=-1)
    pmat = pmat.reshape(S5_GROUPS, L * S5_GROUP, 4 * LANES)
    qf = c_mat[0][None] * power(tau + 1, 0)[:, :, None, :]
    qb = c_mat[1][None] * power(L - tau, 1)[:, :, None, :]

    def state_rows(z):
        pad = ((0, 0), (0, LANES - S5_STATE), (0, 0), (0, 0))
        zr = jnp.pad(jnp.real(z).transpose(1, 3, 0, 2), pad)
        zi = jnp.pad(-jnp.imag(z).transpose(1, 3, 0, 2), pad)
        return jnp.concatenate([zr, zi], axis=1).reshape(S5_GROUPS, 2 * LANES, L * S5_GROUP)

    qmat = jnp.concatenate([state_rows(qf), state_rows(qb)], axis=1)
    lam_l = jnp.exp(float(L) * lam_dt)
    pad = ((0, 0), (0, LANES - S5_STATE))
    coef = jnp.stack([jnp.pad(jnp.real(lam_l[0]), pad), jnp.pad(jnp.imag(lam_l[0]), pad),
                      jnp.pad(jnp.real(lam_l[1]), pad), jnp.pad(jnp.imag(lam_l[1]), pad)], axis=1)
    coef = jnp.pad(coef, ((0, 0), (0, 4), (0, 0)))
    return toep.astype(BF16), pmat.astype(BF16), qmat.astype(BF16), coef


def _atom_transpose(xs, atom):
    n = len(xs)
    group = lax.broadcasted_iota(jnp.int32, xs[0].shape, 1) // atom
    xs = list(xs)
    d = n // 2
    while d:
        keep = (group & d) == 0
        for i in range(n):
            if not i & d:
                lo, hi = xs[i], xs[i + d]
                xs[i] = jnp.where(keep, lo, pltpu.roll(hi, d * atom, 1))
                xs[i + d] = jnp.where(keep, pltpu.roll(lo, LANES - d * atom, 1), hi)
        d //= 2
    return xs


def _s5_body(u_ref, toep_ref, p_ref, q_ref, coef_ref, y_ref, u_scr, y_scr, s_ref, hin_ref, *, nb, nc_ctx, nc):
    nwin = LANES // S5_GROUP
    nrow, nlat = u_ref.shape[1], y_ref.shape[1]
    blk = S5_ROWS
    for half in range(S5_CHUNK // nwin):
        cols = slice(half * LANES, (half + 1) * LANES)
        for r in range(0, nrow, blk):
            words = [pltpu.bitcast(u_ref[half * nwin + w, r:r + blk, :], jnp.uint32) for w in range(nwin)]
            for gl, x in enumerate(_atom_transpose(words, S5_GROUP)):
                u_scr[gl, r:r + blk, cols] = pltpu.bitcast(x, BF16)
    for gl in range(S5_SLAB):
        _s5_group(gl, u_scr, toep_ref, p_ref, q_ref, coef_ref, y_scr, s_ref, hin_ref, nb, nc_ctx, nc)
    for half in range(S5_CHUNK // nwin):
        cols = slice(half * LANES, (half + 1) * LANES)
        for r in range(0, nlat, blk):
            ys = [y_scr[gl, r:r + blk, cols] for gl in range(S5_SLAB)]
            for w, x in enumerate(_atom_transpose(ys, S5_GROUP)):
                y_ref[half * nwin + w, r:r + blk, :] = x


def _s5_group(gl, u_scr, toep_ref, p_ref, q_ref, coef_ref, y_scr, s_ref, hin_ref, nb, nc_ctx, nc):
    u = u_scr[gl]
    s_ref[...] = jnp.dot(u, p_ref[gl], preferred_element_type=F32)
    coef = coef_ref[gl]
    shape = (nb, LANES)
    lr_f, li_f = jnp.broadcast_to(coef[0:1], shape), jnp.broadcast_to(coef[1:2], shape)
    lr_b, li_b = jnp.broadcast_to(coef[2:3], shape), jnp.broadcast_to(coef[3:4], shape)
    zero = jnp.zeros(shape, F32)

    def step(q, carry):
        fr, fi, br, bi = carry
        rows_f = pl.ds(pl.multiple_of(q * nb, nb), nb)
        cb = jnp.where(q < nc_ctx, nc_ctx - 1 - q, nc + nc_ctx - 1 - q)
        rows_b = pl.ds(pl.multiple_of(cb * nb, nb), nb)
        hin_ref[rows_f, 0:LANES] = fr
        hin_ref[rows_f, LANES:2 * LANES] = fi
        hin_ref[rows_b, 2 * LANES:3 * LANES] = br
        hin_ref[rows_b, 3 * LANES:4 * LANES] = bi
        sfr, sfi = s_ref[rows_f, 0:LANES], s_ref[rows_f, LANES:2 * LANES]
        sbr, sbi = s_ref[rows_b, 2 * LANES:3 * LANES], s_ref[rows_b, 3 * LANES:4 * LANES]
        return (lr_f * fr - li_f * fi + sfr, lr_f * fi + li_f * fr + sfi,
                lr_b * br - li_b * bi + sbr, lr_b * bi + li_b * br + sbi)

    lax.fori_loop(0, nc, step, (zero, zero, zero, zero), unroll=True)

    lat = slice(nc_ctx * nb, nc * nb)
    y_scr[gl] = (jnp.dot(u[lat], toep_ref[gl], preferred_element_type=F32)
                 + jnp.dot(hin_ref[lat, :].astype(BF16), q_ref[gl], preferred_element_type=F32))


def _s5_scan(u_t, toep, pmat, qmat, coef, nb, n_ctx):
    L, nrow, wd = u_t.shape
    pk = L * S5_GROUP
    nc = nrow // nb
    nc_ctx = n_ctx // L
    nlat = (nc - nc_ctx) * nb
    assert nrow % S5_ROWS == 0 and nlat % S5_ROWS == 0
    once = pl.Buffered(1)
    slab = lambda shape: pl.BlockSpec((S5_SLAB,) + shape, lambda s: (s, 0, 0), pipeline_mode=once)
    body = functools.partial(_s5_body, nb=nb, nc_ctx=nc_ctx, nc=nc)
    return pl.pallas_call(
        body,
        grid=(wd // LANES,),
        in_specs=[pl.BlockSpec((L, nrow, LANES), lambda s: (0, 0, s)),
                  slab((pk, pk)), slab((pk, 4 * LANES)), slab((4 * LANES, pk)), slab((8, LANES))],
        out_specs=pl.BlockSpec((L, nlat, LANES), lambda s: (0, 0, s)),
        out_shape=jax.ShapeDtypeStruct((L, nlat, wd), F32),
        scratch_shapes=[pltpu.VMEM((S5_SLAB, nrow, pk), BF16), pltpu.VMEM((S5_SLAB, nlat, pk), F32),
                        pltpu.VMEM((nrow, 4 * LANES), F32), pltpu.VMEM((nrow, 4 * LANES), F32)],
        compiler_params=_cparams(("parallel",)),
        name="s5_scan",
    )(u_t, toep, pmat, qmat, coef)


def _rms(x, g):
    return x * lax.rsqrt(jnp.mean(x * x, axis=-1, keepdims=True) + EPS) * g


def _s5_readout_body(y_ref, ga_ref, gb_ref, ha_ref, hb_ref, gw_ref, gbias_ref, w_ref, mod_ref, nf_ref, o_ref):
    sub = ga_ref.shape[1]
    for i, (g_ref, h_ref) in enumerate(((ga_ref, ha_ref), (gb_ref, hb_ref))):
        rows = slice(i * sub, (i + 1) * sub)
        y = jax.nn.gelu(y_ref[0, rows, :])
        t = y * _sigmoid(jnp.dot(y.astype(BF16), gw_ref[...], preferred_element_type=F32) + gbias_ref[...])
        z = (t * _silu(g_ref[0].astype(F32))).astype(BF16)
        out = jnp.dot(z, w_ref[...], preferred_element_type=F32)
        h = h_ref[0] + mod_ref[0, :, 2 * D:3 * D] * out
        o_ref[0, rows, :] = _rms(h, nf_ref[...])


def _s5_readout(y, g, glu_w, glu_b, w_out, h, mod_b, norm_f, n_ctx):
    nb, t_len, _ = y.shape
    sub = n_ctx
    tm = 2 * sub
    lat = lambda wd: pl.BlockSpec((1, tm, wd), lambda b, i: (b, i, 0))
    part = lambda wd, j: pl.BlockSpec((1, sub, wd), lambda b, i: (b, 2 * i + 1 + j, 0))
    const = lambda shape: pl.BlockSpec(shape, lambda b, i: (0, 0))
    return pl.pallas_call(
        _s5_readout_body,
        grid=(nb, t_len // tm),
        in_specs=[lat(S5_WIDTH), part(S5_WIDTH, 0), part(S5_WIDTH, 1), part(D, 0), part(D, 1),
                  const((S5_WIDTH, S5_WIDTH)), const((1, S5_WIDTH)), const((S5_WIDTH, D)),
                  pl.BlockSpec((1, 1, 3 * D), lambda b, i: (b, 0, 0)), const((1, D))],
        out_specs=lat(D),
        out_shape=jax.ShapeDtypeStruct((nb, t_len, D), F32),
        compiler_params=_cparams(("parallel", "parallel")),
        name="s5_readout",
    )(y, g, g, h, h, glu_w.astype(BF16), glu_b.reshape(1, S5_WIDTH), w_out.astype(BF16), mod_b,
      norm_f.reshape(1, D))


def _s5_layer(h, prev, mod_b, mod_lat, gn, w_in, a_re, a_im, log_dt, b_re, b_im, c_re, c_im, d_skip, glu_w, glu_b,
              w_out, norm_f, n_ctx):
    nb, rows, _ = h.shape
    L = S5_CHUNK
    layout = _bmajor(nb, rows, n_ctx, ROW_TILE)
    _, row, _ = layout
    out_shapes, out_specs = _rows_out((nb, rows), row, [(S5_WIDTH, BF16)] * 2)
    h, u, g = _in_proj((h,), [row(D)], _load_block, mod_b, gn, w_in.astype(BF16), layout, out_shapes, out_specs,
                       _s5_epilogue, name="na_out_s5_in_proj", prev=prev)
    nc = rows // L
    u_t = u.reshape(nb, nc, L, S5_WIDTH).transpose(2, 1, 0, 3).reshape(L, nc * nb, S5_WIDTH)
    toep, pmat, qmat, coef = _s5_matrices(a_re, a_im, log_dt, b_re, b_im, c_re, c_im, d_skip)
    y_t = _s5_scan(u_t, toep, pmat, qmat, coef, nb, n_ctx)
    t_len = rows - n_ctx
    y = y_t.reshape(L, t_len // L, nb, S5_WIDTH).transpose(2, 1, 0, 3).reshape(nb, t_len, S5_WIDTH)
    return _s5_readout(y, g, glu_w, glu_b, w_out, h, mod_lat, norm_f, n_ctx)


def kernel(x, c, ctx, c_ctx, ada_w0, ada_b0, norm0, w_in0, conv_w0, conv_b0, lru_wa0, lru_ba0, lru_wx0, lru_bx0, lru_lam0, w_out0, ada_w1, ada_b1, norm1, w_in1, sink1, w_out1, ada_w2, ada_b2, norm2, w_in2, rpb2, w_out2, ada_w3, ada_b3, norm3, w_in3, s5_a_re3, s5_a_im3, s5_log_dt3, s5_b_re3, s5_b_im3, s5_c_re3, s5_c_im3, s5_d3, glu_w3, glu_b3, w_out3, norm_f):
    nb, t_len, _ = x.shape
    n_ctx = ctx.shape[1]
    cvec = jnp.concatenate([c, c_ctx[None, :], jnp.zeros((16 - nb - 1, D), F32)], axis=0)
    mods = [_modulation(cvec, w, b) for w, b in
            ((ada_w0, ada_b0), (ada_w1, ada_b1), (ada_w2, ada_b2), (ada_w3, ada_b3))]
    mod_b = [_bmajor_mod_table(m, nb, n_ctx + t_len, n_ctx) for m in mods]

    mod_t0 = jnp.stack([jnp.broadcast_to(mods[0][nb], (nb, 3 * D)), mods[0][:nb]], axis=0)
    h = _rglru_layer(ctx, x, mod_t0, norm0, w_in0, conv_w0, conv_b0, lru_wa0, lru_ba0, lru_wx0, lru_bx0,
                     lru_lam0, w_out0)
    z = _swa_layer(h, mod_b[1], norm1, w_in1, sink1, n_ctx)
    h, z = _na_layer(h, (z, w_out1.astype(BF16), mod_b[1]), mod_b[2], norm2, w_in2, rpb2, n_ctx)
    return _s5_layer(h, (z, w_out2.astype(BF16), mod_b[2]), mod_b[3], mods[3][:nb, None, :], norm3, w_in3, s5_a_re3, s5_a_im3, s5_log_dt3,
                     s5_b_re3, s5_b_im3, s5_c_re3, s5_c_im3, s5_d3, glu_w3, glu_b3, w_out3, norm_f, n_ctx)
```

```python
import functools
import math

import jax
import jax.numpy as jnp
import numpy as np
from jax import lax
from jax.experimental import pallas as pl
from jax.experimental.pallas import tpu as pltpu

F32 = jnp.float32
BF16 = jnp.bfloat16

D = 1024
HEAD_DIM = 64
GRID_W = 64
EPS = 1e-6
NEG_INF = -1e30
ROPE_BASE = 10000.0
LANES = 128
LOG2E = math.log2(math.e)
QK_SCALE = HEAD_DIM ** -0.5 * LOG2E
ONES_ROWS = 16

LRU_WIDTH = 1408
LRU_BLOCKS = 16
LRU_BW = LRU_WIDTH // LRU_BLOCKS
LRU_WIN = 3
CONV_W = 4
LRU_C = 8.0
LRU_TT = 64

SWA_HEADS = 16
SWA_KV_HEADS = 4
WINDOW = 128
BLOCK_Q = 128
SWA_QTILES = 6
SWA_SLOTS = 4

NA_HEADS = 16
NA_ROWS = 8
NA_COLS = 16
NA_QROWS = 4
NA_KROWS = 12
NA_HSTEP = 8
NA_SLOTS = 4
NA_QBLOCKS = 3
NA_MASKED = 2 * NA_ROWS - 1

S5_WIDTH = 1024
S5_GROUP = 16
S5_GROUPS = S5_WIDTH // S5_GROUP
S5_STATE = 64
S5_CHUNK = 16
S5_SLAB = LANES // S5_GROUP
S5_ROWS = 128

ROW_TILE = 768
VMEM_LIMIT = 56 * 1024 * 1024


def _cparams(sem):
    return pltpu.CompilerParams(dimension_semantics=sem, vmem_limit_bytes=VMEM_LIMIT)


def _sigmoid(v):
    return 0.5 * jnp.tanh(0.5 * v) + 0.5


def _silu(v):
    half = 0.5 * v
    return half * (jnp.tanh(half) + 1.0)


def _bmajor(nb, rows, n_ctx, tm):
    grid = (nb, rows // tm)
    per_batch = rows // tm

    def row(width, off=0):
        return pl.BlockSpec((1, tm, width), lambda b, i: (b, i + off, 0))

    mod = pl.BlockSpec((tm // n_ctx, 1, 3 * D), lambda b, i: (b * per_batch + i, 0, 0))
    return grid, row, mod


def _bmajor_mod_table(mod16, nb, rows, n_ctx):
    groups = rows // n_ctx
    tab = jnp.broadcast_to(mod16[:nb, None, :], (nb, groups, 3 * D))
    tab = tab.at[:, 0, :].set(mod16[nb])
    return tab.reshape(nb * groups, 1, 3 * D)


def _tmajor(nb, rows, n_ctx, tt):
    grid = (1, rows // tt)
    nct = n_ctx // tt

    def row(width, off=0):
        return pl.BlockSpec((tt, nb, width), lambda b, i: (i + off, 0, 0))

    mod = pl.BlockSpec((1, nb, 3 * D), lambda b, i: (jnp.where(i < nct, 0, 1), 0, 0))
    return grid, row, mod


def _mod_body(c_ref, w_ref, b_ref, o_ref):
    s = _silu(c_ref[...]).astype(BF16)
    o_ref[...] = jnp.dot(s, w_ref[...].astype(BF16), preferred_element_type=F32) + b_ref[...]


def _modulation(cvec, ada_w, ada_b):
    n = ada_w.shape[1]
    tn = 1024
    return pl.pallas_call(
        _mod_body,
        grid=(n // tn,),
        in_specs=[pl.BlockSpec((16, D), lambda j: (0, 0)),
                  pl.BlockSpec((D, tn), lambda j: (0, j)),
                  pl.BlockSpec((1, tn), lambda j: (0, j))],
        out_specs=pl.BlockSpec((16, tn), lambda j: (0, j)),
        out_shape=jax.ShapeDtypeStruct((16, n), F32),
        compiler_params=_cparams(("parallel",)),
        name="modulation",
    )(cvec, ada_w, ada_b.reshape(1, n))


def _by_mod_group(x3, mod3):
    groups = mod3.shape[0]
    return x3 if groups == 1 else x3.reshape(groups, x3.shape[1] // groups, x3.shape[2])


def _norm_mod(x3, mod3, gn):
    xg = _by_mod_group(x3, mod3)
    ms = jnp.mean(xg * xg, axis=-1, keepdims=True)
    y = xg * lax.rsqrt(ms + EPS) * gn
    return (y * (1.0 + mod3[:, :, D:2 * D]) + mod3[:, :, 0:D]).reshape(x3.shape)


def _in_proj_body(*refs, load, n_x, epilogue, n_extra, chunk, n_side=0):
    xs, (mod_ref, gn_ref, w_ref), rest = refs[:n_x], refs[n_x:n_x + 3], refs[n_x + 3:]
    extra, outs = rest[:n_extra], rest[n_extra:]
    x3 = load(*xs, *outs[:n_side])
    outs = outs[n_side:]
    n3 = _norm_mod(x3, mod_ref[...], gn_ref[...])
    lead = x3.shape[:2]
    n = n3.reshape(lead[0] * lead[1], D).astype(BF16)
    for j in range(w_ref.shape[1] // chunk):
        acc = jnp.dot(n, w_ref[:, j * chunk:(j + 1) * chunk], preferred_element_type=F32)
        epilogue(j, acc, lead, extra, outs)


def _load_block(x_ref):
    return x_ref[...]


def _load_after_out_proj(z_ref, w_ref, h_ref, mod_ref, h_out_ref):
    z3 = z_ref[...]
    a, b, wd = z3.shape
    y = jnp.dot(z3.reshape(a * b, wd), w_ref[...], preferred_element_type=F32).reshape(a, b, D)
    mod3 = mod_ref[...]
    h = (_by_mod_group(h_ref[...], mod3) + mod3[:, :, 2 * D:3 * D] * _by_mod_group(y, mod3)).reshape(a, b, D)
    h_out_ref[...] = h
    return h


def _in_proj(xs, x_specs, load, mod, gn, w, layout, out_shapes, out_specs, epilogue, extras=(),
             extra_specs=(), chunk=512, name="in_proj", prev=None):
    grid, row, mod_spec = layout
    n = w.shape[1]
    n_side, aliases = 0, {}
    if prev is not None:
        z, w_out, mod_prev = prev
        wd = w_out.shape[0]
        (h,) = xs
        xs = (z, w_out, h, mod_prev)
        x_specs = [row(wd), pl.BlockSpec((wd, D), lambda b, i: (0, 0)), row(D), mod_spec]
        load, n_side, aliases = _load_after_out_proj, 1, {2: 0}
        out_shapes = [jax.ShapeDtypeStruct(h.shape, F32), *out_shapes]
        out_specs = [row(D), *out_specs]
    body = functools.partial(_in_proj_body, load=load, n_x=len(xs), epilogue=epilogue,
                             n_extra=len(extras), chunk=chunk, n_side=n_side)
    return pl.pallas_call(
        body,
        grid=grid,
        in_specs=[*x_specs, mod_spec,
                  pl.BlockSpec((1, D), lambda b, i: (0, 0)),
                  pl.BlockSpec((D, n), lambda b, i: (0, 0)),
                  *extra_specs],
        out_specs=out_specs,
        out_shape=out_shapes,
        input_output_aliases=aliases,
        compiler_params=_cparams(("parallel", "parallel")),
        name=name,
    )(*xs, mod, gn.reshape(1, D), w, *extras)


def _rows_out(shape2, row, outs):
    return ([jax.ShapeDtypeStruct(shape2 + (wd,), dt) for wd, dt in outs], [row(wd) for wd, _ in outs])


def _put(ref, lo, val, lead):
    ref[:, :, lo:lo + val.shape[1]] = val.reshape(lead + (val.shape[1],)).astype(ref.dtype)


def _out_proj_body(*refs, load, n_h, store):
    (z_ref, w_ref), hs, (mod_ref, o_ref) = refs[:2], refs[2:2 + n_h], refs[2 + n_h:]
    z3 = z_ref[...]
    a, b, wd = z3.shape
    y = jnp.dot(z3.reshape(a * b, wd), w_ref[...], preferred_element_type=F32).reshape(a, b, D)
    mod3 = mod_ref[...]
    h = _by_mod_group(load(*hs), mod3) + mod3[:, :, 2 * D:3 * D] * _by_mod_group(y, mod3)
    store(o_ref, h.reshape(a, b, D))


def _lru_epilogue(j, acc, lead, extra, outs):
    _put(outs[j], 0, acc, lead)


def _lru_tile_index(s, nct, ntiles, reverse):
    if not reverse:
        return s
    return jnp.where(s < nct, nct - 1 - s, ntiles + nct - 1 - s)


def _lru_sweep_body(up_ref, uc_ref, un_ref, cw_ref, cb_ref, wg_ref, bg_ref, lam_ref, *rest,
                    nct, ntiles, reverse):
    if reverse:
        hf_ref, g_ref, o_ref, ext_ref, cv_ref, a_ref, b_ref, carry_ref = rest
    else:
        o_ref, ext_ref, cv_ref, a_ref, b_ref, carry_ref = rest
    s = pl.program_id(0)
    ti = _lru_tile_index(s, nct, ntiles, reverse)
    tt, nb, pw = uc_ref.shape

    @pl.when(s == 0)
    def _():
        carry_ref[...] = jnp.zeros_like(carry_ref)

    seg_start = (ti == 0) | (ti == nct)
    seg_end = (ti == nct - 1) | (ti == ntiles - 1)
    ext_ref[0:2] = jnp.where(seg_start, 0.0, up_ref[...])
    ext_ref[2:2 + tt] = uc_ref[...]
    ext_ref[2 + tt:3 + tt] = jnp.where(seg_end, 0.0, un_ref[...])

    rate = (-0.5 * LRU_C * LOG2E) * jax.nn.softplus(-lam_ref[...])
    tiles = [slice(k * LANES, (k + 1) * LANES) for k in range(pw // LANES)]
    for sl in tiles:
        u = cb_ref[:, sl]
        for tap in range(CONV_W):
            u = u + cw_ref[tap:tap + 1, sl] * ext_ref[tap:tap + tt, :, sl]
        cv_ref[:, sl] = u.reshape(tt * nb, LANES)
    for k, sl in enumerate(tiles):
        win = slice(_lru_window(k) * LANES, (_lru_window(k) + LRU_WIN) * LANES)
        pre = jnp.dot(cv_ref[:, win].astype(BF16), wg_ref[k], preferred_element_type=F32) + bg_ref[k]
        tr = jnp.tanh(pre[:, :LANES]) + 1.0
        ti = jnp.tanh(pre[:, LANES:]) + 1.0
        a = jnp.exp2(rate[:, sl] * tr)
        a_ref[:, sl] = a
        b_ref[:, sl] = (0.5 * jnp.sqrt(1.0 - a * a)) * (ti * cv_ref[:, sl])

    def step(q, h):
        t = tt - 1 - q if reverse else q
        rows = pl.ds(pl.multiple_of(t * nb, nb), nb)
        h = a_ref[rows, :] * h + b_ref[rows, :]
        if reverse:
            o_ref[t] = ((hf_ref[t] + h) * _silu(g_ref[t].astype(F32))).astype(o_ref.dtype)
        else:
            o_ref[t] = h
        return h

    carry_ref[...] = lax.fori_loop(0, tt, step, carry_ref[...], unroll=4)


def _lru_sweep(u, cw, cb, wg, bg, lam, nct_rows, reverse, hf=None, g=None):
    rows, nb, pw = u.shape
    tt = LRU_TT
    ntiles = rows // tt
    nct = nct_rows // tt
    tile = functools.partial(_lru_tile_index, nct=nct, ntiles=ntiles, reverse=reverse)
    cur = lambda wd: pl.BlockSpec((tt, nb, wd), lambda s: (tile(s), 0, 0))
    const = lambda shape: pl.BlockSpec(shape, lambda s: (0,) * len(shape))
    in_specs = [
        pl.BlockSpec((2, nb, pw), lambda s: (jnp.maximum(tile(s) * (tt // 2) - 1, 0), 0, 0)),
        cur(pw),
        pl.BlockSpec((1, nb, pw), lambda s: (jnp.minimum((tile(s) + 1) * tt, rows - 1), 0, 0)),
        const((CONV_W, pw)), const((1, pw)),
        const(wg.shape), const(bg.shape), const((1, pw)),
    ]
    args = [u, u, u, cw, cb, wg, bg, lam]
    if reverse:
        in_specs += [cur(pw), cur(pw)]
        args += [hf, g]
    body = functools.partial(_lru_sweep_body, nct=nct, ntiles=ntiles, reverse=reverse)
    return pl.pallas_call(
        body,
        grid=(ntiles,),
        in_specs=in_specs,
        out_specs=cur(pw),
        out_shape=jax.ShapeDtypeStruct(u.shape, BF16 if reverse else F32),
        scratch_shapes=[pltpu.VMEM((tt + 3, nb, pw), F32),
                        pltpu.VMEM((tt * nb, pw), F32),
                        pltpu.VMEM((tt * nb, pw), F32),
                        pltpu.VMEM((tt * nb, pw), F32),
                        pltpu.VMEM((nb, pw), F32)],
        compiler_params=_cparams(("arbitrary",)),
        name="lru_bwd" if reverse else "lru_fwd",
    )(*args)


def _lru_window(k):
    first_block = (k * LANES) // LRU_BW
    return min((first_block * LRU_BW) // LANES, LRU_WIDTH // LANES - LRU_WIN)


def _lru_gate_weights(wa, wx, ba, bx):
    eye = 0.5 * jnp.eye(LRU_BLOCKS, dtype=BF16)

    def dense(w):
        return (w.astype(BF16)[:, :, None, :] * eye[:, None, :, None]).reshape(LRU_WIDTH, LRU_WIDTH)

    da, dx = dense(wa), dense(wx)
    wg, bg = [], []
    for k in range(LRU_WIDTH // LANES):
        rows = slice(_lru_window(k) * LANES, (_lru_window(k) + LRU_WIN) * LANES)
        cols = slice(k * LANES, (k + 1) * LANES)
        wg.append(jnp.concatenate([da[rows, cols], dx[rows, cols]], axis=1))
        bg.append(0.5 * jnp.concatenate([ba[cols], bx[cols]])[None])
    return jnp.stack(wg).astype(BF16), jnp.stack(bg)


def _load_time_major(ctx_ref, x_ref, *, nct):
    blk = jnp.where(pl.program_id(1) < nct, ctx_ref[...], x_ref[...])
    return pltpu.einshape("btd->tbd", blk)


def _store_batch_major(o_ref, val):
    o_ref[...] = pltpu.einshape("tbd->btd", val)


def _rglru_layer(ctx, x, mod_t, gn, w_in, conv_w, conv_b, wa, ba, wx, bx, lam, w_out):
    nb, n_ctx, _ = ctx.shape
    rows = n_ctx + x.shape[1]
    tt = LRU_TT
    nct = n_ctx // tt
    layout = _tmajor(nb, rows, n_ctx, tt)
    grid, row, mod_spec = layout
    src_specs = [pl.BlockSpec((nb, tt, D), lambda b, i: (0, jnp.minimum(i, nct - 1), 0)),
                 pl.BlockSpec((nb, tt, D), lambda b, i: (0, jnp.maximum(i - nct, 0), 0))]
    load = functools.partial(_load_time_major, nct=nct)
    wd = LRU_WIDTH
    out_shapes, out_specs = _rows_out((rows, nb), row, [(wd, F32), (wd, BF16)])
    u, g = _in_proj((ctx, x), src_specs, load, mod_t, gn, w_in.astype(BF16), layout, out_shapes, out_specs,
                    _lru_epilogue, chunk=wd, name="lru_in_proj")
    cb = conv_b.reshape(1, wd)
    sweeps = []
    for d in range(2):
        wg, bg = _lru_gate_weights(wa[d], wx[d], ba[d], bx[d])
        sweeps.append((conv_w, cb, wg, bg, lam[d:d + 1]))
    hf = _lru_sweep(u, *sweeps[0], n_ctx, reverse=False)
    z = _lru_sweep(u, *sweeps[1], n_ctx, reverse=True, hf=hf, g=g)
    body = functools.partial(_out_proj_body, load=load, n_h=2, store=_store_batch_major)
    return pl.pallas_call(
        body,
        grid=grid,
        in_specs=[row(wd), pl.BlockSpec((wd, D), lambda b, i: (0, 0)), *src_specs, mod_spec],
        out_specs=pl.BlockSpec((nb, tt, D), lambda b, i: (0, i, 0)),
        out_shape=jax.ShapeDtypeStruct((nb, rows, D), F32),
        compiler_params=_cparams(("parallel", "parallel")),
        name="lru_out_proj",
    )(z, w_out.astype(BF16), ctx, x, mod_t)


def _rope_tables(n_ctx, t_len):
    pos = jnp.arange(t_len)
    row = (pos // GRID_W).astype(F32)
    col = (pos % GRID_W).astype(F32)
    n_ax = HEAD_DIM // 4
    freqs = ROPE_BASE ** (-jnp.arange(n_ax, dtype=F32) / n_ax)
    ang = jnp.concatenate([row[:, None] * freqs, col[:, None] * freqs], axis=-1)
    cos, sin = jnp.cos(ang), jnp.sin(ang)
    cos_h = jnp.concatenate([cos, cos], axis=-1)
    sin_h = jnp.concatenate([-sin, sin], axis=-1)
    cos_f = jnp.concatenate([jnp.ones((n_ctx, HEAD_DIM), F32), cos_h], axis=0)
    sin_f = jnp.concatenate([jnp.zeros((n_ctx, HEAD_DIM), F32), sin_h], axis=0)
    return jnp.tile(cos_f, (1, 2)), jnp.tile(sin_f, (1, 2))


def _rope(x, cos, sin):
    lane = lax.broadcasted_iota(jnp.int32, x.shape, 1)
    swapped = jnp.where(lane % HEAD_DIM < HEAD_DIM // 2,
                        pltpu.roll(x, LANES - HEAD_DIM // 2, 1), pltpu.roll(x, HEAD_DIM // 2, 1))
    return x * cos + swapped * sin


def _swa_epilogue(j, acc, lead, extra, outs):
    cos_ref, sin_ref = extra
    q_ref, qr_ref, k_ref, v_ref, g_ref = outs
    qd = SWA_HEADS * HEAD_DIM
    kvd = SWA_KV_HEADS * HEAD_DIM
    grp = SWA_HEADS // SWA_KV_HEADS
    width = acc.shape[1]
    lo = j * width
    cos, sin = cos_ref[...], sin_ref[...]

    def roped(x):
        return jnp.concatenate([_rope(x[:, s:s + LANES], cos, sin) for s in range(0, x.shape[1], LANES)], axis=1)

    def head(x, hh):
        return x[:, hh * HEAD_DIM:(hh + 1) * HEAD_DIM]

    if lo < qd:
        q = acc * QK_SCALE
        qr = roped(q)
        for hh in range(width // HEAD_DIM):
            kv, gq = divmod(lo // HEAD_DIM + hh, grp)
            for rt in range(lead[1] // BLOCK_Q):
                rows = slice(rt * BLOCK_Q, (rt + 1) * BLOCK_Q)
                dst = slice(gq * BLOCK_Q, (gq + 1) * BLOCK_Q)
                q_ref[0, kv, rt, dst, :] = head(q, hh)[rows].astype(q_ref.dtype)
                qr_ref[0, kv, rt, dst, :] = head(qr, hh)[rows].astype(qr_ref.dtype)
    elif lo == qd:
        kr = roped(acc[:, :kvd])
        for kv in range(SWA_KV_HEADS):
            k_ref[0, kv] = head(kr, kv).astype(k_ref.dtype)
            v_ref[0, kv] = head(acc[:, kvd:], kv).T.astype(v_ref.dtype)
    else:
        _put(g_ref, lo - qd - 2 * kvd, acc, lead)


def _swa_mask_table():
    grp = SWA_HEADS // SWA_KV_HEADS
    kj = np.arange(3 * BLOCK_Q)[:, None]
    qi = np.arange(BLOCK_Q)[None, :]
    base = np.where(np.abs(kj - BLOCK_Q - qi) <= WINDOW, 0.0, NEG_INF).astype(np.float32)
    first = np.where(kj < BLOCK_Q, NEG_INF, base).astype(np.float32)
    last = np.where(kj >= 2 * BLOCK_Q, NEG_INF, base).astype(np.float32)
    tab = np.stack([np.full_like(base, NEG_INF), first, base, last])
    return jnp.asarray(np.tile(tab, (1, 1, grp)))


def _swa_body(sink_ref, *refs):
    nq, nband = SWA_QTILES, SWA_QTILES + 2
    mask_refs, refs = refs[:nq], refs[nq:]
    (q_ref, qr_ref, kc_ref), refs = refs[:3], refs[3:]
    kb_refs, refs = refs[:nband], refs[nband:]
    vc_ref, refs = refs[0], refs[1:]
    vb_refs, (g_ref, o_ref, s_ref, p_ref) = refs[:nband], refs[nband:]
    grp = SWA_HEADS // SWA_KV_HEADS
    dims = (((1,), (1,)), ((), ()))
    nc = kc_ref.shape[2]
    units = [(t, kv) for t in range(nq) for kv in range(SWA_KV_HEADS)]
    sink_p = {}

    def scores(u):
        t, kv = units[u]
        slot = u % SWA_SLOTS
        kb = jnp.concatenate([kb_refs[t + i][0, kv] for i in range(3)], axis=0)
        s_ref[slot, :nc, :] = lax.dot_general(kc_ref[0, kv], q_ref[0, kv, t], dims, preferred_element_type=F32)
        s_ref[slot, nc:, :] = (lax.dot_general(kb, qr_ref[0, kv, t], dims, preferred_element_type=F32)
                               + mask_refs[t][0])

    def softmax(u):
        _, kv = units[u]
        slot = u % SWA_SLOTS
        sink_p[u] = []
        for gq in range(grp):
            lanes = slice(gq * BLOCK_Q, (gq + 1) * BLOCK_Q)
            s = s_ref[slot, :, lanes]
            sink = sink_ref[kv, :, lanes]
            m = jnp.maximum(jnp.max(s, axis=0, keepdims=True), sink)
            sink_p[u].append(jnp.exp2(sink - m))
            p_ref[slot, :, lanes] = jnp.exp2(s - m).astype(BF16)

    def values(u):
        t, kv = units[u]
        vt = jnp.concatenate([vc_ref[0, kv]] + [vb_refs[t + i][0, kv] for i in range(3)], axis=1)
        vt = jnp.concatenate([vt, jnp.ones((ONES_ROWS, vt.shape[1]), BF16)], axis=0)
        ot = jnp.dot(vt, p_ref[u % SWA_SLOTS], preferred_element_type=F32)
        denom = ot[HEAD_DIM:HEAD_DIM + 1] + jnp.concatenate(sink_p[u], axis=1)
        ot = ot[:HEAD_DIM] * (1.0 / denom)
        slab = jnp.concatenate([ot[:, gq * BLOCK_Q:(gq + 1) * BLOCK_Q].T for gq in range(grp)], axis=1)
        rows = slice(t * BLOCK_Q, (t + 1) * BLOCK_Q)
        lanes = slice(kv * grp * HEAD_DIM, (kv + 1) * grp * HEAD_DIM)
        o_ref[0, rows, lanes] = (slab * _silu(g_ref[0, rows, lanes].astype(F32))).astype(o_ref.dtype)

    scores(0)
    scores(1)
    for u in range(len(units)):
        softmax(u)
        if u + 2 < len(units):
            scores(u + 2)
        values(u)


def _swa_attention(q, qr, k, vt, g, sink, n_ctx):
    nb, nkv, nt, qrows, hd = q.shape
    rows = k.shape[2]
    grp = SWA_HEADS // SWA_KV_HEADS
    nct = n_ctx // BLOCK_Q
    nkeys = n_ctx + 3 * BLOCK_Q
    nq = SWA_QTILES
    assert nt % nq == 0
    qspec = pl.BlockSpec((1, nkv, nq, qrows, hd), lambda b, j: (b, 0, j, 0, 0))
    kctx = pl.BlockSpec((1, nkv, n_ctx, hd), lambda b, j: (b, 0, 0, 0))
    vctx = pl.BlockSpec((1, nkv, hd, n_ctx), lambda b, j: (b, 0, 0, 0))
    rowspec = pl.BlockSpec((1, nq * BLOCK_Q, g.shape[2]), lambda b, j: (b, j, 0))

    def tile(j, off):
        return jnp.clip(j * nq - 1 + off, nct, nt - 1)

    def kband(off):
        return pl.BlockSpec((1, nkv, BLOCK_Q, hd), lambda b, j: (b, 0, tile(j, off), 0))

    def vband(off):
        return pl.BlockSpec((1, nkv, hd, BLOCK_Q), lambda b, j: (b, 0, 0, tile(j, off)))

    def mask(t):
        def variant(b, j):
            n = j * nq + t
            return (jnp.where(n < nct, 0, jnp.where(n == nct, 1, jnp.where(n == nt - 1, 3, 2))), 0, 0)
        return pl.BlockSpec((1, 3 * BLOCK_Q, qrows), variant)

    sink_rows = jnp.repeat(sink.astype(F32).reshape(nkv, grp) * LOG2E, BLOCK_Q, axis=1).reshape(nkv, 1, qrows)
    masks = _swa_mask_table()
    return pl.pallas_call(
        _swa_body,
        grid=(nb, nt // nq),
        in_specs=[pl.BlockSpec((nkv, 1, qrows), lambda b, j: (0, 0, 0)),
                  *[mask(t) for t in range(nq)], qspec, qspec,
                  kctx, *[kband(i) for i in range(nq + 2)], vctx, *[vband(i) for i in range(nq + 2)], rowspec],
        out_specs=rowspec,
        out_shape=jax.ShapeDtypeStruct((nb, rows, g.shape[2]), BF16),
        scratch_shapes=[pltpu.VMEM((SWA_SLOTS, nkeys, qrows), F32), pltpu.VMEM((SWA_SLOTS, nkeys, qrows), BF16)],
        compiler_params=_cparams(("parallel", "parallel")),
        name="swa_attention",
    )(sink_rows, *[masks] * nq, q, qr, k, *[k] * (nq + 2), vt, *[vt] * (nq + 2), g)


def _swa_layer(h, mod_b, gn, w_in, sink, n_ctx):
    nb, rows, _ = h.shape
    tm = ROW_TILE
    layout = _bmajor(nb, rows, n_ctx, tm)
    _, row, _ = layout
    qd = SWA_HEADS * HEAD_DIM
    grp = SWA_HEADS // SWA_KV_HEADS
    nkv = SWA_KV_HEADS
    cos, sin = _rope_tables(n_ctx, rows - n_ctx)
    tab = pl.BlockSpec((tm, LANES), lambda b, i: (i, 0))
    q_shape = jax.ShapeDtypeStruct((nb, nkv, rows // BLOCK_Q, grp * BLOCK_Q, HEAD_DIM), BF16)
    q_spec = pl.BlockSpec((1, nkv, tm // BLOCK_Q, grp * BLOCK_Q, HEAD_DIM), lambda b, i: (b, 0, i, 0, 0))
    k_shape = jax.ShapeDtypeStruct((nb, nkv, rows, HEAD_DIM), BF16)
    k_spec = pl.BlockSpec((1, nkv, tm, HEAD_DIM), lambda b, i: (b, 0, i, 0))
    vt_shape = jax.ShapeDtypeStruct((nb, nkv, HEAD_DIM, rows), BF16)
    vt_spec = pl.BlockSpec((1, nkv, HEAD_DIM, tm), lambda b, i: (b, 0, 0, i))
    q, qr, k, vt, g = _in_proj(
        (h,), [row(D)], _load_block, mod_b, gn, w_in.astype(BF16), layout,
        [q_shape, q_shape, k_shape, vt_shape, jax.ShapeDtypeStruct((nb, rows, qd), BF16)],
        [q_spec, q_spec, k_spec, vt_spec, row(qd)], _swa_epilogue,
        extras=(cos, sin), extra_specs=(tab, tab), name="swa_in_proj")
    return _swa_attention(q, qr, k, vt, g, sink, n_ctx)


def _na_epilogue(j, acc, lead, extra, outs):
    q_ref, k_ref, v_ref, g_ref = outs
    wd = NA_HEADS * HEAD_DIM
    width = acc.shape[1]
    lo = j * width
    which, off = lo // wd, lo % wd
    if which == 3:
        _put(g_ref, off, acc, lead)
        return
    ref = (q_ref, k_ref, v_ref)[which]
    val = acc * QK_SCALE if which == 0 else acc
    for hh in range(width // HEAD_DIM):
        ref[0, off // HEAD_DIM + hh] = val[:, hh * HEAD_DIM:(hh + 1) * HEAD_DIM].astype(ref.dtype)


def _na_bias_tiles(rpb):
    reach = GRID_W - NA_COLS
    period = 2 * GRID_W
    ndy = 2 * NA_ROWS - 1
    edge = jnp.pad(rpb.astype(F32) * LOG2E, ((0, 0), (0, 0), (reach, reach)), mode="edge")
    flat = jnp.tile(jnp.pad(edge[:, :, ::-1], ((0, 0), (0, 0), (0, 1))), (1, 1, GRID_W))[:, :, :GRID_W * (period - 1)]
    e = flat.reshape(NA_HEADS, ndy, GRID_W, period - 1)[:, :, :, GRID_W - 1:]
    col = np.arange(GRID_W)
    cstart = np.clip(col - NA_COLS // 2, 0, GRID_W - NA_COLS)
    col_ok = (col[:, None] >= cstart[None, :]) & (col[:, None] < cstart[None, :] + NA_COLS)
    e = jnp.where(jnp.asarray(col_ok)[None, None], e, NEG_INF)
    e = jnp.pad(e, ((0, 0), (0, 1), (0, 0), (0, 0)), constant_values=NEG_INF)
    shape = (NA_HEADS // NA_HSTEP, NA_HSTEP, ndy + 1, GRID_W, period)
    left = jnp.pad(e, ((0, 0), (0, 0), (0, 0), (0, GRID_W)), constant_values=NEG_INF).reshape(shape)
    right = jnp.pad(e, ((0, 0), (0, 0), (0, 0), (GRID_W, 0)), constant_values=NEG_INF).reshape(shape)
    return left, right


def _na_body(q_ref, k_ref, v_ref, g_ref, bl_ref, br_ref, o_ref, s_ref, p_ref, *, n_ctx, grid_rows):
    nk = NA_KROWS * GRID_W
    nq = NA_QROWS * GRID_W
    a_bt = (((1,), (1,)), ((), ()))
    at_b = (((0,), (0,)), ((), ()))
    left_half = lax.broadcasted_iota(jnp.int32, (GRID_W, 2 * GRID_W), 1) < GRID_W
    ones = jnp.ones((n_ctx + nk, HEAD_DIM), BF16)

    def block_geometry(qb):
        rb = pl.program_id(2) * NA_QBLOCKS + qb
        r0 = (rb - 1) * NA_QROWS
        kb = jnp.clip(r0 - NA_ROWS // 2, 0, grid_rows - NA_KROWS)

        def tile_index(qr, kr):
            qrow, krow = r0 + qr, kb + kr
            first = jnp.clip(qrow - NA_ROWS // 2, 0, grid_rows - NA_ROWS)
            inside = (rb >= 1) & (krow >= first) & (krow < first + NA_ROWS)
            return jnp.where(inside, krow - qrow + NA_ROWS - 1, NA_MASKED)

        idx = [[tile_index(qr, kr) for kr in range(NA_KROWS)] for qr in range(NA_QROWS)]
        return pl.multiple_of(n_ctx + kb * GRID_W, GRID_W), idx

    geometry = [block_geometry(qb) for qb in range(NA_QBLOCKS)]
    units = [(qb, hl) for qb in range(NA_QBLOCKS) for hl in range(NA_HSTEP)]
    outs = {}

    def scores(u):
        qb, hl = units[u]
        start, idx = geometry[qb]
        slot = u % NA_SLOTS
        qh = q_ref[0, hl, qb * nq:(qb + 1) * nq, :]
        bias = jnp.concatenate(
            [jnp.concatenate([jnp.where(left_half, bl_ref[0, hl, idx[2 * m][kr]], br_ref[0, hl, idx[2 * m + 1][kr]])
                              for m in range(NA_QROWS // 2)], axis=1) for kr in range(NA_KROWS)], axis=0)
        s_ref[slot, :n_ctx, :] = lax.dot_general(k_ref[0, hl, 0:n_ctx, :], qh, a_bt, preferred_element_type=F32)
        s_ref[slot, n_ctx:, :] = (lax.dot_general(k_ref[0, hl, pl.ds(start, nk), :], qh, a_bt,
                                                  preferred_element_type=F32) + bias)

    def softmax(u):
        slot = u % NA_SLOTS
        for c in range(nq // LANES):
            lanes = slice(c * LANES, (c + 1) * LANES)
            s = s_ref[slot, :, lanes]
            p_ref[slot, :, lanes] = jnp.exp2(s - jnp.max(s, axis=0, keepdims=True)).astype(BF16)

    def values(u):
        qb, hl = units[u]
        start, _ = geometry[qb]
        v1 = jnp.concatenate(
            [jnp.concatenate([v_ref[0, hl, 0:n_ctx, :], v_ref[0, hl, pl.ds(start, nk), :]], axis=0), ones], axis=1)
        ot = lax.dot_general(v1, p_ref[u % NA_SLOTS], at_b, preferred_element_type=F32)
        outs[hl] = (ot[:HEAD_DIM] * (1.0 / ot[HEAD_DIM:HEAD_DIM + 1])).T
        if hl == NA_HSTEP - 1:
            rows = slice(qb * nq, (qb + 1) * nq)
            slab = jnp.concatenate([outs[h] for h in range(NA_HSTEP)], axis=1)
            o_ref[0, rows, :] = (slab * _silu(g_ref[0, rows, :].astype(F32))).astype(o_ref.dtype)

    scores(0)
    scores(1)
    for u in range(len(units)):
        softmax(u)
        if u + 2 < len(units):
            scores(u + 2)
        values(u)


def _na_attention(q, k, v, g, bias_left, bias_right, n_ctx):
    nb, nh, rows, hd = q.shape
    grid_rows = (rows - n_ctx) // GRID_W
    nq = NA_QROWS * GRID_W
    assert n_ctx == nq, "context tokens form exactly one query block"
    assert rows % (NA_QBLOCKS * nq) == 0
    nkeys = n_ctx + NA_KROWS * GRID_W
    lanes = NA_HSTEP * HEAD_DIM
    qspec = pl.BlockSpec((1, NA_HSTEP, NA_QBLOCKS * nq, hd), lambda p, b, r: (b, p, r, 0))
    kvspec = pl.BlockSpec((1, NA_HSTEP, rows, hd), lambda p, b, r: (b, p, 0, 0))
    rowspec = pl.BlockSpec((1, NA_QBLOCKS * nq, lanes), lambda p, b, r: (b, r, p))
    bspec = pl.BlockSpec((1,) + bias_left.shape[1:], lambda p, b, r: (p, 0, 0, 0, 0))
    body = functools.partial(_na_body, n_ctx=n_ctx, grid_rows=grid_rows)
    return pl.pallas_call(
        body,
        grid=(nh // NA_HSTEP, nb, rows // (NA_QBLOCKS * nq)),
        in_specs=[qspec, kvspec, kvspec, rowspec, bspec, bspec],
        out_specs=rowspec,
        out_shape=jax.ShapeDtypeStruct(g.shape, BF16),
        scratch_shapes=[pltpu.VMEM((NA_SLOTS, nkeys, nq), F32), pltpu.VMEM((NA_SLOTS, nkeys, nq), BF16)],
        compiler_params=_cparams(("parallel", "parallel", "parallel")),
        name="na_attention",
    )(q, k, v, g, bias_left, bias_right)


def _na_layer(h, prev, mod_b, gn, w_in, rpb, n_ctx):
    nb, rows, _ = h.shape
    tm = ROW_TILE
    layout = _bmajor(nb, rows, n_ctx, tm)
    _, row, _ = layout
    wd = NA_HEADS * HEAD_DIM
    head_shape = jax.ShapeDtypeStruct((nb, NA_HEADS, rows, HEAD_DIM), BF16)
    head_spec = pl.BlockSpec((1, NA_HEADS, tm, HEAD_DIM), lambda b, i: (b, 0, i, 0))
    h, q, k, v, g = _in_proj((h,), [row(D)], _load_block, mod_b, gn, w_in.astype(BF16), layout,
                             [head_shape] * 3 + [jax.ShapeDtypeStruct((nb, rows, wd), BF16)],
                             [head_spec] * 3 + [row(wd)], _na_epilogue, name="swa_out_na_in_proj", prev=prev)
    bias_left, bias_right = _na_bias_tiles(rpb)
    return h, _na_attention(q, k, v, g, bias_left, bias_right, n_ctx)


def _s5_epilogue(j, acc, lead, extra, outs):
    u_ref, g_ref = outs
    lo = j * acc.shape[1]
    if lo < S5_WIDTH:
        _put(u_ref, lo, acc, lead)
    else:
        _put(g_ref, lo - S5_WIDTH, acc, lead)


def _s5_matrices(a_re, a_im, log_dt, b_re, b_im, c_re, c_im, d_skip):
    L = S5_CHUNK
    lam = lax.complex(a_re.astype(F32), a_im.astype(F32))
    lam_dt = lam * jnp.exp(log_dt.astype(F32))[..., None]
    lam_bar = jnp.exp(lam_dt)
    b_bar = ((lam_bar - 1.0) / lam)[..., None] * lax.complex(b_re.astype(F32), b_im.astype(F32))
    c_mat = lax.complex(c_re.astype(F32), c_im.astype(F32))
    tau = np.arange(L)

    def power(expo, d):
        return jnp.exp(jnp.asarray(expo, F32).reshape(expo.shape + (1, 1)) * lam_dt[d])

    w = jnp.stack([c_mat[d][None] * power(tau, d)[:, :, None, :] for d in range(2)])
    w2 = jnp.concatenate([jnp.real(w), -jnp.imag(w)], axis=-1)
    b2 = jnp.concatenate([jnp.real(b_bar), jnp.imag(b_bar)], axis=-2)
    kern = jnp.einsum('dkgip,dgpj->dkgij', w2, b2).reshape(2 * L, S5_GROUPS, S5_GROUP, S5_GROUP)
    lag = tau[None, :] - tau[:, None]
    sel = np.concatenate([lag[None] == tau[:, None, None], -lag[None] == tau[:, None, None]]).astype(np.float32)
    both = jnp.einsum('ktu,kgij->tugij', jnp.asarray(sel), kern)
    eye_t = jnp.eye(L, dtype=F32)[:, :, None, None, None]
    skip = jnp.eye(S5_GROUP, dtype=F32)[None] * d_skip.astype(F32).reshape(S5_GROUPS, S5_GROUP, 1)
    both = both + eye_t * skip[None, None]
    toep = both.transpose(2, 0, 4, 1, 3).reshape(S5_GROUPS, L * S5_GROUP, L * S5_GROUP)

    def lanes(z):
        pad = [(0, 0)] * (z.ndim - 1) + [(0, LANES - S5_STATE)]
        return jnp.concatenate([jnp.pad(jnp.real(z), pad), jnp.pad(jnp.imag(z), pad)], axis=-1)

    pf = power(L - 1 - tau, 0)[:, :, :, None] * b_bar[0][None]
    pb = power(tau, 1)[:, :, :, None] * b_bar[1][None]
    pmat = jnp.concatenate([lanes(pf.transpose(1, 0, 3, 2)), lanes(pb.transpose(1, 0, 3, 2))], axis=-1)
    pmat = pmat.reshape(S5_GROUPS, L * S5_GROUP, 4 * LANES)
    qf = c_mat[0][None] * power(tau + 1, 0)[:, :, None, :]
    qb = c_mat[1][None] * power(L - tau, 1)[:, :, None, :]

    def state_rows(z):
        pad = ((0, 0), (0, LANES - S5_STATE), (0, 0), (0, 0))
        zr = jnp.pad(jnp.real(z).transpose(1, 3, 0, 2), pad)
        zi = jnp.pad(-jnp.imag(z).transpose(1, 3, 0, 2), pad)
        return jnp.concatenate([zr, zi], axis=1).reshape(S5_GROUPS, 2 * LANES, L * S5_GROUP)

    qmat = jnp.concatenate([state_rows(qf), state_rows(qb)], axis=1)
    lam_l = jnp.exp(float(L) * lam_dt)
    pad = ((0, 0), (0, LANES - S5_STATE))
    coef = jnp.stack([jnp.pad(jnp.real(lam_l[0]), pad), jnp.pad(jnp.imag(lam_l[0]), pad),
                      jnp.pad(jnp.real(lam_l[1]), pad), jnp.pad(jnp.imag(lam_l[1]), pad)], axis=1)
    coef = jnp.pad(coef, ((0, 0), (0, 4), (0, 0)))
    return toep.astype(BF16), pmat.astype(BF16), qmat.astype(BF16), coef


def _atom_transpose(xs, atom):
    n = len(xs)
    group = lax.broadcasted_iota(jnp.int32, xs[0].shape, 1) // atom
    xs = list(xs)
    d = n // 2
    while d:
        keep = (group & d) == 0
        for i in range(n):
            if not i & d:
                lo, hi = xs[i], xs[i + d]
                xs[i] = jnp.where(keep, lo, pltpu.roll(hi, d * atom, 1))
                xs[i + d] = jnp.where(keep, pltpu.roll(lo, LANES - d * atom, 1), hi)
        d //= 2
    return xs


def _s5_body(u_ref, toep_ref, p_ref, q_ref, coef_ref, y_ref, u_scr, y_scr, s_ref, hin_ref, *, nb, nc_ctx, nc):
    nwin = LANES // S5_GROUP
    nrow, nlat = u_ref.shape[1], y_ref.shape[1]
    blk = S5_ROWS
    for half in range(S5_CHUNK // nwin):
        cols = slice(half * LANES, (half + 1) * LANES)
        for r in range(0, nrow, blk):
            words = [pltpu.bitcast(u_ref[half * nwin + w, r:r + blk, :], jnp.uint32) for w in range(nwin)]
            for gl, x in enumerate(_atom_transpose(words, S5_GROUP)):
                u_scr[gl, r:r + blk, cols] = pltpu.bitcast(x, BF16)
    for gl in range(S5_SLAB):
        _s5_group(gl, u_scr, toep_ref, p_ref, q_ref, coef_ref, y_scr, s_ref, hin_ref, nb, nc_ctx, nc)
    for half in range(S5_CHUNK // nwin):
        cols = slice(half * LANES, (half + 1) * LANES)
        for r in range(0, nlat, blk):
            ys = [y_scr[gl, r:r + blk, cols] for gl in range(S5_SLAB)]
            for w, x in enumerate(_atom_transpose(ys, S5_GROUP)):
                y_ref[half * nwin + w, r:r + blk, :] = x


def _s5_group(gl, u_scr, toep_ref, p_ref, q_ref, coef_ref, y_scr, s_ref, hin_ref, nb, nc_ctx, nc):
    u = u_scr[gl]
    s_ref[...] = jnp.dot(u, p_ref[gl], preferred_element_type=F32)
    coef = coef_ref[gl]
    shape = (nb, LANES)
    lr_f, li_f = jnp.broadcast_to(coef[0:1], shape), jnp.broadcast_to(coef[1:2], shape)
    lr_b, li_b = jnp.broadcast_to(coef[2:3], shape), jnp.broadcast_to(coef[3:4], shape)
    zero = jnp.zeros(shape, F32)

    def step(q, carry):
        fr, fi, br, bi = carry
        rows_f = pl.ds(pl.multiple_of(q * nb, nb), nb)
        cb = jnp.where(q < nc_ctx, nc_ctx - 1 - q, nc + nc_ctx - 1 - q)
        rows_b = pl.ds(pl.multiple_of(cb * nb, nb), nb)
        hin_ref[rows_f, 0:LANES] = fr
        hin_ref[rows_f, LANES:2 * LANES] = fi
        hin_ref[rows_b, 2 * LANES:3 * LANES] = br
        hin_ref[rows_b, 3 * LANES:4 * LANES] = bi
        sfr, sfi = s_ref[rows_f, 0:LANES], s_ref[rows_f, LANES:2 * LANES]
        sbr, sbi = s_ref[rows_b, 2 * LANES:3 * LANES], s_ref[rows_b, 3 * LANES:4 * LANES]
        return (lr_f * fr - li_f * fi + sfr, lr_f * fi + li_f * fr + sfi,
                lr_b * br - li_b * bi + sbr, lr_b * bi + li_b * br + sbi)

    lax.fori_loop(0, nc, step, (zero, zero, zero, zero), unroll=True)

    lat = slice(nc_ctx * nb, nc * nb)
    y_scr[gl] = (jnp.dot(u[lat], toep_ref[gl], preferred_element_type=F32)
                 + jnp.dot(hin_ref[lat, :].astype(BF16), q_ref[gl], preferred_element_type=F32))


def _s5_scan(u_t, toep, pmat, qmat, coef, nb, n_ctx):
    L, nrow, wd = u_t.shape
    pk = L * S5_GROUP
    nc = nrow // nb
    nc_ctx = n_ctx // L
    nlat = (nc - nc_ctx) * nb
    assert nrow % S5_ROWS == 0 and nlat % S5_ROWS == 0
    once = pl.Buffered(1)
    slab = lambda shape: pl.BlockSpec((S5_SLAB,) + shape, lambda s: (s, 0, 0), pipeline_mode=once)
    body = functools.partial(_s5_body, nb=nb, nc_ctx=nc_ctx, nc=nc)
    return pl.pallas_call(
        body,
        grid=(wd // LANES,),
        in_specs=[pl.BlockSpec((L, nrow, LANES), lambda s: (0, 0, s)),
                  slab((pk, pk)), slab((pk, 4 * LANES)), slab((4 * LANES, pk)), slab((8, LANES))],
        out_specs=pl.BlockSpec((L, nlat, LANES), lambda s: (0, 0, s)),
        out_shape=jax.ShapeDtypeStruct((L, nlat, wd), F32),
        scratch_shapes=[pltpu.VMEM((S5_SLAB, nrow, pk), BF16), pltpu.VMEM((S5_SLAB, nlat, pk), F32),
                        pltpu.VMEM((nrow, 4 * LANES), F32), pltpu.VMEM((nrow, 4 * LANES), F32)],
        compiler_params=_cparams(("parallel",)),
        name="s5_scan",
    )(u_t, toep, pmat, qmat, coef)


def _rms(x, g):
    return x * lax.rsqrt(jnp.mean(x * x, axis=-1, keepdims=True) + EPS) * g


def _s5_readout_body(y_ref, ga_ref, gb_ref, ha_ref, hb_ref, gw_ref, gbias_ref, w_ref, mod_ref, nf_ref, o_ref):
    sub = ga_ref.shape[1]
    for i, (g_ref, h_ref) in enumerate(((ga_ref, ha_ref), (gb_ref, hb_ref))):
        rows = slice(i * sub, (i + 1) * sub)
        y = jax.nn.gelu(y_ref[0, rows, :])
        t = y * _sigmoid(jnp.dot(y.astype(BF16), gw_ref[...], preferred_element_type=F32) + gbias_ref[...])
        z = (t * _silu(g_ref[0].astype(F32))).astype(BF16)
        out = jnp.dot(z, w_ref[...], preferred_element_type=F32)
        h = h_ref[0] + mod_ref[0, :, 2 * D:3 * D] * out
        o_ref[0, rows, :] = _rms(h, nf_ref[...])


def _s5_readout(y, g, glu_w, glu_b, w_out, h, mod_b, norm_f, n_ctx):
    nb, t_len, _ = y.shape
    sub = n_ctx
    tm = 2 * sub
    lat = lambda wd: pl.BlockSpec((1, tm, wd), lambda b, i: (b, i, 0))
    part = lambda wd, j: pl.BlockSpec((1, sub, wd), lambda b, i: (b, 2 * i + 1 + j, 0))
    const = lambda shape: pl.BlockSpec(shape, lambda b, i: (0, 0))
    return pl.pallas_call(
        _s5_readout_body,
        grid=(nb, t_len // tm),
        in_specs=[lat(S5_WIDTH), part(S5_WIDTH, 0), part(S5_WIDTH, 1), part(D, 0), part(D, 1),
                  const((S5_WIDTH, S5_WIDTH)), const((1, S5_WIDTH)), const((S5_WIDTH, D)),
                  pl.BlockSpec((1, 1, 3 * D), lambda b, i: (b, 0, 0)), const((1, D))],
        out_specs=lat(D),
        out_shape=jax.ShapeDtypeStruct((nb, t_len, D), F32),
        compiler_params=_cparams(("parallel", "parallel")),
        name="s5_readout",
    )(y, g, g, h, h, glu_w.astype(BF16), glu_b.reshape(1, S5_WIDTH), w_out.astype(BF16), mod_b,
      norm_f.reshape(1, D))


def _s5_layer(h, prev, mod_b, mod_lat, gn, w_in, a_re, a_im, log_dt, b_re, b_im, c_re, c_im, d_skip, glu_w, glu_b,
              w_out, norm_f, n_ctx):
    nb, rows, _ = h.shape
    L = S5_CHUNK
    layout = _bmajor(nb, rows, n_ctx, ROW_TILE)
    _, row, _ = layout
    out_shapes, out_specs = _rows_out((nb, rows), row, [(S5_WIDTH, BF16)] * 2)
    h, u, g = _in_proj((h,), [row(D)], _load_block, mod_b, gn, w_in.astype(BF16), layout, out_shapes, out_specs,
                       _s5_epilogue, name="na_out_s5_in_proj", prev=prev)
    nc = rows // L
    u_t = u.reshape(nb, nc, L, S5_WIDTH).transpose(2, 1, 0, 3).reshape(L, nc * nb, S5_WIDTH)
    toep, pmat, qmat, coef = _s5_matrices(a_re, a_im, log_dt, b_re, b_im, c_re, c_im, d_skip)
    y_t = _s5_scan(u_t, toep, pmat, qmat, coef, nb, n_ctx)
    t_len = rows - n_ctx
    y = y_t.reshape(L, t_len // L, nb, S5_WIDTH).transpose(2, 1, 0, 3).reshape(nb, t_len, S5_WIDTH)
    return _s5_readout(y, g, glu_w, glu_b, w_out, h, mod_lat, norm_f, n_ctx)


def kernel(x, c, ctx, c_ctx, ada_w0, ada_b0, norm0, w_in0, conv_w0, conv_b0, lru_wa0, lru_ba0, lru_wx0, lru_bx0, lru_lam0, w_out0, ada_w1, ada_b1, norm1, w_in1, sink1, w_out1, ada_w2, ada_b2, norm2, w_in2, rpb2, w_out2, ada_w3, ada_b3, norm3, w_in3, s5_a_re3, s5_a_im3, s5_log_dt3, s5_b_re3, s5_b_im3, s5_c_re3, s5_c_im3, s5_d3, glu_w3, glu_b3, w_out3, norm_f):
    nb, t_len, _ = x.shape
    n_ctx = ctx.shape[1]
    cvec = jnp.concatenate([c, c_ctx[None, :], jnp.zeros((16 - nb - 1, D), F32)], axis=0)
    mods = [_modulation(cvec, w, b) for w, b in
            ((ada_w0, ada_b0), (ada_w1, ada_b1), (ada_w2, ada_b2), (ada_w3, ada_b3))]
    mod_b = [_bmajor_mod_table(m, nb, n_ctx + t_len, n_ctx) for m in mods]

    mod_t0 = jnp.stack([jnp.broadcast_to(mods[0][nb], (nb, 3 * D)), mods[0][:nb]], axis=0)
    h = _rglru_layer(ctx, x, mod_t0, norm0, w_in0, conv_w0, conv_b0, lru_wa0, lru_ba0, lru_wx0, lru_bx0,
                     lru_lam0, w_out0)
    z = _swa_layer(h, mod_b[1], norm1, w_in1, sink1, n_ctx)
    h, z = _na_layer(h, (z, w_out1.astype(BF16), mod_b[1]), mod_b[2], norm2, w_in2, rpb2, n_ctx)
    return _s5_layer(h, (z, w_out2.astype(BF16), mod_b[2]), mod_b[3], mods[3][:nb, None, :], norm3, w_in3, s5_a_re3, s5_a_im3, s5_log_dt3,
                     s5_b_re3, s5_b_im3, s5_c_re3, s5_c_im3, s5_d3, glu_w3, glu_b3, w_out3, norm_f, n_ctx)
```

```python
import functools
import math

import jax
import jax.numpy as jnp
import numpy as np
from jax import lax
from jax.experimental import pallas as pl
from jax.experimental.pallas import tpu as pltpu

F32 = jnp.float32
BF16 = jnp.bfloat16

D = 1024
HEAD_DIM = 64
GRID_W = 64
EPS = 1e-6
NEG_INF = -1e30
ROPE_BASE = 10000.0
LANES = 128
LOG2E = math.log2(math.e)
QK_SCALE = HEAD_DIM ** -0.5 * LOG2E
ONES_ROWS = 16

LRU_WIDTH = 1408
LRU_BLOCKS = 16
LRU_BW = LRU_WIDTH // LRU_BLOCKS
LRU_WIN = 3
CONV_W = 4
LRU_C = 8.0
LRU_TT = 64

SWA_HEADS = 16
SWA_KV_HEADS = 4
WINDOW = 128
BLOCK_Q = 128
SWA_QTILES = 6
SWA_SLOTS = 4

NA_HEADS = 16
NA_ROWS = 8
NA_COLS = 16
NA_QROWS = 4
NA_KROWS = 12
NA_HSTEP = 8
NA_SLOTS = 4
NA_QBLOCKS = 3
NA_MASKED = 2 * NA_ROWS - 1

S5_WIDTH = 1024
S5_GROUP = 16
S5_GROUPS = S5_WIDTH // S5_GROUP
S5_STATE = 64
S5_CHUNK = 16
S5_SLAB = LANES // S5_GROUP
S5_ROWS = 128

ROW_TILE = 768
VMEM_LIMIT = 56 * 1024 * 1024


def _cparams(sem):
    return pltpu.CompilerParams(dimension_semantics=sem, vmem_limit_bytes=VMEM_LIMIT)


def _sigmoid(v):
    return 0.5 * jnp.tanh(0.5 * v) + 0.5


def _silu(v):
    half = 0.5 * v
    return half * (jnp.tanh(half) + 1.0)


def _bmajor(nb, rows, n_ctx, tm):
    grid = (nb, rows // tm)
    per_batch = rows // tm

    def row(width, off=0):
        return pl.BlockSpec((1, tm, width), lambda b, i: (b, i + off, 0))

    mod = pl.BlockSpec((tm // n_ctx, 1, 3 * D), lambda b, i: (b * per_batch + i, 0, 0))
    return grid, row, mod


def _bmajor_mod_table(mod16, nb, rows, n_ctx):
    groups = rows // n_ctx
    tab = jnp.broadcast_to(mod16[:nb, None, :], (nb, groups, 3 * D))
    tab = tab.at[:, 0, :].set(mod16[nb])
    return tab.reshape(nb * groups, 1, 3 * D)


def _tmajor(nb, rows, n_ctx, tt):
    grid = (1, rows // tt)
    nct = n_ctx // tt

    def row(width, off=0):
        return pl.BlockSpec((tt, nb, width), lambda b, i: (i + off, 0, 0))

    mod = pl.BlockSpec((1, nb, 3 * D), lambda b, i: (jnp.where(i < nct, 0, 1), 0, 0))
    return grid, row, mod


def _mod_body(c_ref, w_ref, b_ref, o_ref):
    s = _silu(c_ref[...]).astype(BF16)
    o_ref[...] = jnp.dot(s, w_ref[...].astype(BF16), preferred_element_type=F32) + b_ref[...]


def _modulation(cvec, ada_w, ada_b):
    n = ada_w.shape[1]
    tn = 1024
    return pl.pallas_call(
        _mod_body,
        grid=(n // tn,),
        in_specs=[pl.BlockSpec((16, D), lambda j: (0, 0)),
                  pl.BlockSpec((D, tn), lambda j: (0, j)),
                  pl.BlockSpec((1, tn), lambda j: (0, j))],
        out_specs=pl.BlockSpec((16, tn), lambda j: (0, j)),
        out_shape=jax.ShapeDtypeStruct((16, n), F32),
        compiler_params=_cparams(("parallel",)),
        name="modulation",
    )(cvec, ada_w, ada_b.reshape(1, n))


def _by_mod_group(x3, mod3):
    groups = mod3.shape[0]
    return x3 if groups == 1 else x3.reshape(groups, x3.shape[1] // groups, x3.shape[2])


def _norm_mod(x3, mod3, gn):
    xg = _by_mod_group(x3, mod3)
    ms = jnp.mean(xg * xg, axis=-1, keepdims=True)
    y = xg * lax.rsqrt(ms + EPS) * gn
    return (y * (1.0 + mod3[:, :, D:2 * D]) + mod3[:, :, 0:D]).reshape(x3.shape)


def _in_proj_body(*refs, load, n_x, epilogue, n_extra, chunk, n_side=0):
    xs, (mod_ref, gn_ref, w_ref), rest = refs[:n_x], refs[n_x:n_x + 3], refs[n_x + 3:]
    extra, outs = rest[:n_extra], rest[n_extra:]
    x3 = load(*xs, *outs[:n_side])
    outs = outs[n_side:]
    n3 = _norm_mod(x3, mod_ref[...], gn_ref[...])
    lead = x3.shape[:2]
    n = n3.reshape(lead[0] * lead[1], D).astype(BF16)
    for j in range(w_ref.shape[1] // chunk):
        acc = jnp.dot(n, w_ref[:, j * chunk:(j + 1) * chunk], preferred_element_type=F32)
        epilogue(j, acc, lead, extra, outs)


def _load_block(x_ref):
    return x_ref[...]


def _load_after_out_proj(z_ref, w_ref, h_ref, mod_ref, h_out_ref):
    z3 = z_ref[...]
    a, b, wd = z3.shape
    y = jnp.dot(z3.reshape(a * b, wd), w_ref[...], preferred_element_type=F32).reshape(a, b, D)
    mod3 = mod_ref[...]
    h = (_by_mod_group(h_ref[...], mod3) + mod3[:, :, 2 * D:3 * D] * _by_mod_group(y, mod3)).reshape(a, b, D)
    h_out_ref[...] = h
    return h


def _in_proj(xs, x_specs, load, mod, gn, w, layout, out_shapes, out_specs, epilogue, extras=(),
             extra_specs=(), chunk=512, name="in_proj", prev=None):
    grid, row, mod_spec = layout
    n = w.shape[1]
    n_side, aliases = 0, {}
    if prev is not None:
        z, w_out, mod_prev = prev
        wd = w_out.shape[0]
        (h,) = xs
        xs = (z, w_out, h, mod_prev)
        x_specs = [row(wd), pl.BlockSpec((wd, D), lambda b, i: (0, 0)), row(D), mod_spec]
        load, n_side, aliases = _load_after_out_proj, 1, {2: 0}
        out_shapes = [jax.ShapeDtypeStruct(h.shape, F32), *out_shapes]
        out_specs = [row(D), *out_specs]
    body = functools.partial(_in_proj_body, load=load, n_x=len(xs), epilogue=epilogue,
                             n_extra=len(extras), chunk=chunk, n_side=n_side)
    return pl.pallas_call(
        body,
        grid=grid,
        in_specs=[*x_specs, mod_spec,
                  pl.BlockSpec((1, D), lambda b, i: (0, 0)),
                  pl.BlockSpec((D, n), lambda b, i: (0, 0)),
                  *extra_specs],
        out_specs=out_specs,
        out_shape=out_shapes,
        input_output_aliases=aliases,
        compiler_params=_cparams(("parallel", "parallel")),
        name=name,
    )(*xs, mod, gn.reshape(1, D), w, *extras)


def _rows_out(shape2, row, outs):
    return ([jax.ShapeDtypeStruct(shape2 + (wd,), dt) for wd, dt in outs], [row(wd) for wd, _ in outs])


def _put(ref, lo, val, lead):
    ref[:, :, lo:lo + val.shape[1]] = val.reshape(lead + (val.shape[1],)).astype(ref.dtype)


def _lru_out_proj_body(z_ref, w_ref, ctx_ref, x_ref, mod_ref, o_ref, *, nct):
    z3 = z_ref[...]
    a, b, wd = z3.shape
    y = jnp.dot(z3.reshape(a * b, wd), w_ref[...], preferred_element_type=F32).reshape(a, b, D)
    gate = mod_ref[0, :, 2 * D:3 * D]
    resid = jnp.where(pl.program_id(1) < nct, ctx_ref[...], x_ref[...])
    o_ref[...] = resid + gate[:, None, :] * pltpu.einshape("tbd->btd", y)


def _lru_epilogue(j, acc, lead, extra, outs):
    _put(outs[j], 0, acc, lead)


def _lru_tile_index(s, nct, ntiles, reverse):
    if not reverse:
        return s
    return jnp.where(s < nct, nct - 1 - s, ntiles + nct - 1 - s)


def _lru_sweep_body(up_ref, uc_ref, un_ref, cw_ref, cb_ref, wg_ref, bg_ref, lam_ref, *rest,
                    nct, ntiles, reverse):
    if reverse:
        hf_ref, g_ref, o_ref, ext_ref, cv_ref, carry_ref = rest
    else:
        o_ref, ext_ref, cv_ref, carry_ref = rest
    s = pl.program_id(0)
    ti = _lru_tile_index(s, nct, ntiles, reverse)
    tt, nb, pw = uc_ref.shape

    @pl.when(s == 0)
    def _():
        carry_ref[...] = jnp.zeros_like(carry_ref)

    seg_start = (ti == 0) | (ti == nct)
    seg_end = (ti == nct - 1) | (ti == ntiles - 1)
    ext_ref[0:2] = jnp.where(seg_start, 0.0, up_ref[...])
    ext_ref[2:2 + tt] = uc_ref[...]
    ext_ref[2 + tt:3 + tt] = jnp.where(seg_end, 0.0, un_ref[...])

    rate = (-0.5 * LRU_C * LOG2E) * jax.nn.softplus(-lam_ref[...])
    tiles = [slice(k * LANES, (k + 1) * LANES) for k in range(pw // LANES)]
    for sl in tiles:
        u = cb_ref[:, sl]
        for tap in range(CONV_W):
            u = u + cw_ref[tap:tap + 1, sl] * ext_ref[tap:tap + tt, :, sl]
        cv_ref[:, sl] = u.reshape(tt * nb, LANES)
    for k, sl in enumerate(tiles):
        win = slice(_lru_window(k) * LANES, (_lru_window(k) + LRU_WIN) * LANES)
        pre = jnp.dot(cv_ref[:, win].astype(BF16), wg_ref[k], preferred_element_type=F32) + bg_ref[k]
        tr = jnp.tanh(pre[:, :LANES]) + 1.0
        ti = jnp.tanh(pre[:, LANES:]) + 1.0
        a = jnp.exp2(rate[:, sl] * tr)
        b = (0.5 * jnp.sqrt(1.0 - a * a)) * (ti * cv_ref[:, sl])
        h = carry_ref[:, sl]
        for q in range(tt):
            t = tt - 1 - q if reverse else q
            rows = slice(t * nb, (t + 1) * nb)
            h = a[rows] * h + b[rows]
            if reverse:
                o_ref[t, :, sl] = ((hf_ref[t, :, sl] + h) * _silu(g_ref[t, :, sl].astype(F32))).astype(o_ref.dtype)
            else:
                o_ref[t, :, sl] = h
        carry_ref[:, sl] = h


def _lru_sweep(u, cw, cb, wg, bg, lam, nct_rows, reverse, hf=None, g=None):
    rows, nb, pw = u.shape
    tt = LRU_TT
    ntiles = rows // tt
    nct = nct_rows // tt
    tile = functools.partial(_lru_tile_index, nct=nct, ntiles=ntiles, reverse=reverse)
    cur = lambda wd: pl.BlockSpec((tt, nb, wd), lambda s: (tile(s), 0, 0))
    const = lambda shape: pl.BlockSpec(shape, lambda s: (0,) * len(shape))
    in_specs = [
        pl.BlockSpec((2, nb, pw), lambda s: (jnp.maximum(tile(s) * (tt // 2) - 1, 0), 0, 0)),
        cur(pw),
        pl.BlockSpec((1, nb, pw), lambda s: (jnp.minimum((tile(s) + 1) * tt, rows - 1), 0, 0)),
        const((CONV_W, pw)), const((1, pw)),
        const(wg.shape), const(bg.shape), const((1, pw)),
    ]
    args = [u, u, u, cw, cb, wg, bg, lam]
    if reverse:
        in_specs += [cur(pw), cur(pw)]
        args += [hf, g]
    body = functools.partial(_lru_sweep_body, nct=nct, ntiles=ntiles, reverse=reverse)
    return pl.pallas_call(
        body,
        grid=(ntiles,),
        in_specs=in_specs,
        out_specs=cur(pw),
        out_shape=jax.ShapeDtypeStruct(u.shape, BF16 if reverse else F32),
        scratch_shapes=[pltpu.VMEM((tt + 3, nb, pw), F32),
                        pltpu.VMEM((tt * nb, pw), F32),
                        pltpu.VMEM((nb, pw), F32)],
        compiler_params=_cparams(("arbitrary",)),
        name="lru_bwd" if reverse else "lru_fwd",
    )(*args)


def _lru_window(k):
    first_block = (k * LANES) // LRU_BW
    return min((first_block * LRU_BW) // LANES, LRU_WIDTH // LANES - LRU_WIN)


def _lru_gate_weights(wa, wx, ba, bx):
    eye = 0.5 * jnp.eye(LRU_BLOCKS, dtype=BF16)

    def dense(w):
        return (w.astype(BF16)[:, :, None, :] * eye[:, None, :, None]).reshape(LRU_WIDTH, LRU_WIDTH)

    da, dx = dense(wa), dense(wx)
    wg, bg = [], []
    for k in range(LRU_WIDTH // LANES):
        rows = slice(_lru_window(k) * LANES, (_lru_window(k) + LRU_WIN) * LANES)
        cols = slice(k * LANES, (k + 1) * LANES)
        wg.append(jnp.concatenate([da[rows, cols], dx[rows, cols]], axis=1))
        bg.append(0.5 * jnp.concatenate([ba[cols], bx[cols]])[None])
    return jnp.stack(wg).astype(BF16), jnp.stack(bg)


def _load_time_major(ctx_ref, x_ref, *, nct):
    blk = jnp.where(pl.program_id(1) < nct, ctx_ref[...], x_ref[...])
    return pltpu.einshape("btd->tbd", blk)


def _rglru_layer(ctx, x, mod_t, gn, w_in, conv_w, conv_b, wa, ba, wx, bx, lam, w_out):
    nb, n_ctx, _ = ctx.shape
    rows = n_ctx + x.shape[1]
    tt = LRU_TT
    nct = n_ctx // tt
    layout = _tmajor(nb, rows, n_ctx, tt)
    grid, row, mod_spec = layout
    src_specs = [pl.BlockSpec((nb, tt, D), lambda b, i: (0, jnp.minimum(i, nct - 1), 0)),
                 pl.BlockSpec((nb, tt, D), lambda b, i: (0, jnp.maximum(i - nct, 0), 0))]
    load = functools.partial(_load_time_major, nct=nct)
    wd = LRU_WIDTH
    out_shapes, out_specs = _rows_out((rows, nb), row, [(wd, F32), (wd, BF16)])
    u, g = _in_proj((ctx, x), src_specs, load, mod_t, gn, w_in.astype(BF16), layout, out_shapes, out_specs,
                    _lru_epilogue, chunk=wd, name="lru_in_proj")
    cb = conv_b.reshape(1, wd)
    sweeps = []
    for d in range(2):
        wg, bg = _lru_gate_weights(wa[d], wx[d], ba[d], bx[d])
        sweeps.append((conv_w, cb, wg, bg, lam[d:d + 1]))
    hf = _lru_sweep(u, *sweeps[0], n_ctx, reverse=False)
    z = _lru_sweep(u, *sweeps[1], n_ctx, reverse=True, hf=hf, g=g)
    body = functools.partial(_lru_out_proj_body, nct=nct)
    return pl.pallas_call(
        body,
        grid=grid,
        in_specs=[row(wd), pl.BlockSpec((wd, D), lambda b, i: (0, 0)), *src_specs, mod_spec],
        out_specs=pl.BlockSpec((nb, tt, D), lambda b, i: (0, i, 0)),
        out_shape=jax.ShapeDtypeStruct((nb, rows, D), F32),
        compiler_params=_cparams(("parallel", "parallel")),
        name="lru_out_proj",
    )(z, w_out.astype(BF16), ctx, x, mod_t)


def _rope_tables(n_ctx, t_len):
    pos = jnp.arange(t_len)
    row = (pos // GRID_W).astype(F32)
    col = (pos % GRID_W).astype(F32)
    n_ax = HEAD_DIM // 4
    freqs = ROPE_BASE ** (-jnp.arange(n_ax, dtype=F32) / n_ax)
    ang = jnp.concatenate([row[:, None] * freqs, col[:, None] * freqs], axis=-1)
    cos, sin = jnp.cos(ang), jnp.sin(ang)
    cos_h = jnp.concatenate([cos, cos], axis=-1)
    sin_h = jnp.concatenate([-sin, sin], axis=-1)
    cos_f = jnp.concatenate([jnp.ones((n_ctx, HEAD_DIM), F32), cos_h], axis=0)
    sin_f = jnp.concatenate([jnp.zeros((n_ctx, HEAD_DIM), F32), sin_h], axis=0)
    return jnp.tile(cos_f, (1, 2)), jnp.tile(sin_f, (1, 2))


def _rope(x, cos, sin):
    lane = lax.broadcasted_iota(jnp.int32, x.shape, 1)
    swapped = jnp.where(lane % HEAD_DIM < HEAD_DIM // 2,
                        pltpu.roll(x, LANES - HEAD_DIM // 2, 1), pltpu.roll(x, HEAD_DIM // 2, 1))
    return x * cos + swapped * sin


def _swa_epilogue(j, acc, lead, extra, outs):
    cos_ref, sin_ref = extra
    q_ref, qr_ref, k_ref, v_ref, g_ref = outs
    qd = SWA_HEADS * HEAD_DIM
    kvd = SWA_KV_HEADS * HEAD_DIM
    grp = SWA_HEADS // SWA_KV_HEADS
    width = acc.shape[1]
    lo = j * width
    cos, sin = cos_ref[...], sin_ref[...]

    def roped(x):
        return jnp.concatenate([_rope(x[:, s:s + LANES], cos, sin) for s in range(0, x.shape[1], LANES)], axis=1)

    def head(x, hh):
        return x[:, hh * HEAD_DIM:(hh + 1) * HEAD_DIM]

    if lo < qd:
        q = acc * QK_SCALE
        qr = roped(q)
        for hh in range(width // HEAD_DIM):
            kv, gq = divmod(lo // HEAD_DIM + hh, grp)
            for rt in range(lead[1] // BLOCK_Q):
                rows = slice(rt * BLOCK_Q, (rt + 1) * BLOCK_Q)
                dst = slice(gq * BLOCK_Q, (gq + 1) * BLOCK_Q)
                q_ref[0, kv, rt, dst, :] = head(q, hh)[rows].astype(q_ref.dtype)
                qr_ref[0, kv, rt, dst, :] = head(qr, hh)[rows].astype(qr_ref.dtype)
    elif lo == qd:
        kr = roped(acc[:, :kvd])
        for kv in range(SWA_KV_HEADS):
            k_ref[0, kv] = head(kr, kv).astype(k_ref.dtype)
            v_ref[0, kv] = head(acc[:, kvd:], kv).T.astype(v_ref.dtype)
    else:
        _put(g_ref, lo - qd - 2 * kvd, acc, lead)


def _swa_mask_table():
    grp = SWA_HEADS // SWA_KV_HEADS
    kj = np.arange(3 * BLOCK_Q)[:, None]
    qi = np.arange(BLOCK_Q)[None, :]
    base = np.where(np.abs(kj - BLOCK_Q - qi) <= WINDOW, 0.0, NEG_INF).astype(np.float32)
    first = np.where(kj < BLOCK_Q, NEG_INF, base).astype(np.float32)
    last = np.where(kj >= 2 * BLOCK_Q, NEG_INF, base).astype(np.float32)
    tab = np.stack([np.full_like(base, NEG_INF), first, base, last])
    return jnp.asarray(np.tile(tab, (1, 1, grp)))


def _swa_body(sink_ref, *refs):
    nq, nband = SWA_QTILES, SWA_QTILES + 2
    mask_refs, refs = refs[:nq], refs[nq:]
    (q_ref, qr_ref, kc_ref), refs = refs[:3], refs[3:]
    kb_refs, refs = refs[:nband], refs[nband:]
    vc_ref, refs = refs[0], refs[1:]
    vb_refs, (g_ref, o_ref, s_ref, p_ref) = refs[:nband], refs[nband:]
    grp = SWA_HEADS // SWA_KV_HEADS
    dims = (((1,), (1,)), ((), ()))
    nc = kc_ref.shape[2]
    units = [(t, kv) for t in range(nq) for kv in range(SWA_KV_HEADS)]
    sink_p = {}

    def scores(u):
        t, kv = units[u]
        slot = u % SWA_SLOTS
        kb = jnp.concatenate([kb_refs[t + i][0, kv] for i in range(3)], axis=0)
        s_ref[slot, :nc, :] = lax.dot_general(kc_ref[0, kv], q_ref[0, kv, t], dims, preferred_element_type=F32)
        s_ref[slot, nc:, :] = (lax.dot_general(kb, qr_ref[0, kv, t], dims, preferred_element_type=F32)
                               + mask_refs[t][0])

    def softmax(u):
        _, kv = units[u]
        slot = u % SWA_SLOTS
        sink_p[u] = []
        for gq in range(grp):
            lanes = slice(gq * BLOCK_Q, (gq + 1) * BLOCK_Q)
            s = s_ref[slot, :, lanes]
            sink = sink_ref[kv, :, lanes]
            m = jnp.maximum(jnp.max(s, axis=0, keepdims=True), sink)
            sink_p[u].append(jnp.exp2(sink - m))
            p_ref[slot, :, lanes] = jnp.exp2(s - m).astype(BF16)

    def values(u):
        t, kv = units[u]
        vt = jnp.concatenate([vc_ref[0, kv]] + [vb_refs[t + i][0, kv] for i in range(3)], axis=1)
        vt = jnp.concatenate([vt, jnp.ones((ONES_ROWS, vt.shape[1]), BF16)], axis=0)
        ot = jnp.dot(vt, p_ref[u % SWA_SLOTS], preferred_element_type=F32)
        denom = ot[HEAD_DIM:HEAD_DIM + 1] + jnp.concatenate(sink_p[u], axis=1)
        ot = ot[:HEAD_DIM] * (1.0 / denom)
        slab = jnp.concatenate([ot[:, gq * BLOCK_Q:(gq + 1) * BLOCK_Q].T for gq in range(grp)], axis=1)
        rows = slice(t * BLOCK_Q, (t + 1) * BLOCK_Q)
        lanes = slice(kv * grp * HEAD_DIM, (kv + 1) * grp * HEAD_DIM)
        o_ref[0, rows, lanes] = (slab * _silu(g_ref[0, rows, lanes].astype(F32))).astype(o_ref.dtype)

    scores(0)
    scores(1)
    for u in range(len(units)):
        softmax(u)
        if u + 2 < len(units):
            scores(u + 2)
        values(u)


def _swa_attention(q, qr, k, vt, g, sink, n_ctx):
    nb, nkv, nt, qrows, hd = q.shape
    rows = k.shape[2]
    grp = SWA_HEADS // SWA_KV_HEADS
    nct = n_ctx // BLOCK_Q
    nkeys = n_ctx + 3 * BLOCK_Q
    nq = SWA_QTILES
    assert nt % nq == 0
    qspec = pl.BlockSpec((1, nkv, nq, qrows, hd), lambda b, j: (b, 0, j, 0, 0))
    kctx = pl.BlockSpec((1, nkv, n_ctx, hd), lambda b, j: (b, 0, 0, 0))
    vctx = pl.BlockSpec((1, nkv, hd, n_ctx), lambda b, j: (b, 0, 0, 0))
    rowspec = pl.BlockSpec((1, nq * BLOCK_Q, g.shape[2]), lambda b, j: (b, j, 0))

    def tile(j, off):
        return jnp.clip(j * nq - 1 + off, nct, nt - 1)

    def kband(off):
        return pl.BlockSpec((1, nkv, BLOCK_Q, hd), lambda b, j: (b, 0, tile(j, off), 0))

    def vband(off):
        return pl.BlockSpec((1, nkv, hd, BLOCK_Q), lambda b, j: (b, 0, 0, tile(j, off)))

    def mask(t):
        def variant(b, j):
            n = j * nq + t
            return (jnp.where(n < nct, 0, jnp.where(n == nct, 1, jnp.where(n == nt - 1, 3, 2))), 0, 0)
        return pl.BlockSpec((1, 3 * BLOCK_Q, qrows), variant)

    sink_rows = jnp.repeat(sink.astype(F32).reshape(nkv, grp) * LOG2E, BLOCK_Q, axis=1).reshape(nkv, 1, qrows)
    masks = _swa_mask_table()
    return pl.pallas_call(
        _swa_body,
        grid=(nb, nt // nq),
        in_specs=[pl.BlockSpec((nkv, 1, qrows), lambda b, j: (0, 0, 0)),
                  *[mask(t) for t in range(nq)], qspec, qspec,
                  kctx, *[kband(i) for i in range(nq + 2)], vctx, *[vband(i) for i in range(nq + 2)], rowspec],
        out_specs=rowspec,
        out_shape=jax.ShapeDtypeStruct((nb, rows, g.shape[2]), BF16),
        scratch_shapes=[pltpu.VMEM((SWA_SLOTS, nkeys, qrows), F32), pltpu.VMEM((SWA_SLOTS, nkeys, qrows), BF16)],
        compiler_params=_cparams(("parallel", "parallel")),
        name="swa_attention",
    )(sink_rows, *[masks] * nq, q, qr, k, *[k] * (nq + 2), vt, *[vt] * (nq + 2), g)


def _swa_layer(h, mod_b, gn, w_in, sink, n_ctx):
    nb, rows, _ = h.shape
    tm = ROW_TILE
    layout = _bmajor(nb, rows, n_ctx, tm)
    _, row, _ = layout
    qd = SWA_HEADS * HEAD_DIM
    grp = SWA_HEADS // SWA_KV_HEADS
    nkv = SWA_KV_HEADS
    cos, sin = _rope_tables(n_ctx, rows - n_ctx)
    tab = pl.BlockSpec((tm, LANES), lambda b, i: (i, 0))
    q_shape = jax.ShapeDtypeStruct((nb, nkv, rows // BLOCK_Q, grp * BLOCK_Q, HEAD_DIM), BF16)
    q_spec = pl.BlockSpec((1, nkv, tm // BLOCK_Q, grp * BLOCK_Q, HEAD_DIM), lambda b, i: (b, 0, i, 0, 0))
    k_shape = jax.ShapeDtypeStruct((nb, nkv, rows, HEAD_DIM), BF16)
    k_spec = pl.BlockSpec((1, nkv, tm, HEAD_DIM), lambda b, i: (b, 0, i, 0))
    vt_shape = jax.ShapeDtypeStruct((nb, nkv, HEAD_DIM, rows), BF16)
    vt_spec = pl.BlockSpec((1, nkv, HEAD_DIM, tm), lambda b, i: (b, 0, 0, i))
    q, qr, k, vt, g = _in_proj(
        (h,), [row(D)], _load_block, mod_b, gn, w_in.astype(BF16), layout,
        [q_shape, q_shape, k_shape, vt_shape, jax.ShapeDtypeStruct((nb, rows, qd), BF16)],
        [q_spec, q_spec, k_spec, vt_spec, row(qd)], _swa_epilogue,
        extras=(cos, sin), extra_specs=(tab, tab), name="swa_in_proj")
    return _swa_attention(q, qr, k, vt, g, sink, n_ctx)


def _na_epilogue(j, acc, lead, extra, outs):
    q_ref, k_ref, v_ref, g_ref = outs
    wd = NA_HEADS * HEAD_DIM
    width = acc.shape[1]
    lo = j * width
    which, off = lo // wd, lo % wd
    if which == 3:
        _put(g_ref, off, acc, lead)
        return
    ref = (q_ref, k_ref, v_ref)[which]
    val = acc * QK_SCALE if which == 0 else acc
    for hh in range(width // HEAD_DIM):
        ref[0, off // HEAD_DIM + hh] = val[:, hh * HEAD_DIM:(hh + 1) * HEAD_DIM].astype(ref.dtype)


def _na_bias_tiles(rpb):
    reach = GRID_W - NA_COLS
    period = 2 * GRID_W
    ndy = 2 * NA_ROWS - 1
    edge = jnp.pad(rpb.astype(F32) * LOG2E, ((0, 0), (0, 0), (reach, reach)), mode="edge")
    flat = jnp.tile(jnp.pad(edge[:, :, ::-1], ((0, 0), (0, 0), (0, 1))), (1, 1, GRID_W))[:, :, :GRID_W * (period - 1)]
    e = flat.reshape(NA_HEADS, ndy, GRID_W, period - 1)[:, :, :, GRID_W - 1:]
    col = np.arange(GRID_W)
    cstart = np.clip(col - NA_COLS // 2, 0, GRID_W - NA_COLS)
    col_ok = (col[:, None] >= cstart[None, :]) & (col[:, None] < cstart[None, :] + NA_COLS)
    e = jnp.where(jnp.asarray(col_ok)[None, None], e, NEG_INF)
    e = jnp.pad(e, ((0, 0), (0, 1), (0, 0), (0, 0)), constant_values=NEG_INF)
    shape = (NA_HEADS // NA_HSTEP, NA_HSTEP, ndy + 1, GRID_W, period)
    left = jnp.pad(e, ((0, 0), (0, 0), (0, 0), (0, GRID_W)), constant_values=NEG_INF).reshape(shape)
    right = jnp.pad(e, ((0, 0), (0, 0), (0, 0), (GRID_W, 0)), constant_values=NEG_INF).reshape(shape)
    return left, right


def _na_body(q_ref, k_ref, v_ref, g_ref, bl_ref, br_ref, o_ref, s_ref, p_ref, *, n_ctx, grid_rows):
    nk = NA_KROWS * GRID_W
    nq = NA_QROWS * GRID_W
    a_bt = (((1,), (1,)), ((), ()))
    at_b = (((0,), (0,)), ((), ()))
    left_half = lax.broadcasted_iota(jnp.int32, (GRID_W, 2 * GRID_W), 1) < GRID_W
    ones = jnp.ones((n_ctx + nk, HEAD_DIM), BF16)

    def block_geometry(qb):
        rb = pl.program_id(2) * NA_QBLOCKS + qb
        r0 = (rb - 1) * NA_QROWS
        kb = jnp.clip(r0 - NA_ROWS // 2, 0, grid_rows - NA_KROWS)

        def tile_index(qr, kr):
            qrow, krow = r0 + qr, kb + kr
            first = jnp.clip(qrow - NA_ROWS // 2, 0, grid_rows - NA_ROWS)
            inside = (rb >= 1) & (krow >= first) & (krow < first + NA_ROWS)
            return jnp.where(inside, krow - qrow + NA_ROWS - 1, NA_MASKED)

        idx = [[tile_index(qr, kr) for kr in range(NA_KROWS)] for qr in range(NA_QROWS)]
        return pl.multiple_of(n_ctx + kb * GRID_W, GRID_W), idx

    geometry = [block_geometry(qb) for qb in range(NA_QBLOCKS)]
    units = [(qb, hl) for qb in range(NA_QBLOCKS) for hl in range(NA_HSTEP)]
    outs = {}

    def scores(u):
        qb, hl = units[u]
        start, idx = geometry[qb]
        slot = u % NA_SLOTS
        qh = q_ref[0, hl, qb * nq:(qb + 1) * nq, :]
        bias = jnp.concatenate(
            [jnp.concatenate([jnp.where(left_half, bl_ref[0, hl, idx[2 * m][kr]], br_ref[0, hl, idx[2 * m + 1][kr]])
                              for m in range(NA_QROWS // 2)], axis=1) for kr in range(NA_KROWS)], axis=0)
        s_ref[slot, :n_ctx, :] = lax.dot_general(k_ref[0, hl, 0:n_ctx, :], qh, a_bt, preferred_element_type=F32)
        s_ref[slot, n_ctx:, :] = (lax.dot_general(k_ref[0, hl, pl.ds(start, nk), :], qh, a_bt,
                                                  preferred_element_type=F32) + bias)

    def softmax(u):
        slot = u % NA_SLOTS
        for c in range(nq // LANES):
            lanes = slice(c * LANES, (c + 1) * LANES)
            s = s_ref[slot, :, lanes]
            p_ref[slot, :, lanes] = jnp.exp2(s - jnp.max(s, axis=0, keepdims=True)).astype(BF16)

    def values(u):
        qb, hl = units[u]
        start, _ = geometry[qb]
        v1 = jnp.concatenate(
            [jnp.concatenate([v_ref[0, hl, 0:n_ctx, :], v_ref[0, hl, pl.ds(start, nk), :]], axis=0), ones], axis=1)
        ot = lax.dot_general(v1, p_ref[u % NA_SLOTS], at_b, preferred_element_type=F32)
        outs[hl] = (ot[:HEAD_DIM] * (1.0 / ot[HEAD_DIM:HEAD_DIM + 1])).T
        if hl == NA_HSTEP - 1:
            rows = slice(qb * nq, (qb + 1) * nq)
            slab = jnp.concatenate([outs[h] for h in range(NA_HSTEP)], axis=1)
            o_ref[0, rows, :] = (slab * _silu(g_ref[0, rows, :].astype(F32))).astype(o_ref.dtype)

    scores(0)
    scores(1)
    for u in range(len(units)):
        softmax(u)
        if u + 2 < len(units):
            scores(u + 2)
        values(u)


def _na_attention(q, k, v, g, bias_left, bias_right, n_ctx):
    nb, nh, rows, hd = q.shape
    grid_rows = (rows - n_ctx) // GRID_W
    nq = NA_QROWS * GRID_W
    assert n_ctx == nq, "context tokens form exactly one query block"
    assert rows % (NA_QBLOCKS * nq) == 0
    nkeys = n_ctx + NA_KROWS * GRID_W
    lanes = NA_HSTEP * HEAD_DIM
    qspec = pl.BlockSpec((1, NA_HSTEP, NA_QBLOCKS * nq, hd), lambda p, b, r: (b, p, r, 0))
    kvspec = pl.BlockSpec((1, NA_HSTEP, rows, hd), lambda p, b, r: (b, p, 0, 0))
    rowspec = pl.BlockSpec((1, NA_QBLOCKS * nq, lanes), lambda p, b, r: (b, r, p))
    bspec = pl.BlockSpec((1,) + bias_left.shape[1:], lambda p, b, r: (p, 0, 0, 0, 0))
    body = functools.partial(_na_body, n_ctx=n_ctx, grid_rows=grid_rows)
    return pl.pallas_call(
        body,
        grid=(nh // NA_HSTEP, nb, rows // (NA_QBLOCKS * nq)),
        in_specs=[qspec, kvspec, kvspec, rowspec, bspec, bspec],
        out_specs=rowspec,
        out_shape=jax.ShapeDtypeStruct(g.shape, BF16),
        scratch_shapes=[pltpu.VMEM((NA_SLOTS, nkeys, nq), F32), pltpu.VMEM((NA_SLOTS, nkeys, nq), BF16)],
        compiler_params=_cparams(("parallel", "parallel", "parallel")),
        name="na_attention",
    )(q, k, v, g, bias_left, bias_right)


def _na_layer(h, prev, mod_b, gn, w_in, rpb, n_ctx):
    nb, rows, _ = h.shape
    tm = ROW_TILE
    layout = _bmajor(nb, rows, n_ctx, tm)
    _, row, _ = layout
    wd = NA_HEADS * HEAD_DIM
    head_shape = jax.ShapeDtypeStruct((nb, NA_HEADS, rows, HEAD_DIM), BF16)
    head_spec = pl.BlockSpec((1, NA_HEADS, tm, HEAD_DIM), lambda b, i: (b, 0, i, 0))
    h, q, k, v, g = _in_proj((h,), [row(D)], _load_block, mod_b, gn, w_in.astype(BF16), layout,
                             [head_shape] * 3 + [jax.ShapeDtypeStruct((nb, rows, wd), BF16)],
                             [head_spec] * 3 + [row(wd)], _na_epilogue, name="swa_out_na_in_proj", prev=prev)
    bias_left, bias_right = _na_bias_tiles(rpb)
    return h, _na_attention(q, k, v, g, bias_left, bias_right, n_ctx)


def _s5_epilogue(j, acc, lead, extra, outs):
    u_ref, g_ref = outs
    lo = j * acc.shape[1]
    if lo < S5_WIDTH:
        _put(u_ref, lo, acc, lead)
    else:
        _put(g_ref, lo - S5_WIDTH, acc, lead)


def _s5_matrices(a_re, a_im, log_dt, b_re, b_im, c_re, c_im, d_skip):
    L = S5_CHUNK
    lam = lax.complex(a_re.astype(F32), a_im.astype(F32))
    lam_dt = lam * jnp.exp(log_dt.astype(F32))[..., None]
    lam_bar = jnp.exp(lam_dt)
    b_bar = ((lam_bar - 1.0) / lam)[..., None] * lax.complex(b_re.astype(F32), b_im.astype(F32))
    c_mat = lax.complex(c_re.astype(F32), c_im.astype(F32))
    tau = np.arange(L)

    def power(expo, d):
        return jnp.exp(jnp.asarray(expo, F32).reshape(expo.shape + (1, 1)) * lam_dt[d])

    w = jnp.stack([c_mat[d][None] * power(tau, d)[:, :, None, :] for d in range(2)])
    w2 = jnp.concatenate([jnp.real(w), -jnp.imag(w)], axis=-1)
    b2 = jnp.concatenate([jnp.real(b_bar), jnp.imag(b_bar)], axis=-2)
    kern = jnp.einsum('dkgip,dgpj->dkgij', w2, b2).reshape(2 * L, S5_GROUPS, S5_GROUP, S5_GROUP)
    lag = tau[None, :] - tau[:, None]
    sel = np.concatenate([lag[None] == tau[:, None, None], -lag[None] == tau[:, None, None]]).astype(np.float32)
    both = jnp.einsum('ktu,kgij->tugij', jnp.asarray(sel), kern)
    eye_t = jnp.eye(L, dtype=F32)[:, :, None, None, None]
    skip = jnp.eye(S5_GROUP, dtype=F32)[None] * d_skip.astype(F32).reshape(S5_GROUPS, S5_GROUP, 1)
    both = both + eye_t * skip[None, None]
    toep = both.transpose(2, 0, 4, 1, 3).reshape(S5_GROUPS, L * S5_GROUP, L * S5_GROUP)

    def lanes(z):
        pad = [(0, 0)] * (z.ndim - 1) + [(0, LANES - S5_STATE)]
        return jnp.concatenate([jnp.pad(jnp.real(z), pad), jnp.pad(jnp.imag(z), pad)], axis=-1)

    pf = power(L - 1 - tau, 0)[:, :, :, None] * b_bar[0][None]
    pb = power(tau, 1)[:, :, :, None] * b_bar[1][None]
    pmat = jnp.concatenate([lanes(pf.transpose(1, 0, 3, 2)), lanes(pb.transpose(1, 0, 3, 2))], axis=-1)
    pmat = pmat.reshape(S5_GROUPS, L * S5_GROUP, 4 * LANES)
    qf = c_mat[0][None] * power(tau + 1, 0)[:, :, None, :]
    qb = c_mat[1][None] * power(L - tau, 1)[:, :, None, :]

    def state_rows(z):
        pad = ((0, 0), (0, LANES - S5_STATE), (0, 0), (0, 0))
        zr = jnp.pad(jnp.real(z).transpose(1, 3, 0, 2), pad)
        zi = jnp.pad(-jnp.imag(z).transpose(1, 3, 0, 2), pad)
        return jnp.concatenate([zr, zi], axis=1).reshape(S5_GROUPS, 2 * LANES, L * S5_GROUP)

    qmat = jnp.concatenate([state_rows(qf), state_rows(qb)], axis=1)
    lam_l = jnp.exp(float(L) * lam_dt)
    pad = ((0, 0), (0, LANES - S5_STATE))
    coef = jnp.stack([jnp.pad(jnp.real(lam_l[0]), pad), jnp.pad(jnp.imag(lam_l[0]), pad),
                      jnp.pad(jnp.real(lam_l[1]), pad), jnp.pad(jnp.imag(lam_l[1]), pad)], axis=1)
    coef = jnp.pad(coef, ((0, 0), (0, 4), (0, 0)))
    return toep.astype(BF16), pmat.astype(BF16), qmat.astype(BF16), coef


def _atom_transpose(xs, atom):
    n = len(xs)
    group = lax.broadcasted_iota(jnp.int32, xs[0].shape, 1) // atom
    xs = list(xs)
    d = n // 2
    while d:
        keep = (group & d) == 0
        for i in range(n):
            if not i & d:
                lo, hi = xs[i], xs[i + d]
                xs[i] = jnp.where(keep, lo, pltpu.roll(hi, d * atom, 1))
                xs[i + d] = jnp.where(keep, pltpu.roll(lo, LANES - d * atom, 1), hi)
        d //= 2
    return xs


def _s5_body(u_ref, toep_ref, p_ref, q_ref, coef_ref, y_ref, u_scr, y_scr, s_ref, hin_ref, *, nb, nc_ctx, nc):
    nwin = LANES // S5_GROUP
    nrow, nlat = u_ref.shape[1], y_ref.shape[1]
    blk = S5_ROWS
    for half in range(S5_CHUNK // nwin):
        cols = slice(half * LANES, (half + 1) * LANES)
        for r in range(0, nrow, blk):
            words = [pltpu.bitcast(u_ref[half * nwin + w, r:r + blk, :], jnp.uint32) for w in range(nwin)]
            for gl, x in enumerate(_atom_transpose(words, S5_GROUP)):
                u_scr[gl, r:r + blk, cols] = pltpu.bitcast(x, BF16)
    for gl in range(S5_SLAB):
        _s5_group(gl, u_scr, toep_ref, p_ref, q_ref, coef_ref, y_scr, s_ref, hin_ref, nb, nc_ctx, nc)
    for half in range(S5_CHUNK // nwin):
        cols = slice(half * LANES, (half + 1) * LANES)
        for r in range(0, nlat, blk):
            ys = [y_scr[gl, r:r + blk, cols] for gl in range(S5_SLAB)]
            for w, x in enumerate(_atom_transpose(ys, S5_GROUP)):
                y_ref[half * nwin + w, r:r + blk, :] = x


def _s5_group(gl, u_scr, toep_ref, p_ref, q_ref, coef_ref, y_scr, s_ref, hin_ref, nb, nc_ctx, nc):
    u = u_scr[gl]
    s_ref[...] = jnp.dot(u, p_ref[gl], preferred_element_type=F32)
    coef = coef_ref[gl]
    shape = (nb, LANES)
    lr_f, li_f = jnp.broadcast_to(coef[0:1], shape), jnp.broadcast_to(coef[1:2], shape)
    lr_b, li_b = jnp.broadcast_to(coef[2:3], shape), jnp.broadcast_to(coef[3:4], shape)
    zero = jnp.zeros(shape, F32)

    def step(q, carry):
        fr, fi, br, bi = carry
        rows_f = pl.ds(pl.multiple_of(q * nb, nb), nb)
        cb = jnp.where(q < nc_ctx, nc_ctx - 1 - q, nc + nc_ctx - 1 - q)
        rows_b = pl.ds(pl.multiple_of(cb * nb, nb), nb)
        hin_ref[rows_f, 0:LANES] = fr
        hin_ref[rows_f, LANES:2 * LANES] = fi
        hin_ref[rows_b, 2 * LANES:3 * LANES] = br
        hin_ref[rows_b, 3 * LANES:4 * LANES] = bi
        sfr, sfi = s_ref[rows_f, 0:LANES], s_ref[rows_f, LANES:2 * LANES]
        sbr, sbi = s_ref[rows_b, 2 * LANES:3 * LANES], s_ref[rows_b, 3 * LANES:4 * LANES]
        return (lr_f * fr - li_f * fi + sfr, lr_f * fi + li_f * fr + sfi,
                lr_b * br - li_b * bi + sbr, lr_b * bi + li_b * br + sbi)

    lax.fori_loop(0, nc, step, (zero, zero, zero, zero), unroll=True)

    lat = slice(nc_ctx * nb, nc * nb)
    y_scr[gl] = (jnp.dot(u[lat], toep_ref[gl], preferred_element_type=F32)
                 + jnp.dot(hin_ref[lat, :].astype(BF16), q_ref[gl], preferred_element_type=F32))


def _s5_scan(u_t, toep, pmat, qmat, coef, nb, n_ctx):
    L, nrow, wd = u_t.shape
    pk = L * S5_GROUP
    nc = nrow // nb
    nc_ctx = n_ctx // L
    nlat = (nc - nc_ctx) * nb
    assert nrow % S5_ROWS == 0 and nlat % S5_ROWS == 0
    once = pl.Buffered(1)
    slab = lambda shape: pl.BlockSpec((S5_SLAB,) + shape, lambda s: (s, 0, 0), pipeline_mode=once)
    body = functools.partial(_s5_body, nb=nb, nc_ctx=nc_ctx, nc=nc)
    return pl.pallas_call(
        body,
        grid=(wd // LANES,),
        in_specs=[pl.BlockSpec((L, nrow, LANES), lambda s: (0, 0, s)),
                  slab((pk, pk)), slab((pk, 4 * LANES)), slab((4 * LANES, pk)), slab((8, LANES))],
        out_specs=pl.BlockSpec((L, nlat, LANES), lambda s: (0, 0, s)),
        out_shape=jax.ShapeDtypeStruct((L, nlat, wd), F32),
        scratch_shapes=[pltpu.VMEM((S5_SLAB, nrow, pk), BF16), pltpu.VMEM((S5_SLAB, nlat, pk), F32),
                        pltpu.VMEM((nrow, 4 * LANES), F32), pltpu.VMEM((nrow, 4 * LANES), F32)],
        compiler_params=_cparams(("parallel",)),
        name="s5_scan",
    )(u_t, toep, pmat, qmat, coef)


def _rms(x, g):
    return x * lax.rsqrt(jnp.mean(x * x, axis=-1, keepdims=True) + EPS) * g


def _s5_readout_body(y_ref, ga_ref, gb_ref, ha_ref, hb_ref, gw_ref, gbias_ref, w_ref, mod_ref, nf_ref, o_ref):
    sub = ga_ref.shape[1]
    for i, (g_ref, h_ref) in enumerate(((ga_ref, ha_ref), (gb_ref, hb_ref))):
        rows = slice(i * sub, (i + 1) * sub)
        y = jax.nn.gelu(y_ref[0, rows, :])
        t = y * _sigmoid(jnp.dot(y.astype(BF16), gw_ref[...], preferred_element_type=F32) + gbias_ref[...])
        z = (t * _silu(g_ref[0].astype(F32))).astype(BF16)
        out = jnp.dot(z, w_ref[...], preferred_element_type=F32)
        h = h_ref[0] + mod_ref[0, :, 2 * D:3 * D] * out
        o_ref[0, rows, :] = _rms(h, nf_ref[...])


def _s5_readout(y, g, glu_w, glu_b, w_out, h, mod_b, norm_f, n_ctx):
    nb, t_len, _ = y.shape
    sub = n_ctx
    tm = 2 * sub
    lat = lambda wd: pl.BlockSpec((1, tm, wd), lambda b, i: (b, i, 0))
    part = lambda wd, j: pl.BlockSpec((1, sub, wd), lambda b, i: (b, 2 * i + 1 + j, 0))
    const = lambda shape: pl.BlockSpec(shape, lambda b, i: (0, 0))
    return pl.pallas_call(
        _s5_readout_body,
        grid=(nb, t_len // tm),
        in_specs=[lat(S5_WIDTH), part(S5_WIDTH, 0), part(S5_WIDTH, 1), part(D, 0), part(D, 1),
                  const((S5_WIDTH, S5_WIDTH)), const((1, S5_WIDTH)), const((S5_WIDTH, D)),
                  pl.BlockSpec((1, 1, 3 * D), lambda b, i: (b, 0, 0)), const((1, D))],
        out_specs=lat(D),
        out_shape=jax.ShapeDtypeStruct((nb, t_len, D), F32),
        compiler_params=_cparams(("parallel", "parallel")),
        name="s5_readout",
    )(y, g, g, h, h, glu_w.astype(BF16), glu_b.reshape(1, S5_WIDTH), w_out.astype(BF16), mod_b,
      norm_f.reshape(1, D))


def _s5_layer(h, prev, mod_b, mod_lat, gn, w_in, a_re, a_im, log_dt, b_re, b_im, c_re, c_im, d_skip, glu_w, glu_b,
              w_out, norm_f, n_ctx):
    nb, rows, _ = h.shape
    L = S5_CHUNK
    layout = _bmajor(nb, rows, n_ctx, ROW_TILE)
    _, row, _ = layout
    out_shapes, out_specs = _rows_out((nb, rows), row, [(S5_WIDTH, BF16)] * 2)
    h, u, g = _in_proj((h,), [row(D)], _load_block, mod_b, gn, w_in.astype(BF16), layout, out_shapes, out_specs,
                       _s5_epilogue, name="na_out_s5_in_proj", prev=prev)
    nc = rows // L
    u_t = u.reshape(nb, nc, L, S5_WIDTH).transpose(2, 1, 0, 3).reshape(L, nc * nb, S5_WIDTH)
    toep, pmat, qmat, coef = _s5_matrices(a_re, a_im, log_dt, b_re, b_im, c_re, c_im, d_skip)
    y_t = _s5_scan(u_t, toep, pmat, qmat, coef, nb, n_ctx)
    t_len = rows - n_ctx
    y = y_t.reshape(L, t_len // L, nb, S5_WIDTH).transpose(2, 1, 0, 3).reshape(nb, t_len, S5_WIDTH)
    return _s5_readout(y, g, glu_w, glu_b, w_out, h, mod_lat, norm_f, n_ctx)


def kernel(x, c, ctx, c_ctx, ada_w0, ada_b0, norm0, w_in0, conv_w0, conv_b0, lru_wa0, lru_ba0, lru_wx0, lru_bx0, lru_lam0, w_out0, ada_w1, ada_b1, norm1, w_in1, sink1, w_out1, ada_w2, ada_b2, norm2, w_in2, rpb2, w_out2, ada_w3, ada_b3, norm3, w_in3, s5_a_re3, s5_a_im3, s5_log_dt3, s5_b_re3, s5_b_im3, s5_c_re3, s5_c_im3, s5_d3, glu_w3, glu_b3, w_out3, norm_f):
    nb, t_len, _ = x.shape
    n_ctx = ctx.shape[1]
    cvec = jnp.concatenate([c, c_ctx[None, :], jnp.zeros((16 - nb - 1, D), F32)], axis=0)
    mods = [_modulation(cvec, w, b) for w, b in
            ((ada_w0, ada_b0), (ada_w1, ada_b1), (ada_w2, ada_b2), (ada_w3, ada_b3))]
    mod_b = [_bmajor_mod_table(m, nb, n_ctx + t_len, n_ctx) for m in mods]

    mod_t0 = jnp.stack([jnp.broadcast_to(mods[0][nb], (nb, 3 * D)), mods[0][:nb]], axis=0)
    h = _rglru_layer(ctx, x, mod_t0, norm0, w_in0, conv_w0, conv_b0, lru_wa0, lru_ba0, lru_wx0, lru_bx0,
                     lru_lam0, w_out0)
    z = _swa_layer(h, mod_b[1], norm1, w_in1, sink1, n_ctx)
    h, z = _na_layer(h, (z, w_out1.astype(BF16), mod_b[1]), mod_b[2], norm2, w_in2, rpb2, n_ctx)
    return _s5_layer(h, (z, w_out2.astype(BF16), mod_b[2]), mod_b[3], mods[3][:nb, None, :], norm3, w_in3, s5_a_re3, s5_a_im3, s5_log_dt3,
                     s5_b_re3, s5_b_im3, s5_c_re3, s5_c_im3, s5_d3, glu_w3, glu_b3, w_out3, norm_f, n_ctx)
```

```python
import functools
import math

import jax
import jax.numpy as jnp
import numpy as np
from jax import lax
from jax.experimental import pallas as pl
from jax.experimental.pallas import tpu as pltpu

F32 = jnp.float32
BF16 = jnp.bfloat16

D = 1024
HEAD_DIM = 64
GRID_W = 64
EPS = 1e-6
NEG_INF = -1e30
ROPE_BASE = 10000.0
LANES = 128
LOG2E = math.log2(math.e)
QK_SCALE = HEAD_DIM ** -0.5 * LOG2E
BF16_SUBLANES = 16
ONES_ROWS = BF16_SUBLANES
COND_ROWS = BF16_SUBLANES

LRU_WIDTH = 1408
LRU_BLOCKS = 16
LRU_BW = LRU_WIDTH // LRU_BLOCKS
LRU_WIN = 3
CONV_W = 4
LRU_C = 8.0
LRU_TT = 64

SWA_HEADS = 16
SWA_KV_HEADS = 4
WINDOW = 128
BLOCK_Q = 128
SWA_QTILES = 6
SWA_SLOTS = 4

NA_HEADS = 16
NA_ROWS = 8
NA_COLS = 16
NA_QROWS = 4
NA_KROWS = 12
NA_HSTEP = 8
NA_SLOTS = 4
NA_QBLOCKS = 3
NA_MASKED = 2 * NA_ROWS - 1

S5_WIDTH = 1024
S5_GROUP = 16
S5_GROUPS = S5_WIDTH // S5_GROUP
S5_STATE = 64
S5_CHUNK = 16
S5_SLAB = LANES // S5_GROUP
S5_ROWS = 128

ROW_TILE = 768
VMEM_LIMIT = 56 * 1024 * 1024


def _cparams(sem):
    return pltpu.CompilerParams(dimension_semantics=sem, vmem_limit_bytes=VMEM_LIMIT)


def _sigmoid(v):
    return 0.5 * jnp.tanh(0.5 * v) + 0.5


def _silu(v):
    half = 0.5 * v
    return half * (jnp.tanh(half) + 1.0)


def _bmajor(nb, rows, n_ctx, tm):
    grid = (nb, rows // tm)
    per_batch = rows // tm

    def row(width, off=0):
        return pl.BlockSpec((1, tm, width), lambda b, i: (b, i + off, 0))

    mod = pl.BlockSpec((tm // n_ctx, 1, 3 * D), lambda b, i: (b * per_batch + i, 0, 0))
    return grid, row, mod


def _bmajor_mod_table(mod16, nb, rows, n_ctx):
    groups = rows // n_ctx
    tab = jnp.broadcast_to(mod16[:nb, None, :], (nb, groups, 3 * D))
    tab = tab.at[:, 0, :].set(mod16[nb])
    return tab.reshape(nb * groups, 1, 3 * D)


def _tmajor(nb, rows, n_ctx, tt):
    grid = (1, rows // tt)
    nct = n_ctx // tt

    def row(width, off=0):
        return pl.BlockSpec((tt, nb, width), lambda b, i: (i + off, 0, 0))

    mod = pl.BlockSpec((1, nb, 3 * D), lambda b, i: (jnp.where(i < nct, 0, 1), 0, 0))
    return grid, row, mod


def _mod_body(c_ref, w_ref, b_ref, o_ref):
    s = _silu(c_ref[...]).astype(BF16)
    o_ref[...] = jnp.dot(s, w_ref[...].astype(BF16), preferred_element_type=F32) + b_ref[...]


def _modulation(cvec, ada_w, ada_b):
    n = ada_w.shape[1]
    tn = D
    return pl.pallas_call(
        _mod_body,
        grid=(n // tn,),
        in_specs=[pl.BlockSpec((COND_ROWS, D), lambda j: (0, 0)),
                  pl.BlockSpec((D, tn), lambda j: (0, j)),
                  pl.BlockSpec((1, tn), lambda j: (0, j))],
        out_specs=pl.BlockSpec((COND_ROWS, tn), lambda j: (0, j)),
        out_shape=jax.ShapeDtypeStruct((COND_ROWS, n), F32),
        compiler_params=_cparams(("parallel",)),
        name="modulation",
    )(cvec, ada_w, ada_b.reshape(1, n))


def _by_mod_group(x3, mod3):
    groups = mod3.shape[0]
    return x3 if groups == 1 else x3.reshape(groups, x3.shape[1] // groups, x3.shape[2])


def _norm_mod(x3, mod3, gn):
    xg = _by_mod_group(x3, mod3)
    ms = jnp.mean(xg * xg, axis=-1, keepdims=True)
    y = xg * lax.rsqrt(ms + EPS) * gn
    return (y * (1.0 + mod3[:, :, D:2 * D]) + mod3[:, :, 0:D]).reshape(x3.shape)


def _in_proj_body(*refs, load, n_x, epilogue, n_extra, chunk, n_side=0):
    xs, (mod_ref, gn_ref, w_ref), rest = refs[:n_x], refs[n_x:n_x + 3], refs[n_x + 3:]
    extra, outs = rest[:n_extra], rest[n_extra:]
    x3 = load(*xs, *outs[:n_side])
    outs = outs[n_side:]
    n3 = _norm_mod(x3, mod_ref[...], gn_ref[...])
    lead = x3.shape[:2]
    n = n3.reshape(lead[0] * lead[1], D).astype(BF16)
    for j in range(w_ref.shape[1] // chunk):
        acc = jnp.dot(n, w_ref[:, j * chunk:(j + 1) * chunk], preferred_element_type=F32)
        epilogue(j, acc, lead, extra, outs)


def _load_block(x_ref):
    return x_ref[...]


def _load_after_out_proj(z_ref, w_ref, h_ref, mod_ref, h_out_ref):
    z3 = z_ref[...]
    a, b, wd = z3.shape
    y = jnp.dot(z3.reshape(a * b, wd), w_ref[...], preferred_element_type=F32).reshape(a, b, D)
    mod3 = mod_ref[...]
    h = (_by_mod_group(h_ref[...], mod3) + mod3[:, :, 2 * D:3 * D] * _by_mod_group(y, mod3)).reshape(a, b, D)
    h_out_ref[...] = h
    return h


def _in_proj(xs, x_specs, load, mod, gn, w, layout, out_shapes, out_specs, epilogue, extras=(),
             extra_specs=(), chunk=512, name="in_proj", prev=None):
    grid, row, mod_spec = layout
    n = w.shape[1]
    n_side, aliases = 0, {}
    if prev is not None:
        z, w_out, mod_prev = prev
        wd = w_out.shape[0]
        (h,) = xs
        xs = (z, w_out, h, mod_prev)
        x_specs = [row(wd), pl.BlockSpec((wd, D), lambda b, i: (0, 0)), row(D), mod_spec]
        load, n_side, aliases = _load_after_out_proj, 1, {2: 0}
        out_shapes = [jax.ShapeDtypeStruct(h.shape, F32), *out_shapes]
        out_specs = [row(D), *out_specs]
    body = functools.partial(_in_proj_body, load=load, n_x=len(xs), epilogue=epilogue,
                             n_extra=len(extras), chunk=chunk, n_side=n_side)
    return pl.pallas_call(
        body,
        grid=grid,
        in_specs=[*x_specs, mod_spec,
                  pl.BlockSpec((1, D), lambda b, i: (0, 0)),
                  pl.BlockSpec((D, n), lambda b, i: (0, 0)),
                  *extra_specs],
        out_specs=out_specs,
        out_shape=out_shapes,
        input_output_aliases=aliases,
        compiler_params=_cparams(("parallel", "parallel")),
        name=name,
    )(*xs, mod, gn.reshape(1, D), w, *extras)


def _rows_out(shape2, row, outs):
    return ([jax.ShapeDtypeStruct(shape2 + (wd,), dt) for wd, dt in outs], [row(wd) for wd, _ in outs])


def _put(ref, lo, val, lead):
    ref[:, :, lo:lo + val.shape[1]] = val.reshape(lead + (val.shape[1],)).astype(ref.dtype)


def _lru_out_proj_body(z_ref, w_ref, ctx_ref, x_ref, mod_ref, o_ref, *, nct):
    z3 = z_ref[...]
    a, b, wd = z3.shape
    y = jnp.dot(z3.reshape(a * b, wd), w_ref[...], preferred_element_type=F32).reshape(a, b, D)
    gate = mod_ref[0, :, 2 * D:3 * D]
    resid = jnp.where(pl.program_id(1) < nct, ctx_ref[...], x_ref[...])
    o_ref[...] = resid + gate[:, None, :] * pltpu.einshape("tbd->btd", y)


def _lru_epilogue(j, acc, lead, extra, outs):
    _put(outs[j], 0, acc, lead)


def _lru_tile_index(s, nct, ntiles, reverse):
    if not reverse:
        return s
    return jnp.where(s < nct, nct - 1 - s, ntiles + nct - 1 - s)


def _lru_sweep_body(up_ref, uc_ref, un_ref, cw_ref, cb_ref, wg_ref, bg_ref, lam_ref, *rest,
                    nct, ntiles, reverse):
    if reverse:
        hf_ref, g_ref, o_ref, ext_ref, cv_ref, carry_ref = rest
    else:
        o_ref, ext_ref, cv_ref, carry_ref = rest
    s = pl.program_id(0)
    ti = _lru_tile_index(s, nct, ntiles, reverse)
    tt, nb, pw = uc_ref.shape

    @pl.when(s == 0)
    def _():
        carry_ref[...] = jnp.zeros_like(carry_ref)

    seg_start = (ti == 0) | (ti == nct)
    seg_end = (ti == nct - 1) | (ti == ntiles - 1)
    ext_ref[0:2] = jnp.where(seg_start, 0.0, up_ref[...])
    ext_ref[2:2 + tt] = uc_ref[...]
    ext_ref[2 + tt:3 + tt] = jnp.where(seg_end, 0.0, un_ref[...])

    rate = (-0.5 * LRU_C * LOG2E) * jax.nn.softplus(-lam_ref[...])
    tiles = [slice(k * LANES, (k + 1) * LANES) for k in range(pw // LANES)]
    for sl in tiles:
        u = cb_ref[:, sl]
        for tap in range(CONV_W):
            u = u + cw_ref[tap:tap + 1, sl] * ext_ref[tap:tap + tt, :, sl]
        cv_ref[:, sl] = u.reshape(tt * nb, LANES)
    for k, sl in enumerate(tiles):
        win = slice(_lru_window(k) * LANES, (_lru_window(k) + LRU_WIN) * LANES)
        pre = jnp.dot(cv_ref[:, win].astype(BF16), wg_ref[k], preferred_element_type=F32) + bg_ref[k]
        tr = jnp.tanh(pre[:, :LANES]) + 1.0
        ti = jnp.tanh(pre[:, LANES:]) + 1.0
        a = jnp.exp2(rate[:, sl] * tr)
        b = (0.5 * jnp.sqrt(1.0 - a * a)) * (ti * cv_ref[:, sl])
        h = carry_ref[:, sl]
        for q in range(tt):
            t = tt - 1 - q if reverse else q
            rows = slice(t * nb, (t + 1) * nb)
            h = a[rows] * h + b[rows]
            if reverse:
                o_ref[t, :, sl] = ((hf_ref[t, :, sl] + h) * _silu(g_ref[t, :, sl].astype(F32))).astype(o_ref.dtype)
            else:
                o_ref[t, :, sl] = h
        carry_ref[:, sl] = h


def _lru_sweep(u, cw, cb, wg, bg, lam, nct_rows, reverse, hf=None, g=None):
    rows, nb, pw = u.shape
    tt = LRU_TT
    ntiles = rows // tt
    nct = nct_rows // tt
    tile = functools.partial(_lru_tile_index, nct=nct, ntiles=ntiles, reverse=reverse)
    cur = lambda wd: pl.BlockSpec((tt, nb, wd), lambda s: (tile(s), 0, 0))
    const = lambda shape: pl.BlockSpec(shape, lambda s: (0,) * len(shape))
    in_specs = [
        pl.BlockSpec((2, nb, pw), lambda s: (jnp.maximum(tile(s) * (tt // 2) - 1, 0), 0, 0)),
        cur(pw),
        pl.BlockSpec((1, nb, pw), lambda s: (jnp.minimum((tile(s) + 1) * tt, rows - 1), 0, 0)),
        const((CONV_W, pw)), const((1, pw)),
        const(wg.shape), const(bg.shape), const((1, pw)),
    ]
    args = [u, u, u, cw, cb, wg, bg, lam]
    if reverse:
        in_specs += [cur(pw), cur(pw)]
        args += [hf, g]
    body = functools.partial(_lru_sweep_body, nct=nct, ntiles=ntiles, reverse=reverse)
    return pl.pallas_call(
        body,
        grid=(ntiles,),
        in_specs=in_specs,
        out_specs=cur(pw),
        out_shape=jax.ShapeDtypeStruct(u.shape, BF16 if reverse else F32),
        scratch_shapes=[pltpu.VMEM((tt + 3, nb, pw), F32),
                        pltpu.VMEM((tt * nb, pw), F32),
                        pltpu.VMEM((nb, pw), F32)],
        compiler_params=_cparams(("arbitrary",)),
        name="lru_bwd" if reverse else "lru_fwd",
    )(*args)


def _lru_window(k):
    first_block = (k * LANES) // LRU_BW
    return min((first_block * LRU_BW) // LANES, LRU_WIDTH // LANES - LRU_WIN)


def _lru_gate_weights(wa, wx, ba, bx):
    eye = 0.5 * jnp.eye(LRU_BLOCKS, dtype=BF16)

    def dense(w):
        return (w.astype(BF16)[:, :, None, :] * eye[:, None, :, None]).reshape(LRU_WIDTH, LRU_WIDTH)

    da, dx = dense(wa), dense(wx)
    wg, bg = [], []
    for k in range(LRU_WIDTH // LANES):
        rows = slice(_lru_window(k) * LANES, (_lru_window(k) + LRU_WIN) * LANES)
        cols = slice(k * LANES, (k + 1) * LANES)
        wg.append(jnp.concatenate([da[rows, cols], dx[rows, cols]], axis=1))
        bg.append(0.5 * jnp.concatenate([ba[cols], bx[cols]])[None])
    return jnp.stack(wg).astype(BF16), jnp.stack(bg)


def _load_time_major(ctx_ref, x_ref, *, nct):
    blk = jnp.where(pl.program_id(1) < nct, ctx_ref[...], x_ref[...])
    return pltpu.einshape("btd->tbd", blk)


def _rglru_layer(ctx, x, mod_t, gn, w_in, conv_w, conv_b, wa, ba, wx, bx, lam, w_out):
    nb, n_ctx, _ = ctx.shape
    rows = n_ctx + x.shape[1]
    tt = LRU_TT
    nct = n_ctx // tt
    layout = _tmajor(nb, rows, n_ctx, tt)
    grid, row, mod_spec = layout
    src_specs = [pl.BlockSpec((nb, tt, D), lambda b, i: (0, jnp.minimum(i, nct - 1), 0)),
                 pl.BlockSpec((nb, tt, D), lambda b, i: (0, jnp.maximum(i - nct, 0), 0))]
    load = functools.partial(_load_time_major, nct=nct)
    wd = LRU_WIDTH
    out_shapes, out_specs = _rows_out((rows, nb), row, [(wd, F32), (wd, BF16)])
    u, g = _in_proj((ctx, x), src_specs, load, mod_t, gn, w_in.astype(BF16), layout, out_shapes, out_specs,
                    _lru_epilogue, chunk=wd, name="lru_in_proj")
    cb = conv_b.reshape(1, wd)
    sweeps = []
    for d in range(2):
        wg, bg = _lru_gate_weights(wa[d], wx[d], ba[d], bx[d])
        sweeps.append((conv_w, cb, wg, bg, lam[d:d + 1]))
    hf = _lru_sweep(u, *sweeps[0], n_ctx, reverse=False)
    z = _lru_sweep(u, *sweeps[1], n_ctx, reverse=True, hf=hf, g=g)
    body = functools.partial(_lru_out_proj_body, nct=nct)
    return pl.pallas_call(
        body,
        grid=grid,
        in_specs=[row(wd), pl.BlockSpec((wd, D), lambda b, i: (0, 0)), *src_specs, mod_spec],
        out_specs=pl.BlockSpec((nb, tt, D), lambda b, i: (0, i, 0)),
        out_shape=jax.ShapeDtypeStruct((nb, rows, D), F32),
        compiler_params=_cparams(("parallel", "parallel")),
        name="lru_out_proj",
    )(z, w_out.astype(BF16), ctx, x, mod_t)


def _rope_tables(n_ctx, t_len):
    pos = jnp.arange(t_len)
    row = (pos // GRID_W).astype(F32)
    col = (pos % GRID_W).astype(F32)
    n_ax = HEAD_DIM // 4
    freqs = ROPE_BASE ** (-jnp.arange(n_ax, dtype=F32) / n_ax)
    ang = jnp.concatenate([row[:, None] * freqs, col[:, None] * freqs], axis=-1)
    cos, sin = jnp.cos(ang), jnp.sin(ang)
    cos_h = jnp.concatenate([cos, cos], axis=-1)
    sin_h = jnp.concatenate([-sin, sin], axis=-1)
    cos_f = jnp.concatenate([jnp.ones((n_ctx, HEAD_DIM), F32), cos_h], axis=0)
    sin_f = jnp.concatenate([jnp.zeros((n_ctx, HEAD_DIM), F32), sin_h], axis=0)
    return jnp.tile(cos_f, (1, 2)), jnp.tile(sin_f, (1, 2))


def _rope(x, cos, sin):
    lane = lax.broadcasted_iota(jnp.int32, x.shape, 1)
    swapped = jnp.where(lane % HEAD_DIM < HEAD_DIM // 2,
                        pltpu.roll(x, LANES - HEAD_DIM // 2, 1), pltpu.roll(x, HEAD_DIM // 2, 1))
    return x * cos + swapped * sin


def _swa_epilogue(j, acc, lead, extra, outs):
    cos_ref, sin_ref = extra
    q_ref, qr_ref, k_ref, v_ref, g_ref = outs
    qd = SWA_HEADS * HEAD_DIM
    kvd = SWA_KV_HEADS * HEAD_DIM
    grp = SWA_HEADS // SWA_KV_HEADS
    width = acc.shape[1]
    lo = j * width
    cos, sin = cos_ref[...], sin_ref[...]

    def roped(x):
        return jnp.concatenate([_rope(x[:, s:s + LANES], cos, sin) for s in range(0, x.shape[1], LANES)], axis=1)

    def head(x, hh):
        return x[:, hh * HEAD_DIM:(hh + 1) * HEAD_DIM]

    if lo < qd:
        q = acc * QK_SCALE
        qr = roped(q)
        for hh in range(width // HEAD_DIM):
            kv, gq = divmod(lo // HEAD_DIM + hh, grp)
            for rt in range(lead[1] // BLOCK_Q):
                rows = slice(rt * BLOCK_Q, (rt + 1) * BLOCK_Q)
                dst = slice(gq * BLOCK_Q, (gq + 1) * BLOCK_Q)
                q_ref[0, kv, rt, dst, :] = head(q, hh)[rows].astype(q_ref.dtype)
                qr_ref[0, kv, rt, dst, :] = head(qr, hh)[rows].astype(qr_ref.dtype)
    elif lo == qd:
        kr = roped(acc[:, :kvd])
        for kv in range(SWA_KV_HEADS):
            k_ref[0, kv] = head(kr, kv).astype(k_ref.dtype)
            v_ref[0, kv] = head(acc[:, kvd:], kv).T.astype(v_ref.dtype)
    else:
        _put(g_ref, lo - qd - 2 * kvd, acc, lead)


def _swa_mask_table():
    grp = SWA_HEADS // SWA_KV_HEADS
    kj = np.arange(3 * BLOCK_Q)[:, None]
    qi = np.arange(BLOCK_Q)[None, :]
    base = np.where(np.abs(kj - BLOCK_Q - qi) <= WINDOW, 0.0, NEG_INF).astype(np.float32)
    first = np.where(kj < BLOCK_Q, NEG_INF, base).astype(np.float32)
    last = np.where(kj >= 2 * BLOCK_Q, NEG_INF, base).astype(np.float32)
    tab = np.stack([np.full_like(base, NEG_INF), first, base, last])
    return jnp.asarray(np.tile(tab, (1, 1, grp)))


def _swa_body(sink_ref, *refs):
    nq, nband = SWA_QTILES, SWA_QTILES + 2
    mask_refs, refs = refs[:nq], refs[nq:]
    (q_ref, qr_ref, kc_ref), refs = refs[:3], refs[3:]
    kb_refs, refs = refs[:nband], refs[nband:]
    vc_ref, refs = refs[0], refs[1:]
    vb_refs, (g_ref, o_ref, s_ref, p_ref) = refs[:nband], refs[nband:]
    grp = SWA_HEADS // SWA_KV_HEADS
    dims = (((1,), (1,)), ((), ()))
    nc = kc_ref.shape[2]
    units = [(t, kv) for t in range(nq) for kv in range(SWA_KV_HEADS)]
    sink_p = {}

    def scores(u):
        t, kv = units[u]
        slot = u % SWA_SLOTS
        kb = jnp.concatenate([kb_refs[t + i][0, kv] for i in range(3)], axis=0)
        s_ref[slot, :nc, :] = lax.dot_general(kc_ref[0, kv], q_ref[0, kv, t], dims, preferred_element_type=F32)
        s_ref[slot, nc:, :] = (lax.dot_general(kb, qr_ref[0, kv, t], dims, preferred_element_type=F32)
                               + mask_refs[t][0])

    def softmax(u):
        _, kv = units[u]
        slot = u % SWA_SLOTS
        sink_p[u] = []
        for gq in range(grp):
            lanes = slice(gq * BLOCK_Q, (gq + 1) * BLOCK_Q)
            s = s_ref[slot, :, lanes]
            sink = sink_ref[kv, :, lanes]
            m = jnp.maximum(jnp.max(s, axis=0, keepdims=True), sink)
            sink_p[u].append(jnp.exp2(sink - m))
            p_ref[slot, :, lanes] = jnp.exp2(s - m).astype(BF16)

    def values(u):
        t, kv = units[u]
        vt = jnp.concatenate([vc_ref[0, kv]] + [vb_refs[t + i][0, kv] for i in range(3)], axis=1)
        vt = jnp.concatenate([vt, jnp.ones((ONES_ROWS, vt.shape[1]), BF16)], axis=0)
        ot = jnp.dot(vt, p_ref[u % SWA_SLOTS], preferred_element_type=F32)
        denom = ot[HEAD_DIM:HEAD_DIM + 1] + jnp.concatenate(sink_p[u], axis=1)
        ot = ot[:HEAD_DIM] * (1.0 / denom)
        slab = jnp.concatenate([ot[:, gq * BLOCK_Q:(gq + 1) * BLOCK_Q].T for gq in range(grp)], axis=1)
        rows = slice(t * BLOCK_Q, (t + 1) * BLOCK_Q)
        lanes = slice(kv * grp * HEAD_DIM, (kv + 1) * grp * HEAD_DIM)
        o_ref[0, rows, lanes] = (slab * _silu(g_ref[0, rows, lanes].astype(F32))).astype(o_ref.dtype)

    scores(0)
    scores(1)
    for u in range(len(units)):
        softmax(u)
        if u + 2 < len(units):
            scores(u + 2)
        values(u)


def _swa_attention(q, qr, k, vt, g, sink, n_ctx):
    nb, nkv, nt, qrows, hd = q.shape
    rows = k.shape[2]
    grp = SWA_HEADS // SWA_KV_HEADS
    nct = n_ctx // BLOCK_Q
    nkeys = n_ctx + 3 * BLOCK_Q
    nq = SWA_QTILES
    assert nt % nq == 0
    qspec = pl.BlockSpec((1, nkv, nq, qrows, hd), lambda b, j: (b, 0, j, 0, 0))
    kctx = pl.BlockSpec((1, nkv, n_ctx, hd), lambda b, j: (b, 0, 0, 0))
    vctx = pl.BlockSpec((1, nkv, hd, n_ctx), lambda b, j: (b, 0, 0, 0))
    rowspec = pl.BlockSpec((1, nq * BLOCK_Q, g.shape[2]), lambda b, j: (b, j, 0))

    def tile(j, off):
        return jnp.clip(j * nq - 1 + off, nct, nt - 1)

    def kband(off):
        return pl.BlockSpec((1, nkv, BLOCK_Q, hd), lambda b, j: (b, 0, tile(j, off), 0))

    def vband(off):
        return pl.BlockSpec((1, nkv, hd, BLOCK_Q), lambda b, j: (b, 0, 0, tile(j, off)))

    def mask(t):
        def variant(b, j):
            n = j * nq + t
            return (jnp.where(n < nct, 0, jnp.where(n == nct, 1, jnp.where(n == nt - 1, 3, 2))), 0, 0)
        return pl.BlockSpec((1, 3 * BLOCK_Q, qrows), variant)

    sink_rows = jnp.repeat(sink.astype(F32).reshape(nkv, grp) * LOG2E, BLOCK_Q, axis=1).reshape(nkv, 1, qrows)
    masks = _swa_mask_table()
    return pl.pallas_call(
        _swa_body,
        grid=(nb, nt // nq),
        in_specs=[pl.BlockSpec((nkv, 1, qrows), lambda b, j: (0, 0, 0)),
                  *[mask(t) for t in range(nq)], qspec, qspec,
                  kctx, *[kband(i) for i in range(nq + 2)], vctx, *[vband(i) for i in range(nq + 2)], rowspec],
        out_specs=rowspec,
        out_shape=jax.ShapeDtypeStruct((nb, rows, g.shape[2]), BF16),
        scratch_shapes=[pltpu.VMEM((SWA_SLOTS, nkeys, qrows), F32), pltpu.VMEM((SWA_SLOTS, nkeys, qrows), BF16)],
        compiler_params=_cparams(("parallel", "parallel")),
        name="swa_attention",
    )(sink_rows, *[masks] * nq, q, qr, k, *[k] * (nq + 2), vt, *[vt] * (nq + 2), g)


def _swa_layer(h, mod_b, gn, w_in, sink, n_ctx):
    nb, rows, _ = h.shape
    tm = ROW_TILE
    layout = _bmajor(nb, rows, n_ctx, tm)
    _, row, _ = layout
    qd = SWA_HEADS * HEAD_DIM
    grp = SWA_HEADS // SWA_KV_HEADS
    nkv = SWA_KV_HEADS
    cos, sin = _rope_tables(n_ctx, rows - n_ctx)
    tab = pl.BlockSpec((tm, LANES), lambda b, i: (i, 0))
    q_shape = jax.ShapeDtypeStruct((nb, nkv, rows // BLOCK_Q, grp * BLOCK_Q, HEAD_DIM), BF16)
    q_spec = pl.BlockSpec((1, nkv, tm // BLOCK_Q, grp * BLOCK_Q, HEAD_DIM), lambda b, i: (b, 0, i, 0, 0))
    k_shape = jax.ShapeDtypeStruct((nb, nkv, rows, HEAD_DIM), BF16)
    k_spec = pl.BlockSpec((1, nkv, tm, HEAD_DIM), lambda b, i: (b, 0, i, 0))
    vt_shape = jax.ShapeDtypeStruct((nb, nkv, HEAD_DIM, rows), BF16)
    vt_spec = pl.BlockSpec((1, nkv, HEAD_DIM, tm), lambda b, i: (b, 0, 0, i))
    q, qr, k, vt, g = _in_proj(
        (h,), [row(D)], _load_block, mod_b, gn, w_in.astype(BF16), layout,
        [q_shape, q_shape, k_shape, vt_shape, jax.ShapeDtypeStruct((nb, rows, qd), BF16)],
        [q_spec, q_spec, k_spec, vt_spec, row(qd)], _swa_epilogue,
        extras=(cos, sin), extra_specs=(tab, tab), name="swa_in_proj")
    return _swa_attention(q, qr, k, vt, g, sink, n_ctx)


def _na_epilogue(j, acc, lead, extra, outs):
    q_ref, k_ref, v_ref, g_ref = outs
    wd = NA_HEADS * HEAD_DIM
    width = acc.shape[1]
    lo = j * width
    which, off = lo // wd, lo % wd
    if which == 3:
        _put(g_ref, off, acc, lead)
        return
    ref = (q_ref, k_ref, v_ref)[which]
    val = acc * QK_SCALE if which == 0 else acc
    for hh in range(width // HEAD_DIM):
        ref[0, off // HEAD_DIM + hh] = val[:, hh * HEAD_DIM:(hh + 1) * HEAD_DIM].astype(ref.dtype)


def _na_bias_tiles(rpb):
    reach = GRID_W - NA_COLS
    period = 2 * GRID_W
    ndy = 2 * NA_ROWS - 1
    edge = jnp.pad(rpb.astype(F32) * LOG2E, ((0, 0), (0, 0), (reach, reach)), mode="edge")
    flat = jnp.tile(jnp.pad(edge[:, :, ::-1], ((0, 0), (0, 0), (0, 1))), (1, 1, GRID_W))[:, :, :GRID_W * (period - 1)]
    e = flat.reshape(NA_HEADS, ndy, GRID_W, period - 1)[:, :, :, GRID_W - 1:]
    col = np.arange(GRID_W)
    cstart = np.clip(col - NA_COLS // 2, 0, GRID_W - NA_COLS)
    col_ok = (col[:, None] >= cstart[None, :]) & (col[:, None] < cstart[None, :] + NA_COLS)
    e = jnp.where(jnp.asarray(col_ok)[None, None], e, NEG_INF)
    e = jnp.pad(e, ((0, 0), (0, 1), (0, 0), (0, 0)), constant_values=NEG_INF)
    shape = (NA_HEADS // NA_HSTEP, NA_HSTEP, ndy + 1, GRID_W, period)
    left = jnp.pad(e, ((0, 0), (0, 0), (0, 0), (0, GRID_W)), constant_values=NEG_INF).reshape(shape)
    right = jnp.pad(e, ((0, 0), (0, 0), (0, 0), (GRID_W, 0)), constant_values=NEG_INF).reshape(shape)
    return left, right


def _na_body(q_ref, k_ref, v_ref, g_ref, bl_ref, br_ref, o_ref, s_ref, p_ref, *, n_ctx, grid_rows):
    nk = NA_KROWS * GRID_W
    nq = NA_QROWS * GRID_W
    a_bt = (((1,), (1,)), ((), ()))
    at_b = (((0,), (0,)), ((), ()))
    left_half = lax.broadcasted_iota(jnp.int32, (GRID_W, 2 * GRID_W), 1) < GRID_W
    ones = jnp.ones((n_ctx + nk, HEAD_DIM), BF16)

    def block_geometry(qb):
        rb = pl.program_id(2) * NA_QBLOCKS + qb
        r0 = (rb - 1) * NA_QROWS
        kb = jnp.clip(r0 - NA_ROWS // 2, 0, grid_rows - NA_KROWS)

        def tile_index(qr, kr):
            qrow, krow = r0 + qr, kb + kr
            first = jnp.clip(qrow - NA_ROWS // 2, 0, grid_rows - NA_ROWS)
            inside = (rb >= 1) & (krow >= first) & (krow < first + NA_ROWS)
            return jnp.where(inside, krow - qrow + NA_ROWS - 1, NA_MASKED)

        idx = [[tile_index(qr, kr) for kr in range(NA_KROWS)] for qr in range(NA_QROWS)]
        return pl.multiple_of(n_ctx + kb * GRID_W, GRID_W), idx

    geometry = [block_geometry(qb) for qb in range(NA_QBLOCKS)]
    units = [(qb, hl) for qb in range(NA_QBLOCKS) for hl in range(NA_HSTEP)]
    outs = {}

    def scores(u):
        qb, hl = units[u]
        start, idx = geometry[qb]
        slot = u % NA_SLOTS
        qh = q_ref[0, hl, qb * nq:(qb + 1) * nq, :]
        bias = jnp.concatenate(
            [jnp.concatenate([jnp.where(left_half, bl_ref[0, hl, idx[2 * m][kr]], br_ref[0, hl, idx[2 * m + 1][kr]])
                              for m in range(NA_QROWS // 2)], axis=1) for kr in range(NA_KROWS)], axis=0)
        s_ref[slot, :n_ctx, :] = lax.dot_general(k_ref[0, hl, 0:n_ctx, :], qh, a_bt, preferred_element_type=F32)
        s_ref[slot, n_ctx:, :] = (lax.dot_general(k_ref[0, hl, pl.ds(start, nk), :], qh, a_bt,
                                                  preferred_element_type=F32) + bias)

    def softmax(u):
        slot = u % NA_SLOTS
        for c in range(nq // LANES):
            lanes = slice(c * LANES, (c + 1) * LANES)
            s = s_ref[slot, :, lanes]
            p_ref[slot, :, lanes] = jnp.exp2(s - jnp.max(s, axis=0, keepdims=True)).astype(BF16)

    def values(u):
        qb, hl = units[u]
        start, _ = geometry[qb]
        v1 = jnp.concatenate(
            [jnp.concatenate([v_ref[0, hl, 0:n_ctx, :], v_ref[0, hl, pl.ds(start, nk), :]], axis=0), ones], axis=1)
        ot = lax.dot_general(v1, p_ref[u % NA_SLOTS], at_b, preferred_element_type=F32)
        outs[hl] = (ot[:HEAD_DIM] * (1.0 / ot[HEAD_DIM:HEAD_DIM + 1])).T
        if hl == NA_HSTEP - 1:
            rows = slice(qb * nq, (qb + 1) * nq)
            slab = jnp.concatenate([outs[h] for h in range(NA_HSTEP)], axis=1)
            o_ref[0, rows, :] = (slab * _silu(g_ref[0, rows, :].astype(F32))).astype(o_ref.dtype)

    scores(0)
    scores(1)
    for u in range(len(units)):
        softmax(u)
        if u + 2 < len(units):
            scores(u + 2)
        values(u)


def _na_attention(q, k, v, g, bias_left, bias_right, n_ctx):
    nb, nh, rows, hd = q.shape
    grid_rows = (rows - n_ctx) // GRID_W
    nq = NA_QROWS * GRID_W
    assert n_ctx == nq, "context tokens form exactly one query block"
    assert rows % (NA_QBLOCKS * nq) == 0
    nkeys = n_ctx + NA_KROWS * GRID_W
    lanes = NA_HSTEP * HEAD_DIM
    qspec = pl.BlockSpec((1, NA_HSTEP, NA_QBLOCKS * nq, hd), lambda p, b, r: (b, p, r, 0))
    kvspec = pl.BlockSpec((1, NA_HSTEP, rows, hd), lambda p, b, r: (b, p, 0, 0))
    rowspec = pl.BlockSpec((1, NA_QBLOCKS * nq, lanes), lambda p, b, r: (b, r, p))
    bspec = pl.BlockSpec((1,) + bias_left.shape[1:], lambda p, b, r: (p, 0, 0, 0, 0))
    body = functools.partial(_na_body, n_ctx=n_ctx, grid_rows=grid_rows)
    return pl.pallas_call(
        body,
        grid=(nh // NA_HSTEP, nb, rows // (NA_QBLOCKS * nq)),
        in_specs=[qspec, kvspec, kvspec, rowspec, bspec, bspec],
        out_specs=rowspec,
        out_shape=jax.ShapeDtypeStruct(g.shape, BF16),
        scratch_shapes=[pltpu.VMEM((NA_SLOTS, nkeys, nq), F32), pltpu.VMEM((NA_SLOTS, nkeys, nq), BF16)],
        compiler_params=_cparams(("parallel", "parallel", "parallel")),
        name="na_attention",
    )(q, k, v, g, bias_left, bias_right)


def _na_layer(h, prev, mod_b, gn, w_in, rpb, n_ctx):
    nb, rows, _ = h.shape
    tm = ROW_TILE
    layout = _bmajor(nb, rows, n_ctx, tm)
    _, row, _ = layout
    wd = NA_HEADS * HEAD_DIM
    head_shape = jax.ShapeDtypeStruct((nb, NA_HEADS, rows, HEAD_DIM), BF16)
    head_spec = pl.BlockSpec((1, NA_HEADS, tm, HEAD_DIM), lambda b, i: (b, 0, i, 0))
    h, q, k, v, g = _in_proj((h,), [row(D)], _load_block, mod_b, gn, w_in.astype(BF16), layout,
                             [head_shape] * 3 + [jax.ShapeDtypeStruct((nb, rows, wd), BF16)],
                             [head_spec] * 3 + [row(wd)], _na_epilogue, name="swa_out_na_in_proj", prev=prev)
    bias_left, bias_right = _na_bias_tiles(rpb)
    return h, _na_attention(q, k, v, g, bias_left, bias_right, n_ctx)


def _s5_epilogue(j, acc, lead, extra, outs):
    u_ref, g_ref = outs
    lo = j * acc.shape[1]
    if lo < S5_WIDTH:
        _put(u_ref, lo, acc, lead)
    else:
        _put(g_ref, lo - S5_WIDTH, acc, lead)


def _s5_matrices(a_re, a_im, log_dt, b_re, b_im, c_re, c_im, d_skip):
    L = S5_CHUNK
    lam = lax.complex(a_re.astype(F32), a_im.astype(F32))
    lam_dt = lam * jnp.exp(log_dt.astype(F32))[..., None]
    lam_bar = jnp.exp(lam_dt)
    b_bar = ((lam_bar - 1.0) / lam)[..., None] * lax.complex(b_re.astype(F32), b_im.astype(F32))
    c_mat = lax.complex(c_re.astype(F32), c_im.astype(F32))
    tau = np.arange(L)

    def power(expo, d):
        return jnp.exp(jnp.asarray(expo, F32).reshape(expo.shape + (1, 1)) * lam_dt[d])

    w = jnp.stack([c_mat[d][None] * power(tau, d)[:, :, None, :] for d in range(2)])
    w2 = jnp.concatenate([jnp.real(w), -jnp.imag(w)], axis=-1)
    b2 = jnp.concatenate([jnp.real(b_bar), jnp.imag(b_bar)], axis=-2)
    kern = jnp.einsum('dkgip,dgpj->dkgij', w2, b2).reshape(2 * L, S5_GROUPS, S5_GROUP, S5_GROUP)
    lag = tau[None, :] - tau[:, None]
    sel = np.concatenate([lag[None] == tau[:, None, None], -lag[None] == tau[:, None, None]]).astype(np.float32)
    both = jnp.einsum('ktu,kgij->tugij', jnp.asarray(sel), kern)
    eye_t = jnp.eye(L, dtype=F32)[:, :, None, None, None]
    skip = jnp.eye(S5_GROUP, dtype=F32)[None] * d_skip.astype(F32).reshape(S5_GROUPS, S5_GROUP, 1)
    both = both + eye_t * skip[None, None]
    toep = both.transpose(2, 0, 4, 1, 3).reshape(S5_GROUPS, L * S5_GROUP, L * S5_GROUP)

    def lanes(z):
        pad = [(0, 0)] * (z.ndim - 1) + [(0, LANES - S5_STATE)]
        return jnp.concatenate([jnp.pad(jnp.real(z), pad), jnp.pad(jnp.imag(z), pad)], axis=-1)

    pf = power(L - 1 - tau, 0)[:, :, :, None] * b_bar[0][None]
    pb = power(tau, 1)[:, :, :, None] * b_bar[1][None]
    pmat = jnp.concatenate([lanes(pf.transpose(1, 0, 3, 2)), lanes(pb.transpose(1, 0, 3, 2))], axis=-1)
    pmat = pmat.reshape(S5_GROUPS, L * S5_GROUP, 4 * LANES)
    qf = c_mat[0][None] * power(tau + 1, 0)[:, :, None, :]
    qb = c_mat[1][None] * power(L - tau, 1)[:, :, None, :]

    def state_rows(z):
        pad = ((0, 0), (0, LANES - S5_STATE), (0, 0), (0, 0))
        zr = jnp.pad(jnp.real(z).transpose(1, 3, 0, 2), pad)
        zi = jnp.pad(-jnp.imag(z).transpose(1, 3, 0, 2), pad)
        return jnp.concatenate([zr, zi], axis=1).reshape(S5_GROUPS, 2 * LANES, L * S5_GROUP)

    qmat = jnp.concatenate([state_rows(qf), state_rows(qb)], axis=1)
    lam_l = jnp.exp(float(L) * lam_dt)
    pad = ((0, 0), (0, LANES - S5_STATE))
    coef = jnp.stack([jnp.pad(jnp.real(lam_l[0]), pad), jnp.pad(jnp.imag(lam_l[0]), pad),
                      jnp.pad(jnp.real(lam_l[1]), pad), jnp.pad(jnp.imag(lam_l[1]), pad)], axis=1)
    coef = jnp.pad(coef, ((0, 0), (0, 4), (0, 0)))
    return toep.astype(BF16), pmat.astype(BF16), qmat.astype(BF16), coef


def _atom_transpose(xs, atom):
    n = len(xs)
    group = lax.broadcasted_iota(jnp.int32, xs[0].shape, 1) // atom
    xs = list(xs)
    d = n // 2
    while d:
        keep = (group & d) == 0
        for i in range(n):
            if not i & d:
                lo, hi = xs[i], xs[i + d]
                xs[i] = jnp.where(keep, lo, pltpu.roll(hi, d * atom, 1))
                xs[i + d] = jnp.where(keep, pltpu.roll(lo, LANES - d * atom, 1), hi)
        d //= 2
    return xs


def _s5_body(u_ref, toep_ref, p_ref, q_ref, coef_ref, y_ref, u_scr, y_scr, s_ref, hin_ref, *, nb, nc_ctx, nc):
    nwin = LANES // S5_GROUP
    nrow, nlat = u_ref.shape[1], y_ref.shape[1]
    blk = S5_ROWS
    for half in range(S5_CHUNK // nwin):
        cols = slice(half * LANES, (half + 1) * LANES)
        for r in range(0, nrow, blk):
            words = [pltpu.bitcast(u_ref[half * nwin + w, r:r + blk, :], jnp.uint32) for w in range(nwin)]
            for gl, x in enumerate(_atom_transpose(words, S5_GROUP)):
                u_scr[gl, r:r + blk, cols] = pltpu.bitcast(x, BF16)
    for gl in range(S5_SLAB):
        _s5_group(gl, u_scr, toep_ref, p_ref, q_ref, coef_ref, y_scr, s_ref, hin_ref, nb, nc_ctx, nc)
    for half in range(S5_CHUNK // nwin):
        cols = slice(half * LANES, (half + 1) * LANES)
        for r in range(0, nlat, blk):
            ys = [y_scr[gl, r:r + blk, cols] for gl in range(S5_SLAB)]
            for w, x in enumerate(_atom_transpose(ys, S5_GROUP)):
                y_ref[half * nwin + w, r:r + blk, :] = x


def _s5_group(gl, u_scr, toep_ref, p_ref, q_ref, coef_ref, y_scr, s_ref, hin_ref, nb, nc_ctx, nc):
    u = u_scr[gl]
    s_ref[...] = jnp.dot(u, p_ref[gl], preferred_element_type=F32)
    coef = coef_ref[gl]
    shape = (nb, LANES)
    lr_f, li_f = jnp.broadcast_to(coef[0:1], shape), jnp.broadcast_to(coef[1:2], shape)
    lr_b, li_b = jnp.broadcast_to(coef[2:3], shape), jnp.broadcast_to(coef[3:4], shape)
    zero = jnp.zeros(shape, F32)

    def step(q, carry):
        fr, fi, br, bi = carry
        rows_f = pl.ds(pl.multiple_of(q * nb, nb), nb)
        cb = jnp.where(q < nc_ctx, nc_ctx - 1 - q, nc + nc_ctx - 1 - q)
        rows_b = pl.ds(pl.multiple_of(cb * nb, nb), nb)
        hin_ref[rows_f, 0:LANES] = fr
        hin_ref[rows_f, LANES:2 * LANES] = fi
        hin_ref[rows_b, 2 * LANES:3 * LANES] = br
        hin_ref[rows_b, 3 * LANES:4 * LANES] = bi
        sfr, sfi = s_ref[rows_f, 0:LANES], s_ref[rows_f, LANES:2 * LANES]
        sbr, sbi = s_ref[rows_b, 2 * LANES:3 * LANES], s_ref[rows_b, 3 * LANES:4 * LANES]
        return (lr_f * fr - li_f * fi + sfr, lr_f * fi + li_f * fr + sfi,
                lr_b * br - li_b * bi + sbr, lr_b * bi + li_b * br + sbi)

    lax.fori_loop(0, nc, step, (zero, zero, zero, zero), unroll=True)

    lat = slice(nc_ctx * nb, nc * nb)
    y_scr[gl] = (jnp.dot(u[lat], toep_ref[gl], preferred_element_type=F32)
                 + jnp.dot(hin_ref[lat, :].astype(BF16), q_ref[gl], preferred_element_type=F32))


def _s5_scan(u_t, toep, pmat, qmat, coef, nb, n_ctx):
    L, nrow, wd = u_t.shape
    pk = L * S5_GROUP
    nc = nrow // nb
    nc_ctx = n_ctx // L
    nlat = (nc - nc_ctx) * nb
    assert nrow % S5_ROWS == 0 and nlat % S5_ROWS == 0
    once = pl.Buffered(1)
    slab = lambda shape: pl.BlockSpec((S5_SLAB,) + shape, lambda s: (s, 0, 0), pipeline_mode=once)
    body = functools.partial(_s5_body, nb=nb, nc_ctx=nc_ctx, nc=nc)
    return pl.pallas_call(
        body,
        grid=(wd // LANES,),
        in_specs=[pl.BlockSpec((L, nrow, LANES), lambda s: (0, 0, s)),
                  slab((pk, pk)), slab((pk, 4 * LANES)), slab((4 * LANES, pk)), slab((8, LANES))],
        out_specs=pl.BlockSpec((L, nlat, LANES), lambda s: (0, 0, s)),
        out_shape=jax.ShapeDtypeStruct((L, nlat, wd), F32),
        scratch_shapes=[pltpu.VMEM((S5_SLAB, nrow, pk), BF16), pltpu.VMEM((S5_SLAB, nlat, pk), F32),
                        pltpu.VMEM((nrow, 4 * LANES), F32), pltpu.VMEM((nrow, 4 * LANES), F32)],
        compiler_params=_cparams(("parallel",)),
        name="s5_scan",
    )(u_t, toep, pmat, qmat, coef)


def _rms(x, g):
    return x * lax.rsqrt(jnp.mean(x * x, axis=-1, keepdims=True) + EPS) * g


def _s5_readout_body(y_ref, ga_ref, gb_ref, ha_ref, hb_ref, gw_ref, gbias_ref, w_ref, mod_ref, nf_ref, o_ref):
    sub = ga_ref.shape[1]
    for i, (g_ref, h_ref) in enumerate(((ga_ref, ha_ref), (gb_ref, hb_ref))):
        rows = slice(i * sub, (i + 1) * sub)
        y = jax.nn.gelu(y_ref[0, rows, :])
        t = y * _sigmoid(jnp.dot(y.astype(BF16), gw_ref[...], preferred_element_type=F32) + gbias_ref[...])
        z = (t * _silu(g_ref[0].astype(F32))).astype(BF16)
        out = jnp.dot(z, w_ref[...], preferred_element_type=F32)
        h = h_ref[0] + mod_ref[0, :, 2 * D:3 * D] * out
        o_ref[0, rows, :] = _rms(h, nf_ref[...])


def _s5_readout(y, g, glu_w, glu_b, w_out, h, mod_b, norm_f, n_ctx):
    nb, t_len, _ = y.shape
    sub = n_ctx
    tm = 2 * sub
    lat = lambda wd: pl.BlockSpec((1, tm, wd), lambda b, i: (b, i, 0))
    part = lambda wd, j: pl.BlockSpec((1, sub, wd), lambda b, i: (b, 2 * i + 1 + j, 0))
    const = lambda shape: pl.BlockSpec(shape, lambda b, i: (0, 0))
    return pl.pallas_call(
        _s5_readout_body,
        grid=(nb, t_len // tm),
        in_specs=[lat(S5_WIDTH), part(S5_WIDTH, 0), part(S5_WIDTH, 1), part(D, 0), part(D, 1),
                  const((S5_WIDTH, S5_WIDTH)), const((1, S5_WIDTH)), const((S5_WIDTH, D)),
                  pl.BlockSpec((1, 1, 3 * D), lambda b, i: (b, 0, 0)), const((1, D))],
        out_specs=lat(D),
        out_shape=jax.ShapeDtypeStruct((nb, t_len, D), F32),
        compiler_params=_cparams(("parallel", "parallel")),
        name="s5_readout",
    )(y, g, g, h, h, glu_w.astype(BF16), glu_b.reshape(1, S5_WIDTH), w_out.astype(BF16), mod_b,
      norm_f.reshape(1, D))


def _s5_layer(h, prev, mod_b, mod_lat, gn, w_in, a_re, a_im, log_dt, b_re, b_im, c_re, c_im, d_skip, glu_w, glu_b,
              w_out, norm_f, n_ctx):
    nb, rows, _ = h.shape
    L = S5_CHUNK
    layout = _bmajor(nb, rows, n_ctx, ROW_TILE)
    _, row, _ = layout
    out_shapes, out_specs = _rows_out((nb, rows), row, [(S5_WIDTH, BF16)] * 2)
    h, u, g = _in_proj((h,), [row(D)], _load_block, mod_b, gn, w_in.astype(BF16), layout, out_shapes, out_specs,
                       _s5_epilogue, name="na_out_s5_in_proj", prev=prev)
    nc = rows // L
    u_t = u.reshape(nb, nc, L, S5_WIDTH).transpose(2, 1, 0, 3).reshape(L, nc * nb, S5_WIDTH)
    toep, pmat, qmat, coef = _s5_matrices(a_re, a_im, log_dt, b_re, b_im, c_re, c_im, d_skip)
    y_t = _s5_scan(u_t, toep, pmat, qmat, coef, nb, n_ctx)
    t_len = rows - n_ctx
    y = y_t.reshape(L, t_len // L, nb, S5_WIDTH).transpose(2, 1, 0, 3).reshape(nb, t_len, S5_WIDTH)
    return _s5_readout(y, g, glu_w, glu_b, w_out, h, mod_lat, norm_f, n_ctx)


def kernel(x, c, ctx, c_ctx, ada_w0, ada_b0, norm0, w_in0, conv_w0, conv_b0, lru_wa0, lru_ba0, lru_wx0, lru_bx0, lru_lam0, w_out0, ada_w1, ada_b1, norm1, w_in1, sink1, w_out1, ada_w2, ada_b2, norm2, w_in2, rpb2, w_out2, ada_w3, ada_b3, norm3, w_in3, s5_a_re3, s5_a_im3, s5_log_dt3, s5_b_re3, s5_b_im3, s5_c_re3, s5_c_im3, s5_d3, glu_w3, glu_b3, w_out3, norm_f):
    nb, t_len, _ = x.shape
    n_ctx = ctx.shape[1]
    assert nb < COND_ROWS
    cvec = jnp.concatenate([c, c_ctx[None, :], jnp.zeros((COND_ROWS - nb - 1, D), F32)], axis=0)
    mods = [_modulation(cvec, w, b) for w, b in
            ((ada_w0, ada_b0), (ada_w1, ada_b1), (ada_w2, ada_b2), (ada_w3, ada_b3))]
    mod_b = [_bmajor_mod_table(m, nb, n_ctx + t_len, n_ctx) for m in mods]

    mod_t0 = jnp.stack([jnp.broadcast_to(mods[0][nb], (nb, 3 * D)), mods[0][:nb]], axis=0)
    h = _rglru_layer(ctx, x, mod_t0, norm0, w_in0, conv_w0, conv_b0, lru_wa0, lru_ba0, lru_wx0, lru_bx0,
                     lru_lam0, w_out0)
    z = _swa_layer(h, mod_b[1], norm1, w_in1, sink1, n_ctx)
    h, z = _na_layer(h, (z, w_out1.astype(BF16), mod_b[1]), mod_b[2], norm2, w_in2, rpb2, n_ctx)
    return _s5_layer(h, (z, w_out2.astype(BF16), mod_b[2]), mod_b[3], mods[3][:nb, None, :], norm3, w_in3, s5_a_re3, s5_a_im3, s5_log_dt3,
                     s5_b_re3, s5_b_im3, s5_c_re3, s5_c_im3, s5_d3, glu_w3, glu_b3, w_out3, norm_f, n_ctx)
```

```python
import functools
import math

import jax
import jax.numpy as jnp
import numpy as np
from jax import lax
from jax.experimental import pallas as pl
from jax.experimental.pallas import tpu as pltpu

F32 = jnp.float32
BF16 = jnp.bfloat16

D = 1024
HEAD_DIM = 64
GRID_W = 64
EPS = 1e-6
NEG_INF = -1e30
ROPE_BASE = 10000.0
LANES = 128
LOG2E = math.log2(math.e)
QK_SCALE = HEAD_DIM ** -0.5 * LOG2E
ONES_ROWS = 16

LRU_WIDTH = 1408
LRU_BLOCKS = 16
LRU_BW = LRU_WIDTH // LRU_BLOCKS
LRU_WIN = 3
CONV_W = 4
LRU_C = 8.0
LRU_TT = 128

SWA_HEADS = 16
SWA_KV_HEADS = 4
WINDOW = 128
BLOCK_Q = 128
SWA_QTILES = 6
SWA_SLOTS = 4

NA_HEADS = 16
NA_ROWS = 8
NA_COLS = 16
NA_QROWS = 4
NA_KROWS = 12
NA_HSTEP = 8
NA_SLOTS = 4
NA_QBLOCKS = 3
NA_MASKED = 2 * NA_ROWS - 1

S5_WIDTH = 1024
S5_GROUP = 16
S5_GROUPS = S5_WIDTH // S5_GROUP
S5_STATE = 64
S5_CHUNK = 16
S5_SLAB = LANES // S5_GROUP
S5_ROWS = 128

ROW_TILE = 768
VMEM_LIMIT = 56 * 1024 * 1024


def _cparams(sem):
    return pltpu.CompilerParams(dimension_semantics=sem, vmem_limit_bytes=VMEM_LIMIT)


def _sigmoid(v):
    return 0.5 * jnp.tanh(0.5 * v) + 0.5


def _silu(v):
    half = 0.5 * v
    return half * (jnp.tanh(half) + 1.0)


def _bmajor(nb, rows, n_ctx, tm):
    grid = (nb, rows // tm)
    per_batch = rows // tm

    def row(width, off=0):
        return pl.BlockSpec((1, tm, width), lambda b, i: (b, i + off, 0))

    mod = pl.BlockSpec((tm // n_ctx, 1, 3 * D), lambda b, i: (b * per_batch + i, 0, 0))
    return grid, row, mod


def _bmajor_mod_table(mod16, nb, rows, n_ctx):
    groups = rows // n_ctx
    tab = jnp.broadcast_to(mod16[:nb, None, :], (nb, groups, 3 * D))
    tab = tab.at[:, 0, :].set(mod16[nb])
    return tab.reshape(nb * groups, 1, 3 * D)


def _tmajor(nb, rows, n_ctx, tt):
    grid = (1, rows // tt)
    nct = n_ctx // tt

    def row(width, off=0):
        return pl.BlockSpec((tt, nb, width), lambda b, i: (i + off, 0, 0))

    mod = pl.BlockSpec((1, nb, 3 * D), lambda b, i: (jnp.where(i < nct, 0, 1), 0, 0))
    return grid, row, mod


def _mod_body(c_ref, w_ref, b_ref, o_ref):
    s = _silu(c_ref[...]).astype(BF16)
    o_ref[...] = jnp.dot(s, w_ref[...].astype(BF16), preferred_element_type=F32) + b_ref[...]


def _modulation(cvec, ada_w, ada_b):
    n = ada_w.shape[1]
    tn = 1024
    return pl.pallas_call(
        _mod_body,
        grid=(n // tn,),
        in_specs=[pl.BlockSpec((16, D), lambda j: (0, 0)),
                  pl.BlockSpec((D, tn), lambda j: (0, j)),
                  pl.BlockSpec((1, tn), lambda j: (0, j))],
        out_specs=pl.BlockSpec((16, tn), lambda j: (0, j)),
        out_shape=jax.ShapeDtypeStruct((16, n), F32),
        compiler_params=_cparams(("parallel",)),
        name="modulation",
    )(cvec, ada_w, ada_b.reshape(1, n))


def _by_mod_group(x3, mod3):
    groups = mod3.shape[0]
    return x3 if groups == 1 else x3.reshape(groups, x3.shape[1] // groups, x3.shape[2])


def _norm_mod(x3, mod3, gn):
    xg = _by_mod_group(x3, mod3)
    ms = jnp.mean(xg * xg, axis=-1, keepdims=True)
    y = xg * lax.rsqrt(ms + EPS) * gn
    return (y * (1.0 + mod3[:, :, D:2 * D]) + mod3[:, :, 0:D]).reshape(x3.shape)


def _in_proj_body(*refs, load, n_x, epilogue, n_extra, chunk, n_side=0):
    xs, (mod_ref, gn_ref, w_ref), rest = refs[:n_x], refs[n_x:n_x + 3], refs[n_x + 3:]
    extra, outs = rest[:n_extra], rest[n_extra:]
    x3 = load(*xs, *outs[:n_side])
    outs = outs[n_side:]
    n3 = _norm_mod(x3, mod_ref[...], gn_ref[...])
    lead = x3.shape[:2]
    n = n3.reshape(lead[0] * lead[1], D).astype(BF16)
    for j in range(w_ref.shape[1] // chunk):
        acc = jnp.dot(n, w_ref[:, j * chunk:(j + 1) * chunk], preferred_element_type=F32)
        epilogue(j, acc, lead, extra, outs)


def _load_block(x_ref):
    return x_ref[...]


def _load_after_out_proj(z_ref, w_ref, h_ref, mod_ref, h_out_ref):
    z3 = z_ref[...]
    a, b, wd = z3.shape
    y = jnp.dot(z3.reshape(a * b, wd), w_ref[...], preferred_element_type=F32).reshape(a, b, D)
    mod3 = mod_ref[...]
    h = (_by_mod_group(h_ref[...], mod3) + mod3[:, :, 2 * D:3 * D] * _by_mod_group(y, mod3)).reshape(a, b, D)
    h_out_ref[...] = h
    return h


def _in_proj(xs, x_specs, load, mod, gn, w, layout, out_shapes, out_specs, epilogue, extras=(),
             extra_specs=(), chunk=512, name="in_proj", prev=None):
    grid, row, mod_spec = layout
    n = w.shape[1]
    n_side, aliases = 0, {}
    if prev is not None:
        z, w_out, mod_prev = prev
        wd = w_out.shape[0]
        (h,) = xs
        xs = (z, w_out, h, mod_prev)
        x_specs = [row(wd), pl.BlockSpec((wd, D), lambda b, i: (0, 0)), row(D), mod_spec]
        load, n_side, aliases = _load_after_out_proj, 1, {2: 0}
        out_shapes = [jax.ShapeDtypeStruct(h.shape, F32), *out_shapes]
        out_specs = [row(D), *out_specs]
    body = functools.partial(_in_proj_body, load=load, n_x=len(xs), epilogue=epilogue,
                             n_extra=len(extras), chunk=chunk, n_side=n_side)
    return pl.pallas_call(
        body,
        grid=grid,
        in_specs=[*x_specs, mod_spec,
                  pl.BlockSpec((1, D), lambda b, i: (0, 0)),
                  pl.BlockSpec((D, n), lambda b, i: (0, 0)),
                  *extra_specs],
        out_specs=out_specs,
        out_shape=out_shapes,
        input_output_aliases=aliases,
        compiler_params=_cparams(("parallel", "parallel")),
        name=name,
    )(*xs, mod, gn.reshape(1, D), w, *extras)


def _rows_out(shape2, row, outs):
    return ([jax.ShapeDtypeStruct(shape2 + (wd,), dt) for wd, dt in outs], [row(wd) for wd, _ in outs])


def _put(ref, lo, val, lead):
    ref[:, :, lo:lo + val.shape[1]] = val.reshape(lead + (val.shape[1],)).astype(ref.dtype)


def _lru_out_proj_body(z_ref, w_ref, ctx_ref, x_ref, mod_ref, o_ref, *, nct):
    z3 = z_ref[...]
    a, b, wd = z3.shape
    y = jnp.dot(z3.reshape(a * b, wd), w_ref[...], preferred_element_type=F32).reshape(a, b, D)
    gate = mod_ref[0, :, 2 * D:3 * D]
    resid = jnp.where(pl.program_id(1) < nct, ctx_ref[...], x_ref[...])
    o_ref[...] = resid + gate[:, None, :] * pltpu.einshape("tbd->btd", y)


def _lru_epilogue(j, acc, lead, extra, outs):
    _put(outs[j], 0, acc, lead)


def _lru_tile_index(s, nct, ntiles, reverse):
    if not reverse:
        return s
    return jnp.where(s < nct, nct - 1 - s, ntiles + nct - 1 - s)


def _lru_sweep_body(up_ref, uc_ref, un_ref, cw_ref, cb_ref, wg_ref, bg_ref, lam_ref, *rest,
                    nct, ntiles, reverse):
    if reverse:
        hf_ref, g_ref, o_ref, ext_ref, cv_ref, carry_ref = rest
    else:
        o_ref, ext_ref, cv_ref, carry_ref = rest
    s = pl.program_id(0)
    ti = _lru_tile_index(s, nct, ntiles, reverse)
    tt, nb, pw = uc_ref.shape

    @pl.when(s == 0)
    def _():
        carry_ref[...] = jnp.zeros_like(carry_ref)

    seg_start = (ti == 0) | (ti == nct)
    seg_end = (ti == nct - 1) | (ti == ntiles - 1)
    ext_ref[0:2] = jnp.where(seg_start, 0.0, up_ref[...])
    ext_ref[2:2 + tt] = uc_ref[...]
    ext_ref[2 + tt:3 + tt] = jnp.where(seg_end, 0.0, un_ref[...])

    rate = (-0.5 * LRU_C * LOG2E) * jax.nn.softplus(-lam_ref[...])
    tiles = [slice(k * LANES, (k + 1) * LANES) for k in range(pw // LANES)]
    for sl in tiles:
        u = cb_ref[:, sl]
        for tap in range(CONV_W):
            u = u + cw_ref[tap:tap + 1, sl] * ext_ref[tap:tap + tt, :, sl]
        cv_ref[:, sl] = u.reshape(tt * nb, LANES)
    for k, sl in enumerate(tiles):
        win = slice(_lru_window(k) * LANES, (_lru_window(k) + LRU_WIN) * LANES)
        pre = jnp.dot(cv_ref[:, win].astype(BF16), wg_ref[k], preferred_element_type=F32) + bg_ref[k]
        tr = jnp.tanh(pre[:, :LANES]) + 1.0
        ti = jnp.tanh(pre[:, LANES:]) + 1.0
        a = jnp.exp2(rate[:, sl] * tr)
        b = (0.5 * jnp.sqrt(1.0 - a * a)) * (ti * cv_ref[:, sl])
        h = carry_ref[:, sl]
        for q in range(tt):
            t = tt - 1 - q if reverse else q
            rows = slice(t * nb, (t + 1) * nb)
            h = a[rows] * h + b[rows]
            if reverse:
                o_ref[t, :, sl] = ((hf_ref[t, :, sl] + h) * _silu(g_ref[t, :, sl].astype(F32))).astype(o_ref.dtype)
            else:
                o_ref[t, :, sl] = h
        carry_ref[:, sl] = h


def _lru_sweep(u, cw, cb, wg, bg, lam, nct_rows, reverse, hf=None, g=None):
    rows, nb, pw = u.shape
    tt = LRU_TT
    ntiles = rows // tt
    nct = nct_rows // tt
    tile = functools.partial(_lru_tile_index, nct=nct, ntiles=ntiles, reverse=reverse)
    cur = lambda wd: pl.BlockSpec((tt, nb, wd), lambda s: (tile(s), 0, 0))
    const = lambda shape: pl.BlockSpec(shape, lambda s: (0,) * len(shape))
    in_specs = [
        pl.BlockSpec((2, nb, pw), lambda s: (jnp.maximum(tile(s) * (tt // 2) - 1, 0), 0, 0)),
        cur(pw),
        pl.BlockSpec((1, nb, pw), lambda s: (jnp.minimum((tile(s) + 1) * tt, rows - 1), 0, 0)),
        const((CONV_W, pw)), const((1, pw)),
        const(wg.shape), const(bg.shape), const((1, pw)),
    ]
    args = [u, u, u, cw, cb, wg, bg, lam]
    if reverse:
        in_specs += [cur(pw), cur(pw)]
        args += [hf, g]
    body = functools.partial(_lru_sweep_body, nct=nct, ntiles=ntiles, reverse=reverse)
    return pl.pallas_call(
        body,
        grid=(ntiles,),
        in_specs=in_specs,
        out_specs=cur(pw),
        out_shape=jax.ShapeDtypeStruct(u.shape, BF16 if reverse else F32),
        scratch_shapes=[pltpu.VMEM((tt + 3, nb, pw), F32),
                        pltpu.VMEM((tt * nb, pw), F32),
                        pltpu.VMEM((nb, pw), F32)],
        compiler_params=_cparams(("arbitrary",)),
        name="lru_bwd" if reverse else "lru_fwd",
    )(*args)


def _lru_window(k):
    first_block = (k * LANES) // LRU_BW
    return min((first_block * LRU_BW) // LANES, LRU_WIDTH // LANES - LRU_WIN)


def _lru_gate_weights(wa, wx, ba, bx):
    eye = 0.5 * jnp.eye(LRU_BLOCKS, dtype=BF16)

    def dense(w):
        return (w.astype(BF16)[:, :, None, :] * eye[:, None, :, None]).reshape(LRU_WIDTH, LRU_WIDTH)

    da, dx = dense(wa), dense(wx)
    wg, bg = [], []
    for k in range(LRU_WIDTH // LANES):
        rows = slice(_lru_window(k) * LANES, (_lru_window(k) + LRU_WIN) * LANES)
        cols = slice(k * LANES, (k + 1) * LANES)
        wg.append(jnp.concatenate([da[rows, cols], dx[rows, cols]], axis=1))
        bg.append(0.5 * jnp.concatenate([ba[cols], bx[cols]])[None])
    return jnp.stack(wg).astype(BF16), jnp.stack(bg)


def _load_time_major(ctx_ref, x_ref, *, nct):
    blk = jnp.where(pl.program_id(1) < nct, ctx_ref[...], x_ref[...])
    return pltpu.einshape("btd->tbd", blk)


def _rglru_layer(ctx, x, mod_t, gn, w_in, conv_w, conv_b, wa, ba, wx, bx, lam, w_out):
    nb, n_ctx, _ = ctx.shape
    rows = n_ctx + x.shape[1]
    tt = LRU_TT
    nct = n_ctx // tt
    layout = _tmajor(nb, rows, n_ctx, tt)
    grid, row, mod_spec = layout
    src_specs = [pl.BlockSpec((nb, tt, D), lambda b, i: (0, jnp.minimum(i, nct - 1), 0)),
                 pl.BlockSpec((nb, tt, D), lambda b, i: (0, jnp.maximum(i - nct, 0), 0))]
    load = functools.partial(_load_time_major, nct=nct)
    wd = LRU_WIDTH
    out_shapes, out_specs = _rows_out((rows, nb), row, [(wd, F32), (wd, BF16)])
    u, g = _in_proj((ctx, x), src_specs, load, mod_t, gn, w_in.astype(BF16), layout, out_shapes, out_specs,
                    _lru_epilogue, chunk=wd, name="lru_in_proj")
    cb = conv_b.reshape(1, wd)
    sweeps = []
    for d in range(2):
        wg, bg = _lru_gate_weights(wa[d], wx[d], ba[d], bx[d])
        sweeps.append((conv_w, cb, wg, bg, lam[d:d + 1]))
    hf = _lru_sweep(u, *sweeps[0], n_ctx, reverse=False)
    z = _lru_sweep(u, *sweeps[1], n_ctx, reverse=True, hf=hf, g=g)
    body = functools.partial(_lru_out_proj_body, nct=nct)
    return pl.pallas_call(
        body,
        grid=grid,
        in_specs=[row(wd), pl.BlockSpec((wd, D), lambda b, i: (0, 0)), *src_specs, mod_spec],
        out_specs=pl.BlockSpec((nb, tt, D), lambda b, i: (0, i, 0)),
        out_shape=jax.ShapeDtypeStruct((nb, rows, D), F32),
        compiler_params=_cparams(("parallel", "parallel")),
        name="lru_out_proj",
    )(z, w_out.astype(BF16), ctx, x, mod_t)


def _rope_tables(n_ctx, t_len):
    pos = jnp.arange(t_len)
    row = (pos // GRID_W).astype(F32)
    col = (pos % GRID_W).astype(F32)
    n_ax = HEAD_DIM // 4
    freqs = ROPE_BASE ** (-jnp.arange(n_ax, dtype=F32) / n_ax)
    ang = jnp.concatenate([row[:, None] * freqs, col[:, None] * freqs], axis=-1)
    cos, sin = jnp.cos(ang), jnp.sin(ang)
    cos_h = jnp.concatenate([cos, cos], axis=-1)
    sin_h = jnp.concatenate([-sin, sin], axis=-1)
    cos_f = jnp.concatenate([jnp.ones((n_ctx, HEAD_DIM), F32), cos_h], axis=0)
    sin_f = jnp.concatenate([jnp.zeros((n_ctx, HEAD_DIM), F32), sin_h], axis=0)
    return jnp.tile(cos_f, (1, 2)), jnp.tile(sin_f, (1, 2))


def _rope(x, cos, sin):
    lane = lax.broadcasted_iota(jnp.int32, x.shape, 1)
    swapped = jnp.where(lane % HEAD_DIM < HEAD_DIM // 2,
                        pltpu.roll(x, LANES - HEAD_DIM // 2, 1), pltpu.roll(x, HEAD_DIM // 2, 1))
    return x * cos + swapped * sin


def _swa_epilogue(j, acc, lead, extra, outs):
    cos_ref, sin_ref = extra
    q_ref, qr_ref, k_ref, v_ref, g_ref = outs
    qd = SWA_HEADS * HEAD_DIM
    kvd = SWA_KV_HEADS * HEAD_DIM
    grp = SWA_HEADS // SWA_KV_HEADS
    width = acc.shape[1]
    lo = j * width
    cos, sin = cos_ref[...], sin_ref[...]

    def roped(x):
        return jnp.concatenate([_rope(x[:, s:s + LANES], cos, sin) for s in range(0, x.shape[1], LANES)], axis=1)

    def head(x, hh):
        return x[:, hh * HEAD_DIM:(hh + 1) * HEAD_DIM]

    if lo < qd:
        q = acc * QK_SCALE
        qr = roped(q)
        for hh in range(width // HEAD_DIM):
            kv, gq = divmod(lo // HEAD_DIM + hh, grp)
            for rt in range(lead[1] // BLOCK_Q):
                rows = slice(rt * BLOCK_Q, (rt + 1) * BLOCK_Q)
                dst = slice(gq * BLOCK_Q, (gq + 1) * BLOCK_Q)
                q_ref[0, kv, rt, dst, :] = head(q, hh)[rows].astype(q_ref.dtype)
                qr_ref[0, kv, rt, dst, :] = head(qr, hh)[rows].astype(qr_ref.dtype)
    elif lo == qd:
        kr = roped(acc[:, :kvd])
        for kv in range(SWA_KV_HEADS):
            k_ref[0, kv] = head(kr, kv).astype(k_ref.dtype)
            v_ref[0, kv] = head(acc[:, kvd:], kv).T.astype(v_ref.dtype)
    else:
        _put(g_ref, lo - qd - 2 * kvd, acc, lead)


def _swa_mask_table():
    grp = SWA_HEADS // SWA_KV_HEADS
    kj = np.arange(3 * BLOCK_Q)[:, None]
    qi = np.arange(BLOCK_Q)[None, :]
    base = np.where(np.abs(kj - BLOCK_Q - qi) <= WINDOW, 0.0, NEG_INF).astype(np.float32)
    first = np.where(kj < BLOCK_Q, NEG_INF, base).astype(np.float32)
    last = np.where(kj >= 2 * BLOCK_Q, NEG_INF, base).astype(np.float32)
    tab = np.stack([np.full_like(base, NEG_INF), first, base, last])
    return jnp.asarray(np.tile(tab, (1, 1, grp)))


def _swa_body(sink_ref, *refs):
    nq, nband = SWA_QTILES, SWA_QTILES + 2
    mask_refs, refs = refs[:nq], refs[nq:]
    (q_ref, qr_ref, kc_ref), refs = refs[:3], refs[3:]
    kb_refs, refs = refs[:nband], refs[nband:]
    vc_ref, refs = refs[0], refs[1:]
    vb_refs, (g_ref, o_ref, s_ref, p_ref) = refs[:nband], refs[nband:]
    grp = SWA_HEADS // SWA_KV_HEADS
    dims = (((1,), (1,)), ((), ()))
    nc = kc_ref.shape[2]
    units = [(t, kv) for t in range(nq) for kv in range(SWA_KV_HEADS)]
    sink_p = {}

    def scores(u):
        t, kv = units[u]
        slot = u % SWA_SLOTS
        kb = jnp.concatenate([kb_refs[t + i][0, kv] for i in range(3)], axis=0)
        s_ref[slot, :nc, :] = lax.dot_general(kc_ref[0, kv], q_ref[0, kv, t], dims, preferred_element_type=F32)
        s_ref[slot, nc:, :] = (lax.dot_general(kb, qr_ref[0, kv, t], dims, preferred_element_type=F32)
                               + mask_refs[t][0])

    def softmax(u):
        _, kv = units[u]
        slot = u % SWA_SLOTS
        sink_p[u] = []
        for gq in range(grp):
            lanes = slice(gq * BLOCK_Q, (gq + 1) * BLOCK_Q)
            s = s_ref[slot, :, lanes]
            sink = sink_ref[kv, :, lanes]
            m = jnp.maximum(jnp.max(s, axis=0, keepdims=True), sink)
            sink_p[u].append(jnp.exp2(sink - m))
            p_ref[slot, :, lanes] = jnp.exp2(s - m).astype(BF16)

    def values(u):
        t, kv = units[u]
        vt = jnp.concatenate([vc_ref[0, kv]] + [vb_refs[t + i][0, kv] for i in range(3)], axis=1)
        vt = jnp.concatenate([vt, jnp.ones((ONES_ROWS, vt.shape[1]), BF16)], axis=0)
        ot = jnp.dot(vt, p_ref[u % SWA_SLOTS], preferred_element_type=F32)
        denom = ot[HEAD_DIM:HEAD_DIM + 1] + jnp.concatenate(sink_p[u], axis=1)
        ot = ot[:HEAD_DIM] * (1.0 / denom)
        slab = jnp.concatenate([ot[:, gq * BLOCK_Q:(gq + 1) * BLOCK_Q].T for gq in range(grp)], axis=1)
        rows = slice(t * BLOCK_Q, (t + 1) * BLOCK_Q)
        lanes = slice(kv * grp * HEAD_DIM, (kv + 1) * grp * HEAD_DIM)
        o_ref[0, rows, lanes] = (slab * _silu(g_ref[0, rows, lanes].astype(F32))).astype(o_ref.dtype)

    scores(0)
    scores(1)
    for u in range(len(units)):
        softmax(u)
        if u + 2 < len(units):
            scores(u + 2)
        values(u)


def _swa_attention(q, qr, k, vt, g, sink, n_ctx):
    nb, nkv, nt, qrows, hd = q.shape
    rows = k.shape[2]
    grp = SWA_HEADS // SWA_KV_HEADS
    nct = n_ctx // BLOCK_Q
    nkeys = n_ctx + 3 * BLOCK_Q
    nq = SWA_QTILES
    assert nt % nq == 0
    qspec = pl.BlockSpec((1, nkv, nq, qrows, hd), lambda b, j: (b, 0, j, 0, 0))
    kctx = pl.BlockSpec((1, nkv, n_ctx, hd), lambda b, j: (b, 0, 0, 0))
    vctx = pl.BlockSpec((1, nkv, hd, n_ctx), lambda b, j: (b, 0, 0, 0))
    rowspec = pl.BlockSpec((1, nq * BLOCK_Q, g.shape[2]), lambda b, j: (b, j, 0))

    def tile(j, off):
        return jnp.clip(j * nq - 1 + off, nct, nt - 1)

    def kband(off):
        return pl.BlockSpec((1, nkv, BLOCK_Q, hd), lambda b, j: (b, 0, tile(j, off), 0))

    def vband(off):
        return pl.BlockSpec((1, nkv, hd, BLOCK_Q), lambda b, j: (b, 0, 0, tile(j, off)))

    def mask(t):
        def variant(b, j):
            n = j * nq + t
            return (jnp.where(n < nct, 0, jnp.where(n == nct, 1, jnp.where(n == nt - 1, 3, 2))), 0, 0)
        return pl.BlockSpec((1, 3 * BLOCK_Q, qrows), variant)

    sink_rows = jnp.repeat(sink.astype(F32).reshape(nkv, grp) * LOG2E, BLOCK_Q, axis=1).reshape(nkv, 1, qrows)
    masks = _swa_mask_table()
    return pl.pallas_call(
        _swa_body,
        grid=(nb, nt // nq),
        in_specs=[pl.BlockSpec((nkv, 1, qrows), lambda b, j: (0, 0, 0)),
                  *[mask(t) for t in range(nq)], qspec, qspec,
                  kctx, *[kband(i) for i in range(nq + 2)], vctx, *[vband(i) for i in range(nq + 2)], rowspec],
        out_specs=rowspec,
        out_shape=jax.ShapeDtypeStruct((nb, rows, g.shape[2]), BF16),
        scratch_shapes=[pltpu.VMEM((SWA_SLOTS, nkeys, qrows), F32), pltpu.VMEM((SWA_SLOTS, nkeys, qrows), BF16)],
        compiler_params=_cparams(("parallel", "parallel")),
        name="swa_attention",
    )(sink_rows, *[masks] * nq, q, qr, k, *[k] * (nq + 2), vt, *[vt] * (nq + 2), g)


def _swa_layer(h, mod_b, gn, w_in, sink, n_ctx):
    nb, rows, _ = h.shape
    tm = ROW_TILE
    layout = _bmajor(nb, rows, n_ctx, tm)
    _, row, _ = layout
    qd = SWA_HEADS * HEAD_DIM
    grp = SWA_HEADS // SWA_KV_HEADS
    nkv = SWA_KV_HEADS
    cos, sin = _rope_tables(n_ctx, rows - n_ctx)
    tab = pl.BlockSpec((tm, LANES), lambda b, i: (i, 0))
    q_shape = jax.ShapeDtypeStruct((nb, nkv, rows // BLOCK_Q, grp * BLOCK_Q, HEAD_DIM), BF16)
    q_spec = pl.BlockSpec((1, nkv, tm // BLOCK_Q, grp * BLOCK_Q, HEAD_DIM), lambda b, i: (b, 0, i, 0, 0))
    k_shape = jax.ShapeDtypeStruct((nb, nkv, rows, HEAD_DIM), BF16)
    k_spec = pl.BlockSpec((1, nkv, tm, HEAD_DIM), lambda b, i: (b, 0, i, 0))
    vt_shape = jax.ShapeDtypeStruct((nb, nkv, HEAD_DIM, rows), BF16)
    vt_spec = pl.BlockSpec((1, nkv, HEAD_DIM, tm), lambda b, i: (b, 0, 0, i))
    q, qr, k, vt, g = _in_proj(
        (h,), [row(D)], _load_block, mod_b, gn, w_in.astype(BF16), layout,
        [q_shape, q_shape, k_shape, vt_shape, jax.ShapeDtypeStruct((nb, rows, qd), BF16)],
        [q_spec, q_spec, k_spec, vt_spec, row(qd)], _swa_epilogue,
        extras=(cos, sin), extra_specs=(tab, tab), name="swa_in_proj")
    return _swa_attention(q, qr, k, vt, g, sink, n_ctx)


def _na_epilogue(j, acc, lead, extra, outs):
    q_ref, k_ref, v_ref, g_ref = outs
    wd = NA_HEADS * HEAD_DIM
    width = acc.shape[1]
    lo = j * width
    which, off = lo // wd, lo % wd
    if which == 3:
        _put(g_ref, off, acc, lead)
        return
    ref = (q_ref, k_ref, v_ref)[which]
    val = acc * QK_SCALE if which == 0 else acc
    for hh in range(width // HEAD_DIM):
        ref[0, off // HEAD_DIM + hh] = val[:, hh * HEAD_DIM:(hh + 1) * HEAD_DIM].astype(ref.dtype)


def _na_bias_tiles(rpb):
    reach = GRID_W - NA_COLS
    period = 2 * GRID_W
    ndy = 2 * NA_ROWS - 1
    edge = jnp.pad(rpb.astype(F32) * LOG2E, ((0, 0), (0, 0), (reach, reach)), mode="edge")
    flat = jnp.tile(jnp.pad(edge[:, :, ::-1], ((0, 0), (0, 0), (0, 1))), (1, 1, GRID_W))[:, :, :GRID_W * (period - 1)]
    e = flat.reshape(NA_HEADS, ndy, GRID_W, period - 1)[:, :, :, GRID_W - 1:]
    col = np.arange(GRID_W)
    cstart = np.clip(col - NA_COLS // 2, 0, GRID_W - NA_COLS)
    col_ok = (col[:, None] >= cstart[None, :]) & (col[:, None] < cstart[None, :] + NA_COLS)
    e = jnp.where(jnp.asarray(col_ok)[None, None], e, NEG_INF)
    e = jnp.pad(e, ((0, 0), (0, 1), (0, 0), (0, 0)), constant_values=NEG_INF)
    shape = (NA_HEADS // NA_HSTEP, NA_HSTEP, ndy + 1, GRID_W, period)
    left = jnp.pad(e, ((0, 0), (0, 0), (0, 0), (0, GRID_W)), constant_values=NEG_INF).reshape(shape)
    right = jnp.pad(e, ((0, 0), (0, 0), (0, 0), (GRID_W, 0)), constant_values=NEG_INF).reshape(shape)
    return left, right


def _na_body(q_ref, k_ref, v_ref, g_ref, bl_ref, br_ref, o_ref, s_ref, p_ref, *, n_ctx, grid_rows):
    nk = NA_KROWS * GRID_W
    nq = NA_QROWS * GRID_W
    a_bt = (((1,), (1,)), ((), ()))
    at_b = (((0,), (0,)), ((), ()))
    left_half = lax.broadcasted_iota(jnp.int32, (GRID_W, 2 * GRID_W), 1) < GRID_W
    ones = jnp.ones((n_ctx + nk, HEAD_DIM), BF16)

    def block_geometry(qb):
        rb = pl.program_id(2) * NA_QBLOCKS + qb
        r0 = (rb - 1) * NA_QROWS
        kb = jnp.clip(r0 - NA_ROWS // 2, 0, grid_rows - NA_KROWS)

        def tile_index(qr, kr):
            qrow, krow = r0 + qr, kb + kr
            first = jnp.clip(qrow - NA_ROWS // 2, 0, grid_rows - NA_ROWS)
            inside = (rb >= 1) & (krow >= first) & (krow < first + NA_ROWS)
            return jnp.where(inside, krow - qrow + NA_ROWS - 1, NA_MASKED)

        idx = [[tile_index(qr, kr) for kr in range(NA_KROWS)] for qr in range(NA_QROWS)]
        return pl.multiple_of(n_ctx + kb * GRID_W, GRID_W), idx

    geometry = [block_geometry(qb) for qb in range(NA_QBLOCKS)]
    units = [(qb, hl) for qb in range(NA_QBLOCKS) for hl in range(NA_HSTEP)]
    outs = {}

    def scores(u):
        qb, hl = units[u]
        start, idx = geometry[qb]
        slot = u % NA_SLOTS
        qh = q_ref[0, hl, qb * nq:(qb + 1) * nq, :]
        bias = jnp.concatenate(
            [jnp.concatenate([jnp.where(left_half, bl_ref[0, hl, idx[2 * m][kr]], br_ref[0, hl, idx[2 * m + 1][kr]])
                              for m in range(NA_QROWS // 2)], axis=1) for kr in range(NA_KROWS)], axis=0)
        s_ref[slot, :n_ctx, :] = lax.dot_general(k_ref[0, hl, 0:n_ctx, :], qh, a_bt, preferred_element_type=F32)
        s_ref[slot, n_ctx:, :] = (lax.dot_general(k_ref[0, hl, pl.ds(start, nk), :], qh, a_bt,
                                                  preferred_element_type=F32) + bias)

    def softmax(u):
        slot = u % NA_SLOTS
        for c in range(nq // LANES):
            lanes = slice(c * LANES, (c + 1) * LANES)
            s = s_ref[slot, :, lanes]
            p_ref[slot, :, lanes] = jnp.exp2(s - jnp.max(s, axis=0, keepdims=True)).astype(BF16)

    def values(u):
        qb, hl = units[u]
        start, _ = geometry[qb]
        v1 = jnp.concatenate(
            [jnp.concatenate([v_ref[0, hl, 0:n_ctx, :], v_ref[0, hl, pl.ds(start, nk), :]], axis=0), ones], axis=1)
        ot = lax.dot_general(v1, p_ref[u % NA_SLOTS], at_b, preferred_element_type=F32)
        outs[hl] = (ot[:HEAD_DIM] * (1.0 / ot[HEAD_DIM:HEAD_DIM + 1])).T
        if hl == NA_HSTEP - 1:
            rows = slice(qb * nq, (qb + 1) * nq)
            slab = jnp.concatenate([outs[h] for h in range(NA_HSTEP)], axis=1)
            o_ref[0, rows, :] = (slab * _silu(g_ref[0, rows, :].astype(F32))).astype(o_ref.dtype)

    scores(0)
    scores(1)
    for u in range(len(units)):
        softmax(u)
        if u + 2 < len(units):
            scores(u + 2)
        values(u)


def _na_attention(q, k, v, g, bias_left, bias_right, n_ctx):
    nb, nh, rows, hd = q.shape
    grid_rows = (rows - n_ctx) // GRID_W
    nq = NA_QROWS * GRID_W
    assert n_ctx == nq, "context tokens form exactly one query block"
    assert rows % (NA_QBLOCKS * nq) == 0
    nkeys = n_ctx + NA_KROWS * GRID_W
    lanes = NA_HSTEP * HEAD_DIM
    qspec = pl.BlockSpec((1, NA_HSTEP, NA_QBLOCKS * nq, hd), lambda p, b, r: (b, p, r, 0))
    kvspec = pl.BlockSpec((1, NA_HSTEP, rows, hd), lambda p, b, r: (b, p, 0, 0))
    rowspec = pl.BlockSpec((1, NA_QBLOCKS * nq, lanes), lambda p, b, r: (b, r, p))
    bspec = pl.BlockSpec((1,) + bias_left.shape[1:], lambda p, b, r: (p, 0, 0, 0, 0))
    body = functools.partial(_na_body, n_ctx=n_ctx, grid_rows=grid_rows)
    return pl.pallas_call(
        body,
        grid=(nh // NA_HSTEP, nb, rows // (NA_QBLOCKS * nq)),
        in_specs=[qspec, kvspec, kvspec, rowspec, bspec, bspec],
        out_specs=rowspec,
        out_shape=jax.ShapeDtypeStruct(g.shape, BF16),
        scratch_shapes=[pltpu.VMEM((NA_SLOTS, nkeys, nq), F32), pltpu.VMEM((NA_SLOTS, nkeys, nq), BF16)],
        compiler_params=_cparams(("parallel", "parallel", "parallel")),
        name="na_attention",
    )(q, k, v, g, bias_left, bias_right)


def _na_layer(h, prev, mod_b, gn, w_in, rpb, n_ctx):
    nb, rows, _ = h.shape
    tm = ROW_TILE
    layout = _bmajor(nb, rows, n_ctx, tm)
    _, row, _ = layout
    wd = NA_HEADS * HEAD_DIM
    head_shape = jax.ShapeDtypeStruct((nb, NA_HEADS, rows, HEAD_DIM), BF16)
    head_spec = pl.BlockSpec((1, NA_HEADS, tm, HEAD_DIM), lambda b, i: (b, 0, i, 0))
    h, q, k, v, g = _in_proj((h,), [row(D)], _load_block, mod_b, gn, w_in.astype(BF16), layout,
                             [head_shape] * 3 + [jax.ShapeDtypeStruct((nb, rows, wd), BF16)],
                             [head_spec] * 3 + [row(wd)], _na_epilogue, name="swa_out_na_in_proj", prev=prev)
    bias_left, bias_right = _na_bias_tiles(rpb)
    return h, _na_attention(q, k, v, g, bias_left, bias_right, n_ctx)


def _s5_epilogue(j, acc, lead, extra, outs):
    u_ref, g_ref = outs
    lo = j * acc.shape[1]
    if lo < S5_WIDTH:
        _put(u_ref, lo, acc, lead)
    else:
        _put(g_ref, lo - S5_WIDTH, acc, lead)


def _s5_matrices(a_re, a_im, log_dt, b_re, b_im, c_re, c_im, d_skip):
    L = S5_CHUNK
    lam = lax.complex(a_re.astype(F32), a_im.astype(F32))
    lam_dt = lam * jnp.exp(log_dt.astype(F32))[..., None]
    lam_bar = jnp.exp(lam_dt)
    b_bar = ((lam_bar - 1.0) / lam)[..., None] * lax.complex(b_re.astype(F32), b_im.astype(F32))
    c_mat = lax.complex(c_re.astype(F32), c_im.astype(F32))
    tau = np.arange(L)

    def power(expo, d):
        return jnp.exp(jnp.asarray(expo, F32).reshape(expo.shape + (1, 1)) * lam_dt[d])

    w = jnp.stack([c_mat[d][None] * power(tau, d)[:, :, None, :] for d in range(2)])
    w2 = jnp.concatenate([jnp.real(w), -jnp.imag(w)], axis=-1)
    b2 = jnp.concatenate([jnp.real(b_bar), jnp.imag(b_bar)], axis=-2)
    kern = jnp.einsum('dkgip,dgpj->dkgij', w2, b2).reshape(2 * L, S5_GROUPS, S5_GROUP, S5_GROUP)
    lag = tau[None, :] - tau[:, None]
    sel = np.concatenate([lag[None] == tau[:, None, None], -lag[None] == tau[:, None, None]]).astype(np.float32)
    both = jnp.einsum('ktu,kgij->tugij', jnp.asarray(sel), kern)
    eye_t = jnp.eye(L, dtype=F32)[:, :, None, None, None]
    skip = jnp.eye(S5_GROUP, dtype=F32)[None] * d_skip.astype(F32).reshape(S5_GROUPS, S5_GROUP, 1)
    both = both + eye_t * skip[None, None]
    toep = both.transpose(2, 0, 4, 1, 3).reshape(S5_GROUPS, L * S5_GROUP, L * S5_GROUP)

    def lanes(z):
        pad = [(0, 0)] * (z.ndim - 1) + [(0, LANES - S5_STATE)]
        return jnp.concatenate([jnp.pad(jnp.real(z), pad), jnp.pad(jnp.imag(z), pad)], axis=-1)

    pf = power(L - 1 - tau, 0)[:, :, :, None] * b_bar[0][None]
    pb = power(tau, 1)[:, :, :, None] * b_bar[1][None]
    pmat = jnp.concatenate([lanes(pf.transpose(1, 0, 3, 2)), lanes(pb.transpose(1, 0, 3, 2))], axis=-1)
    pmat = pmat.reshape(S5_GROUPS, L * S5_GROUP, 4 * LANES)
    qf = c_mat[0][None] * power(tau + 1, 0)[:, :, None, :]
    qb = c_mat[1][None] * power(L - tau, 1)[:, :, None, :]

    def state_rows(z):
        pad = ((0, 0), (0, LANES - S5_STATE), (0, 0), (0, 0))
        zr = jnp.pad(jnp.real(z).transpose(1, 3, 0, 2), pad)
        zi = jnp.pad(-jnp.imag(z).transpose(1, 3, 0, 2), pad)
        return jnp.concatenate([zr, zi], axis=1).reshape(S5_GROUPS, 2 * LANES, L * S5_GROUP)

    qmat = jnp.concatenate([state_rows(qf), state_rows(qb)], axis=1)
    lam_l = jnp.exp(float(L) * lam_dt)
    pad = ((0, 0), (0, LANES - S5_STATE))
    coef = jnp.stack([jnp.pad(jnp.real(lam_l[0]), pad), jnp.pad(jnp.imag(lam_l[0]), pad),
                      jnp.pad(jnp.real(lam_l[1]), pad), jnp.pad(jnp.imag(lam_l[1]), pad)], axis=1)
    coef = jnp.pad(coef, ((0, 0), (0, 4), (0, 0)))
    return toep.astype(BF16), pmat.astype(BF16), qmat.astype(BF16), coef


def _atom_transpose(xs, atom):
    n = len(xs)
    group = lax.broadcasted_iota(jnp.int32, xs[0].shape, 1) // atom
    xs = list(xs)
    d = n // 2
    while d:
        keep = (group & d) == 0
        for i in range(n):
            if not i & d:
                lo, hi = xs[i], xs[i + d]
                xs[i] = jnp.where(keep, lo, pltpu.roll(hi, d * atom, 1))
                xs[i + d] = jnp.where(keep, pltpu.roll(lo, LANES - d * atom, 1), hi)
        d //= 2
    return xs


def _s5_body(u_ref, toep_ref, p_ref, q_ref, coef_ref, y_ref, u_scr, y_scr, s_ref, hin_ref, *, nb, nc_ctx, nc):
    nwin = LANES // S5_GROUP
    nrow, nlat = u_ref.shape[1], y_ref.shape[1]
    blk = S5_ROWS
    for half in range(S5_CHUNK // nwin):
        cols = slice(half * LANES, (half + 1) * LANES)
        for r in range(0, nrow, blk):
            words = [pltpu.bitcast(u_ref[half * nwin + w, r:r + blk, :], jnp.uint32) for w in range(nwin)]
            for gl, x in enumerate(_atom_transpose(words, S5_GROUP)):
                u_scr[gl, r:r + blk, cols] = pltpu.bitcast(x, BF16)
    for gl in range(S5_SLAB):
        _s5_group(gl, u_scr, toep_ref, p_ref, q_ref, coef_ref, y_scr, s_ref, hin_ref, nb, nc_ctx, nc)
    for half in range(S5_CHUNK // nwin):
        cols = slice(half * LANES, (half + 1) * LANES)
        for r in range(0, nlat, blk):
            ys = [y_scr[gl, r:r + blk, cols] for gl in range(S5_SLAB)]
            for w, x in enumerate(_atom_transpose(ys, S5_GROUP)):
                y_ref[half * nwin + w, r:r + blk, :] = x


def _s5_group(gl, u_scr, toep_ref, p_ref, q_ref, coef_ref, y_scr, s_ref, hin_ref, nb, nc_ctx, nc):
    u = u_scr[gl]
    s_ref[...] = jnp.dot(u, p_ref[gl], preferred_element_type=F32)
    coef = coef_ref[gl]
    shape = (nb, LANES)
    lr_f, li_f = jnp.broadcast_to(coef[0:1], shape), jnp.broadcast_to(coef[1:2], shape)
    lr_b, li_b = jnp.broadcast_to(coef[2:3], shape), jnp.broadcast_to(coef[3:4], shape)
    zero = jnp.zeros(shape, F32)

    def step(q, carry):
        fr, fi, br, bi = carry
        rows_f = pl.ds(pl.multiple_of(q * nb, nb), nb)
        cb = jnp.where(q < nc_ctx, nc_ctx - 1 - q, nc + nc_ctx - 1 - q)
        rows_b = pl.ds(pl.multiple_of(cb * nb, nb), nb)
        hin_ref[rows_f, 0:LANES] = fr
        hin_ref[rows_f, LANES:2 * LANES] = fi
        hin_ref[rows_b, 2 * LANES:3 * LANES] = br
        hin_ref[rows_b, 3 * LANES:4 * LANES] = bi
        sfr, sfi = s_ref[rows_f, 0:LANES], s_ref[rows_f, LANES:2 * LANES]
        sbr, sbi = s_ref[rows_b, 2 * LANES:3 * LANES], s_ref[rows_b, 3 * LANES:4 * LANES]
        return (lr_f * fr - li_f * fi + sfr, lr_f * fi + li_f * fr + sfi,
                lr_b * br - li_b * bi + sbr, lr_b * bi + li_b * br + sbi)

    lax.fori_loop(0, nc, step, (zero, zero, zero, zero), unroll=True)

    lat = slice(nc_ctx * nb, nc * nb)
    y_scr[gl] = (jnp.dot(u[lat], toep_ref[gl], preferred_element_type=F32)
                 + jnp.dot(hin_ref[lat, :].astype(BF16), q_ref[gl], preferred_element_type=F32))


def _s5_scan(u_t, toep, pmat, qmat, coef, nb, n_ctx):
    L, nrow, wd = u_t.shape
    pk = L * S5_GROUP
    nc = nrow // nb
    nc_ctx = n_ctx // L
    nlat = (nc - nc_ctx) * nb
    assert nrow % S5_ROWS == 0 and nlat % S5_ROWS == 0
    once = pl.Buffered(1)
    slab = lambda shape: pl.BlockSpec((S5_SLAB,) + shape, lambda s: (s, 0, 0), pipeline_mode=once)
    body = functools.partial(_s5_body, nb=nb, nc_ctx=nc_ctx, nc=nc)
    return pl.pallas_call(
        body,
        grid=(wd // LANES,),
        in_specs=[pl.BlockSpec((L, nrow, LANES), lambda s: (0, 0, s)),
                  slab((pk, pk)), slab((pk, 4 * LANES)), slab((4 * LANES, pk)), slab((8, LANES))],
        out_specs=pl.BlockSpec((L, nlat, LANES), lambda s: (0, 0, s)),
        out_shape=jax.ShapeDtypeStruct((L, nlat, wd), F32),
        scratch_shapes=[pltpu.VMEM((S5_SLAB, nrow, pk), BF16), pltpu.VMEM((S5_SLAB, nlat, pk), F32),
                        pltpu.VMEM((nrow, 4 * LANES), F32), pltpu.VMEM((nrow, 4 * LANES), F32)],
        compiler_params=_cparams(("parallel",)),
        name="s5_scan",
    )(u_t, toep, pmat, qmat, coef)


def _rms(x, g):
    return x * lax.rsqrt(jnp.mean(x * x, axis=-1, keepdims=True) + EPS) * g


def _s5_readout_body(y_ref, ga_ref, gb_ref, ha_ref, hb_ref, gw_ref, gbias_ref, w_ref, mod_ref, nf_ref, o_ref):
    sub = ga_ref.shape[1]
    for i, (g_ref, h_ref) in enumerate(((ga_ref, ha_ref), (gb_ref, hb_ref))):
        rows = slice(i * sub, (i + 1) * sub)
        y = jax.nn.gelu(y_ref[0, rows, :])
        t = y * _sigmoid(jnp.dot(y.astype(BF16), gw_ref[...], preferred_element_type=F32) + gbias_ref[...])
        z = (t * _silu(g_ref[0].astype(F32))).astype(BF16)
        out = jnp.dot(z, w_ref[...], preferred_element_type=F32)
        h = h_ref[0] + mod_ref[0, :, 2 * D:3 * D] * out
        o_ref[0, rows, :] = _rms(h, nf_ref[...])


def _s5_readout(y, g, glu_w, glu_b, w_out, h, mod_b, norm_f, n_ctx):
    nb, t_len, _ = y.shape
    sub = n_ctx
    tm = 2 * sub
    lat = lambda wd: pl.BlockSpec((1, tm, wd), lambda b, i: (b, i, 0))
    part = lambda wd, j: pl.BlockSpec((1, sub, wd), lambda b, i: (b, 2 * i + 1 + j, 0))
    const = lambda shape: pl.BlockSpec(shape, lambda b, i: (0, 0))
    return pl.pallas_call(
        _s5_readout_body,
        grid=(nb, t_len // tm),
        in_specs=[lat(S5_WIDTH), part(S5_WIDTH, 0), part(S5_WIDTH, 1), part(D, 0), part(D, 1),
                  const((S5_WIDTH, S5_WIDTH)), const((1, S5_WIDTH)), const((S5_WIDTH, D)),
                  pl.BlockSpec((1, 1, 3 * D), lambda b, i: (b, 0, 0)), const((1, D))],
        out_specs=lat(D),
        out_shape=jax.ShapeDtypeStruct((nb, t_len, D), F32),
        compiler_params=_cparams(("parallel", "parallel")),
        name="s5_readout",
    )(y, g, g, h, h, glu_w.astype(BF16), glu_b.reshape(1, S5_WIDTH), w_out.astype(BF16), mod_b,
      norm_f.reshape(1, D))


def _s5_layer(h, prev, mod_b, mod_lat, gn, w_in, a_re, a_im, log_dt, b_re, b_im, c_re, c_im, d_skip, glu_w, glu_b,
              w_out, norm_f, n_ctx):
    nb, rows, _ = h.shape
    L = S5_CHUNK
    layout = _bmajor(nb, rows, n_ctx, ROW_TILE)
    _, row, _ = layout
    out_shapes, out_specs = _rows_out((nb, rows), row, [(S5_WIDTH, BF16)] * 2)
    h, u, g = _in_proj((h,), [row(D)], _load_block, mod_b, gn, w_in.astype(BF16), layout, out_shapes, out_specs,
                       _s5_epilogue, name="na_out_s5_in_proj", prev=prev)
    nc = rows // L
    u_t = u.reshape(nb, nc, L, S5_WIDTH).transpose(2, 1, 0, 3).reshape(L, nc * nb, S5_WIDTH)
    toep, pmat, qmat, coef = _s5_matrices(a_re, a_im, log_dt, b_re, b_im, c_re, c_im, d_skip)
    y_t = _s5_scan(u_t, toep, pmat, qmat, coef, nb, n_ctx)
    t_len = rows - n_ctx
    y = y_t.reshape(L, t_len // L, nb, S5_WIDTH).transpose(2, 1, 0, 3).reshape(nb, t_len, S5_WIDTH)
    return _s5_readout(y, g, glu_w, glu_b, w_out, h, mod_lat, norm_f, n_ctx)


def kernel(x, c, ctx, c_ctx, ada_w0, ada_b0, norm0, w_in0, conv_w0, conv_b0, lru_wa0, lru_ba0, lru_wx0, lru_bx0, lru_lam0, w_out0, ada_w1, ada_b1, norm1, w_in1, sink1, w_out1, ada_w2, ada_b2, norm2, w_in2, rpb2, w_out2, ada_w3, ada_b3, norm3, w_in3, s5_a_re3, s5_a_im3, s5_log_dt3, s5_b_re3, s5_b_im3, s5_c_re3, s5_c_im3, s5_d3, glu_w3, glu_b3, w_out3, norm_f):
    nb, t_len, _ = x.shape
    n_ctx = ctx.shape[1]
    cvec = jnp.concatenate([c, c_ctx[None, :], jnp.zeros((16 - nb - 1, D), F32)], axis=0)
    mods = [_modulation(cvec, w, b) for w, b in
            ((ada_w0, ada_b0), (ada_w1, ada_b1), (ada_w2, ada_b2), (ada_w3, ada_b3))]
    mod_b = [_bmajor_mod_table(m, nb, n_ctx + t_len, n_ctx) for m in mods]

    mod_t0 = jnp.stack([jnp.broadcast_to(mods[0][nb], (nb, 3 * D)), mods[0][:nb]], axis=0)
    h = _rglru_layer(ctx, x, mod_t0, norm0, w_in0, conv_w0, conv_b0, lru_wa0, lru_ba0, lru_wx0, lru_bx0,
                     lru_lam0, w_out0)
    z = _swa_layer(h, mod_b[1], norm1, w_in1, sink1, n_ctx)
    h, z = _na_layer(h, (z, w_out1.astype(BF16), mod_b[1]), mod_b[2], norm2, w_in2, rpb2, n_ctx)
    return _s5_layer(h, (z, w_out2.astype(BF16), mod_b[2]), mod_b[3], mods[3][:nb, None, :], norm3, w_in3, s5_a_re3, s5_a_im3, s5_log_dt3,
                     s5_b_re3, s5_b_im3, s5_c_re3, s5_c_im3, s5_d3, glu_w3, glu_b3, w_out3, norm_f, n_ctx)
```

```python
import functools
import math

import jax
import jax.numpy as jnp
import numpy as np
from jax import lax
from jax.experimental import pallas as pl
from jax.experimental.pallas import tpu as pltpu

F32 = jnp.float32
BF16 = jnp.bfloat16

D = 1024
HEAD_DIM = 64
GRID_W = 64
EPS = 1e-6
NEG_INF = -1e30
ROPE_BASE = 10000.0
LANES = 128
LOG2E = math.log2(math.e)
QK_SCALE = HEAD_DIM ** -0.5 * LOG2E
ONES_ROWS = 16

LRU_WIDTH = 1408
LRU_BLOCKS = 16
LRU_BW = LRU_WIDTH // LRU_BLOCKS
LRU_WIN = 3
CONV_W = 4
LRU_C = 8.0
LRU_TT = 128

SWA_HEADS = 16
SWA_KV_HEADS = 4
WINDOW = 128
BLOCK_Q = 128
SWA_QTILES = 6
SWA_SLOTS = 4

NA_HEADS = 16
NA_ROWS = 8
NA_COLS = 16
NA_QROWS = 4
NA_KROWS = 12
NA_HSTEP = 8
NA_SLOTS = 4
NA_QBLOCKS = 3
NA_MASKED = 2 * NA_ROWS - 1

S5_WIDTH = 1024
S5_GROUP = 16
S5_GROUPS = S5_WIDTH // S5_GROUP
S5_STATE = 64
S5_CHUNK = 16
S5_SLAB = LANES // S5_GROUP
S5_ROWS = 128

ROW_TILE = 768
VMEM_LIMIT = 56 * 1024 * 1024


def _cparams(sem):
    return pltpu.CompilerParams(dimension_semantics=sem, vmem_limit_bytes=VMEM_LIMIT)


def _sigmoid(v):
    return 0.5 * jnp.tanh(0.5 * v) + 0.5


def _silu(v):
    half = 0.5 * v
    return half * (jnp.tanh(half) + 1.0)


def _bmajor(nb, rows, n_ctx, tm):
    grid = (nb, rows // tm)
    per_batch = rows // tm

    def row(width, off=0):
        return pl.BlockSpec((1, tm, width), lambda b, i: (b, i + off, 0))

    mod = pl.BlockSpec((tm // n_ctx, 1, 3 * D), lambda b, i: (b * per_batch + i, 0, 0))
    return grid, row, mod


def _bmajor_mod_table(mod16, nb, rows, n_ctx):
    groups = rows // n_ctx
    tab = jnp.broadcast_to(mod16[:nb, None, :], (nb, groups, 3 * D))
    tab = tab.at[:, 0, :].set(mod16[nb])
    return tab.reshape(nb * groups, 1, 3 * D)


def _tmajor(nb, rows, n_ctx, tt):
    grid = (1, rows // tt)
    nct = n_ctx // tt

    def row(width, off=0):
        return pl.BlockSpec((tt, nb, width), lambda b, i: (i + off, 0, 0))

    mod = pl.BlockSpec((1, nb, 3 * D), lambda b, i: (jnp.where(i < nct, 0, 1), 0, 0))
    return grid, row, mod


def _mod_body(c_ref, w_ref, b_ref, o_ref):
    s = _silu(c_ref[...]).astype(BF16)
    o_ref[...] = jnp.dot(s, w_ref[...].astype(BF16), preferred_element_type=F32) + b_ref[...]


def _modulation(cvec, ada_w, ada_b):
    n = ada_w.shape[1]
    tn = 1024
    return pl.pallas_call(
        _mod_body,
        grid=(n // tn,),
        in_specs=[pl.BlockSpec((16, D), lambda j: (0, 0)),
                  pl.BlockSpec((D, tn), lambda j: (0, j)),
                  pl.BlockSpec((1, tn), lambda j: (0, j))],
        out_specs=pl.BlockSpec((16, tn), lambda j: (0, j)),
        out_shape=jax.ShapeDtypeStruct((16, n), F32),
        compiler_params=_cparams(("parallel",)),
        name="modulation",
    )(cvec, ada_w, ada_b.reshape(1, n))


def _by_mod_group(x3, mod3):
    groups = mod3.shape[0]
    return x3 if groups == 1 else x3.reshape(groups, x3.shape[1] // groups, x3.shape[2])


def _norm_mod(x3, mod3, gn):
    xg = _by_mod_group(x3, mod3)
    ms = jnp.mean(xg * xg, axis=-1, keepdims=True)
    y = xg * lax.rsqrt(ms + EPS) * gn
    return (y * (1.0 + mod3[:, :, D:2 * D]) + mod3[:, :, 0:D]).reshape(x3.shape)


def _in_proj_body(*refs, load, n_x, epilogue, n_extra, chunk, n_side=0):
    xs, (mod_ref, gn_ref, w_ref), rest = refs[:n_x], refs[n_x:n_x + 3], refs[n_x + 3:]
    extra, outs = rest[:n_extra], rest[n_extra:]
    x3 = load(*xs, *outs[:n_side])
    outs = outs[n_side:]
    n3 = _norm_mod(x3, mod_ref[...], gn_ref[...])
    lead = x3.shape[:2]
    n = n3.reshape(lead[0] * lead[1], D).astype(BF16)
    for j in range(w_ref.shape[1] // chunk):
        acc = jnp.dot(n, w_ref[:, j * chunk:(j + 1) * chunk], preferred_element_type=F32)
        epilogue(j, acc, lead, extra, outs)


def _load_block(x_ref):
    return x_ref[...]


def _load_after_out_proj(z_ref, w_ref, h_ref, mod_ref, h_out_ref):
    z3 = z_ref[...]
    a, b, wd = z3.shape
    y = jnp.dot(z3.reshape(a * b, wd), w_ref[...], preferred_element_type=F32).reshape(a, b, D)
    mod3 = mod_ref[...]
    h = (_by_mod_group(h_ref[...], mod3) + mod3[:, :, 2 * D:3 * D] * _by_mod_group(y, mod3)).reshape(a, b, D)
    h_out_ref[...] = h
    return h


def _in_proj(xs, x_specs, load, mod, gn, w, layout, out_shapes, out_specs, epilogue, extras=(),
             extra_specs=(), chunk=512, name="in_proj", prev=None):
    grid, row, mod_spec = layout
    n = w.shape[1]
    n_side, aliases = 0, {}
    if prev is not None:
        z, w_out, mod_prev = prev
        wd = w_out.shape[0]
        (h,) = xs
        xs = (z, w_out, h, mod_prev)
        x_specs = [row(wd), pl.BlockSpec((wd, D), lambda b, i: (0, 0)), row(D), mod_spec]
        load, n_side, aliases = _load_after_out_proj, 1, {2: 0}
        out_shapes = [jax.ShapeDtypeStruct(h.shape, F32), *out_shapes]
        out_specs = [row(D), *out_specs]
    body = functools.partial(_in_proj_body, load=load, n_x=len(xs), epilogue=epilogue,
                             n_extra=len(extras), chunk=chunk, n_side=n_side)
    return pl.pallas_call(
        body,
        grid=grid,
        in_specs=[*x_specs, mod_spec,
                  pl.BlockSpec((1, D), lambda b, i: (0, 0)),
                  pl.BlockSpec((D, n), lambda b, i: (0, 0)),
                  *extra_specs],
        out_specs=out_specs,
        out_shape=out_shapes,
        input_output_aliases=aliases,
        compiler_params=_cparams(("parallel", "parallel")),
        name=name,
    )(*xs, mod, gn.reshape(1, D), w, *extras)


def _rows_out(shape2, row, outs):
    return ([jax.ShapeDtypeStruct(shape2 + (wd,), dt) for wd, dt in outs], [row(wd) for wd, _ in outs])


def _put(ref, lo, val, lead):
    ref[:, :, lo:lo + val.shape[1]] = val.reshape(lead + (val.shape[1],)).astype(ref.dtype)


def _lru_out_proj_body(z_ref, w_ref, ctx_ref, x_ref, mod_ref, o_ref, *, nct):
    z3 = z_ref[...]
    a, b, wd = z3.shape
    y = jnp.dot(z3.reshape(a * b, wd), w_ref[...], preferred_element_type=F32).reshape(a, b, D)
    gate = mod_ref[0, :, 2 * D:3 * D]
    resid = jnp.where(pl.program_id(1) < nct, ctx_ref[...], x_ref[...])
    o_ref[...] = resid + gate[:, None, :] * pltpu.einshape("tbd->btd", y)


def _lru_epilogue(j, acc, lead, extra, outs):
    _put(outs[j], 0, acc, lead)


def _lru_tile_index(s, nct, ntiles, reverse):
    if not reverse:
        return s
    return jnp.where(s < nct, nct - 1 - s, ntiles + nct - 1 - s)


def _lru_sweep_body(up_ref, uc_ref, un_ref, cw_ref, cb_ref, wg_ref, bg_ref, lam_ref, *rest,
                    nct, ntiles, reverse):
    if reverse:
        hf_ref, g_ref, o_ref, ext_ref, cv_ref, carry_ref = rest
    else:
        o_ref, ext_ref, cv_ref, carry_ref = rest
    s = pl.program_id(0)
    ti = _lru_tile_index(s, nct, ntiles, reverse)
    tt, nb, pw = uc_ref.shape

    @pl.when(s == 0)
    def _():
        carry_ref[...] = jnp.zeros_like(carry_ref)

    seg_start = (ti == 0) | (ti == nct)
    seg_end = (ti == nct - 1) | (ti == ntiles - 1)
    ext_ref[0:2] = jnp.where(seg_start, 0.0, up_ref[...])
    ext_ref[2:2 + tt] = uc_ref[...]
    ext_ref[2 + tt:3 + tt] = jnp.where(seg_end, 0.0, un_ref[...])

    rate = (-0.5 * LRU_C * LOG2E) * jax.nn.softplus(-lam_ref[...])
    tiles = [slice(k * LANES, (k + 1) * LANES) for k in range(pw // LANES)]
    for sl in tiles:
        u = cb_ref[:, sl]
        for tap in range(CONV_W):
            u = u + cw_ref[tap:tap + 1, sl] * ext_ref[tap:tap + tt, :, sl]
        cv_ref[:, sl] = u.reshape(tt * nb, LANES)
    for k, sl in enumerate(tiles):
        win = slice(_lru_window(k) * LANES, (_lru_window(k) + LRU_WIN) * LANES)
        pre = jnp.dot(cv_ref[:, win].astype(BF16), wg_ref[k], preferred_element_type=F32) + bg_ref[k]
        tr = jnp.tanh(pre[:, :LANES]) + 1.0
        ti = jnp.tanh(pre[:, LANES:]) + 1.0
        a = jnp.exp2(rate[:, sl] * tr)
        b = (0.5 * jnp.sqrt(1.0 - a * a)) * (ti * cv_ref[:, sl])
        h = carry_ref[:, sl]
        for q in range(tt):
            t = tt - 1 - q if reverse else q
            rows = slice(t * nb, (t + 1) * nb)
            h = a[rows] * h + b[rows]
            if reverse:
                o_ref[t, :, sl] = ((hf_ref[t, :, sl] + h) * _silu(g_ref[t, :, sl].astype(F32))).astype(o_ref.dtype)
            else:
                o_ref[t, :, sl] = h
        carry_ref[:, sl] = h


def _lru_sweep(u, cw, cb, wg, bg, lam, nct_rows, reverse, hf=None, g=None):
    rows, nb, pw = u.shape
    tt = LRU_TT
    ntiles = rows // tt
    nct = nct_rows // tt
    tile = functools.partial(_lru_tile_index, nct=nct, ntiles=ntiles, reverse=reverse)
    cur = lambda wd: pl.BlockSpec((tt, nb, wd), lambda s: (tile(s), 0, 0))
    const = lambda shape: pl.BlockSpec(shape, lambda s: (0,) * len(shape))
    in_specs = [
        pl.BlockSpec((2, nb, pw), lambda s: (jnp.maximum(tile(s) * (tt // 2) - 1, 0), 0, 0)),
        cur(pw),
        pl.BlockSpec((1, nb, pw), lambda s: (jnp.minimum((tile(s) + 1) * tt, rows - 1), 0, 0)),
        const((CONV_W, pw)), const((1, pw)),
        const(wg.shape), const(bg.shape), const((1, pw)),
    ]
    args = [u, u, u, cw, cb, wg, bg, lam]
    if reverse:
        in_specs += [cur(pw), cur(pw)]
        args += [hf, g]
    body = functools.partial(_lru_sweep_body, nct=nct, ntiles=ntiles, reverse=reverse)
    return pl.pallas_call(
        body,
        grid=(ntiles,),
        in_specs=in_specs,
        out_specs=cur(pw),
        out_shape=jax.ShapeDtypeStruct(u.shape, BF16 if reverse else F32),
        scratch_shapes=[pltpu.VMEM((tt + 3, nb, pw), F32),
                        pltpu.VMEM((tt * nb, pw), F32),
                        pltpu.VMEM((nb, pw), F32)],
        compiler_params=_cparams(("arbitrary",)),
        name="lru_bwd" if reverse else "lru_fwd",
    )(*args)


def _lru_window(k):
    first_block = (k * LANES) // LRU_BW
    return min((first_block * LRU_BW) // LANES, LRU_WIDTH // LANES - LRU_WIN)


def _lru_gate_weights(wa, wx, ba, bx):
    eye = 0.5 * jnp.eye(LRU_BLOCKS, dtype=BF16)

    def dense(w):
        return (w.astype(BF16)[:, :, None, :] * eye[:, None, :, None]).reshape(LRU_WIDTH, LRU_WIDTH)

    da, dx = dense(wa), dense(wx)
    wg, bg = [], []
    for k in range(LRU_WIDTH // LANES):
        rows = slice(_lru_window(k) * LANES, (_lru_window(k) + LRU_WIN) * LANES)
        cols = slice(k * LANES, (k + 1) * LANES)
        wg.append(jnp.concatenate([da[rows, cols], dx[rows, cols]], axis=1))
        bg.append(0.5 * jnp.concatenate([ba[cols], bx[cols]])[None])
    return jnp.stack(wg).astype(BF16), jnp.stack(bg)


def _load_time_major(ctx_ref, x_ref, *, nct):
    blk = jnp.where(pl.program_id(1) < nct, ctx_ref[...], x_ref[...])
    return pltpu.einshape("btd->tbd", blk)


def _rglru_layer(ctx, x, mod_t, gn, w_in, conv_w, conv_b, wa, ba, wx, bx, lam, w_out):
    nb, n_ctx, _ = ctx.shape
    rows = n_ctx + x.shape[1]
    tt = LRU_TT
    nct = n_ctx // tt
    layout = _tmajor(nb, rows, n_ctx, tt)
    grid, row, mod_spec = layout
    src_specs = [pl.BlockSpec((nb, tt, D), lambda b, i: (0, jnp.minimum(i, nct - 1), 0)),
                 pl.BlockSpec((nb, tt, D), lambda b, i: (0, jnp.maximum(i - nct, 0), 0))]
    load = functools.partial(_load_time_major, nct=nct)
    wd = LRU_WIDTH
    out_shapes, out_specs = _rows_out((rows, nb), row, [(wd, F32), (wd, BF16)])
    u, g = _in_proj((ctx, x), src_specs, load, mod_t, gn, w_in.astype(BF16), layout, out_shapes, out_specs,
                    _lru_epilogue, chunk=wd, name="lru_in_proj")
    cb = conv_b.reshape(1, wd)
    sweeps = []
    for d in range(2):
        wg, bg = _lru_gate_weights(wa[d], wx[d], ba[d], bx[d])
        sweeps.append((conv_w, cb, wg, bg, lam[d:d + 1]))
    hf = _lru_sweep(u, *sweeps[0], n_ctx, reverse=False)
    z = _lru_sweep(u, *sweeps[1], n_ctx, reverse=True, hf=hf, g=g)
    body = functools.partial(_lru_out_proj_body, nct=nct)
    return pl.pallas_call(
        body,
        grid=grid,
        in_specs=[row(wd), pl.BlockSpec((wd, D), lambda b, i: (0, 0)), *src_specs, mod_spec],
        out_specs=pl.BlockSpec((nb, tt, D), lambda b, i: (0, i, 0)),
        out_shape=jax.ShapeDtypeStruct((nb, rows, D), F32),
        compiler_params=_cparams(("parallel", "parallel")),
        name="lru_out_proj",
    )(z, w_out.astype(BF16), ctx, x, mod_t)


def _rope_tables(n_ctx, t_len):
    pos = jnp.arange(t_len)
    row = (pos // GRID_W).astype(F32)
    col = (pos % GRID_W).astype(F32)
    n_ax = HEAD_DIM // 4
    freqs = ROPE_BASE ** (-jnp.arange(n_ax, dtype=F32) / n_ax)
    ang = jnp.concatenate([row[:, None] * freqs, col[:, None] * freqs], axis=-1)
    cos, sin = jnp.cos(ang), jnp.sin(ang)
    cos_h = jnp.concatenate([cos, cos], axis=-1)
    sin_h = jnp.concatenate([-sin, sin], axis=-1)
    cos_f = jnp.concatenate([jnp.ones((n_ctx, HEAD_DIM), F32), cos_h], axis=0)
    sin_f = jnp.concatenate([jnp.zeros((n_ctx, HEAD_DIM), F32), sin_h], axis=0)
    return jnp.tile(cos_f, (1, 2)), jnp.tile(sin_f, (1, 2))


def _rope(x, cos, sin):
    lane = lax.broadcasted_iota(jnp.int32, x.shape, 1)
    swapped = jnp.where(lane % HEAD_DIM < HEAD_DIM // 2,
                        pltpu.roll(x, LANES - HEAD_DIM // 2, 1), pltpu.roll(x, HEAD_DIM // 2, 1))
    return x * cos + swapped * sin


def _swa_epilogue(j, acc, lead, extra, outs):
    cos_ref, sin_ref = extra
    q_ref, qr_ref, k_ref, v_ref, g_ref = outs
    qd = SWA_HEADS * HEAD_DIM
    kvd = SWA_KV_HEADS * HEAD_DIM
    grp = SWA_HEADS // SWA_KV_HEADS
    width = acc.shape[1]
    lo = j * width
    cos, sin = cos_ref[...], sin_ref[...]

    def roped(x):
        return jnp.concatenate([_rope(x[:, s:s + LANES], cos, sin) for s in range(0, x.shape[1], LANES)], axis=1)

    def head(x, hh):
        return x[:, hh * HEAD_DIM:(hh + 1) * HEAD_DIM]

    if lo < qd:
        q = acc * QK_SCALE
        qr = roped(q)
        for hh in range(width // HEAD_DIM):
            kv, gq = divmod(lo // HEAD_DIM + hh, grp)
            for rt in range(lead[1] // BLOCK_Q):
                rows = slice(rt * BLOCK_Q, (rt + 1) * BLOCK_Q)
                dst = slice(gq * BLOCK_Q, (gq + 1) * BLOCK_Q)
                q_ref[0, kv, rt, dst, :] = head(q, hh)[rows].astype(q_ref.dtype)
                qr_ref[0, kv, rt, dst, :] = head(qr, hh)[rows].astype(qr_ref.dtype)
    elif lo == qd:
        kr = roped(acc[:, :kvd])
        for kv in range(SWA_KV_HEADS):
            k_ref[0, kv] = head(kr, kv).astype(k_ref.dtype)
            v_ref[0, kv] = head(acc[:, kvd:], kv).T.astype(v_ref.dtype)
    else:
        _put(g_ref, lo - qd - 2 * kvd, acc, lead)


def _swa_mask_table():
    grp = SWA_HEADS // SWA_KV_HEADS
    kj = np.arange(3 * BLOCK_Q)[:, None]
    qi = np.arange(BLOCK_Q)[None, :]
    base = np.where(np.abs(kj - BLOCK_Q - qi) <= WINDOW, 0.0, NEG_INF).astype(np.float32)
    first = np.where(kj < BLOCK_Q, NEG_INF, base).astype(np.float32)
    last = np.where(kj >= 2 * BLOCK_Q, NEG_INF, base).astype(np.float32)
    tab = np.stack([np.full_like(base, NEG_INF), first, base, last])
    return jnp.asarray(np.tile(tab, (1, 1, grp)))


def _swa_body(sink_ref, *refs):
    nq, nband = SWA_QTILES, SWA_QTILES + 2
    mask_refs, refs = refs[:nq], refs[nq:]
    (q_ref, qr_ref, kc_ref), refs = refs[:3], refs[3:]
    kb_refs, refs = refs[:nband], refs[nband:]
    vc_ref, refs = refs[0], refs[1:]
    vb_refs, (g_ref, o_ref, s_ref, p_ref) = refs[:nband], refs[nband:]
    grp = SWA_HEADS // SWA_KV_HEADS
    dims = (((1,), (1,)), ((), ()))
    nc = kc_ref.shape[2]
    units = [(t, kv) for t in range(nq) for kv in range(SWA_KV_HEADS)]
    sink_p = {}

    def scores(u):
        t, kv = units[u]
        slot = u % SWA_SLOTS
        kb = jnp.concatenate([kb_refs[t + i][0, kv] for i in range(3)], axis=0)
        s_ref[slot, :nc, :] = lax.dot_general(kc_ref[0, kv], q_ref[0, kv, t], dims, preferred_element_type=F32)
        s_ref[slot, nc:, :] = (lax.dot_general(kb, qr_ref[0, kv, t], dims, preferred_element_type=F32)
                               + mask_refs[t][0])

    def softmax(u):
        _, kv = units[u]
        slot = u % SWA_SLOTS
        sink_p[u] = []
        for gq in range(grp):
            lanes = slice(gq * BLOCK_Q, (gq + 1) * BLOCK_Q)
            s = s_ref[slot, :, lanes]
            sink = sink_ref[kv, :, lanes]
            m = jnp.maximum(jnp.max(s, axis=0, keepdims=True), sink)
            sink_p[u].append(jnp.exp2(sink - m))
            p_ref[slot, :, lanes] = jnp.exp2(s - m).astype(BF16)

    def values(u):
        t, kv = units[u]
        vt = jnp.concatenate([vc_ref[0, kv]] + [vb_refs[t + i][0, kv] for i in range(3)], axis=1)
        vt = jnp.concatenate([vt, jnp.ones((ONES_ROWS, vt.shape[1]), BF16)], axis=0)
        ot = jnp.dot(vt, p_ref[u % SWA_SLOTS], preferred_element_type=F32)
        denom = ot[HEAD_DIM:HEAD_DIM + 1] + jnp.concatenate(sink_p[u], axis=1)
        ot = ot[:HEAD_DIM] * (1.0 / denom)
        slab = jnp.concatenate([ot[:, gq * BLOCK_Q:(gq + 1) * BLOCK_Q].T for gq in range(grp)], axis=1)
        rows = slice(t * BLOCK_Q, (t + 1) * BLOCK_Q)
        lanes = slice(kv * grp * HEAD_DIM, (kv + 1) * grp * HEAD_DIM)
        o_ref[0, rows, lanes] = (slab * _silu(g_ref[0, rows, lanes].astype(F32))).astype(o_ref.dtype)

    scores(0)
    scores(1)
    for u in range(len(units)):
        softmax(u)
        if u + 2 < len(units):
            scores(u + 2)
        values(u)


def _swa_attention(q, qr, k, vt, g, sink, n_ctx):
    nb, nkv, nt, qrows, hd = q.shape
    rows = k.shape[2]
    grp = SWA_HEADS // SWA_KV_HEADS
    nct = n_ctx // BLOCK_Q
    nkeys = n_ctx + 3 * BLOCK_Q
    nq = SWA_QTILES
    assert nt % nq == 0
    qspec = pl.BlockSpec((1, nkv, nq, qrows, hd), lambda b, j: (b, 0, j, 0, 0))
    kctx = pl.BlockSpec((1, nkv, n_ctx, hd), lambda b, j: (b, 0, 0, 0))
    vctx = pl.BlockSpec((1, nkv, hd, n_ctx), lambda b, j: (b, 0, 0, 0))
    rowspec = pl.BlockSpec((1, nq * BLOCK_Q, g.shape[2]), lambda b, j: (b, j, 0))

    def tile(j, off):
        return jnp.clip(j * nq - 1 + off, nct, nt - 1)

    def kband(off):
        return pl.BlockSpec((1, nkv, BLOCK_Q, hd), lambda b, j: (b, 0, tile(j, off), 0))

    def vband(off):
        return pl.BlockSpec((1, nkv, hd, BLOCK_Q), lambda b, j: (b, 0, 0, tile(j, off)))

    def mask(t):
        def variant(b, j):
            n = j * nq + t
            return (jnp.where(n < nct, 0, jnp.where(n == nct, 1, jnp.where(n == nt - 1, 3, 2))), 0, 0)
        return pl.BlockSpec((1, 3 * BLOCK_Q, qrows), variant)

    sink_rows = jnp.repeat(sink.astype(F32).reshape(nkv, grp) * LOG2E, BLOCK_Q, axis=1).reshape(nkv, 1, qrows)
    masks = _swa_mask_table()
    return pl.pallas_call(
        _swa_body,
        grid=(nb, nt // nq),
        in_specs=[pl.BlockSpec((nkv, 1, qrows), lambda b, j: (0, 0, 0)),
                  *[mask(t) for t in range(nq)], qspec, qspec,
                  kctx, *[kband(i) for i in range(nq + 2)], vctx, *[vband(i) for i in range(nq + 2)], rowspec],
        out_specs=rowspec,
        out_shape=jax.ShapeDtypeStruct((nb, rows, g.shape[2]), BF16),
        scratch_shapes=[pltpu.VMEM((SWA_SLOTS, nkeys, qrows), F32), pltpu.VMEM((SWA_SLOTS, nkeys, qrows), BF16)],
        compiler_params=_cparams(("parallel", "parallel")),
        name="swa_attention",
    )(sink_rows, *[masks] * nq, q, qr, k, *[k] * (nq + 2), vt, *[vt] * (nq + 2), g)


def _swa_layer(h, mod_b, gn, w_in, sink, n_ctx):
    nb, rows, _ = h.shape
    tm = ROW_TILE
    layout = _bmajor(nb, rows, n_ctx, tm)
    _, row, _ = layout
    qd = SWA_HEADS * HEAD_DIM
    grp = SWA_HEADS // SWA_KV_HEADS
    nkv = SWA_KV_HEADS
    cos, sin = _rope_tables(n_ctx, rows - n_ctx)
    tab = pl.BlockSpec((tm, LANES), lambda b, i: (i, 0))
    q_shape = jax.ShapeDtypeStruct((nb, nkv, rows // BLOCK_Q, grp * BLOCK_Q, HEAD_DIM), BF16)
    q_spec = pl.BlockSpec((1, nkv, tm // BLOCK_Q, grp * BLOCK_Q, HEAD_DIM), lambda b, i: (b, 0, i, 0, 0))
    k_shape = jax.ShapeDtypeStruct((nb, nkv, rows, HEAD_DIM), BF16)
    k_spec = pl.BlockSpec((1, nkv, tm, HEAD_DIM), lambda b, i: (b, 0, i, 0))
    vt_shape = jax.ShapeDtypeStruct((nb, nkv, HEAD_DIM, rows), BF16)
    vt_spec = pl.BlockSpec((1, nkv, HEAD_DIM, tm), lambda b, i: (b, 0, 0, i))
    q, qr, k, vt, g = _in_proj(
        (h,), [row(D)], _load_block, mod_b, gn, w_in.astype(BF16), layout,
        [q_shape, q_shape, k_shape, vt_shape, jax.ShapeDtypeStruct((nb, rows, qd), BF16)],
        [q_spec, q_spec, k_spec, vt_spec, row(qd)], _swa_epilogue,
        extras=(cos, sin), extra_specs=(tab, tab), name="swa_in_proj")
    return _swa_attention(q, qr, k, vt, g, sink, n_ctx)


def _na_epilogue(j, acc, lead, extra, outs):
    q_ref, k_ref, v_ref, g_ref = outs
    wd = NA_HEADS * HEAD_DIM
    width = acc.shape[1]
    lo = j * width
    which, off = lo // wd, lo % wd
    if which == 3:
        _put(g_ref, off, acc, lead)
        return
    ref = (q_ref, k_ref, v_ref)[which]
    val = acc * QK_SCALE if which == 0 else acc
    for hh in range(width // HEAD_DIM):
        ref[0, off // HEAD_DIM + hh] = val[:, hh * HEAD_DIM:(hh + 1) * HEAD_DIM].astype(ref.dtype)


def _na_bias_tiles(rpb):
    reach = GRID_W - NA_COLS
    period = 2 * GRID_W
    ndy = 2 * NA_ROWS - 1
    edge = jnp.pad(rpb.astype(F32) * LOG2E, ((0, 0), (0, 0), (reach, reach)), mode="edge")
    flat = jnp.tile(jnp.pad(edge[:, :, ::-1], ((0, 0), (0, 0), (0, 1))), (1, 1, GRID_W))[:, :, :GRID_W * (period - 1)]
    e = flat.reshape(NA_HEADS, ndy, GRID_W, period - 1)[:, :, :, GRID_W - 1:]
    col = np.arange(GRID_W)
    cstart = np.clip(col - NA_COLS // 2, 0, GRID_W - NA_COLS)
    col_ok = (col[:, None] >= cstart[None, :]) & (col[:, None] < cstart[None, :] + NA_COLS)
    e = jnp.where(jnp.asarray(col_ok)[None, None], e, NEG_INF)
    e = jnp.pad(e, ((0, 0), (0, 1), (0, 0), (0, 0)), constant_values=NEG_INF)
    masked = e[:, ndy:ndy + 1]
    blank = jnp.broadcast_to(masked, e[:, :ndy].shape)
    tab = jnp.concatenate([jnp.concatenate([e[:, 1:ndy], e[:, 0:ndy - 1]], axis=-1),
                           jnp.concatenate([e[:, :ndy], blank], axis=-1),
                           jnp.concatenate([blank, e[:, :ndy]], axis=-1),
                           jnp.concatenate([masked, masked], axis=-1)], axis=1)
    return tab.reshape(NA_HEADS // NA_HSTEP, NA_HSTEP, 3 * ndy, GRID_W, period)


def _na_body(q_ref, k_ref, v_ref, g_ref, bt_ref, o_ref, s_ref, p_ref, *, n_ctx, grid_rows):
    nk = NA_KROWS * GRID_W
    nq = NA_QROWS * GRID_W
    a_bt = (((1,), (1,)), ((), ()))
    at_b = (((0,), (0,)), ((), ()))
    ones = jnp.ones((n_ctx + nk, HEAD_DIM), BF16)

    def block_geometry(qb):
        rb = pl.program_id(2) * NA_QBLOCKS + qb
        r0 = (rb - 1) * NA_QROWS
        kb = jnp.clip(r0 - NA_ROWS // 2, 0, grid_rows - NA_KROWS)

        def tile_index(qr, kr):
            qrow, krow = r0 + qr, kb + kr
            first = jnp.clip(qrow - NA_ROWS // 2, 0, grid_rows - NA_ROWS)
            inside = (rb >= 1) & (krow >= first) & (krow < first + NA_ROWS)
            return jnp.where(inside, krow - qrow + NA_ROWS - 1, NA_MASKED)

        def pair_index(m, kr):
            dl, dr = tile_index(2 * m, kr), tile_index(2 * m + 1, kr)
            lv, rv = dl != NA_MASKED, dr != NA_MASKED
            return jnp.where(lv & rv, dl - 1,
                             jnp.where(lv, NA_MASKED - 1 + dl, jnp.where(rv, 2 * NA_MASKED - 1 + dr, 3 * NA_MASKED - 1)))

        idx = [[pair_index(m, kr) for kr in range(NA_KROWS)] for m in range(NA_QROWS // 2)]
        return pl.multiple_of(n_ctx + kb * GRID_W, GRID_W), idx

    geometry = [block_geometry(qb) for qb in range(NA_QBLOCKS)]
    units = [(qb, hl) for qb in range(NA_QBLOCKS) for hl in range(NA_HSTEP)]
    outs = {}

    def scores(u):
        qb, hl = units[u]
        start, idx = geometry[qb]
        slot = u % NA_SLOTS
        qh = q_ref[0, hl, qb * nq:(qb + 1) * nq, :]
        bias = jnp.concatenate(
            [jnp.concatenate([bt_ref[0, hl, idx[m][kr]] for m in range(NA_QROWS // 2)], axis=1)
             for kr in range(NA_KROWS)], axis=0)
        s_ref[slot, :n_ctx, :] = lax.dot_general(k_ref[0, hl, 0:n_ctx, :], qh, a_bt, preferred_element_type=F32)
        s_ref[slot, n_ctx:, :] = (lax.dot_general(k_ref[0, hl, pl.ds(start, nk), :], qh, a_bt,
                                                  preferred_element_type=F32) + bias)

    def softmax(u):
        slot = u % NA_SLOTS
        for c in range(nq // LANES):
            lanes = slice(c * LANES, (c + 1) * LANES)
            s = s_ref[slot, :, lanes]
            p_ref[slot, :, lanes] = jnp.exp2(s - jnp.max(s, axis=0, keepdims=True)).astype(BF16)

    def values(u):
        qb, hl = units[u]
        start, _ = geometry[qb]
        v1 = jnp.concatenate(
            [jnp.concatenate([v_ref[0, hl, 0:n_ctx, :], v_ref[0, hl, pl.ds(start, nk), :]], axis=0), ones], axis=1)
        ot = lax.dot_general(v1, p_ref[u % NA_SLOTS], at_b, preferred_element_type=F32)
        outs[hl] = (ot[:HEAD_DIM] * (1.0 / ot[HEAD_DIM:HEAD_DIM + 1])).T
        if hl == NA_HSTEP - 1:
            rows = slice(qb * nq, (qb + 1) * nq)
            slab = jnp.concatenate([outs[h] for h in range(NA_HSTEP)], axis=1)
            o_ref[0, rows, :] = (slab * _silu(g_ref[0, rows, :].astype(F32))).astype(o_ref.dtype)

    scores(0)
    scores(1)
    for u in range(len(units)):
        softmax(u)
        if u + 2 < len(units):
            scores(u + 2)
        values(u)


def _na_attention(q, k, v, g, bias_pairs, n_ctx):
    nb, nh, rows, hd = q.shape
    grid_rows = (rows - n_ctx) // GRID_W
    nq = NA_QROWS * GRID_W
    assert n_ctx == nq, "context tokens form exactly one query block"
    assert rows % (NA_QBLOCKS * nq) == 0
    nkeys = n_ctx + NA_KROWS * GRID_W
    lanes = NA_HSTEP * HEAD_DIM
    qspec = pl.BlockSpec((1, NA_HSTEP, NA_QBLOCKS * nq, hd), lambda p, b, r: (b, p, r, 0))
    kvspec = pl.BlockSpec((1, NA_HSTEP, rows, hd), lambda p, b, r: (b, p, 0, 0))
    rowspec = pl.BlockSpec((1, NA_QBLOCKS * nq, lanes), lambda p, b, r: (b, r, p))
    bspec = pl.BlockSpec((1,) + bias_pairs.shape[1:], lambda p, b, r: (p, 0, 0, 0, 0), pipeline_mode=pl.Buffered(1))
    body = functools.partial(_na_body, n_ctx=n_ctx, grid_rows=grid_rows)
    return pl.pallas_call(
        body,
        grid=(nh // NA_HSTEP, nb, rows // (NA_QBLOCKS * nq)),
        in_specs=[qspec, kvspec, kvspec, rowspec, bspec],
        out_specs=rowspec,
        out_shape=jax.ShapeDtypeStruct(g.shape, BF16),
        scratch_shapes=[pltpu.VMEM((NA_SLOTS, nkeys, nq), F32), pltpu.VMEM((NA_SLOTS, nkeys, nq), BF16)],
        compiler_params=_cparams(("parallel", "parallel", "parallel")),
        name="na_attention",
    )(q, k, v, g, bias_pairs)


def _na_layer(h, prev, mod_b, gn, w_in, rpb, n_ctx):
    nb, rows, _ = h.shape
    tm = ROW_TILE
    layout = _bmajor(nb, rows, n_ctx, tm)
    _, row, _ = layout
    wd = NA_HEADS * HEAD_DIM
    head_shape = jax.ShapeDtypeStruct((nb, NA_HEADS, rows, HEAD_DIM), BF16)
    head_spec = pl.BlockSpec((1, NA_HEADS, tm, HEAD_DIM), lambda b, i: (b, 0, i, 0))
    h, q, k, v, g = _in_proj((h,), [row(D)], _load_block, mod_b, gn, w_in.astype(BF16), layout,
                             [head_shape] * 3 + [jax.ShapeDtypeStruct((nb, rows, wd), BF16)],
                             [head_spec] * 3 + [row(wd)], _na_epilogue, name="swa_out_na_in_proj", prev=prev)
    return h, _na_attention(q, k, v, g, _na_bias_tiles(rpb), n_ctx)


def _s5_epilogue(j, acc, lead, extra, outs):
    u_ref, g_ref = outs
    lo = j * acc.shape[1]
    if lo < S5_WIDTH:
        _put(u_ref, lo, acc, lead)
    else:
        _put(g_ref, lo - S5_WIDTH, acc, lead)


def _s5_matrices(a_re, a_im, log_dt, b_re, b_im, c_re, c_im, d_skip):
    L = S5_CHUNK
    lam = lax.complex(a_re.astype(F32), a_im.astype(F32))
    lam_dt = lam * jnp.exp(log_dt.astype(F32))[..., None]
    lam_bar = jnp.exp(lam_dt)
    b_bar = ((lam_bar - 1.0) / lam)[..., None] * lax.complex(b_re.astype(F32), b_im.astype(F32))
    c_mat = lax.complex(c_re.astype(F32), c_im.astype(F32))
    tau = np.arange(L)

    def power(expo, d):
        return jnp.exp(jnp.asarray(expo, F32).reshape(expo.shape + (1, 1)) * lam_dt[d])

    w = jnp.stack([c_mat[d][None] * power(tau, d)[:, :, None, :] for d in range(2)])
    w2 = jnp.concatenate([jnp.real(w), -jnp.imag(w)], axis=-1)
    b2 = jnp.concatenate([jnp.real(b_bar), jnp.imag(b_bar)], axis=-2)
    kern = jnp.einsum('dkgip,dgpj->dkgij', w2, b2).reshape(2 * L, S5_GROUPS, S5_GROUP, S5_GROUP)
    lag = tau[None, :] - tau[:, None]
    sel = np.concatenate([lag[None] == tau[:, None, None], -lag[None] == tau[:, None, None]]).astype(np.float32)
    both = jnp.einsum('ktu,kgij->tugij', jnp.asarray(sel), kern)
    eye_t = jnp.eye(L, dtype=F32)[:, :, None, None, None]
    skip = jnp.eye(S5_GROUP, dtype=F32)[None] * d_skip.astype(F32).reshape(S5_GROUPS, S5_GROUP, 1)
    both = both + eye_t * skip[None, None]
    toep = both.transpose(2, 0, 4, 1, 3).reshape(S5_GROUPS, L * S5_GROUP, L * S5_GROUP)

    def lanes(z):
        pad = [(0, 0)] * (z.ndim - 1) + [(0, LANES - S5_STATE)]
        return jnp.concatenate([jnp.pad(jnp.real(z), pad), jnp.pad(jnp.imag(z), pad)], axis=-1)

    pf = power(L - 1 - tau, 0)[:, :, :, None] * b_bar[0][None]
    pb = power(tau, 1)[:, :, :, None] * b_bar[1][None]
    pmat = jnp.concatenate([lanes(pf.transpose(1, 0, 3, 2)), lanes(pb.transpose(1, 0, 3, 2))], axis=-1)
    pmat = pmat.reshape(S5_GROUPS, L * S5_GROUP, 4 * LANES)
    qf = c_mat[0][None] * power(tau + 1, 0)[:, :, None, :]
    qb = c_mat[1][None] * power(L - tau, 1)[:, :, None, :]

    def state_rows(z):
        pad = ((0, 0), (0, LANES - S5_STATE), (0, 0), (0, 0))
        zr = jnp.pad(jnp.real(z).transpose(1, 3, 0, 2), pad)
        zi = jnp.pad(-jnp.imag(z).transpose(1, 3, 0, 2), pad)
        return jnp.concatenate([zr, zi], axis=1).reshape(S5_GROUPS, 2 * LANES, L * S5_GROUP)

    qmat = jnp.concatenate([state_rows(qf), state_rows(qb)], axis=1)
    lam_l = jnp.exp(float(L) * lam_dt)
    pad = ((0, 0), (0, LANES - S5_STATE))
    coef = jnp.stack([jnp.pad(jnp.real(lam_l[0]), pad), jnp.pad(jnp.imag(lam_l[0]), pad),
                      jnp.pad(jnp.real(lam_l[1]), pad), jnp.pad(jnp.imag(lam_l[1]), pad)], axis=1)
    coef = jnp.pad(coef, ((0, 0), (0, 4), (0, 0)))
    return toep.astype(BF16), pmat.astype(BF16), qmat.astype(BF16), coef


def _atom_transpose(xs, atom):
    n = len(xs)
    group = lax.broadcasted_iota(jnp.int32, xs[0].shape, 1) // atom
    xs = list(xs)
    d = n // 2
    while d:
        keep = (group & d) == 0
        for i in range(n):
            if not i & d:
                lo, hi = xs[i], xs[i + d]
                xs[i] = jnp.where(keep, lo, pltpu.roll(hi, d * atom, 1))
                xs[i + d] = jnp.where(keep, pltpu.roll(lo, LANES - d * atom, 1), hi)
        d //= 2
    return xs


def _s5_body(u_ref, toep_ref, p_ref, q_ref, coef_ref, y_ref, u_scr, y_scr, s_ref, hin_ref, *, nb, nc_ctx, nc):
    nwin = LANES // S5_GROUP
    nrow, nlat = u_ref.shape[1], y_ref.shape[1]
    blk = S5_ROWS
    for half in range(S5_CHUNK // nwin):
        cols = slice(half * LANES, (half + 1) * LANES)
        for r in range(0, nrow, blk):
            words = [pltpu.bitcast(u_ref[half * nwin + w, r:r + blk, :], jnp.uint32) for w in range(nwin)]
            for gl, x in enumerate(_atom_transpose(words, S5_GROUP)):
                u_scr[gl, r:r + blk, cols] = pltpu.bitcast(x, BF16)
    for gl in range(S5_SLAB):
        _s5_group(gl, u_scr, toep_ref, p_ref, q_ref, coef_ref, y_scr, s_ref, hin_ref, nb, nc_ctx, nc)
    for half in range(S5_CHUNK // nwin):
        cols = slice(half * LANES, (half + 1) * LANES)
        for r in range(0, nlat, blk):
            ys = [y_scr[gl, r:r + blk, cols] for gl in range(S5_SLAB)]
            for w, x in enumerate(_atom_transpose(ys, S5_GROUP)):
                y_ref[half * nwin + w, r:r + blk, :] = x


def _s5_group(gl, u_scr, toep_ref, p_ref, q_ref, coef_ref, y_scr, s_ref, hin_ref, nb, nc_ctx, nc):
    u = u_scr[gl]
    s_ref[...] = jnp.dot(u, p_ref[gl], preferred_element_type=F32)
    coef = coef_ref[gl]
    shape = (nb, LANES)
    lr_f, li_f = jnp.broadcast_to(coef[0:1], shape), jnp.broadcast_to(coef[1:2], shape)
    lr_b, li_b = jnp.broadcast_to(coef[2:3], shape), jnp.broadcast_to(coef[3:4], shape)
    zero = jnp.zeros(shape, F32)

    def step(q, carry):
        fr, fi, br, bi = carry
        rows_f = pl.ds(pl.multiple_of(q * nb, nb), nb)
        cb = jnp.where(q < nc_ctx, nc_ctx - 1 - q, nc + nc_ctx - 1 - q)
        rows_b = pl.ds(pl.multiple_of(cb * nb, nb), nb)
        hin_ref[rows_f, 0:LANES] = fr
        hin_ref[rows_f, LANES:2 * LANES] = fi
        hin_ref[rows_b, 2 * LANES:3 * LANES] = br
        hin_ref[rows_b, 3 * LANES:4 * LANES] = bi
        sfr, sfi = s_ref[rows_f, 0:LANES], s_ref[rows_f, LANES:2 * LANES]
        sbr, sbi = s_ref[rows_b, 2 * LANES:3 * LANES], s_ref[rows_b, 3 * LANES:4 * LANES]
        return (lr_f * fr - li_f * fi + sfr, lr_f * fi + li_f * fr + sfi,
                lr_b * br - li_b * bi + sbr, lr_b * bi + li_b * br + sbi)

    lax.fori_loop(0, nc, step, (zero, zero, zero, zero), unroll=True)

    lat = slice(nc_ctx * nb, nc * nb)
    y_scr[gl] = (jnp.dot(u[lat], toep_ref[gl], preferred_element_type=F32)
                 + jnp.dot(hin_ref[lat, :].astype(BF16), q_ref[gl], preferred_element_type=F32))


def _s5_scan(u_t, toep, pmat, qmat, coef, nb, n_ctx):
    L, nrow, wd = u_t.shape
    pk = L * S5_GROUP
    nc = nrow // nb
    nc_ctx = n_ctx // L
    nlat = (nc - nc_ctx) * nb
    assert nrow % S5_ROWS == 0 and nlat % S5_ROWS == 0
    once = pl.Buffered(1)
    slab = lambda shape: pl.BlockSpec((S5_SLAB,) + shape, lambda s: (s, 0, 0), pipeline_mode=once)
    body = functools.partial(_s5_body, nb=nb, nc_ctx=nc_ctx, nc=nc)
    return pl.pallas_call(
        body,
        grid=(wd // LANES,),
        in_specs=[pl.BlockSpec((L, nrow, LANES), lambda s: (0, 0, s)),
                  slab((pk, pk)), slab((pk, 4 * LANES)), slab((4 * LANES, pk)), slab((8, LANES))],
        out_specs=pl.BlockSpec((L, nlat, LANES), lambda s: (0, 0, s)),
        out_shape=jax.ShapeDtypeStruct((L, nlat, wd), F32),
        scratch_shapes=[pltpu.VMEM((S5_SLAB, nrow, pk), BF16), pltpu.VMEM((S5_SLAB, nlat, pk), F32),
                        pltpu.VMEM((nrow, 4 * LANES), F32), pltpu.VMEM((nrow, 4 * LANES), F32)],
        compiler_params=_cparams(("parallel",)),
        name="s5_scan",
    )(u_t, toep, pmat, qmat, coef)


def _rms(x, g):
    return x * lax.rsqrt(jnp.mean(x * x, axis=-1, keepdims=True) + EPS) * g


def _s5_readout_body(y_ref, ga_ref, gb_ref, ha_ref, hb_ref, gw_ref, gbias_ref, w_ref, mod_ref, nf_ref, o_ref):
    sub = ga_ref.shape[1]
    for i, (g_ref, h_ref) in enumerate(((ga_ref, ha_ref), (gb_ref, hb_ref))):
        rows = slice(i * sub, (i + 1) * sub)
        y = jax.nn.gelu(y_ref[0, rows, :])
        t = y * _sigmoid(jnp.dot(y.astype(BF16), gw_ref[...], preferred_element_type=F32) + gbias_ref[...])
        z = (t * _silu(g_ref[0].astype(F32))).astype(BF16)
        out = jnp.dot(z, w_ref[...], preferred_element_type=F32)
        h = h_ref[0] + mod_ref[0, :, 2 * D:3 * D] * out
        o_ref[0, rows, :] = _rms(h, nf_ref[...])


def _s5_readout(y, g, glu_w, glu_b, w_out, h, mod_b, norm_f, n_ctx):
    nb, t_len, _ = y.shape
    sub = n_ctx
    tm = 2 * sub
    lat = lambda wd: pl.BlockSpec((1, tm, wd), lambda b, i: (b, i, 0))
    part = lambda wd, j: pl.BlockSpec((1, sub, wd), lambda b, i: (b, 2 * i + 1 + j, 0))
    const = lambda shape: pl.BlockSpec(shape, lambda b, i: (0, 0))
    return pl.pallas_call(
        _s5_readout_body,
        grid=(nb, t_len // tm),
        in_specs=[lat(S5_WIDTH), part(S5_WIDTH, 0), part(S5_WIDTH, 1), part(D, 0), part(D, 1),
                  const((S5_WIDTH, S5_WIDTH)), const((1, S5_WIDTH)), const((S5_WIDTH, D)),
                  pl.BlockSpec((1, 1, 3 * D), lambda b, i: (b, 0, 0)), const((1, D))],
        out_specs=lat(D),
        out_shape=jax.ShapeDtypeStruct((nb, t_len, D), F32),
        compiler_params=_cparams(("parallel", "parallel")),
        name="s5_readout",
    )(y, g, g, h, h, glu_w.astype(BF16), glu_b.reshape(1, S5_WIDTH), w_out.astype(BF16), mod_b,
      norm_f.reshape(1, D))


def _s5_layer(h, prev, mod_b, mod_lat, gn, w_in, a_re, a_im, log_dt, b_re, b_im, c_re, c_im, d_skip, glu_w, glu_b,
              w_out, norm_f, n_ctx):
    nb, rows, _ = h.shape
    L = S5_CHUNK
    layout = _bmajor(nb, rows, n_ctx, ROW_TILE)
    _, row, _ = layout
    out_shapes, out_specs = _rows_out((nb, rows), row, [(S5_WIDTH, BF16)] * 2)
    h, u, g = _in_proj((h,), [row(D)], _load_block, mod_b, gn, w_in.astype(BF16), layout, out_shapes, out_specs,
                       _s5_epilogue, name="na_out_s5_in_proj", prev=prev)
    nc = rows // L
    u_t = u.reshape(nb, nc, L, S5_WIDTH).transpose(2, 1, 0, 3).reshape(L, nc * nb, S5_WIDTH)
    toep, pmat, qmat, coef = _s5_matrices(a_re, a_im, log_dt, b_re, b_im, c_re, c_im, d_skip)
    y_t = _s5_scan(u_t, toep, pmat, qmat, coef, nb, n_ctx)
    t_len = rows - n_ctx
    y = y_t.reshape(L, t_len // L, nb, S5_WIDTH).transpose(2, 1, 0, 3).reshape(nb, t_len, S5_WIDTH)
    return _s5_readout(y, g, glu_w, glu_b, w_out, h, mod_lat, norm_f, n_ctx)


def kernel(x, c, ctx, c_ctx, ada_w0, ada_b0, norm0, w_in0, conv_w0, conv_b0, lru_wa0, lru_ba0, lru_wx0, lru_bx0, lru_lam0, w_out0, ada_w1, ada_b1, norm1, w_in1, sink1, w_out1, ada_w2, ada_b2, norm2, w_in2, rpb2, w_out2, ada_w3, ada_b3, norm3, w_in3, s5_a_re3, s5_a_im3, s5_log_dt3, s5_b_re3, s5_b_im3, s5_c_re3, s5_c_im3, s5_d3, glu_w3, glu_b3, w_out3, norm_f):
    nb, t_len, _ = x.shape
    n_ctx = ctx.shape[1]
    cvec = jnp.concatenate([c, c_ctx[None, :], jnp.zeros((16 - nb - 1, D), F32)], axis=0)
    mods = [_modulation(cvec, w, b) for w, b in
            ((ada_w0, ada_b0), (ada_w1, ada_b1), (ada_w2, ada_b2), (ada_w3, ada_b3))]
    mod_b = [_bmajor_mod_table(m, nb, n_ctx + t_len, n_ctx) for m in mods]

    mod_t0 = jnp.stack([jnp.broadcast_to(mods[0][nb], (nb, 3 * D)), mods[0][:nb]], axis=0)
    h = _rglru_layer(ctx, x, mod_t0, norm0, w_in0, conv_w0, conv_b0, lru_wa0, lru_ba0, lru_wx0, lru_bx0,
                     lru_lam0, w_out0)
    z = _swa_layer(h, mod_b[1], norm1, w_in1, sink1, n_ctx)
    h, z = _na_layer(h, (z, w_out1.astype(BF16), mod_b[1]), mod_b[2], norm2, w_in2, rpb2, n_ctx)
    return _s5_layer(h, (z, w_out2.astype(BF16), mod_b[2]), mod_b[3], mods[3][:nb, None, :], norm3, w_in3, s5_a_re3, s5_a_im3, s5_log_dt3,
                     s5_b_re3, s5_b_im3, s5_c_re3, s5_c_im3, s5_d3, glu_w3, glu_b3, w_out3, norm_f, n_ctx)
```
